```python
import math
import jax
import jax.numpy as jnp
from jax import lax
import numpy as np

D_MODEL = 1024
BATCH = 1
SEQ = 16384
DEPTH = 2
DEC_BATCH = 32
DEC_SEQ = 1
PAST_LEN = 16384
PAGE_SIZE = 128

N_A_LAYERS = DEPTH // 2
N_B_LAYERS = DEPTH - N_A_LAYERS

A_HEAD_DIM = 128
A_HEADS = D_MODEL // A_HEAD_DIM
A_WIDTH = A_HEADS * A_HEAD_DIM
A_CONV = 4
CHUNK = 64
A_PROJ = 3 * A_WIDTH + 2 * A_HEADS + A_WIDTH

HEAD_DIM = 128
GROUPS = ((128, 1), (512, 4), (2048, 16))
N_GROUPS = len(GROUPS)
Q_PER_GROUP = 4
KV_PER_GROUP = 2
Q_REP = Q_PER_GROUP // KV_PER_GROUP
N_Q_HEADS = N_GROUPS * Q_PER_GROUP
N_KV_HEADS = N_GROUPS * KV_PER_GROUP
ROPE_DIM = HEAD_DIM // 4
ROPE_THETA = 500000.0

D_FF = 2816
FFN_CONV = 3
PLE_DIM = 256
EPS = 1e-6
F32 = jnp.float32

kernel_name = 'yoco_gated_deltanet_dilated_window_step'


def rmsnorm(x, gain):
    xf = x.astype(F32)
    y = xf * lax.rsqrt(jnp.mean(xf * xf, axis=-1, keepdims=True) + EPS)
    return (y * gain.astype(F32)).astype(x.dtype)


def l2norm(x):
    return x * lax.rsqrt(jnp.sum(x * x, axis=-1, keepdims=True) + EPS)


def partial_rope(x, pos):
    half = ROPE_DIM // 2
    inv_freq = ROPE_THETA ** (-jnp.arange(half, dtype=F32) * 2.0 / ROPE_DIM)
    ang = pos.astype(F32)[:, None] * inv_freq[None, :]
    shape = (1, pos.shape[0]) + (1,) * (x.ndim - 3) + (half,)
    cos = jnp.cos(ang).reshape(shape)
    sin = jnp.sin(ang).reshape(shape)
    xr = x[..., :ROPE_DIM].astype(F32)
    x1, x2 = xr[..., :half], xr[..., half:]
    rot = jnp.concatenate([x1 * cos - x2 * sin, x2 * cos + x1 * sin], axis=-1).astype(x.dtype)
    return jnp.concatenate([rot, x[..., ROPE_DIM:]], axis=-1)


def causal_dwconv(x, buf, w):
    width, t = w.shape[0], x.shape[1]
    xp = jnp.concatenate([buf.astype(x.dtype), x], axis=1)
    y = xp[:, 0:t] * w[0]
    for j in range(1, width):
        y = y + xp[:, j:j + t] * w[j]
    return y, xp[:, t:]


def gated_delta_chunked(q, k, v, beta, g, s0):
    b, t, h, _ = q.shape
    dv = v.shape[-1]
    nc = t // CHUNK

    def to_chunks(a):
        a = a.reshape((b, nc, CHUNK, h) + a.shape[3:])
        return jnp.moveaxis(a, 3, 2)

    q, k, v, beta, g = (to_chunks(a) for a in (q, k, v, beta, g))
    gc = jnp.cumsum(g, axis=-1)
    idx = jnp.arange(CHUNK)
    incl = idx[:, None] >= idx[None, :]
    strict = idx[:, None] > idx[None, :]
    decay = jnp.exp(jnp.where(incl, gc[..., :, None] - gc[..., None, :], -jnp.inf))
    kk = jnp.einsum('bnhic,bnhjc->bnhij', k, k)
    tri = jnp.eye(CHUNK, dtype=F32) + jnp.where(strict, beta[..., :, None] * kk * decay, 0.0)
    gam = jnp.exp(gc)
    rhs = jnp.concatenate([v * beta[..., None], k * (beta * gam)[..., None]], axis=-1)
    sol = lax.linalg.triangular_solve(tri, rhs, left_side=True, lower=True, unit_diagonal=True)
    u0, wk = sol[..., :dv], sol[..., dv:]
    qk = jnp.einsum('bnhic,bnhjc->bnhij', q, k) * decay
    qg = q * gam[..., None]
    kd = k * jnp.exp(gc[..., -1:] - gc)[..., None]
    gend = jnp.exp(gc[..., -1])

    def step(s, xs):
        u0c, wkc, qkc, qgc, kdc, gec = xs
        u = u0c - jnp.einsum('bhck,bhkv->bhcv', wkc, s)
        o = jnp.einsum('bhck,bhkv->bhcv', qgc, s) + jnp.einsum('bhij,bhjv->bhiv', qkc, u)
        s = gec[..., None, None] * s + jnp.einsum('bhck,bhcv->bhkv', kdc, u)
        return s, o

    xs = tuple(jnp.moveaxis(a, 1, 0) for a in (u0, wk, qk, qg, kd, gend))
    s1, o = lax.scan(step, s0, xs)
    o = jnp.moveaxis(jnp.moveaxis(o, 0, 1), 2, 3).reshape(b, t, h, dv)
    return o, s1


def gated_delta_steps(q, k, v, beta, g, s0):
    def step(s, xs):
        qt, kt, vt, bt, gt = xs
        s = jnp.exp(gt)[..., None, None] * s
        u = bt[..., None] * (vt - jnp.einsum('bhk,bhkv->bhv', kt, s))
        s = s + kt[..., :, None] * u[..., None, :]
        return s, jnp.einsum('bhk,bhkv->bhv', qt, s)

    xs = tuple(jnp.moveaxis(a, 1, 0) for a in (q, k, v, beta, g))
    s1, o = lax.scan(step, s0, xs)
    return jnp.moveaxis(o, 0, 1), s1


def deltanet_mixer(hn, qkv_buf, s0, w_in, w_conv, a_log, dt_bias, g_out, w_out, chunked):
    b, t, _ = hn.shape
    proj = hn @ w_in
    o1, o2 = 3 * A_WIDTH, 3 * A_WIDTH + A_HEADS
    qkv, qkv_buf = causal_dwconv(proj[..., :o1], qkv_buf, w_conv)
    qkv = jax.nn.silu(qkv.astype(F32)).reshape(b, t, 3, A_HEADS, A_HEAD_DIM)
    q = l2norm(qkv[:, :, 0]) * (A_HEAD_DIM ** -0.5)
    k = l2norm(qkv[:, :, 1])
    v = qkv[:, :, 2]
    a = proj[..., o1:o2].astype(F32)
    beta = jax.nn.sigmoid(proj[..., o2:o2 + A_HEADS].astype(F32))
    z = proj[..., o2 + A_HEADS:].reshape(b, t, A_HEADS, A_HEAD_DIM).astype(F32)
    g = -jnp.exp(a_log.astype(F32)) * jax.nn.softplus(a + dt_bias.astype(F32))
    s0 = s0.astype(F32)
    if chunked:
        o, s1 = gated_delta_chunked(q, k, v, beta, g, s0)
    else:
        o, s1 = gated_delta_steps(q, k, v, beta, g, s0)
    o = rmsnorm(o, g_out) * jax.nn.silu(z)
    return o.reshape(b, t, A_WIDTH).astype(hn.dtype) @ w_out, s1, qkv_buf


def conv_ffn(hn, buf, w_up, w_conv, w_down):
    u, buf = causal_dwconv(hn @ w_up, buf, w_conv)
    return (jax.nn.silu(u[..., :D_FF]) * u[..., D_FF:]) @ w_down, buf


def shared_kv(h, pos, g_kv_norm, w_kv, g_k_norm):
    b, t, _ = h.shape
    kv = (rmsnorm(h, g_kv_norm) @ w_kv).reshape(b, t, 2, N_KV_HEADS, HEAD_DIM)
    k = partial_rope(rmsnorm(kv[:, :, 0], g_k_norm), pos)
    return k, kv[:, :, 1]


def dilated_band_prompt(q, k, v, window, dil):
    b, s = q.shape[:2]
    ln = s // dil
    n = window // dil
    nb = -(-ln // n)
    lp = nb * n

    def strided(a):
        a = jnp.moveaxis(a.reshape((b, ln, dil) + a.shape[2:]), 2, 1)
        a = jnp.pad(a, [(0, 0), (0, 0), (0, lp - ln)] + [(0, 0)] * (a.ndim - 3))
        return a.reshape((b, dil, nb, n) + a.shape[3:])

    def with_prev(a):
        prev = jnp.pad(a, [(0, 0), (0, 0), (1, 0)] + [(0, 0)] * (a.ndim - 3))[:, :, :-1]
        return jnp.concatenate([prev, a], axis=3)

    qb = strided(q)
    kw, vw = with_prev(strided(k)), with_prev(strided(v))
    sc = jnp.einsum('brnigec,brnjgc->brngeij', qb, kw, preferred_element_type=F32) * (HEAD_DIM ** -0.5)
    ii = jnp.arange(n)[:, None]
    jj = jnp.arange(2 * n)[None, :]
    dist = n + ii - jj
    band = (dist >= 0) & (dist <= n)
    first = (jnp.arange(nb)[:, None, None] > 0) | (jj >= n)[None]
    mask = band[None] & first
    sc = jnp.where(mask[None, None, :, None, None], sc, -jnp.inf)
    m = jnp.max(sc, axis=-1, keepdims=True)
    pe = jnp.exp(sc - m)
    den = jnp.sum(pe, axis=-1, keepdims=True)
    o = jnp.einsum('brngeij,brnjgc->brnigec', pe, vw)
    o = o / jnp.moveaxis(den[..., 0], -1, 3)[..., None]
    lse = jnp.moveaxis((m + jnp.log(den))[..., 0], -1, 3)

    def unstride(a):
        rest = a.shape[4:]
        a = a.reshape((b, dil, lp) + rest)[:, :, :ln]
        return jnp.moveaxis(a, 1, 2).reshape((b, s) + rest)

    return unstride(o), unstride(lse)


def dilated_gather_sample(q, k, v, cache, window, dil):
    t = q.shape[1]
    wb = cache.shape[1]
    keys = jnp.concatenate([cache[:, :, 0].astype(k.dtype), k], axis=1)
    vals = jnp.concatenate([cache[:, :, 1].astype(v.dtype), v], axis=1)
    n_keys = window // dil + 1
    idx = wb + jnp.arange(t)[:, None] - dil * jnp.arange(n_keys)[None, :]
    valid = idx >= 0
    idx = jnp.maximum(idx, 0)
    kg, vg = keys[:, idx], vals[:, idx]
    sc = jnp.einsum('btgec,btmgc->btgem', q, kg, preferred_element_type=F32) * (HEAD_DIM ** -0.5)
    sc = jnp.where(valid[None, :, None, None, :], sc, -jnp.inf)
    m = jnp.max(sc, axis=-1, keepdims=True)
    pe = jnp.exp(sc - m)
    den = jnp.sum(pe, axis=-1, keepdims=True)
    o = jnp.einsum('btgem,btmgc->btgec', pe, vg) / den
    return o, (m + jnp.log(den))[..., 0]


def dilated_mixer(hn, k_sh, v_sh, pos, w_q, g_q_norm, w_o, caches):
    b, t, _ = hn.shape
    q = (hn @ w_q).reshape(b, t, N_GROUPS, KV_PER_GROUP, Q_REP, HEAD_DIM)
    q = partial_rope(rmsnorm(q, g_q_norm), pos)
    outs, lses = [], []
    for gi, (window, dil) in enumerate(GROUPS):
        sl = slice(gi * KV_PER_GROUP, (gi + 1) * KV_PER_GROUP)
        if caches is None:
            o, lse = dilated_band_prompt(q[:, :, gi], k_sh[:, :, sl], v_sh[:, :, sl], window, dil)
        else:
            o, lse = dilated_gather_sample(q[:, :, gi], k_sh[:, :, sl], v_sh[:, :, sl], caches[gi], window, dil)
        outs.append(o)
        lses.append(lse)
    wts = jax.nn.softmax(jnp.stack(lses, axis=2), axis=2)
    o = jnp.sum(wts[..., None] * jnp.stack(outs, axis=2), axis=2)
    return o.reshape(b, t, Q_PER_GROUP * HEAD_DIM).astype(hn.dtype) @ w_o


def run_trunk(x, p, pos, state, prm):
    is_prompt = state is None
    b, t, _ = x.shape
    dt = x.dtype
    h = x
    new_delta, new_qkv, new_ffn = [], [], []
    k_sh = v_sh = None
    new_kv = None
    for i in range(DEPTH):
        hn = rmsnorm(h, prm['g_mix_norm'][i])
        if i < N_A_LAYERS:
            if is_prompt:
                qkv_buf = jnp.zeros((b, A_CONV - 1, 3 * A_WIDTH), dt)
                s0 = jnp.zeros((b, A_HEADS, A_HEAD_DIM, A_HEAD_DIM), F32)
            else:
                s0, qkv_buf = state[0][i], state[1][i]
            mix, s1, qkv_buf1 = deltanet_mixer(hn, qkv_buf, s0, prm['w_a_in'][i], prm['w_a_conv'][i],
                                               prm['a_log'][i], prm['a_dt_bias'][i], prm['g_a_out_norm'][i],
                                               prm['w_a_out'][i], is_prompt)
            new_delta.append(s1.astype(dt))
            new_qkv.append(qkv_buf1)
        else:
            j = i - N_A_LAYERS
            mix = dilated_mixer(hn, k_sh, v_sh, pos, prm['w_q'][j], prm['g_q_norm'][j], prm['w_o'][j],
                                None if is_prompt else state[3])
        h = h + mix
        ffn_buf = jnp.zeros((b, FFN_CONV - 1, 2 * D_FF), dt) if is_prompt else state[2][i]
        f, ffn_buf1 = conv_ffn(rmsnorm(h, prm['g_ffn_norm'][i]), ffn_buf, prm['w_ffn_up'][i],
                               prm['w_ffn_conv'][i], prm['w_ffn_down'][i])
        h = h + f
        gate = jax.nn.sigmoid(rmsnorm(h, prm['g_ple_norm'][i]) @ prm['w_ple_gate'][i])
        h = h + gate * (p[i] @ prm['w_ple_proj'][i])
        new_ffn.append(ffn_buf1)
        if i == N_A_LAYERS - 1:
            k_sh, v_sh = shared_kv(h, pos, prm['g_kv_norm'], prm['w_kv'], prm['g_k_norm'])
            kv_rows = jnp.stack([k_sh, v_sh], axis=2)
            new_kv = tuple(
                kv_rows[:, t - (min(w, t) if is_prompt else t):, :, gi * KV_PER_GROUP:(gi + 1) * KV_PER_GROUP]
                for gi, (w, _) in enumerate(GROUPS))
    return h, jnp.stack(new_delta), jnp.stack(new_qkv), jnp.stack(new_ffn), new_kv


def setup_inputs(seed: int = 0) -> dict:
    key = jax.random.key(seed)
    ks = iter(jax.random.split(key, 32))

    def nrm(shape, scale):
        return jax.random.normal(next(ks), shape, F32) * scale

    def gain(shape):
        return 1.0 + 0.02 * jax.random.normal(next(ks), shape, F32)

    x_prompt = nrm((BATCH, SEQ, D_MODEL), 1.0)
    x_sample = nrm((DEC_BATCH, DEC_SEQ, D_MODEL), 1.0)
    p_prompt = nrm((DEPTH, BATCH, SEQ, PLE_DIM), 1.0)
    p_sample = nrm((DEPTH, DEC_BATCH, DEC_SEQ, PLE_DIM), 1.0)
    state_delta = nrm((N_A_LAYERS, DEC_BATCH, A_HEADS, A_HEAD_DIM, A_HEAD_DIM), 0.05)
    state_qkv_conv = nrm((N_A_LAYERS, DEC_BATCH, A_CONV - 1, 3 * A_WIDTH), 1.0)
    state_ffn_conv = nrm((DEPTH, DEC_BATCH, FFN_CONV - 1, 2 * D_FF), 1.0)
    cache_kv_w128 = nrm((DEC_BATCH, min(GROUPS[0][0], PAST_LEN), 2, KV_PER_GROUP, HEAD_DIM), 1.0)
    cache_kv_w512 = nrm((DEC_BATCH, min(GROUPS[1][0], PAST_LEN), 2, KV_PER_GROUP, HEAD_DIM), 1.0)
    cache_kv_w2048 = nrm((DEC_BATCH, min(GROUPS[2][0], PAST_LEN), 2, KV_PER_GROUP, HEAD_DIM), 1.0)
    g_mix_norm = gain((DEPTH, D_MODEL))
    g_ffn_norm = gain((DEPTH, D_MODEL))
    w_ffn_up = nrm((DEPTH, D_MODEL, 2 * D_FF), D_MODEL ** -0.5)
    w_ffn_conv = nrm((DEPTH, FFN_CONV, 2 * D_FF), FFN_CONV ** -0.5)
    w_ffn_down = nrm((DEPTH, D_FF, D_MODEL), D_FF ** -0.5)
    g_ple_norm = gain((DEPTH, D_MODEL))
    w_ple_gate = nrm((DEPTH, D_MODEL, D_MODEL), D_MODEL ** -0.5)
    w_ple_proj = nrm((DEPTH, PLE_DIM, D_MODEL), PLE_DIM ** -0.5)
    w_a_in = nrm((N_A_LAYERS, D_MODEL, A_PROJ), D_MODEL ** -0.5)
    w_a_conv = nrm((N_A_LAYERS, A_CONV, 3 * A_WIDTH), A_CONV ** -0.5)
    a_log = jnp.log(jax.random.uniform(next(ks), (N_A_LAYERS, A_HEADS), F32, 1.0, 16.0))
    dt = jnp.exp(jax.random.uniform(next(ks), (N_A_LAYERS, A_HEADS), F32, math.log(1e-3), math.log(1e-1)))
    a_dt_bias = dt + jnp.log(-jnp.expm1(-dt))
    g_a_out_norm = gain((N_A_LAYERS, A_HEAD_DIM))
    w_a_out = nrm((N_A_LAYERS, A_WIDTH, D_MODEL), A_WIDTH ** -0.5)
    g_kv_norm = gain((D_MODEL,))
    w_kv = nrm((D_MODEL, 2 * N_KV_HEADS * HEAD_DIM), D_MODEL ** -0.5)
    g_k_norm = gain((HEAD_DIM,))
    w_q = nrm((N_B_LAYERS, D_MODEL, N_Q_HEADS * HEAD_DIM), D_MODEL ** -0.5)
    g_q_norm = gain((N_B_LAYERS, HEAD_DIM))
    w_o = nrm((N_B_LAYERS, Q_PER_GROUP * HEAD_DIM, D_MODEL), (Q_PER_GROUP * HEAD_DIM) ** -0.5)
    return {'x_prompt': x_prompt, 'x_sample': x_sample, 'p_prompt': p_prompt, 'p_sample': p_sample,
            'state_delta': state_delta, 'state_qkv_conv': state_qkv_conv, 'state_ffn_conv': state_ffn_conv,
            'cache_kv_w128': cache_kv_w128, 'cache_kv_w512': cache_kv_w512, 'cache_kv_w2048': cache_kv_w2048,
            'g_mix_norm': g_mix_norm, 'g_ffn_norm': g_ffn_norm, 'w_ffn_up': w_ffn_up, 'w_ffn_conv': w_ffn_conv,
            'w_ffn_down': w_ffn_down, 'g_ple_norm': g_ple_norm, 'w_ple_gate': w_ple_gate, 'w_ple_proj': w_ple_proj,
            'w_a_in': w_a_in, 'w_a_conv': w_a_conv, 'a_log': a_log, 'a_dt_bias': a_dt_bias,
            'g_a_out_norm': g_a_out_norm, 'w_a_out': w_a_out, 'g_kv_norm': g_kv_norm, 'w_kv': w_kv,
            'g_k_norm': g_k_norm, 'w_q': w_q, 'g_q_norm': g_q_norm, 'w_o': w_o}


def reference(x_prompt, x_sample, p_prompt, p_sample, state_delta, state_qkv_conv, state_ffn_conv,
              cache_kv_w128, cache_kv_w512, cache_kv_w2048, g_mix_norm, g_ffn_norm, w_ffn_up, w_ffn_conv,
              w_ffn_down, g_ple_norm, w_ple_gate, w_ple_proj, w_a_in, w_a_conv, a_log, a_dt_bias,
              g_a_out_norm, w_a_out, g_kv_norm, w_kv, g_k_norm, w_q, g_q_norm, w_o):
    prm = dict(g_mix_norm=g_mix_norm, g_ffn_norm=g_ffn_norm, w_ffn_up=w_ffn_up, w_ffn_conv=w_ffn_conv,
               w_ffn_down=w_ffn_down, g_ple_norm=g_ple_norm, w_ple_gate=w_ple_gate, w_ple_proj=w_ple_proj,
               w_a_in=w_a_in, w_a_conv=w_a_conv, a_log=a_log, a_dt_bias=a_dt_bias,
               g_a_out_norm=g_a_out_norm, w_a_out=w_a_out, g_kv_norm=g_kv_norm, w_kv=w_kv,
               g_k_norm=g_k_norm, w_q=w_q, g_q_norm=g_q_norm, w_o=w_o)
    pos_prompt = jnp.arange(x_prompt.shape[1], dtype=jnp.int32)
    pos_sample = PAST_LEN + jnp.arange(x_sample.shape[1], dtype=jnp.int32)
    y_prompt, sd_p, sq_p, sf_p, kv_p = run_trunk(x_prompt, p_prompt, pos_prompt, None, prm)
    y_sample, sd_s, sq_s, sf_s, kv_s = run_trunk(
        x_sample, p_sample, pos_sample,
        (state_delta, state_qkv_conv, state_ffn_conv, (cache_kv_w128, cache_kv_w512, cache_kv_w2048)), prm)
    kv128_p, kv512_p, kv2048_p = kv_p
    kv128_s, kv512_s, kv2048_s = kv_s
    return (y_prompt, y_sample, sd_p, sq_p, sf_p, kv128_p, kv512_p, kv2048_p,
            sd_s, sq_s, sf_s, kv128_s, kv512_s, kv2048_s)
```

```python
import functools

import jax
import jax.numpy as jnp
from jax import lax
from jax.experimental import pallas as pl
from jax.experimental.pallas import tpu as pltpu

F32 = jnp.float32
BF16 = jnp.bfloat16

EPS = 1e-6
HEAD = 128
CHUNK = 64
GROUPS = ((128, 1), (512, 4), (2048, 16))
KV_PER_GROUP = 2
Q_REP = 2
Q_PER_GROUP = KV_PER_GROUP * Q_REP
ROPE_DIM = HEAD // 4
ROPE_THETA = 500000.0
PAST_LEN = 16384

LANES = 128
SUBLANES = 8
MXU_COLS = 256
VMEM_LIMIT_BYTES = 56 * 1024 * 1024


def _dot(a, b):
    return jnp.dot(a, b, preferred_element_type=F32)


def _dot_nt(a, b):
    return lax.dot_general(a, b, (((1,), (1,)), ((), ())), preferred_element_type=F32)


def _bf(x):
    return x.astype(BF16)


def _rms(x):
    return x * lax.rsqrt(jnp.mean(x * x, axis=-1, keepdims=True) + EPS)


def _sigmoid(x):
    return 1.0 / (1.0 + jnp.exp(-x))


def _silu(x):
    return x * _sigmoid(x)


def _split2(x):
    hi = _bf(x)
    return hi, _bf(x - hi.astype(F32))


def _split3(x):
    hi = _bf(x)
    r = x - hi.astype(F32)
    mid = _bf(r)
    return hi, mid, _bf(r - mid.astype(F32))


def _mm3(a, b):
    ah, al = _split2(a)
    bh, bl = _split2(b)
    return _dot(ah, bh) + (_dot(ah, bl) + _dot(al, bh))


def _const_spec(a):
    nd = a.ndim
    return pl.BlockSpec(a.shape, lambda *_: (0,) * nd, pipeline_mode=pl.Buffered(1))


def _params(n_axes=1):
    return pltpu.CompilerParams(dimension_semantics=("arbitrary",) * n_axes,
                                vmem_limit_bytes=VMEM_LIMIT_BYTES)


def _row_call(body, name, m, tm, row_ins, const_ins, row_outs, const_outs=(), scratch=()):
    assert m % tm == 0
    in_specs = [pl.BlockSpec((tm, a.shape[1]), lambda i: (i, 0)) for a in row_ins]
    in_specs += [_const_spec(a) for a in const_ins]
    out_shape = [jax.ShapeDtypeStruct((m, c), dt) for c, dt in row_outs]
    out_specs = [pl.BlockSpec((tm, c), lambda i: (i, 0)) for c, _ in row_outs]
    for shape, dt in const_outs:
        out_shape.append(jax.ShapeDtypeStruct(shape, dt))
        out_specs.append(pl.BlockSpec(shape, lambda i, _n=len(shape): (0,) * _n))
    return pl.pallas_call(body, out_shape=out_shape, grid=(m // tm,), in_specs=in_specs, out_specs=out_specs,
                          scratch_shapes=list(scratch), compiler_params=_params(), name=name)(*row_ins, *const_ins)


def _a_in_body(*refs, tm, n_heads, per_row_state):
    if per_row_state:
        (x_ref, g_ref, wqkv_ref, wab_ref, wz_ref, wconv_ref, alog_ref, dtb_ref, buf_ref,
         q_ref, k_ref, v_ref, z_ref, gb_ref, tail_ref) = refs
    else:
        (x_ref, g_ref, wqkv_ref, wab_ref, wz_ref, wconv_ref, alog_ref, dtb_ref,
         q_ref, k_ref, v_ref, z_ref, gb_ref, tail_ref, cbuf) = refs

        @pl.when(pl.program_id(0) == 0)
        def _():
            cbuf[0:SUBLANES, :] = jnp.zeros((SUBLANES, cbuf.shape[1]), F32)

    hn = _bf(_rms(x_ref[...]) * g_ref[...])
    outs = (q_ref, k_ref, v_ref)
    width = q_ref.shape[1]
    per_part = width // MXU_COLS
    for c in range(3 * per_part):
        cols = slice(MXU_COLS * c, MXU_COLS * (c + 1))
        pq = _dot(hn, wqkv_ref[:, cols])
        w = wconv_ref[:, cols]
        if per_row_state:
            y = buf_ref[0, :, cols] * w[0:1]
            y = y + buf_ref[1, :, cols] * w[1:2]
            y = y + buf_ref[2, :, cols] * w[2:3]
            y = y + pq * w[3:4]
            tail_ref[:, cols] = pq
        else:
            cbuf[SUBLANES:SUBLANES + tm, cols] = pq
            y = cbuf[5:5 + tm, cols] * w[0:1]
            y = y + cbuf[6:6 + tm, cols] * w[1:2]
            y = y + cbuf[7:7 + tm, cols] * w[2:3]
            y = y + pq * w[3:4]
            tail = cbuf[tm:tm + SUBLANES, cols]
            tail_ref[:, cols] = tail
            cbuf[0:SUBLANES, cols] = tail
        a = _silu(y)
        part, sub = divmod(c, per_part)
        if part < 2:
            halves = []
            for hh in range(MXU_COLS // HEAD):
                ah = a[:, HEAD * hh:HEAD * (hh + 1)]
                nrm = ah * lax.rsqrt(jnp.sum(ah * ah, axis=-1, keepdims=True) + EPS)
                if part == 0:
                    nrm = nrm * (HEAD ** -0.5)
                halves.append(nrm)
            a = jnp.concatenate(halves, axis=1)
        outs[part][:, MXU_COLS * sub:MXU_COLS * (sub + 1)] = a

    pab = _dot(hn, wab_ref[...])
    lane = lax.broadcasted_iota(jnp.int32, pab.shape, 1)
    ap = pab + dtb_ref[...]
    softplus = jnp.maximum(ap, 0.0) + jnp.log1p(jnp.exp(-jnp.abs(ap)))
    g = -jnp.exp(alog_ref[...]) * softplus
    gb_ref[...] = jnp.where(lane < n_heads, g, _sigmoid(pab))
    z_ref[...] = _dot(hn, wz_ref[...])


def _delta_prompt_body(q_ref, k_ref, v_ref, gb_ref, o_ref, sout_ref,
                       s_scr, u0_s, wkqg_s, qk_s, kdt_s, gl_s, *, n_heads, n_chunks):
    @pl.when(pl.program_id(0) == 0)
    def _():
        s_scr[...] = jnp.zeros(s_scr.shape, F32)

    ri = lax.broadcasted_iota(jnp.int32, (CHUNK, CHUNK), 0)
    ci = lax.broadcasted_iota(jnp.int32, (CHUNK, CHUNK), 1)
    incl = ri >= ci
    strict = ri > ci
    eye = jnp.where(ri == ci, 1.0, 0.0).astype(F32)
    ltri = _bf(jnp.where(incl, 1.0, 0.0))
    sel_r = lax.broadcasted_iota(jnp.int32, (n_heads * CHUNK, LANES), 0)
    sel_c = lax.broadcasted_iota(jnp.int32, (n_heads * CHUNK, LANES), 1)
    sel = _bf(jnp.where((sel_r >= CHUNK * sel_c) & (sel_r < CHUNK * (sel_c + 1)), 1.0, 0.0))
    er = lax.broadcasted_iota(jnp.int32, (HEAD, HEAD), 0)
    ec = lax.broadcasted_iota(jnp.int32, (HEAD, HEAD), 1)
    eye_h = _bf(jnp.where(er == ec, 1.0, 0.0))

    def prep(c, carry):
        rows = pl.ds(pl.multiple_of(c * CHUNK, CHUNK), CHUNK)
        gbc = gb_ref[rows, :]
        g3 = _split3(gbc)
        gc = _dot(ltri, g3[0]) + (_dot(ltri, g3[1]) + _dot(ltri, g3[2]))
        c3 = _split3(gc)
        grow_all = _dot_nt(sel, c3[0]) + (_dot_nt(sel, c3[1]) + _dot_nt(sel, c3[2]))
        glast = gc[CHUNK - 1:CHUNK, :]
        gl_s[c] = jnp.broadcast_to(glast, (SUBLANES, LANES))
        for h in range(n_heads):
            hs = slice(HEAD * h, HEAD * (h + 1))
            gcol = gc[:, h:h + 1]
            bcol = gbc[:, n_heads + h:n_heads + h + 1]
            grow = grow_all[CHUNK * h:CHUNK * (h + 1), :]
            dec = jnp.exp(jnp.where(incl, gcol - grow, -jnp.inf))
            qh = q_ref[rows, hs]
            kh = k_ref[rows, hs]
            vh = v_ref[rows, hs]
            kb = _bf(kh)
            a = jnp.where(strict, bcol * _dot_nt(kb, kb) * dec, 0.0)
            b = -a
            tinv = eye + b
            for _ in range(5):
                b = _mm3(b, b)
                tinv = tinv + _mm3(tinv, b)
            gam = jnp.exp(gcol)
            sol = _mm3(tinv, jnp.concatenate([vh * bcol, kh * (bcol * gam)], axis=1))
            qk = _dot_nt(_bf(qh), kb) * dec
            kd = kh * jnp.exp(glast[:, h:h + 1] - gcol)
            u0_s[c, h] = sol[:, :HEAD]
            wkqg_s[c, h] = _bf(jnp.concatenate([sol[:, HEAD:], qh * gam], axis=0))
            qk_s[c, h] = _bf(qk)
            kdt_s[c, h] = _bf(_dot_nt(eye_h, _bf(kd)))
        return carry

    lax.fori_loop(0, n_chunks, prep, 0)

    def scan(c, carry):
        rows = pl.ds(pl.multiple_of(c * CHUNK, CHUNK), CHUNK)
        glr = gl_s[c]
        for h in range(n_heads):
            s = s_scr[h]
            ws = _dot(wkqg_s[c, h], _bf(s))
            ub = _bf(u0_s[c, h] - ws[:CHUNK])
            o_ref[rows, HEAD * h:HEAD * (h + 1)] = ws[CHUNK:] + _dot(qk_s[c, h], ub)
            s_scr[h] = jnp.exp(glr[0:1, h:h + 1]) * s + _dot(kdt_s[c, h], ub)
        return carry

    lax.fori_loop(0, n_chunks, scan, 0)
    sout_ref[...] = s_scr[...]


def _delta_prompt(q, k, v, gb, n_heads):
    t = q.shape[0]
    n_chunks = 8
    tt = n_chunks * CHUNK
    scratch = [pltpu.VMEM((n_heads, HEAD, HEAD), F32),
               pltpu.VMEM((n_chunks, n_heads, CHUNK, HEAD), F32),
               pltpu.VMEM((n_chunks, n_heads, 2 * CHUNK, HEAD), BF16),
               pltpu.VMEM((n_chunks, n_heads, CHUNK, CHUNK), BF16),
               pltpu.VMEM((n_chunks, n_heads, HEAD, CHUNK), BF16),
               pltpu.VMEM((n_chunks, SUBLANES, LANES), F32)]
    return _row_call(functools.partial(_delta_prompt_body, n_heads=n_heads, n_chunks=n_chunks), "delta_prompt",
                     t, tt, [q, k, v, gb], [], [(q.shape[1], F32)],
                     const_outs=[((n_heads, HEAD, HEAD), F32)], scratch=scratch)


def _delta_step_body(qt_ref, kt_ref, v_ref, gbt_ref, s_ref, o_ref, so_ref, *, n_seq, n_heads):
    h = pl.program_id(0)
    decay = jnp.exp(gbt_ref[pl.ds(h, 1), :])
    beta = gbt_ref[pl.ds(n_heads + h, 1), :]
    for b in range(n_seq):
        s = decay[:, b:b + 1] * s_ref[b]
        kc = kt_ref[:, b:b + 1]
        u = beta[:, b:b + 1] * (v_ref[b:b + 1, :] - jnp.sum(kc * s, axis=0, keepdims=True))
        s = s + kc * u
        so_ref[b] = s
        o_ref[b:b + 1, :] = jnp.sum(qt_ref[:, b:b + 1] * s, axis=0, keepdims=True)


def _delta_step(q, k, v, gb, state, n_heads):
    n_seq = q.shape[0]
    col = pl.BlockSpec((HEAD, n_seq), lambda h: (h, 0))
    row = pl.BlockSpec((n_seq, HEAD), lambda h: (0, h))
    st = pl.BlockSpec((n_seq, None, HEAD, HEAD), lambda h: (0, h, 0, 0))
    gbt = gb.T
    return pl.pallas_call(
        functools.partial(_delta_step_body, n_seq=n_seq, n_heads=n_heads),
        out_shape=[jax.ShapeDtypeStruct(q.shape, F32), jax.ShapeDtypeStruct(state.shape, F32)],
        grid=(n_heads,), in_specs=[col, col, row, _const_spec(gbt), st], out_specs=[row, st],
        compiler_params=_params(), name="delta_step")(q.T, k.T, v, gbt, state)


def _a_out_body(o_ref, z_ref, res_ref, gout_ref, w_ref, out_ref):
    parts = []
    for h in range(o_ref.shape[1] // HEAD):
        hs = slice(HEAD * h, HEAD * (h + 1))
        parts.append(_bf(_rms(o_ref[:, hs]) * gout_ref[...] * _silu(z_ref[:, hs])))
    out_ref[...] = res_ref[...] + _dot(jnp.concatenate(parts, axis=1), w_ref[...])


def _ffn_body(*refs, tm, per_row_state):
    if per_row_state:
        (h_ref, p_ref, buf_ref, gffn_ref, wup_ref, wconv_ref, wdown_ref, gple_ref, wgate_ref, wproj_ref,
         out_ref, tail_ref, act_s) = refs
    else:
        (h_ref, p_ref, gffn_ref, wup_ref, wconv_ref, wdown_ref, gple_ref, wgate_ref, wproj_ref,
         out_ref, tail_ref, ubuf, act_s) = refs

        @pl.when(pl.program_id(0) == 0)
        def _():
            ubuf[0:SUBLANES, :] = jnp.zeros((SUBLANES, ubuf.shape[1]), F32)

    h = h_ref[...]
    hn = _bf(_rms(h) * gffn_ref[...])
    dff = wdown_ref.shape[0]
    for c in range(dff // MXU_COLS):
        ys = []
        for half in range(2):
            lo = half * dff + MXU_COLS * c
            cols = slice(lo, lo + MXU_COLS)
            u = _dot(hn, wup_ref[:, cols])
            w = wconv_ref[:, cols]
            if per_row_state:
                y = buf_ref[:, cols] * w[0:1]
                y = y + buf_ref[:, 2 * dff + lo:2 * dff + lo + MXU_COLS] * w[1:2]
                y = y + u * w[2:3]
                tail_ref[:, cols] = u
            else:
                ubuf[SUBLANES:SUBLANES + tm, cols] = u
                y = ubuf[6:6 + tm, cols] * w[0:1]
                y = y + ubuf[7:7 + tm, cols] * w[1:2]
                y = y + u * w[2:3]
                tail = ubuf[tm:tm + SUBLANES, cols]
                tail_ref[:, cols] = tail
                ubuf[0:SUBLANES, cols] = tail
            ys.append(y)
        act_s[:, MXU_COLS * c:MXU_COLS * (c + 1)] = _bf(_silu(ys[0]) * ys[1])
    h2 = h + _dot(act_s[...], wdown_ref[...])
    gate = _sigmoid(_dot(_bf(_rms(h2) * gple_ref[...]), wgate_ref[...]))
    out_ref[...] = h2 + gate * _dot(_bf(p_ref[...]), wproj_ref[...])


def _ffn(h, p, w, layer, tm, state):
    m = h.shape[0]
    per_row = state is not None
    dff = w["w_down"][layer].shape[0]
    consts = [w["g_ffn"][layer], w["w_up"][layer], w["w_fconv"][layer], w["w_down"][layer],
              w["g_ple"][layer], w["w_gate"][layer], w["w_proj"][layer]]
    rows = [h, p] + ([state.reshape(m, -1)] if per_row else [])
    scratch = ([] if per_row else [pltpu.VMEM((tm + SUBLANES, 2 * dff), F32)]) + [pltpu.VMEM((tm, dff), BF16)]
    out, tail = _row_call(functools.partial(_ffn_body, tm=tm, per_row_state=per_row), f"ffn{layer}", m, tm,
                          rows, consts, [(h.shape[1], F32)],
                          const_outs=[((m if per_row else SUBLANES, 2 * dff), F32)], scratch=scratch)
    return out, tail


def _kvq_body(h_ref, cos_ref, sa_ref, sb_ref, gkv_ref, gmix_ref, wkv_ref, wq_ref, gk_ref, gq_ref,
              q_ref, k_ref, v_ref):
    h = h_ref[...]
    base = _rms(h)
    hkv = _bf(base * gkv_ref[...])
    hq = _bf(base * gmix_ref[...])
    cos, sa, sb = cos_ref[...], sa_ref[...], sb_ref[...]

    def norm_rope(x, g):
        y = _rms(x) * g
        return y * cos + pltpu.roll(y, ROPE_DIM // 2, 1) * sa + pltpu.roll(y, HEAD - ROPE_DIM // 2, 1) * sb

    def heads(src, w_ref, lo, n_cols, g, dst_ref):
        for c in range(n_cols // MXU_COLS):
            x = _dot(src, w_ref[:, lo + MXU_COLS * c:lo + MXU_COLS * (c + 1)])
            dst_ref[:, MXU_COLS * c:MXU_COLS * (c + 1)] = jnp.concatenate(
                [norm_rope(x[:, HEAD * j:HEAD * (j + 1)], g) for j in range(MXU_COLS // HEAD)], axis=1)

    n_kv = k_ref.shape[1]
    heads(hkv, wkv_ref, 0, n_kv, gk_ref[...], k_ref)
    v_ref[...] = _dot(hkv, wkv_ref[:, n_kv:])
    heads(hq, wq_ref, 0, q_ref.shape[1], gq_ref[...], q_ref)


def _rope_tables(pos):
    half = ROPE_DIM // 2
    inv_freq = ROPE_THETA ** (-jnp.arange(half, dtype=F32) * 2.0 / ROPE_DIM)
    ang = pos.astype(F32)[:, None] * inv_freq[None, :]
    cos, sin = jnp.cos(ang), jnp.sin(ang)
    m = pos.shape[0]
    zero = jnp.zeros((m, half), F32)
    rest0 = jnp.zeros((m, HEAD - ROPE_DIM), F32)
    return (jnp.concatenate([cos, cos, jnp.ones((m, HEAD - ROPE_DIM), F32)], axis=1),
            jnp.concatenate([zero, sin, rest0], axis=1),
            jnp.concatenate([-sin, zero, rest0], axis=1))


def _attn_prompt_body(q_ref, kc_ref, kp_ref, vc_ref, vp_ref, o_ref, lse_ref, *, n, blocks):
    first_lo = jnp.where(pl.program_id(1) > 0, 0, n)
    qi = lax.broadcasted_iota(jnp.int32, (Q_REP * n, 2 * n), 0) & (n - 1)
    kj = lax.broadcasted_iota(jnp.int32, (Q_REP * n, 2 * n), 1)
    band = (kj >= qi) & (kj <= qi + n)
    lane = lax.broadcasted_iota(jnp.int32, (n, LANES), 1)
    for j in range(blocks):
        rows = slice(n * j, n * (j + 1))
        lse_tile = jnp.zeros((n, LANES), F32)
        for g in range(KV_PER_GROUP):
            hs = slice(HEAD * g, HEAD * (g + 1))
            if j == 0:
                kprev, vprev = kp_ref[:, hs], vp_ref[:, hs]
                valid = band & (kj >= first_lo)
            else:
                prev = slice(n * (j - 1), n * j)
                kprev, vprev = kc_ref[prev, hs], vc_ref[prev, hs]
                valid = band
            kwin = _bf(jnp.concatenate([kprev, kc_ref[rows, hs]], axis=0))
            vwin = _bf(jnp.concatenate([vprev, vc_ref[rows, hs]], axis=0))
            q0 = Q_REP * HEAD * g
            qg = _bf(jnp.concatenate([q_ref[rows, q0 + HEAD * e:q0 + HEAD * (e + 1)] for e in range(Q_REP)], axis=0))
            s = jnp.where(valid, _dot_nt(qg, kwin) * (HEAD ** -0.5), -jnp.inf)
            mx = jnp.max(s, axis=-1, keepdims=True)
            p = jnp.exp(s - mx)
            den = jnp.sum(p, axis=-1, keepdims=True)
            o = _dot(_bf(p), vwin) / den
            lse = mx + jnp.log(den)
            for e in range(Q_REP):
                o_ref[rows, q0 + HEAD * e:q0 + HEAD * (e + 1)] = o[n * e:n * (e + 1)]
                lse_tile = jnp.where(lane == Q_REP * g + e, lse[n * e:n * (e + 1)], lse_tile)
        lse_ref[rows, :] = lse_tile


def _attn_prompt(q, k, v, gi, window, dil):
    t = q.shape[0]
    n = window // dil
    ln = t // dil
    blocks = min(8, ln // n)
    rows = n * blocks
    assert n == HEAD and ln % rows == 0
    qw, kw = Q_PER_GROUP * HEAD, KV_PER_GROUP * HEAD
    n_groups = len(GROUPS)
    cur = lambda r, b: (b, n_groups * r + gi)
    prev = lambda r, b: (jnp.maximum(b * blocks - 1, 0), n_groups * r + gi)
    out = lambda r, b: (b, r)
    o, lse = pl.pallas_call(
        functools.partial(_attn_prompt_body, n=n, blocks=blocks),
        out_shape=[jax.ShapeDtypeStruct((ln, dil * qw), F32), jax.ShapeDtypeStruct((ln, dil * LANES), F32)],
        grid=(dil, ln // rows),
        in_specs=[pl.BlockSpec((rows, qw), cur), pl.BlockSpec((rows, kw), cur), pl.BlockSpec((n, kw), prev),
                  pl.BlockSpec((rows, kw), cur), pl.BlockSpec((n, kw), prev)],
        out_specs=[pl.BlockSpec((rows, qw), out), pl.BlockSpec((rows, LANES), out)],
        compiler_params=_params(2), name=f"attn_prompt{gi}",
    )(q.reshape(ln, -1), k.reshape(ln, -1), k.reshape(ln, -1), v.reshape(ln, -1), v.reshape(ln, -1))
    return o.reshape(t, qw), lse.reshape(t, LANES)


def _attn_sample_body(q_ref, kn_ref, vn_ref, *refs, bb):
    n_groups = len(GROUPS)
    caches, outs, lses = refs[:n_groups], refs[n_groups:2 * n_groups], refs[2 * n_groups:]
    lane = lax.broadcasted_iota(jnp.int32, (1, LANES), 1)
    kw = KV_PER_GROUP * HEAD
    for gi in range(n_groups):
        c_ref, o_ref, l_ref = caches[gi], outs[gi], lses[gi]
        for b in range(bb):
            lrow = jnp.zeros((1, LANES), F32)
            for g in range(KV_PER_GROUP):
                kc = c_ref[b, :, HEAD * g:HEAD * (g + 1)]
                vc = c_ref[b, :, kw + HEAD * g:kw + HEAD * (g + 1)]
                kcol = kw * gi + HEAD * g
                kn = kn_ref[b:b + 1, kcol:kcol + HEAD]
                vn = vn_ref[b:b + 1, kcol:kcol + HEAD]
                for e in range(Q_REP):
                    oc = HEAD * (Q_REP * g + e)
                    qv = q_ref[b:b + 1, Q_PER_GROUP * HEAD * gi + oc:Q_PER_GROUP * HEAD * gi + oc + HEAD]
                    s = jnp.sum(kc * qv, axis=1, keepdims=True) * (HEAD ** -0.5)
                    sn = jnp.sum(kn * qv, axis=1, keepdims=True) * (HEAD ** -0.5)
                    mx = jnp.maximum(jnp.max(s, axis=0, keepdims=True), sn)
                    p = jnp.exp(s - mx)
                    pn = jnp.exp(sn - mx)
                    den = jnp.sum(p, axis=0, keepdims=True) + pn
                    o_ref[b:b + 1, oc:oc + HEAD] = (jnp.sum(p * vc, axis=0, keepdims=True) + pn * vn) / den
                    lrow = jnp.where(lane == Q_REP * g + e, mx + jnp.log(den), lrow)
            l_ref[b:b + 1, :] = lrow


def _attn_sample(q, k, v, caches):
    n_seq = q.shape[0]
    bb = SUBLANES
    assert n_seq % bb == 0
    qw, kw = Q_PER_GROUP * HEAD, KV_PER_GROUP * HEAD
    views, cache_specs = [], []
    for cache, (window, dil) in zip(caches, GROUPS):
        n = window // dil
        assert cache.shape[1] == window and n == HEAD
        views.append(cache.reshape(n_seq, n, dil * 2 * kw))
        cache_specs.append(pl.BlockSpec((bb, n, 2 * kw), lambda i: (i, 0, 0)))
    row = lambda c: pl.BlockSpec((bb, c), lambda i: (i, 0))
    n_groups = len(GROUPS)
    res = pl.pallas_call(
        functools.partial(_attn_sample_body, bb=bb),
        out_shape=[jax.ShapeDtypeStruct((n_seq, qw), F32)] * n_groups
        + [jax.ShapeDtypeStruct((n_seq, LANES), F32)] * n_groups,
        grid=(n_seq // bb,),
        in_specs=[row(q.shape[1]), row(k.shape[1]), row(v.shape[1])] + cache_specs,
        out_specs=[row(qw)] * n_groups + [row(LANES)] * n_groups,
        compiler_params=_params(), name="attn_sample")(q, k, v, *views)
    return res[:n_groups], res[n_groups:]


def _o_proj_body(*refs):
    n_groups = len(GROUPS)
    o_refs, l_refs = refs[:n_groups], refs[n_groups:2 * n_groups]
    res_ref, w_ref, out_ref = refs[2 * n_groups:]
    lses = [r[...] for r in l_refs]
    parts = []
    for hh in range(Q_PER_GROUP):
        hs = slice(HEAD * hh, HEAD * (hh + 1))
        lh = [l[:, hh:hh + 1] for l in lses]
        mx = functools.reduce(jnp.maximum, lh)
        ex = [jnp.exp(x - mx) for x in lh]
        den = functools.reduce(lambda a, b: a + b, ex)
        acc = (ex[0] / den) * o_refs[0][:, hs]
        for gi in range(1, n_groups):
            acc = acc + (ex[gi] / den) * o_refs[gi][:, hs]
        parts.append(_bf(acc))
    out_ref[...] = res_ref[...] + _dot(jnp.concatenate(parts, axis=1), w_ref[...])


def _trunk(x, p, pos, state, w, tm):
    m, d = x.shape
    per_row = state is not None
    n_heads = w["n_heads"]
    aw = w["w_z"].shape[1]

    consts = [w["g_mix"][0], w["w_qkv"], w["w_ab"], w["w_z"], w["w_aconv"], w["a_log"], w["dt_bias"]]
    if per_row:
        consts.append(jnp.swapaxes(state["qkv_conv"], 0, 1))
    q, k, v, z, gb, qkv_tail = _row_call(
        functools.partial(_a_in_body, tm=tm, n_heads=n_heads, per_row_state=per_row), "a_in", m, tm, [x], consts,
        [(aw, F32)] * 4 + [(LANES, F32)], const_outs=[((m if per_row else SUBLANES, 3 * aw), F32)],
        scratch=[] if per_row else [pltpu.VMEM((tm + SUBLANES, 3 * aw), F32)])
    if per_row:
        o, s1 = _delta_step(q, k, v, gb, state["delta"], n_heads)
    else:
        o, s1 = _delta_prompt(q, k, v, gb, n_heads)
    (h,) = _row_call(_a_out_body, "a_out", m, tm, [o, z, x], [w["g_aout"], w["w_aout"]], [(d, F32)])
    h, ffn_tail0 = _ffn(h, p[0], w, 0, tm, state["ffn_conv"][0] if per_row else None)

    cos, sa, sb = _rope_tables(pos)
    qa, ka, va = _row_call(_kvq_body, "kvq", m, tm, [h, cos, sa, sb],
                           [w["g_kv"], w["g_mix"][1], w["w_kv"], w["w_q"], w["g_k"], w["g_q"]],
                           [(w["w_q"].shape[1], F32), (w["w_kv"].shape[1] // 2, F32), (w["w_kv"].shape[1] // 2, F32)])

    if per_row:
        outs, lses = _attn_sample(qa, ka, va, state["caches"])
    else:
        res = [_attn_prompt(qa, ka, va, gi, window, dil) for gi, (window, dil) in enumerate(GROUPS)]
        outs, lses = [r[0] for r in res], [r[1] for r in res]
    (h,) = _row_call(_o_proj_body, "o_proj", m, tm, list(outs) + list(lses) + [h], [w["w_o"]], [(d, F32)])
    h, ffn_tail1 = _ffn(h, p[1], w, 1, tm, state["ffn_conv"][1] if per_row else None)
    return h, s1, qkv_tail, (ffn_tail0, ffn_tail1), ka, va


def _kv_rows(k, v, gi, n_rows):
    kw = KV_PER_GROUP * HEAD
    sl = slice(kw * gi, kw * (gi + 1))
    shape = (n_rows, 1, KV_PER_GROUP, HEAD)
    return jnp.concatenate([k[k.shape[0] - n_rows:, sl].reshape(shape), v[v.shape[0] - n_rows:, sl].reshape(shape)],
                           axis=1)


def kernel(x_prompt, x_sample, p_prompt, p_sample, state_delta, state_qkv_conv, state_ffn_conv, cache_kv_w128, cache_kv_w512, cache_kv_w2048, g_mix_norm, g_ffn_norm, w_ffn_up, w_ffn_conv, w_ffn_down, g_ple_norm, w_ple_gate, w_ple_proj, w_a_in, w_a_conv, a_log, a_dt_bias, g_a_out_norm, w_a_out, g_kv_norm, w_kv, g_k_norm, w_q, g_q_norm, w_o):
    n_heads = a_log.shape[1]
    aw = n_heads * HEAD
    assert w_a_in.shape[0] == 1 and w_q.shape[0] == 1 and w_a_in.shape[2] == 4 * aw + 2 * n_heads
    row = lambda a: a.reshape(1, -1)
    lane_pad = lambda a: jnp.pad(a, ((0, 0), (0, LANES - a.shape[1])))
    w = {
        "n_heads": n_heads,
        "g_mix": [row(g) for g in g_mix_norm], "g_ffn": [row(g) for g in g_ffn_norm],
        "g_ple": [row(g) for g in g_ple_norm],
        "w_up": _bf(w_ffn_up), "w_fconv": w_ffn_conv, "w_down": _bf(w_ffn_down),
        "w_gate": _bf(w_ple_gate), "w_proj": _bf(w_ple_proj),
        "w_qkv": _bf(w_a_in[0][:, :3 * aw]), "w_ab": _bf(lane_pad(w_a_in[0][:, 3 * aw:3 * aw + 2 * n_heads])),
        "w_z": _bf(w_a_in[0][:, 3 * aw + 2 * n_heads:]), "w_aconv": w_a_conv[0],
        "a_log": lane_pad(a_log), "dt_bias": lane_pad(a_dt_bias),
        "g_aout": row(g_a_out_norm[0]), "w_aout": _bf(w_a_out[0]),
        "g_kv": row(g_kv_norm), "w_kv": _bf(w_kv), "g_k": row(g_k_norm),
        "w_q": _bf(w_q[0]), "g_q": row(g_q_norm[0]), "w_o": _bf(w_o[0]),
    }
    assert x_prompt.shape[0] == 1 and x_sample.shape[1] == 1
    seq, d = x_prompt.shape[1], x_prompt.shape[2]
    n_seq = x_sample.shape[0]

    yp, sd_p, qkv_tail_p, ffn_tails_p, kp, vp = _trunk(
        x_prompt.reshape(seq, d), p_prompt[:, 0], jnp.arange(seq, dtype=jnp.int32), None, w, tm=256)
    a_keep, f_keep = state_qkv_conv.shape[2], state_ffn_conv.shape[2]
    sq_p = qkv_tail_p[SUBLANES - a_keep:][None, None]
    sf_p = jnp.stack([t[SUBLANES - f_keep:] for t in ffn_tails_p])[:, None]
    kv_p = [_kv_rows(kp, vp, gi, min(window, seq))[None] for gi, (window, _) in enumerate(GROUPS)]

    state = {"delta": state_delta[0], "qkv_conv": state_qkv_conv[0], "ffn_conv": state_ffn_conv,
             "caches": (cache_kv_w128, cache_kv_w512, cache_kv_w2048)}
    ys, sd_s, qkv_tail_s, ffn_tails_s, ks, vs = _trunk(
        x_sample.reshape(n_seq, d), p_sample[:, :, 0], jnp.full((n_seq,), PAST_LEN, jnp.int32), state, w, tm=n_seq)
    sq_s = jnp.concatenate([state_qkv_conv[0][:, 1:], qkv_tail_s[:, None]], axis=1)[None]
    sf_s = jnp.stack([jnp.concatenate([state_ffn_conv[i][:, 1:], ffn_tails_s[i][:, None]], axis=1)
                      for i in range(2)])
    kv_s = [_kv_rows(ks, vs, gi, n_seq).reshape(n_seq, 1, 2, KV_PER_GROUP, HEAD) for gi in range(len(GROUPS))]

    return (yp[None], ys[:, None], sd_p[None, None], sq_p, sf_p, kv_p[0], kv_p[1], kv_p[2],
            sd_s[None], sq_s, sf_s, kv_s[0], kv_s[1], kv_s[2])
```

```python
import functools

import jax
import jax.numpy as jnp
from jax import lax
from jax.experimental import pallas as pl
from jax.experimental.pallas import tpu as pltpu

F32 = jnp.float32
BF16 = jnp.bfloat16

EPS = 1e-6
HEAD = 128
CHUNK = 64
GROUPS = ((128, 1), (512, 4), (2048, 16))
KV_PER_GROUP = 2
Q_REP = 2
Q_PER_GROUP = KV_PER_GROUP * Q_REP
ROPE_DIM = HEAD // 4
ROPE_THETA = 500000.0
PAST_LEN = 16384

LANES = 128
SUBLANES = 8
MXU_COLS = 256
VMEM_LIMIT_BYTES = 56 * 1024 * 1024


def _dot(a, b):
    return jnp.dot(a, b, preferred_element_type=F32)


def _dot_nt(a, b):
    return lax.dot_general(a, b, (((1,), (1,)), ((), ())), preferred_element_type=F32)


def _bf(x):
    return x.astype(BF16)


def _rms(x):
    return x * lax.rsqrt(jnp.mean(x * x, axis=-1, keepdims=True) + EPS)


def _sigmoid(x):
    return 1.0 / (1.0 + jnp.exp(-x))


def _silu(x):
    return x * _sigmoid(x)


def _split2(x):
    hi = _bf(x)
    return hi, _bf(x - hi.astype(F32))


def _split3(x):
    hi = _bf(x)
    r = x - hi.astype(F32)
    mid = _bf(r)
    return hi, mid, _bf(r - mid.astype(F32))


def _mm3(a, b):
    ah, al = _split2(a)
    bh, bl = _split2(b)
    return _dot(ah, bh) + (_dot(ah, bl) + _dot(al, bh))


def _const_spec(a):
    nd = a.ndim
    return pl.BlockSpec(a.shape, lambda *_: (0,) * nd, pipeline_mode=pl.Buffered(1))


def _params(n_axes=1):
    return pltpu.CompilerParams(dimension_semantics=("arbitrary",) * n_axes,
                                vmem_limit_bytes=VMEM_LIMIT_BYTES)


def _row_call(body, name, m, tm, row_ins, const_ins, row_outs, const_outs=(), scratch=()):
    assert m % tm == 0
    in_specs = [pl.BlockSpec((tm, a.shape[1]), lambda i: (i, 0)) for a in row_ins]
    in_specs += [_const_spec(a) for a in const_ins]
    out_shape = [jax.ShapeDtypeStruct((m, c), dt) for c, dt in row_outs]
    out_specs = [pl.BlockSpec((tm, c), lambda i: (i, 0)) for c, _ in row_outs]
    for shape, dt in const_outs:
        out_shape.append(jax.ShapeDtypeStruct(shape, dt))
        out_specs.append(pl.BlockSpec(shape, lambda i, _n=len(shape): (0,) * _n))
    return pl.pallas_call(body, out_shape=out_shape, grid=(m // tm,), in_specs=in_specs, out_specs=out_specs,
                          scratch_shapes=list(scratch), compiler_params=_params(), name=name)(*row_ins, *const_ins)


def _a_in_body(*refs, tm, n_heads, per_row_state):
    if per_row_state:
        (x_ref, g_ref, wqkv_ref, wab_ref, wz_ref, wconv_ref, alog_ref, dtb_ref, buf_ref,
         q_ref, k_ref, v_ref, z_ref, gb_ref, tail_ref) = refs
    else:
        (x_ref, g_ref, wqkv_ref, wab_ref, wz_ref, wconv_ref, alog_ref, dtb_ref,
         q_ref, k_ref, v_ref, z_ref, gb_ref, tail_ref, cbuf) = refs

        @pl.when(pl.program_id(0) == 0)
        def _():
            cbuf[0:SUBLANES, :] = jnp.zeros((SUBLANES, cbuf.shape[1]), F32)

    hn = _bf(_rms(x_ref[...]) * g_ref[...])
    outs = (q_ref, k_ref, v_ref)
    width = q_ref.shape[1]
    per_part = width // MXU_COLS
    for c in range(3 * per_part):
        cols = slice(MXU_COLS * c, MXU_COLS * (c + 1))
        pq = _dot(hn, wqkv_ref[:, cols])
        w = wconv_ref[:, cols]
        if per_row_state:
            y = buf_ref[0, :, cols] * w[0:1]
            y = y + buf_ref[1, :, cols] * w[1:2]
            y = y + buf_ref[2, :, cols] * w[2:3]
            y = y + pq * w[3:4]
            tail_ref[:, cols] = pq
        else:
            cbuf[SUBLANES:SUBLANES + tm, cols] = pq
            y = cbuf[5:5 + tm, cols] * w[0:1]
            y = y + cbuf[6:6 + tm, cols] * w[1:2]
            y = y + cbuf[7:7 + tm, cols] * w[2:3]
            y = y + pq * w[3:4]
            tail = cbuf[tm:tm + SUBLANES, cols]
            tail_ref[:, cols] = tail
            cbuf[0:SUBLANES, cols] = tail
        a = _silu(y)
        part, sub = divmod(c, per_part)
        if part < 2:
            halves = []
            for hh in range(MXU_COLS // HEAD):
                ah = a[:, HEAD * hh:HEAD * (hh + 1)]
                nrm = ah * lax.rsqrt(jnp.sum(ah * ah, axis=-1, keepdims=True) + EPS)
                if part == 0:
                    nrm = nrm * (HEAD ** -0.5)
                halves.append(nrm)
            a = jnp.concatenate(halves, axis=1)
        outs[part][:, MXU_COLS * sub:MXU_COLS * (sub + 1)] = a

    pab = _dot(hn, wab_ref[...])
    lane = lax.broadcasted_iota(jnp.int32, pab.shape, 1)
    ap = pab + dtb_ref[...]
    softplus = jnp.maximum(ap, 0.0) + jnp.log1p(jnp.exp(-jnp.abs(ap)))
    g = -jnp.exp(alog_ref[...]) * softplus
    gb_ref[...] = jnp.where(lane < n_heads, g, _sigmoid(pab))
    z_ref[...] = _dot(hn, wz_ref[...])


def _delta_prompt_body(q_ref, k_ref, v_ref, gb_ref, o_ref, sout_ref,
                       s_scr, u0_s, wkqg_s, qk_s, kdt_s, gl_s, *, n_heads, n_chunks):
    @pl.when(pl.program_id(0) == 0)
    def _():
        s_scr[...] = jnp.zeros(s_scr.shape, F32)

    ri = lax.broadcasted_iota(jnp.int32, (CHUNK, CHUNK), 0)
    ci = lax.broadcasted_iota(jnp.int32, (CHUNK, CHUNK), 1)
    incl = ri >= ci
    strict = ri > ci
    eye = jnp.where(ri == ci, 1.0, 0.0).astype(F32)
    ltri = _bf(jnp.where(incl, 1.0, 0.0))
    sel_r = lax.broadcasted_iota(jnp.int32, (n_heads * CHUNK, LANES), 0)
    sel_c = lax.broadcasted_iota(jnp.int32, (n_heads * CHUNK, LANES), 1)
    sel = _bf(jnp.where((sel_r >= CHUNK * sel_c) & (sel_r < CHUNK * (sel_c + 1)), 1.0, 0.0))
    er = lax.broadcasted_iota(jnp.int32, (HEAD, HEAD), 0)
    ec = lax.broadcasted_iota(jnp.int32, (HEAD, HEAD), 1)
    eye_h = _bf(jnp.where(er == ec, 1.0, 0.0))

    def prep(c, carry):
        rows = pl.ds(pl.multiple_of(c * CHUNK, CHUNK), CHUNK)
        gbc = gb_ref[rows, :]
        g3 = _split3(gbc)
        gc = _dot(ltri, g3[0]) + (_dot(ltri, g3[1]) + _dot(ltri, g3[2]))
        c3 = _split3(gc)
        grow_all = _dot_nt(sel, c3[0]) + (_dot_nt(sel, c3[1]) + _dot_nt(sel, c3[2]))
        glast = gc[CHUNK - 1:CHUNK, :]
        gl_s[c] = jnp.broadcast_to(glast, (SUBLANES, LANES))
        heads = range(n_heads)
        hs = [slice(HEAD * h, HEAD * (h + 1)) for h in heads]
        gcol = [gc[:, h:h + 1] for h in heads]
        bcol = [gbc[:, n_heads + h:n_heads + h + 1] for h in heads]
        dec = [jnp.exp(jnp.where(incl, gcol[h] - grow_all[CHUNK * h:CHUNK * (h + 1), :], -jnp.inf)) for h in heads]
        kh = [k_ref[rows, hs[h]] for h in heads]
        kb = [_bf(x) for x in kh]
        a = [jnp.where(strict, bcol[h] * _dot_nt(kb[h], kb[h]) * dec[h], 0.0) for h in heads]
        pw = [_bf(-x) for x in a]
        tinv = [eye - x for x in a]
        for _ in range(5):
            pw = [_bf(_dot(x, x)) for x in pw]
            tinv = [tinv[h] + _dot(_bf(tinv[h]), pw[h]) for h in heads]
        tb = [_bf(x) for x in tinv]
        gam = [jnp.exp(x) for x in gcol]
        rhs = [jnp.concatenate([v_ref[rows, hs[h]] * bcol[h], kh[h] * (bcol[h] * gam[h])], axis=1) for h in heads]
        x0 = [_dot(tb[h], _bf(rhs[h])) for h in heads]
        resid = [rhs[h] - x0[h] - _mm3(a[h], x0[h]) for h in heads]
        sol = [x0[h] + _dot(tb[h], _bf(resid[h])) for h in heads]
        qh = [q_ref[rows, hs[h]] for h in heads]
        qk = [_dot_nt(_bf(qh[h]), kb[h]) * dec[h] for h in heads]
        kdt = [_dot_nt(eye_h, _bf(kh[h] * jnp.exp(glast[:, h:h + 1] - gcol[h]))) for h in heads]
        for h in heads:
            u0_s[c, h] = sol[h][:, :HEAD]
            wkqg_s[c, h] = _bf(jnp.concatenate([sol[h][:, HEAD:], qh[h] * gam[h]], axis=0))
            qk_s[c, h] = _bf(qk[h])
            kdt_s[c, h] = _bf(kdt[h])
        return carry

    lax.fori_loop(0, n_chunks, prep, 0)

    def scan(c, carry):
        rows = pl.ds(pl.multiple_of(c * CHUNK, CHUNK), CHUNK)
        glr = gl_s[c]
        heads = range(n_heads)
        ws = [_dot(wkqg_s[c, h], _bf(s_scr[h])) for h in heads]
        ub = [_bf(u0_s[c, h] - ws[h][:CHUNK]) for h in heads]
        for h in heads:
            o_ref[rows, HEAD * h:HEAD * (h + 1)] = ws[h][CHUNK:] + _dot(qk_s[c, h], ub[h])
        for h in heads:
            s_scr[h] = jnp.exp(glr[0:1, h:h + 1]) * s_scr[h] + _dot(kdt_s[c, h], ub[h])
        return carry

    lax.fori_loop(0, n_chunks, scan, 0)
    sout_ref[...] = s_scr[...]


def _delta_prompt(q, k, v, gb, n_heads):
    t = q.shape[0]
    n_chunks = 8
    tt = n_chunks * CHUNK
    scratch = [pltpu.VMEM((n_heads, HEAD, HEAD), F32),
               pltpu.VMEM((n_chunks, n_heads, CHUNK, HEAD), F32),
               pltpu.VMEM((n_chunks, n_heads, 2 * CHUNK, HEAD), BF16),
               pltpu.VMEM((n_chunks, n_heads, CHUNK, CHUNK), BF16),
               pltpu.VMEM((n_chunks, n_heads, HEAD, CHUNK), BF16),
               pltpu.VMEM((n_chunks, SUBLANES, LANES), F32)]
    return _row_call(functools.partial(_delta_prompt_body, n_heads=n_heads, n_chunks=n_chunks), "delta_prompt",
                     t, tt, [q, k, v, gb], [], [(q.shape[1], F32)],
                     const_outs=[((n_heads, HEAD, HEAD), F32)], scratch=scratch)


def _delta_step_body(qt_ref, kt_ref, v_ref, gbt_ref, s_ref, o_ref, so_ref, *, n_seq, n_heads):
    h = pl.program_id(0)
    decay = jnp.exp(gbt_ref[pl.ds(h, 1), :])
    beta = gbt_ref[pl.ds(n_heads + h, 1), :]
    for b in range(n_seq):
        s = decay[:, b:b + 1] * s_ref[b]
        kc = kt_ref[:, b:b + 1]
        u = beta[:, b:b + 1] * (v_ref[b:b + 1, :] - jnp.sum(kc * s, axis=0, keepdims=True))
        s = s + kc * u
        so_ref[b] = s
        o_ref[b:b + 1, :] = jnp.sum(qt_ref[:, b:b + 1] * s, axis=0, keepdims=True)


def _delta_step(q, k, v, gb, state, n_heads):
    n_seq = q.shape[0]
    col = pl.BlockSpec((HEAD, n_seq), lambda h: (h, 0))
    row = pl.BlockSpec((n_seq, HEAD), lambda h: (0, h))
    st = pl.BlockSpec((n_seq, None, HEAD, HEAD), lambda h: (0, h, 0, 0))
    gbt = gb.T
    return pl.pallas_call(
        functools.partial(_delta_step_body, n_seq=n_seq, n_heads=n_heads),
        out_shape=[jax.ShapeDtypeStruct(q.shape, F32), jax.ShapeDtypeStruct(state.shape, F32)],
        grid=(n_heads,), in_specs=[col, col, row, _const_spec(gbt), st], out_specs=[row, st],
        compiler_params=_params(), name="delta_step")(q.T, k.T, v, gbt, state)


def _a_out_body(o_ref, z_ref, res_ref, gout_ref, w_ref, out_ref):
    parts = []
    for h in range(o_ref.shape[1] // HEAD):
        hs = slice(HEAD * h, HEAD * (h + 1))
        parts.append(_bf(_rms(o_ref[:, hs]) * gout_ref[...] * _silu(z_ref[:, hs])))
    out_ref[...] = res_ref[...] + _dot(jnp.concatenate(parts, axis=1), w_ref[...])


def _ffn_body(*refs, tm, per_row_state):
    if per_row_state:
        (h_ref, p_ref, buf_ref, gffn_ref, wup_ref, wconv_ref, wdown_ref, gple_ref, wgate_ref, wproj_ref,
         out_ref, tail_ref, act_s) = refs
    else:
        (h_ref, p_ref, gffn_ref, wup_ref, wconv_ref, wdown_ref, gple_ref, wgate_ref, wproj_ref,
         out_ref, tail_ref, ubuf, act_s) = refs

        @pl.when(pl.program_id(0) == 0)
        def _():
            ubuf[0:SUBLANES, :] = jnp.zeros((SUBLANES, ubuf.shape[1]), F32)

    h = h_ref[...]
    hn = _bf(_rms(h) * gffn_ref[...])
    dff = wdown_ref.shape[0]
    for c in range(dff // MXU_COLS):
        ys = []
        for half in range(2):
            lo = half * dff + MXU_COLS * c
            cols = slice(lo, lo + MXU_COLS)
            u = _dot(hn, wup_ref[:, cols])
            w = wconv_ref[:, cols]
            if per_row_state:
                y = buf_ref[:, cols] * w[0:1]
                y = y + buf_ref[:, 2 * dff + lo:2 * dff + lo + MXU_COLS] * w[1:2]
                y = y + u * w[2:3]
                tail_ref[:, cols] = u
            else:
                ubuf[SUBLANES:SUBLANES + tm, cols] = u
                y = ubuf[6:6 + tm, cols] * w[0:1]
                y = y + ubuf[7:7 + tm, cols] * w[1:2]
                y = y + u * w[2:3]
                tail = ubuf[tm:tm + SUBLANES, cols]
                tail_ref[:, cols] = tail
                ubuf[0:SUBLANES, cols] = tail
            ys.append(y)
        act_s[:, MXU_COLS * c:MXU_COLS * (c + 1)] = _bf(_silu(ys[0]) * ys[1])
    h2 = h + _dot(act_s[...], wdown_ref[...])
    gate = _sigmoid(_dot(_bf(_rms(h2) * gple_ref[...]), wgate_ref[...]))
    out_ref[...] = h2 + gate * _dot(_bf(p_ref[...]), wproj_ref[...])


def _ffn(h, p, w, layer, tm, state):
    m = h.shape[0]
    per_row = state is not None
    dff = w["w_down"][layer].shape[0]
    consts = [w["g_ffn"][layer], w["w_up"][layer], w["w_fconv"][layer], w["w_down"][layer],
              w["g_ple"][layer], w["w_gate"][layer], w["w_proj"][layer]]
    rows = [h, p] + ([state.reshape(m, -1)] if per_row else [])
    scratch = ([] if per_row else [pltpu.VMEM((tm + SUBLANES, 2 * dff), F32)]) + [pltpu.VMEM((tm, dff), BF16)]
    out, tail = _row_call(functools.partial(_ffn_body, tm=tm, per_row_state=per_row), f"ffn{layer}", m, tm,
                          rows, consts, [(h.shape[1], F32)],
                          const_outs=[((m if per_row else SUBLANES, 2 * dff), F32)], scratch=scratch)
    return out, tail


def _kvq_body(*refs, tm, dils):
    h_ref, cos_ref, sa_ref, sb_ref, gkv_ref, gmix_ref, wkv_ref, wq_ref, gk_ref, gq_ref = refs[:10]
    if dils is None:
        q_ref, k_ref, v_ref = refs[10:]
    else:
        ng = len(dils)
        k_ref, v_ref = refs[10:12]
        qd, kd, vd = refs[12:12 + ng], refs[12 + ng:12 + 2 * ng], refs[12 + 2 * ng:12 + 3 * ng]
        stage = refs[12 + 3 * ng]
    h = h_ref[...]
    base = _rms(h)
    hkv = _bf(base * gkv_ref[...])
    hq = _bf(base * gmix_ref[...])
    cos, sa, sb = cos_ref[...], sa_ref[...], sb_ref[...]
    kw = KV_PER_GROUP * HEAD

    def norm_rope(x, g):
        y = _rms(x) * g
        return y * cos + pltpu.roll(y, ROPE_DIM // 2, 1) * sa + pltpu.roll(y, HEAD - ROPE_DIM // 2, 1) * sb

    def heads(src, w_ref, c, g):
        x = _dot(src, w_ref[:, MXU_COLS * c:MXU_COLS * (c + 1)])
        return jnp.concatenate([norm_rope(x[:, HEAD * j:HEAD * (j + 1)], g) for j in range(MXU_COLS // HEAD)], axis=1)

    def by_class(x, dst_ref, dil, cols):
        if dil == 1:
            dst_ref[0, :, cols] = _bf(x)
        else:
            for j in range(MXU_COLS // LANES):
                stage[j] = x[:, LANES * j:LANES * (j + 1)]
            for r in range(dil):
                for j in range(MXU_COLS // LANES):
                    lo = cols.start + LANES * j
                    dst_ref[r, :, lo:lo + LANES] = _bf(stage[j, pl.ds(r, tm // dil, stride=dil), :])

    assert kw == MXU_COLS and Q_PER_GROUP * HEAD == 2 * MXU_COLS
    n_kv = k_ref.shape[1]
    for c in range(n_kv // MXU_COLS):
        x = heads(hkv, wkv_ref, c, gk_ref[...])
        k_ref[:, MXU_COLS * c:MXU_COLS * (c + 1)] = x
        if dils is not None:
            by_class(x, kd[c], dils[c], slice(0, kw))
    vv = _dot(hkv, wkv_ref[:, n_kv:])
    v_ref[...] = vv
    if dils is not None:
        for c in range(n_kv // MXU_COLS):
            by_class(vv[:, MXU_COLS * c:MXU_COLS * (c + 1)], vd[c], dils[c], slice(0, kw))
    for c in range(wq_ref.shape[1] // MXU_COLS):
        x = heads(hq, wq_ref, c, gq_ref[...])
        if dils is None:
            q_ref[:, MXU_COLS * c:MXU_COLS * (c + 1)] = x
        else:
            by_class(x, qd[c // 2], dils[c // 2], slice(MXU_COLS * (c % 2), MXU_COLS * (c % 2 + 1)))


def _kvq(h, tables, w, tm, dils):
    m, d = h.shape
    n_q, n_kv = w["w_q"].shape[1], w["w_kv"].shape[1] // 2
    qw, kw = Q_PER_GROUP * HEAD, KV_PER_GROUP * HEAD
    consts = [w["g_kv"], w["g_mix"][1], w["w_kv"], w["w_q"], w["g_k"], w["g_q"]]
    rows = [h, *tables]
    row_spec = lambda c: pl.BlockSpec((tm, c), lambda i: (i, 0))
    in_specs = [row_spec(a.shape[1]) for a in rows] + [_const_spec(a) for a in consts]
    nat = lambda c: jax.ShapeDtypeStruct((m, c), F32)
    if dils is None:
        out_shape = [nat(n_q), nat(n_kv), nat(n_kv)]
        out_specs, scratch = [row_spec(n_q), row_spec(n_kv), row_spec(n_kv)], []
    else:
        out_shape, out_specs = [nat(n_kv), nat(n_kv)], [row_spec(n_kv), row_spec(n_kv)]
        for width in (qw, kw, kw):
            for dil in dils:
                assert tm % (dil * 2 * SUBLANES) == 0
                out_shape.append(jax.ShapeDtypeStruct((dil, m // dil, width), BF16))
                out_specs.append(pl.BlockSpec((dil, tm // dil, width), lambda i: (0, i, 0)))
        scratch = [pltpu.VMEM((MXU_COLS // LANES, tm, LANES), F32)]
    return pl.pallas_call(functools.partial(_kvq_body, tm=tm, dils=dils), out_shape=out_shape, grid=(m // tm,),
                          in_specs=in_specs, out_specs=out_specs, scratch_shapes=scratch,
                          compiler_params=_params(), name="kvq")(*rows, *consts)


def _rope_tables(pos):
    half = ROPE_DIM // 2
    inv_freq = ROPE_THETA ** (-jnp.arange(half, dtype=F32) * 2.0 / ROPE_DIM)
    ang = pos.astype(F32)[:, None] * inv_freq[None, :]
    cos, sin = jnp.cos(ang), jnp.sin(ang)
    m = pos.shape[0]
    zero = jnp.zeros((m, half), F32)
    rest0 = jnp.zeros((m, HEAD - ROPE_DIM), F32)
    return (jnp.concatenate([cos, cos, jnp.ones((m, HEAD - ROPE_DIM), F32)], axis=1),
            jnp.concatenate([zero, sin, rest0], axis=1),
            jnp.concatenate([-sin, zero, rest0], axis=1))


def _attn_prompt_body(q_ref, kc_ref, kp_ref, vc_ref, vp_ref, o_ref, lse_ref, *, n, blocks):
    first_lo = jnp.where(pl.program_id(1) > 0, 0, n)
    qi = lax.broadcasted_iota(jnp.int32, (Q_REP * n, 2 * n), 0) & (n - 1)
    kj = lax.broadcasted_iota(jnp.int32, (Q_REP * n, 2 * n), 1)
    band = (kj >= qi) & (kj <= qi + n)
    lane = lax.broadcasted_iota(jnp.int32, (n, LANES), 1)
    for j in range(blocks):
        rows = slice(n * j, n * (j + 1))
        lse_tile = jnp.zeros((n, LANES), F32)
        for g in range(KV_PER_GROUP):
            hs = slice(HEAD * g, HEAD * (g + 1))
            if j == 0:
                kprev, vprev = kp_ref[:, hs], vp_ref[:, hs]
                valid = band & (kj >= first_lo)
            else:
                prev = slice(n * (j - 1), n * j)
                kprev, vprev = kc_ref[prev, hs], vc_ref[prev, hs]
                valid = band
            kwin = jnp.concatenate([kprev, kc_ref[rows, hs]], axis=0)
            vwin = jnp.concatenate([vprev, vc_ref[rows, hs]], axis=0)
            q0 = Q_REP * HEAD * g
            qg = jnp.concatenate([q_ref[rows, q0 + HEAD * e:q0 + HEAD * (e + 1)] for e in range(Q_REP)], axis=0)
            s = jnp.where(valid, _dot_nt(qg, kwin) * (HEAD ** -0.5), -jnp.inf)
            mx = jnp.max(s, axis=-1, keepdims=True)
            p = jnp.exp(s - mx)
            den = jnp.sum(p, axis=-1, keepdims=True)
            o = _dot(_bf(p), vwin) / den
            lse = mx + jnp.log(den)
            for e in range(Q_REP):
                o_ref[rows, q0 + HEAD * e:q0 + HEAD * (e + 1)] = o[n * e:n * (e + 1)]
                lse_tile = jnp.where(lane == Q_REP * g + e, lse[n * e:n * (e + 1)], lse_tile)
        lse_ref[rows, :] = lse_tile


def _attn_prompt(qd, kd, vd, gi, window):
    dil, ln, qw = qd.shape
    kw = kd.shape[2]
    n = window // dil
    blocks = min(8, ln // n)
    rows = n * blocks
    assert n == HEAD and ln % rows == 0
    cur = lambda r, b: (r, b, 0)
    prev = lambda r, b: (r, jnp.maximum(b * blocks - 1, 0), 0)
    return pl.pallas_call(
        functools.partial(_attn_prompt_body, n=n, blocks=blocks),
        out_shape=[jax.ShapeDtypeStruct((dil, ln, qw), F32), jax.ShapeDtypeStruct((dil, ln, LANES), F32)],
        grid=(dil, ln // rows),
        in_specs=[pl.BlockSpec((None, rows, qw), cur), pl.BlockSpec((None, rows, kw), cur),
                  pl.BlockSpec((None, n, kw), prev), pl.BlockSpec((None, rows, kw), cur),
                  pl.BlockSpec((None, n, kw), prev)],
        out_specs=[pl.BlockSpec((None, rows, qw), cur), pl.BlockSpec((None, rows, LANES), cur)],
        compiler_params=_params(2), name=f"attn_prompt{gi}")(qd, kd, kd, vd, vd)


def _attn_sample_body(q_ref, kn_ref, vn_ref, *refs, bb):
    n_groups = len(GROUPS)
    caches, outs, lses = refs[:n_groups], refs[n_groups:2 * n_groups], refs[2 * n_groups:]
    lane = lax.broadcasted_iota(jnp.int32, (1, LANES), 1)
    kw = KV_PER_GROUP * HEAD
    for gi in range(n_groups):
        c_ref, o_ref, l_ref = caches[gi], outs[gi], lses[gi]
        for b in range(bb):
            lrow = jnp.zeros((1, LANES), F32)
            for g in range(KV_PER_GROUP):
                kc = c_ref[b, :, HEAD * g:HEAD * (g + 1)]
                vc = c_ref[b, :, kw + HEAD * g:kw + HEAD * (g + 1)]
                kcol = kw * gi + HEAD * g
                kn = kn_ref[b:b + 1, kcol:kcol + HEAD]
                vn = vn_ref[b:b + 1, kcol:kcol + HEAD]
                for e in range(Q_REP):
                    oc = HEAD * (Q_REP * g + e)
                    qv = q_ref[b:b + 1, Q_PER_GROUP * HEAD * gi + oc:Q_PER_GROUP * HEAD * gi + oc + HEAD]
                    s = jnp.sum(kc * qv, axis=1, keepdims=True) * (HEAD ** -0.5)
                    sn = jnp.sum(kn * qv, axis=1, keepdims=True) * (HEAD ** -0.5)
                    mx = jnp.maximum(jnp.max(s, axis=0, keepdims=True), sn)
                    p = jnp.exp(s - mx)
                    pn = jnp.exp(sn - mx)
                    den = jnp.sum(p, axis=0, keepdims=True) + pn
                    o_ref[b:b + 1, oc:oc + HEAD] = (jnp.sum(p * vc, axis=0, keepdims=True) + pn * vn) / den
                    lrow = jnp.where(lane == Q_REP * g + e, mx + jnp.log(den), lrow)
            l_ref[b:b + 1, :] = lrow


def _attn_sample(q, k, v, caches):
    n_seq = q.shape[0]
    bb = SUBLANES
    assert n_seq % bb == 0
    qw, kw = Q_PER_GROUP * HEAD, KV_PER_GROUP * HEAD
    views, cache_specs = [], []
    for cache, (window, dil) in zip(caches, GROUPS):
        n = window // dil
        assert cache.shape[1] == window and n == HEAD
        views.append(cache.reshape(n_seq, n, dil * 2 * kw))
        cache_specs.append(pl.BlockSpec((bb, n, 2 * kw), lambda i: (i, 0, 0)))
    row = lambda c: pl.BlockSpec((bb, c), lambda i: (i, 0))
    n_groups = len(GROUPS)
    res = pl.pallas_call(
        functools.partial(_attn_sample_body, bb=bb),
        out_shape=[jax.ShapeDtypeStruct((n_seq, qw), F32)] * n_groups
        + [jax.ShapeDtypeStruct((n_seq, LANES), F32)] * n_groups,
        grid=(n_seq // bb,),
        in_specs=[row(q.shape[1]), row(k.shape[1]), row(v.shape[1])] + cache_specs,
        out_specs=[row(qw)] * n_groups + [row(LANES)] * n_groups,
        compiler_params=_params(), name="attn_sample")(q, k, v, *views)
    return res[:n_groups], res[n_groups:]


def _o_proj_body(*refs, tm, dils):
    ng = len(dils)
    o_refs, l_refs = refs[:ng], refs[ng:2 * ng]
    res_ref, w_ref, out_ref = refs[2 * ng:2 * ng + 3]
    scratch = list(refs[2 * ng + 3:])
    get_o, lses = [], []
    for gi, dil in enumerate(dils):
        if dil == 1:
            get_o.append(lambda hh, r=o_refs[gi]: r[0, :, HEAD * hh:HEAD * (hh + 1)])
            lses.append(l_refs[gi][0])
        else:
            o_s, l_s = scratch.pop(0), scratch.pop(0)
            for r in range(dil):
                dst = pl.ds(r, tm // dil, stride=dil)
                for hh in range(Q_PER_GROUP):
                    o_s[hh, dst, :] = o_refs[gi][r, :, HEAD * hh:HEAD * (hh + 1)]
                l_s[dst, :] = l_refs[gi][r]
            get_o.append(lambda hh, r=o_s: r[hh])
            lses.append(l_s[...])
    parts = []
    for hh in range(Q_PER_GROUP):
        lh = [l[:, hh:hh + 1] for l in lses]
        mx = functools.reduce(jnp.maximum, lh)
        ex = [jnp.exp(x - mx) for x in lh]
        den = functools.reduce(lambda a, b: a + b, ex)
        acc = (ex[0] / den) * get_o[0](hh)
        for gi in range(1, ng):
            acc = acc + (ex[gi] / den) * get_o[gi](hh)
        parts.append(_bf(acc))
    out_ref[...] = res_ref[...] + _dot(jnp.concatenate(parts, axis=1), w_ref[...])


def _o_proj(outs, lses, res, w_o, tm, dils):
    m, d = res.shape
    qw = Q_PER_GROUP * HEAD
    grouped = lambda a: pl.BlockSpec((a.shape[0], tm // a.shape[0], a.shape[2]), lambda i: (0, i, 0))
    row_spec = pl.BlockSpec((tm, d), lambda i: (i, 0))
    scratch = []
    for dil in dils:
        if dil > 1:
            assert tm % (dil * SUBLANES) == 0
            scratch += [pltpu.VMEM((qw // HEAD, tm, HEAD), F32), pltpu.VMEM((tm, LANES), F32)]
    return pl.pallas_call(
        functools.partial(_o_proj_body, tm=tm, dils=dils), out_shape=jax.ShapeDtypeStruct((m, d), F32),
        grid=(m // tm,), in_specs=[grouped(a) for a in (*outs, *lses)] + [row_spec, _const_spec(w_o)],
        out_specs=row_spec, scratch_shapes=scratch, compiler_params=_params(), name="o_proj")(*outs, *lses, res, w_o)


def _trunk(x, p, pos, state, w, tm):
    m, d = x.shape
    per_row = state is not None
    n_heads = w["n_heads"]
    aw = w["w_z"].shape[1]

    consts = [w["g_mix"][0], w["w_qkv"], w["w_ab"], w["w_z"], w["w_aconv"], w["a_log"], w["dt_bias"]]
    if per_row:
        consts.append(jnp.swapaxes(state["qkv_conv"], 0, 1))
    q, k, v, z, gb, qkv_tail = _row_call(
        functools.partial(_a_in_body, tm=tm, n_heads=n_heads, per_row_state=per_row), "a_in", m, tm, [x], consts,
        [(aw, F32)] * 4 + [(LANES, F32)], const_outs=[((m if per_row else SUBLANES, 3 * aw), F32)],
        scratch=[] if per_row else [pltpu.VMEM((tm + SUBLANES, 3 * aw), F32)])
    if per_row:
        o, s1 = _delta_step(q, k, v, gb, state["delta"], n_heads)
    else:
        o, s1 = _delta_prompt(q, k, v, gb, n_heads)
    (h,) = _row_call(_a_out_body, "a_out", m, tm, [o, z, x], [w["g_aout"], w["w_aout"]], [(d, F32)])
    h, ffn_tail0 = _ffn(h, p[0], w, 0, tm, state["ffn_conv"][0] if per_row else None)

    tables = _rope_tables(pos)
    if per_row:
        qa, ka, va = _kvq(h, tables, w, tm, None)
        outs, lses = _attn_sample(qa, ka, va, state["caches"])
        outs, lses = [a[None] for a in outs], [a[None] for a in lses]
        dils = (1,) * len(GROUPS)
    else:
        dils = tuple(dil for _, dil in GROUPS)
        ng = len(dils)
        res = _kvq(h, tables, w, tm, dils)
        ka, va = res[:2]
        qd, kd, vd = res[2:2 + ng], res[2 + ng:2 + 2 * ng], res[2 + 2 * ng:]
        att = [_attn_prompt(qd[gi], kd[gi], vd[gi], gi, window) for gi, (window, _) in enumerate(GROUPS)]
        outs, lses = [a[0] for a in att], [a[1] for a in att]
    h = _o_proj(outs, lses, h, w["w_o"], tm, dils)
    h, ffn_tail1 = _ffn(h, p[1], w, 1, tm, state["ffn_conv"][1] if per_row else None)
    return h, s1, qkv_tail, (ffn_tail0, ffn_tail1), ka, va


def _kv_rows(k, v, gi, n_rows):
    kw = KV_PER_GROUP * HEAD
    sl = slice(kw * gi, kw * (gi + 1))
    shape = (n_rows, 1, KV_PER_GROUP, HEAD)
    return jnp.concatenate([k[k.shape[0] - n_rows:, sl].reshape(shape), v[v.shape[0] - n_rows:, sl].reshape(shape)],
                           axis=1)


def kernel(x_prompt, x_sample, p_prompt, p_sample, state_delta, state_qkv_conv, state_ffn_conv, cache_kv_w128, cache_kv_w512, cache_kv_w2048, g_mix_norm, g_ffn_norm, w_ffn_up, w_ffn_conv, w_ffn_down, g_ple_norm, w_ple_gate, w_ple_proj, w_a_in, w_a_conv, a_log, a_dt_bias, g_a_out_norm, w_a_out, g_kv_norm, w_kv, g_k_norm, w_q, g_q_norm, w_o):
    n_heads = a_log.shape[1]
    aw = n_heads * HEAD
    assert w_a_in.shape[0] == 1 and w_q.shape[0] == 1 and w_a_in.shape[2] == 4 * aw + 2 * n_heads
    row = lambda a: a.reshape(1, -1)
    lane_pad = lambda a: jnp.pad(a, ((0, 0), (0, LANES - a.shape[1])))
    w = {
        "n_heads": n_heads,
        "g_mix": [row(g) for g in g_mix_norm], "g_ffn": [row(g) for g in g_ffn_norm],
        "g_ple": [row(g) for g in g_ple_norm],
        "w_up": _bf(w_ffn_up), "w_fconv": w_ffn_conv, "w_down": _bf(w_ffn_down),
        "w_gate": _bf(w_ple_gate), "w_proj": _bf(w_ple_proj),
        "w_qkv": _bf(w_a_in[0][:, :3 * aw]), "w_ab": _bf(lane_pad(w_a_in[0][:, 3 * aw:3 * aw + 2 * n_heads])),
        "w_z": _bf(w_a_in[0][:, 3 * aw + 2 * n_heads:]), "w_aconv": w_a_conv[0],
        "a_log": lane_pad(a_log), "dt_bias": lane_pad(a_dt_bias),
        "g_aout": row(g_a_out_norm[0]), "w_aout": _bf(w_a_out[0]),
        "g_kv": row(g_kv_norm), "w_kv": _bf(w_kv), "g_k": row(g_k_norm),
        "w_q": _bf(w_q[0]), "g_q": row(g_q_norm[0]), "w_o": _bf(w_o[0]),
    }
    assert x_prompt.shape[0] == 1 and x_sample.shape[1] == 1
    seq, d = x_prompt.shape[1], x_prompt.shape[2]
    n_seq = x_sample.shape[0]

    yp, sd_p, qkv_tail_p, ffn_tails_p, kp, vp = _trunk(
        x_prompt.reshape(seq, d), p_prompt[:, 0], jnp.arange(seq, dtype=jnp.int32), None, w, tm=256)
    a_keep, f_keep = state_qkv_conv.shape[2], state_ffn_conv.shape[2]
    sq_p = qkv_tail_p[SUBLANES - a_keep:][None, None]
    sf_p = jnp.stack([t[SUBLANES - f_keep:] for t in ffn_tails_p])[:, None]
    kv_p = [_kv_rows(kp, vp, gi, min(window, seq))[None] for gi, (window, _) in enumerate(GROUPS)]

    state = {"delta": state_delta[0], "qkv_conv": state_qkv_conv[0], "ffn_conv": state_ffn_conv,
             "caches": (cache_kv_w128, cache_kv_w512, cache_kv_w2048)}
    ys, sd_s, qkv_tail_s, ffn_tails_s, ks, vs = _trunk(
        x_sample.reshape(n_seq, d), p_sample[:, :, 0], jnp.full((n_seq,), PAST_LEN, jnp.int32), state, w, tm=n_seq)
    sq_s = jnp.concatenate([state_qkv_conv[0][:, 1:], qkv_tail_s[:, None]], axis=1)[None]
    sf_s = jnp.stack([jnp.concatenate([state_ffn_conv[i][:, 1:], ffn_tails_s[i][:, None]], axis=1)
                      for i in range(2)])
    kv_s = [_kv_rows(ks, vs, gi, n_seq).reshape(n_seq, 1, 2, KV_PER_GROUP, HEAD) for gi in range(len(GROUPS))]

    return (yp[None], ys[:, None], sd_p[None, None], sq_p, sf_p, kv_p[0], kv_p[1], kv_p[2],
            sd_s[None], sq_s, sf_s, kv_s[0], kv_s[1], kv_s[2])
```

```python
import functools

import jax
import jax.numpy as jnp
from jax import lax
from jax.experimental import pallas as pl
from jax.experimental.pallas import tpu as pltpu

F32 = jnp.float32
BF16 = jnp.bfloat16

EPS = 1e-6
HEAD = 128
CHUNK = 64
GROUPS = ((128, 1), (512, 4), (2048, 16))
KV_PER_GROUP = 2
Q_REP = 2
Q_PER_GROUP = KV_PER_GROUP * Q_REP
ROPE_DIM = HEAD // 4
ROPE_THETA = 500000.0
PAST_LEN = 16384

LANES = 128
SUBLANES = 8
MXU_COLS = 256
VMEM_LIMIT_BYTES = 56 * 1024 * 1024

PREP_CHUNKS = 2


def _dot(a, b):
    return jnp.dot(a, b, preferred_element_type=F32)


def _dot_nt(a, b):
    return lax.dot_general(a, b, (((1,), (1,)), ((), ())), preferred_element_type=F32)


def _bf(x):
    return x.astype(BF16)


def _rms(x):
    return x * lax.rsqrt(jnp.mean(x * x, axis=-1, keepdims=True) + EPS)


def _sigmoid(x):
    return 1.0 / (1.0 + jnp.exp(-x))


def _silu(x):
    return x * _sigmoid(x)


def _split2(x):
    hi = _bf(x)
    return hi, _bf(x - hi.astype(F32))


def _split3(x):
    hi = _bf(x)
    r = x - hi.astype(F32)
    mid = _bf(r)
    return hi, mid, _bf(r - mid.astype(F32))


def _mm3(a, b):
    ah, al = _split2(a)
    bh, bl = _split2(b)
    return _dot(ah, bh) + (_dot(ah, bl) + _dot(al, bh))


def _causal_conv_tile(x, w, carry_ref, cols):
    rows = x.shape[0]
    first = lax.broadcasted_iota(jnp.int32, (SUBLANES, x.shape[1]), 0) == 0
    z = x * w[0:1]
    for j in range(1, w.shape[0]):
        rolled = pltpu.roll(z, 1, 0)
        head = jnp.where(first, carry_ref[j - 1:j, cols], rolled[0:SUBLANES])
        carry_ref[j - 1:j, cols] = z[rows - 1:, :]
        z = jnp.concatenate([head, rolled[SUBLANES:]], axis=0) + x * w[j:j + 1]
    return z


def _const_spec(a):
    nd = a.ndim
    return pl.BlockSpec(a.shape, lambda *_: (0,) * nd, pipeline_mode=pl.Buffered(1))


def _params(n_axes=1):
    return pltpu.CompilerParams(dimension_semantics=("arbitrary",) * n_axes,
                                vmem_limit_bytes=VMEM_LIMIT_BYTES)


def _row_call(body, name, m, tm, row_ins, const_ins, row_outs, const_outs=(), scratch=()):
    assert m % tm == 0
    in_specs = [pl.BlockSpec((tm, a.shape[1]), lambda i: (i, 0)) for a in row_ins]
    in_specs += [_const_spec(a) for a in const_ins]
    out_shape = [jax.ShapeDtypeStruct((m, c), dt) for c, dt in row_outs]
    out_specs = [pl.BlockSpec((tm, c), lambda i: (i, 0)) for c, _ in row_outs]
    for shape, dt in const_outs:
        out_shape.append(jax.ShapeDtypeStruct(shape, dt))
        out_specs.append(pl.BlockSpec(shape, lambda i, _n=len(shape): (0,) * _n))
    return pl.pallas_call(body, out_shape=out_shape, grid=(m // tm,), in_specs=in_specs, out_specs=out_specs,
                          scratch_shapes=list(scratch), compiler_params=_params(), name=name)(*row_ins, *const_ins)


def _a_in_body(*refs, tm, n_heads, per_row_state):
    if per_row_state:
        (x_ref, g_ref, wqkv_ref, wab_ref, wz_ref, wconv_ref, alog_ref, dtb_ref, buf_ref,
         q_ref, k_ref, v_ref, z_ref, gb_ref, tail_ref) = refs
    else:
        (x_ref, g_ref, wqkv_ref, wab_ref, wz_ref, wconv_ref, alog_ref, dtb_ref,
         q_ref, k_ref, v_ref, z_ref, gb_ref, tail_ref, cbuf) = refs

        @pl.when(pl.program_id(0) == 0)
        def _():
            cbuf[0:SUBLANES, :] = jnp.zeros((SUBLANES, cbuf.shape[1]), F32)

    hn = _bf(_rms(x_ref[...]) * g_ref[...])
    outs = (q_ref, k_ref, v_ref)
    width = q_ref.shape[1]
    per_part = width // MXU_COLS
    for c in range(3 * per_part):
        cols = slice(MXU_COLS * c, MXU_COLS * (c + 1))
        pq = _dot(hn, wqkv_ref[:, cols])
        w = wconv_ref[:, cols]
        if per_row_state:
            y = buf_ref[0, :, cols] * w[0:1]
            y = y + buf_ref[1, :, cols] * w[1:2]
            y = y + buf_ref[2, :, cols] * w[2:3]
            y = y + pq * w[3:4]
            tail_ref[:, cols] = pq
        else:
            y = _causal_conv_tile(pq, w, cbuf, cols)
            tail_ref[:, cols] = pq[tm - SUBLANES:, :]
        a = _silu(y)
        part, sub = divmod(c, per_part)
        if part < 2:
            halves = []
            for hh in range(MXU_COLS // HEAD):
                ah = a[:, HEAD * hh:HEAD * (hh + 1)]
                nrm = ah * lax.rsqrt(jnp.sum(ah * ah, axis=-1, keepdims=True) + EPS)
                if part == 0:
                    nrm = nrm * (HEAD ** -0.5)
                halves.append(nrm)
            a = jnp.concatenate(halves, axis=1)
        outs[part][:, MXU_COLS * sub:MXU_COLS * (sub + 1)] = a

    pab = _dot(hn, wab_ref[...])
    lane = lax.broadcasted_iota(jnp.int32, pab.shape, 1)
    ap = pab + dtb_ref[...]
    softplus = jnp.maximum(ap, 0.0) + jnp.log1p(jnp.exp(-jnp.abs(ap)))
    g = -jnp.exp(alog_ref[...]) * softplus
    gb_ref[...] = jnp.where(lane < n_heads, g, _sigmoid(pab))
    z_ref[...] = _dot(hn, wz_ref[...])


def _delta_prompt_body(q_ref, k_ref, v_ref, gb_ref, o_ref, sout_ref,
                       s_scr, u0_s, wkqg_s, qk_s, kdt_s, gl_s, *, n_heads, n_chunks):
    @pl.when(pl.program_id(0) == 0)
    def _():
        s_scr[...] = jnp.zeros(s_scr.shape, F32)

    ri = lax.broadcasted_iota(jnp.int32, (CHUNK, CHUNK), 0)
    ci = lax.broadcasted_iota(jnp.int32, (CHUNK, CHUNK), 1)
    incl = ri >= ci
    strict = ri > ci
    eye = jnp.where(ri == ci, 1.0, 0.0).astype(F32)
    ltri = _bf(jnp.where(incl, 1.0, 0.0))
    sel_r = lax.broadcasted_iota(jnp.int32, (n_heads * CHUNK, LANES), 0)
    sel_c = lax.broadcasted_iota(jnp.int32, (n_heads * CHUNK, LANES), 1)
    sel = _bf(jnp.where((sel_r >= CHUNK * sel_c) & (sel_r < CHUNK * (sel_c + 1)), 1.0, 0.0))
    er = lax.broadcasted_iota(jnp.int32, (HEAD, HEAD), 0)
    ec = lax.broadcasted_iota(jnp.int32, (HEAD, HEAD), 1)
    eye_h = _bf(jnp.where(er == ec, 1.0, 0.0))

    ltri3 = jnp.concatenate([ltri] * 3, axis=1)
    sel3 = jnp.concatenate([sel] * 3, axis=1)
    heads = range(n_heads)
    hs = [slice(HEAD * h, HEAD * (h + 1)) for h in heads]

    def prep(i, carry):
        chunks = [i * PREP_CHUNKS + j for j in range(PREP_CHUNKS)]
        rows = [pl.ds(pl.multiple_of(c * CHUNK, CHUNK), CHUNK) for c in chunks]
        units = [(j, h) for j in range(PREP_CHUNKS) for h in heads]
        gbc = [gb_ref[r, :] for r in rows]
        gc = [_dot(ltri3, jnp.concatenate(_split3(x), axis=0)) for x in gbc]
        grow_all = [_dot_nt(sel3, jnp.concatenate(_split3(x), axis=1)) for x in gc]
        glast = [x[CHUNK - 1:CHUNK, :] for x in gc]
        for j, c in enumerate(chunks):
            gl_s[c] = jnp.broadcast_to(glast[j], (SUBLANES, LANES))
        gcol = [gc[j][:, h:h + 1] for j, h in units]
        bcol = [gbc[j][:, n_heads + h:n_heads + h + 1] for j, h in units]
        dec = [jnp.exp(jnp.where(incl, gcol[u] - grow_all[j][CHUNK * h:CHUNK * (h + 1), :], -jnp.inf))
               for u, (j, h) in enumerate(units)]
        kh = [k_ref[rows[j], hs[h]] for j, h in units]
        qh = [q_ref[rows[j], hs[h]] for j, h in units]
        kb = [_bf(x) for x in kh]
        qkk = [_dot_nt(jnp.concatenate([_bf(qh[u]), kb[u]], axis=0), kb[u]) for u in range(len(units))]
        a = [jnp.where(strict, bcol[u] * qkk[u][CHUNK:] * dec[u], 0.0) for u in range(len(units))]
        tinv = [eye - x for x in a]
        pw = [_bf(-x) for x in a]
        pw = [_bf(_dot(x, x)) for x in pw]
        for _ in range(4):
            res = [_dot(jnp.concatenate([_bf(tinv[u]), pw[u]], axis=0), pw[u]) for u in range(len(units))]
            tinv = [tinv[u] + res[u][:CHUNK] for u in range(len(units))]
            pw = [_bf(r[CHUNK:]) for r in res]
        tinv = [tinv[u] + _dot(_bf(tinv[u]), pw[u]) for u in range(len(units))]
        tb = [_bf(x) for x in tinv]
        gam = [jnp.exp(x) for x in gcol]
        rhs = [jnp.concatenate([v_ref[rows[j], hs[h]] * bcol[u], kh[u] * (bcol[u] * gam[u])], axis=1)
               for u, (j, h) in enumerate(units)]
        x0 = [_dot(tb[u], _bf(rhs[u])) for u in range(len(units))]
        ax = [_dot(jnp.concatenate(_split2(a[u]), axis=0), jnp.concatenate(_split2(x0[u]), axis=1))
              for u in range(len(units))]
        wide = 2 * HEAD
        resid = [rhs[u] - x0[u] - (ax[u][:CHUNK, :wide] + ax[u][:CHUNK, wide:] + ax[u][CHUNK:, :wide])
                 for u in range(len(units))]
        sol = [x0[u] + _dot(tb[u], _bf(resid[u])) for u in range(len(units))]
        kdt = [_dot_nt(eye_h, _bf(kh[u] * jnp.exp(glast[j][:, h:h + 1] - gcol[u])))
               for u, (j, h) in enumerate(units)]
        for u, (j, h) in enumerate(units):
            c = chunks[j]
            u0_s[c, h] = sol[u][:, :HEAD]
            wkqg_s[c, h] = _bf(jnp.concatenate([sol[u][:, HEAD:], qh[u] * gam[u]], axis=0))
            qk_s[c, h] = _bf(qkk[u][:CHUNK] * dec[u])
            kdt_s[c, h] = _bf(kdt[u])
        return carry

    lax.fori_loop(0, n_chunks // PREP_CHUNKS, prep, 0)

    def scan(c, carry):
        rows = pl.ds(pl.multiple_of(c * CHUNK, CHUNK), CHUNK)
        glr = gl_s[c]
        ws = [_dot(wkqg_s[c, h], _bf(s_scr[h])) for h in heads]
        ub = [_bf(u0_s[c, h] - ws[h][:CHUNK]) for h in heads]
        for h in heads:
            o_ref[rows, HEAD * h:HEAD * (h + 1)] = ws[h][CHUNK:] + _dot(qk_s[c, h], ub[h])
        for h in heads:
            s_scr[h] = jnp.exp(glr[0:1, h:h + 1]) * s_scr[h] + _dot(kdt_s[c, h], ub[h])
        return carry

    lax.fori_loop(0, n_chunks, scan, 0)
    sout_ref[...] = s_scr[...]


def _delta_prompt(q, k, v, gb, n_heads):
    t = q.shape[0]
    n_chunks = 8
    tt = n_chunks * CHUNK
    scratch = [pltpu.VMEM((n_heads, HEAD, HEAD), F32),
               pltpu.VMEM((n_chunks, n_heads, CHUNK, HEAD), F32),
               pltpu.VMEM((n_chunks, n_heads, 2 * CHUNK, HEAD), BF16),
               pltpu.VMEM((n_chunks, n_heads, CHUNK, CHUNK), BF16),
               pltpu.VMEM((n_chunks, n_heads, HEAD, CHUNK), BF16),
               pltpu.VMEM((n_chunks, SUBLANES, LANES), F32)]
    return _row_call(functools.partial(_delta_prompt_body, n_heads=n_heads, n_chunks=n_chunks), "delta_prompt",
                     t, tt, [q, k, v, gb], [], [(q.shape[1], F32)],
                     const_outs=[((n_heads, HEAD, HEAD), F32)], scratch=scratch)


def _delta_step_body(qt_ref, kt_ref, v_ref, gbt_ref, s_ref, o_ref, so_ref, *, n_seq, n_heads):
    h = pl.program_id(0)
    decay = jnp.exp(gbt_ref[pl.ds(h, 1), :])
    beta = gbt_ref[pl.ds(n_heads + h, 1), :]
    for b in range(n_seq):
        s = decay[:, b:b + 1] * s_ref[b]
        kc = kt_ref[:, b:b + 1]
        u = beta[:, b:b + 1] * (v_ref[b:b + 1, :] - jnp.sum(kc * s, axis=0, keepdims=True))
        s = s + kc * u
        so_ref[b] = s
        o_ref[b:b + 1, :] = jnp.sum(qt_ref[:, b:b + 1] * s, axis=0, keepdims=True)


def _delta_step(q, k, v, gb, state, n_heads):
    n_seq = q.shape[0]
    col = pl.BlockSpec((HEAD, n_seq), lambda h: (h, 0))
    row = pl.BlockSpec((n_seq, HEAD), lambda h: (0, h))
    st = pl.BlockSpec((n_seq, None, HEAD, HEAD), lambda h: (0, h, 0, 0))
    gbt = gb.T
    return pl.pallas_call(
        functools.partial(_delta_step_body, n_seq=n_seq, n_heads=n_heads),
        out_shape=[jax.ShapeDtypeStruct(q.shape, F32), jax.ShapeDtypeStruct(state.shape, F32)],
        grid=(n_heads,), in_specs=[col, col, row, _const_spec(gbt), st], out_specs=[row, st],
        compiler_params=_params(), name="delta_step")(q.T, k.T, v, gbt, state)


def _a_out_body(o_ref, z_ref, res_ref, gout_ref, w_ref, out_ref):
    parts = []
    for h in range(o_ref.shape[1] // HEAD):
        hs = slice(HEAD * h, HEAD * (h + 1))
        parts.append(_bf(_rms(o_ref[:, hs]) * gout_ref[...] * _silu(z_ref[:, hs])))
    out_ref[...] = res_ref[...] + _dot(jnp.concatenate(parts, axis=1), w_ref[...])


def _ffn_body(*refs, tm, per_row_state):
    if per_row_state:
        (h_ref, p_ref, buf_ref, gffn_ref, wup_ref, wconv_ref, wdown_ref, gple_ref, wgate_ref, wproj_ref,
         out_ref, tail_ref, act_s) = refs
    else:
        (h_ref, p_ref, gffn_ref, wup_ref, wconv_ref, wdown_ref, gple_ref, wgate_ref, wproj_ref,
         out_ref, tail_ref, ubuf, act_s) = refs

        @pl.when(pl.program_id(0) == 0)
        def _():
            ubuf[0:SUBLANES, :] = jnp.zeros((SUBLANES, ubuf.shape[1]), F32)

    h = h_ref[...]
    hn = _bf(_rms(h) * gffn_ref[...])
    dff = wdown_ref.shape[0]
    for c in range(dff // MXU_COLS):
        ys = []
        for half in range(2):
            lo = half * dff + MXU_COLS * c
            cols = slice(lo, lo + MXU_COLS)
            u = _dot(hn, wup_ref[:, cols])
            w = wconv_ref[:, cols]
            if per_row_state:
                y = buf_ref[:, cols] * w[0:1]
                y = y + buf_ref[:, 2 * dff + lo:2 * dff + lo + MXU_COLS] * w[1:2]
                y = y + u * w[2:3]
                tail_ref[:, cols] = u
            else:
                y = _causal_conv_tile(u, w, ubuf, cols)
                tail_ref[:, cols] = u[tm - SUBLANES:, :]
            ys.append(y)
        act_s[:, MXU_COLS * c:MXU_COLS * (c + 1)] = _bf(_silu(ys[0]) * ys[1])
    h2 = h + _dot(act_s[...], wdown_ref[...])
    gate = _sigmoid(_dot(_bf(_rms(h2) * gple_ref[...]), wgate_ref[...]))
    out_ref[...] = h2 + gate * _dot(_bf(p_ref[...]), wproj_ref[...])


def _ffn(h, p, w, layer, tm, state):
    m = h.shape[0]
    per_row = state is not None
    dff = w["w_down"][layer].shape[0]
    consts = [w["g_ffn"][layer], w["w_up"][layer], w["w_fconv"][layer], w["w_down"][layer],
              w["g_ple"][layer], w["w_gate"][layer], w["w_proj"][layer]]
    rows = [h, p] + ([state.reshape(m, -1)] if per_row else [])
    scratch = ([] if per_row else [pltpu.VMEM((SUBLANES, 2 * dff), F32)]) + [pltpu.VMEM((tm, dff), BF16)]
    out, tail = _row_call(functools.partial(_ffn_body, tm=tm, per_row_state=per_row), f"ffn{layer}", m, tm,
                          rows, consts, [(h.shape[1], F32)],
                          const_outs=[((m if per_row else SUBLANES, 2 * dff), F32)], scratch=scratch)
    return out, tail


def _kvq_body(*refs, tm, dils):
    h_ref, cos_ref, sa_ref, sb_ref, gkv_ref, gmix_ref, wkv_ref, wq_ref, gk_ref, gq_ref = refs[:10]
    if dils is None:
        q_ref, k_ref, v_ref = refs[10:]
    else:
        ng = len(dils)
        k_ref, v_ref = refs[10:12]
        qd, kd, vd = refs[12:12 + ng], refs[12 + ng:12 + 2 * ng], refs[12 + 2 * ng:12 + 3 * ng]
        stage = refs[12 + 3 * ng]
    h = h_ref[...]
    base = _rms(h)
    hkv = _bf(base * gkv_ref[...])
    hq = _bf(base * gmix_ref[...])
    cos, sa, sb = cos_ref[...], sa_ref[...], sb_ref[...]
    kw = KV_PER_GROUP * HEAD

    def norm_rope(x, g):
        y = _rms(x) * g
        return y * cos + pltpu.roll(y, ROPE_DIM // 2, 1) * sa + pltpu.roll(y, HEAD - ROPE_DIM // 2, 1) * sb

    def heads(src, w_ref, c, g):
        x = _dot(src, w_ref[:, MXU_COLS * c:MXU_COLS * (c + 1)])
        return jnp.concatenate([norm_rope(x[:, HEAD * j:HEAD * (j + 1)], g) for j in range(MXU_COLS // HEAD)], axis=1)

    def by_class(x, dst_ref, dil, cols):
        if dil == 1:
            dst_ref[0, :, cols] = _bf(x)
        else:
            for j in range(MXU_COLS // LANES):
                stage[j] = x[:, LANES * j:LANES * (j + 1)]
            for r in range(dil):
                for j in range(MXU_COLS // LANES):
                    lo = cols.start + LANES * j
                    dst_ref[r, :, lo:lo + LANES] = _bf(stage.at[j][pl.ds(r, tm // dil, stride=dil), :])

    assert kw == MXU_COLS and Q_PER_GROUP * HEAD == 2 * MXU_COLS
    n_kv = k_ref.shape[1]
    for c in range(n_kv // MXU_COLS):
        x = heads(hkv, wkv_ref, c, gk_ref[...])
        k_ref[:, MXU_COLS * c:MXU_COLS * (c + 1)] = x
        if dils is not None:
            by_class(x, kd[c], dils[c], slice(0, kw))
    vv = _dot(hkv, wkv_ref[:, n_kv:])
    v_ref[...] = vv
    if dils is not None:
        for c in range(n_kv // MXU_COLS):
            by_class(vv[:, MXU_COLS * c:MXU_COLS * (c + 1)], vd[c], dils[c], slice(0, kw))
    for c in range(wq_ref.shape[1] // MXU_COLS):
        x = heads(hq, wq_ref, c, gq_ref[...])
        if dils is None:
            q_ref[:, MXU_COLS * c:MXU_COLS * (c + 1)] = x
        else:
            by_class(x, qd[c // 2], dils[c // 2], slice(MXU_COLS * (c % 2), MXU_COLS * (c % 2 + 1)))


def _kvq(h, tables, w, tm, dils):
    m, d = h.shape
    n_q, n_kv = w["w_q"].shape[1], w["w_kv"].shape[1] // 2
    qw, kw = Q_PER_GROUP * HEAD, KV_PER_GROUP * HEAD
    consts = [w["g_kv"], w["g_mix"][1], w["w_kv"], w["w_q"], w["g_k"], w["g_q"]]
    rows = [h, *tables]
    row_spec = lambda c: pl.BlockSpec((tm, c), lambda i: (i, 0))
    in_specs = [row_spec(a.shape[1]) for a in rows] + [_const_spec(a) for a in consts]
    nat = lambda c: jax.ShapeDtypeStruct((m, c), F32)
    if dils is None:
        out_shape = [nat(n_q), nat(n_kv), nat(n_kv)]
        out_specs, scratch = [row_spec(n_q), row_spec(n_kv), row_spec(n_kv)], []
    else:
        out_shape, out_specs = [nat(n_kv), nat(n_kv)], [row_spec(n_kv), row_spec(n_kv)]
        for width in (qw, kw, kw):
            for dil in dils:
                assert tm % (dil * 2 * SUBLANES) == 0
                out_shape.append(jax.ShapeDtypeStruct((dil, m // dil, width), BF16))
                out_specs.append(pl.BlockSpec((dil, tm // dil, width), lambda i: (0, i, 0)))
        scratch = [pltpu.VMEM((MXU_COLS // LANES, tm, LANES), F32)]
    return pl.pallas_call(functools.partial(_kvq_body, tm=tm, dils=dils), out_shape=out_shape, grid=(m // tm,),
                          in_specs=in_specs, out_specs=out_specs, scratch_shapes=scratch,
                          compiler_params=_params(), name="kvq")(*rows, *consts)


def _rope_tables(pos):
    half = ROPE_DIM // 2
    inv_freq = ROPE_THETA ** (-jnp.arange(half, dtype=F32) * 2.0 / ROPE_DIM)
    ang = pos.astype(F32)[:, None] * inv_freq[None, :]
    cos, sin = jnp.cos(ang), jnp.sin(ang)
    m = pos.shape[0]
    zero = jnp.zeros((m, half), F32)
    rest0 = jnp.zeros((m, HEAD - ROPE_DIM), F32)
    return (jnp.concatenate([cos, cos, jnp.ones((m, HEAD - ROPE_DIM), F32)], axis=1),
            jnp.concatenate([zero, sin, rest0], axis=1),
            jnp.concatenate([-sin, zero, rest0], axis=1))


def _attn_prompt_body(q_ref, kc_ref, kp_ref, vc_ref, vp_ref, o_ref, lse_ref, *, n, blocks):
    first_lo = jnp.where(pl.program_id(1) > 0, 0, n)
    qi = lax.broadcasted_iota(jnp.int32, (Q_REP * n, 2 * n), 0) & (n - 1)
    kj = lax.broadcasted_iota(jnp.int32, (Q_REP * n, 2 * n), 1)
    band = (kj >= qi) & (kj <= qi + n)
    lane = lax.broadcasted_iota(jnp.int32, (n, LANES), 1)
    for j in range(blocks):
        rows = slice(n * j, n * (j + 1))
        lse_tile = jnp.zeros((n, LANES), F32)
        for g in range(KV_PER_GROUP):
            hs = slice(HEAD * g, HEAD * (g + 1))
            if j == 0:
                kprev, vprev = kp_ref[:, hs], vp_ref[:, hs]
                valid = band & (kj >= first_lo)
            else:
                prev = slice(n * (j - 1), n * j)
                kprev, vprev = kc_ref[prev, hs], vc_ref[prev, hs]
                valid = band
            kwin = jnp.concatenate([kprev, kc_ref[rows, hs]], axis=0)
            vwin = jnp.concatenate([vprev, vc_ref[rows, hs]], axis=0)
            q0 = Q_REP * HEAD * g
            qg = jnp.concatenate([q_ref[rows, q0 + HEAD * e:q0 + HEAD * (e + 1)] for e in range(Q_REP)], axis=0)
            s = jnp.where(valid, _dot_nt(qg, kwin) * (HEAD ** -0.5), -jnp.inf)
            mx = jnp.max(s, axis=-1, keepdims=True)
            p = jnp.exp(s - mx)
            den = jnp.sum(p, axis=-1, keepdims=True)
            o = _dot(_bf(p), vwin) / den
            lse = mx + jnp.log(den)
            for e in range(Q_REP):
                o_ref[rows, q0 + HEAD * e:q0 + HEAD * (e + 1)] = o[n * e:n * (e + 1)]
                lse_tile = jnp.where(lane == Q_REP * g + e, lse[n * e:n * (e + 1)], lse_tile)
        lse_ref[rows, :] = lse_tile


def _attn_prompt(qd, kd, vd, gi, window):
    dil, ln, qw = qd.shape
    kw = kd.shape[2]
    n = window // dil
    blocks = min(8, ln // n)
    rows = n * blocks
    assert n == HEAD and ln % rows == 0
    cur = lambda r, b: (r, b, 0)
    prev = lambda r, b: (r, jnp.maximum(b * blocks - 1, 0), 0)
    return pl.pallas_call(
        functools.partial(_attn_prompt_body, n=n, blocks=blocks),
        out_shape=[jax.ShapeDtypeStruct((dil, ln, qw), F32), jax.ShapeDtypeStruct((dil, ln, LANES), F32)],
        grid=(dil, ln // rows),
        in_specs=[pl.BlockSpec((None, rows, qw), cur), pl.BlockSpec((None, rows, kw), cur),
                  pl.BlockSpec((None, n, kw), prev), pl.BlockSpec((None, rows, kw), cur),
                  pl.BlockSpec((None, n, kw), prev)],
        out_specs=[pl.BlockSpec((None, rows, qw), cur), pl.BlockSpec((None, rows, LANES), cur)],
        compiler_params=_params(2), name=f"attn_prompt{gi}")(qd, kd, kd, vd, vd)


def _attn_sample_body(q_ref, kn_ref, vn_ref, *refs, bb):
    n_groups = len(GROUPS)
    caches, outs, lses = refs[:n_groups], refs[n_groups:2 * n_groups], refs[2 * n_groups:]
    lane = lax.broadcasted_iota(jnp.int32, (1, LANES), 1)
    kw = KV_PER_GROUP * HEAD
    for gi in range(n_groups):
        c_ref, o_ref, l_ref = caches[gi], outs[gi], lses[gi]
        for b in range(bb):
            lrow = jnp.zeros((1, LANES), F32)
            for g in range(KV_PER_GROUP):
                kc = c_ref[b, :, 0, g, :]
                vc = c_ref[b, :, 1, g, :]
                kcol = kw * gi + HEAD * g
                kn = kn_ref[b:b + 1, kcol:kcol + HEAD]
                vn = vn_ref[b:b + 1, kcol:kcol + HEAD]
                for e in range(Q_REP):
                    oc = HEAD * (Q_REP * g + e)
                    qv = q_ref[b:b + 1, Q_PER_GROUP * HEAD * gi + oc:Q_PER_GROUP * HEAD * gi + oc + HEAD]
                    s = jnp.sum(kc * qv, axis=1, keepdims=True) * (HEAD ** -0.5)
                    sn = jnp.sum(kn * qv, axis=1, keepdims=True) * (HEAD ** -0.5)
                    mx = jnp.maximum(jnp.max(s, axis=0, keepdims=True), sn)
                    p = jnp.exp(s - mx)
                    pn = jnp.exp(sn - mx)
                    den = jnp.sum(p, axis=0, keepdims=True) + pn
                    o_ref[b:b + 1, oc:oc + HEAD] = (jnp.sum(p * vc, axis=0, keepdims=True) + pn * vn) / den
                    lrow = jnp.where(lane == Q_REP * g + e, mx + jnp.log(den), lrow)
            l_ref[b:b + 1, :] = lrow


def _attn_sample(q, k, v, caches):
    n_seq = q.shape[0]
    bb = SUBLANES
    assert n_seq % bb == 0
    qw, kw = Q_PER_GROUP * HEAD, KV_PER_GROUP * HEAD
    views, cache_specs = [], []
    for cache, (window, dil) in zip(caches, GROUPS):
        n = window // dil
        assert cache.shape[1] == window and n == HEAD
        views.append(cache.reshape(n_seq, n, dil, *cache.shape[2:]))
        cache_specs.append(pl.BlockSpec((bb, n, None, *cache.shape[2:]), lambda i: (i, 0, 0, 0, 0, 0)))
    row = lambda c: pl.BlockSpec((bb, c), lambda i: (i, 0))
    n_groups = len(GROUPS)
    res = pl.pallas_call(
        functools.partial(_attn_sample_body, bb=bb),
        out_shape=[jax.ShapeDtypeStruct((n_seq, qw), F32)] * n_groups
        + [jax.ShapeDtypeStruct((n_seq, LANES), F32)] * n_groups,
        grid=(n_seq // bb,),
        in_specs=[row(q.shape[1]), row(k.shape[1]), row(v.shape[1])] + cache_specs,
        out_specs=[row(qw)] * n_groups + [row(LANES)] * n_groups,
        compiler_params=_params(), name="attn_sample")(q, k, v, *views)
    return res[:n_groups], res[n_groups:]


def _o_proj_body(*refs, tm, dils):
    ng = len(dils)
    o_refs, l_refs = refs[:ng], refs[ng:2 * ng]
    res_ref, w_ref, out_ref = refs[2 * ng:2 * ng + 3]
    scratch = list(refs[2 * ng + 3:])
    get_o, lses = [], []
    for gi, dil in enumerate(dils):
        if dil == 1:
            get_o.append(lambda hh, r=o_refs[gi]: r[0, :, HEAD * hh:HEAD * (hh + 1)])
            lses.append(l_refs[gi][0])
        else:
            o_s, l_s = scratch.pop(0), scratch.pop(0)
            for r in range(dil):
                dst = pl.ds(r, tm // dil, stride=dil)
                for hh in range(Q_PER_GROUP):
                    o_s.at[hh][dst, :] = o_refs[gi][r, :, HEAD * hh:HEAD * (hh + 1)]
                l_s[dst, :] = l_refs[gi][r]
            get_o.append(lambda hh, r=o_s: r[hh])
            lses.append(l_s[...])
    parts = []
    for hh in range(Q_PER_GROUP):
        lh = [l[:, hh:hh + 1] for l in lses]
        mx = functools.reduce(jnp.maximum, lh)
        ex = [jnp.exp(x - mx) for x in lh]
        den = functools.reduce(lambda a, b: a + b, ex)
        acc = (ex[0] / den) * get_o[0](hh)
        for gi in range(1, ng):
            acc = acc + (ex[gi] / den) * get_o[gi](hh)
        parts.append(_bf(acc))
    out_ref[...] = res_ref[...] + _dot(jnp.concatenate(parts, axis=1), w_ref[...])


def _o_proj(outs, lses, res, w_o, tm, dils):
    m, d = res.shape
    qw = Q_PER_GROUP * HEAD
    grouped = lambda a: pl.BlockSpec((a.shape[0], tm // a.shape[0], a.shape[2]), lambda i: (0, i, 0))
    row_spec = pl.BlockSpec((tm, d), lambda i: (i, 0))
    scratch = []
    for dil in dils:
        if dil > 1:
            assert tm % (dil * SUBLANES) == 0
            scratch += [pltpu.VMEM((qw // HEAD, tm, HEAD), F32), pltpu.VMEM((tm, LANES), F32)]
    return pl.pallas_call(
        functools.partial(_o_proj_body, tm=tm, dils=dils), out_shape=jax.ShapeDtypeStruct((m, d), F32),
        grid=(m // tm,), in_specs=[grouped(a) for a in (*outs, *lses)] + [row_spec, _const_spec(w_o)],
        out_specs=row_spec, scratch_shapes=scratch, compiler_params=_params(), name="o_proj")(*outs, *lses, res, w_o)


def _trunk(x, p, pos, state, w, tm, tm_ffn):
    m, d = x.shape
    per_row = state is not None
    n_heads = w["n_heads"]
    aw = w["w_z"].shape[1]

    consts = [w["g_mix"][0], w["w_qkv"], w["w_ab"], w["w_z"], w["w_aconv"], w["a_log"], w["dt_bias"]]
    if per_row:
        consts.append(jnp.swapaxes(state["qkv_conv"], 0, 1))
    q, k, v, z, gb, qkv_tail = _row_call(
        functools.partial(_a_in_body, tm=tm, n_heads=n_heads, per_row_state=per_row), "a_in", m, tm, [x], consts,
        [(aw, F32)] * 4 + [(LANES, F32)], const_outs=[((m if per_row else SUBLANES, 3 * aw), F32)],
        scratch=[] if per_row else [pltpu.VMEM((SUBLANES, 3 * aw), F32)])
    if per_row:
        o, s1 = _delta_step(q, k, v, gb, state["delta"], n_heads)
    else:
        o, s1 = _delta_prompt(q, k, v, gb, n_heads)
    (h,) = _row_call(_a_out_body, "a_out", m, tm, [o, z, x], [w["g_aout"], w["w_aout"]], [(d, F32)])
    h, ffn_tail0 = _ffn(h, p[0], w, 0, tm_ffn, state["ffn_conv"][0] if per_row else None)

    tables = _rope_tables(pos)
    if per_row:
        qa, ka, va = _kvq(h, tables, w, tm, None)
        outs, lses = _attn_sample(qa, ka, va, state["caches"])
        outs, lses = [a[None] for a in outs], [a[None] for a in lses]
        dils = (1,) * len(GROUPS)
    else:
        dils = tuple(dil for _, dil in GROUPS)
        ng = len(dils)
        res = _kvq(h, tables, w, tm, dils)
        ka, va = res[:2]
        qd, kd, vd = res[2:2 + ng], res[2 + ng:2 + 2 * ng], res[2 + 2 * ng:]
        att = [_attn_prompt(qd[gi], kd[gi], vd[gi], gi, window) for gi, (window, _) in enumerate(GROUPS)]
        outs, lses = [a[0] for a in att], [a[1] for a in att]
    h = _o_proj(outs, lses, h, w["w_o"], tm, dils)
    h, ffn_tail1 = _ffn(h, p[1], w, 1, tm_ffn, state["ffn_conv"][1] if per_row else None)
    return h, s1, qkv_tail, (ffn_tail0, ffn_tail1), ka, va


def _kv_rows(k, v, gi, n_rows):
    kw = KV_PER_GROUP * HEAD
    sl = slice(kw * gi, kw * (gi + 1))
    shape = (n_rows, 1, KV_PER_GROUP, HEAD)
    return jnp.concatenate([k[k.shape[0] - n_rows:, sl].reshape(shape), v[v.shape[0] - n_rows:, sl].reshape(shape)],
                           axis=1)


def kernel(x_prompt, x_sample, p_prompt, p_sample, state_delta, state_qkv_conv, state_ffn_conv, cache_kv_w128, cache_kv_w512, cache_kv_w2048, g_mix_norm, g_ffn_norm, w_ffn_up, w_ffn_conv, w_ffn_down, g_ple_norm, w_ple_gate, w_ple_proj, w_a_in, w_a_conv, a_log, a_dt_bias, g_a_out_norm, w_a_out, g_kv_norm, w_kv, g_k_norm, w_q, g_q_norm, w_o):
    n_heads = a_log.shape[1]
    aw = n_heads * HEAD
    assert w_a_in.shape[0] == 1 and w_q.shape[0] == 1 and w_a_in.shape[2] == 4 * aw + 2 * n_heads
    row = lambda a: a.reshape(1, -1)
    lane_pad = lambda a: jnp.pad(a, ((0, 0), (0, LANES - a.shape[1])))
    w = {
        "n_heads": n_heads,
        "g_mix": [row(g) for g in g_mix_norm], "g_ffn": [row(g) for g in g_ffn_norm],
        "g_ple": [row(g) for g in g_ple_norm],
        "w_up": _bf(w_ffn_up), "w_fconv": w_ffn_conv, "w_down": _bf(w_ffn_down),
        "w_gate": _bf(w_ple_gate), "w_proj": _bf(w_ple_proj),
        "w_qkv": _bf(w_a_in[0][:, :3 * aw]), "w_ab": _bf(lane_pad(w_a_in[0][:, 3 * aw:3 * aw + 2 * n_heads])),
        "w_z": _bf(w_a_in[0][:, 3 * aw + 2 * n_heads:]), "w_aconv": w_a_conv[0],
        "a_log": lane_pad(a_log), "dt_bias": lane_pad(a_dt_bias),
        "g_aout": row(g_a_out_norm[0]), "w_aout": _bf(w_a_out[0]),
        "g_kv": row(g_kv_norm), "w_kv": _bf(w_kv), "g_k": row(g_k_norm),
        "w_q": _bf(w_q[0]), "g_q": row(g_q_norm[0]), "w_o": _bf(w_o[0]),
    }
    assert x_prompt.shape[0] == 1 and x_sample.shape[1] == 1
    seq, d = x_prompt.shape[1], x_prompt.shape[2]
    n_seq = x_sample.shape[0]

    yp, sd_p, qkv_tail_p, ffn_tails_p, kp, vp = _trunk(
        x_prompt.reshape(seq, d), p_prompt[:, 0], jnp.arange(seq, dtype=jnp.int32), None, w, tm=256, tm_ffn=512)
    a_keep, f_keep = state_qkv_conv.shape[2], state_ffn_conv.shape[2]
    sq_p = qkv_tail_p[SUBLANES - a_keep:][None, None]
    sf_p = jnp.stack([t[SUBLANES - f_keep:] for t in ffn_tails_p])[:, None]
    kv_p = [_kv_rows(kp, vp, gi, min(window, seq))[None] for gi, (window, _) in enumerate(GROUPS)]

    state = {"delta": state_delta[0], "qkv_conv": state_qkv_conv[0], "ffn_conv": state_ffn_conv,
             "caches": (cache_kv_w128, cache_kv_w512, cache_kv_w2048)}
    ys, sd_s, qkv_tail_s, ffn_tails_s, ks, vs = _trunk(
        x_sample.reshape(n_seq, d), p_sample[:, :, 0], jnp.full((n_seq,), PAST_LEN, jnp.int32), state, w, tm=n_seq, tm_ffn=n_seq)
    sq_s = jnp.concatenate([state_qkv_conv[0][:, 1:], qkv_tail_s[:, None]], axis=1)[None]
    sf_s = jnp.stack([jnp.concatenate([state_ffn_conv[i][:, 1:], ffn_tails_s[i][:, None]], axis=1)
                      for i in range(2)])
    kv_s = [_kv_rows(ks, vs, gi, n_seq).reshape(n_seq, 1, 2, KV_PER_GROUP, HEAD) for gi in range(len(GROUPS))]

    return (yp[None], ys[:, None], sd_p[None, None], sq_p, sf_p, kv_p[0], kv_p[1], kv_p[2],
            sd_s[None], sq_s, sf_s, kv_s[0], kv_s[1], kv_s[2])
```

```python
import functools

import jax
import jax.numpy as jnp
from jax import lax
from jax.experimental import pallas as pl
from jax.experimental.pallas import tpu as pltpu

F32 = jnp.float32
BF16 = jnp.bfloat16

EPS = 1e-6
HEAD = 128
CHUNK = 64
GROUPS = ((128, 1), (512, 4), (2048, 16))
KV_PER_GROUP = 2
Q_REP = 2
Q_PER_GROUP = KV_PER_GROUP * Q_REP
ROPE_DIM = HEAD // 4
ROPE_THETA = 500000.0
PAST_LEN = 16384

LANES = 128
SUBLANES = 8
MXU_COLS = 256
VMEM_LIMIT_BYTES = 56 * 1024 * 1024

PREP_CHUNKS = 4


def _dot(a, b):
    return jnp.dot(a, b, preferred_element_type=F32)


def _dot_nt(a, b):
    return lax.dot_general(a, b, (((1,), (1,)), ((), ())), preferred_element_type=F32)


def _bf(x):
    return x.astype(BF16)


def _rms(x):
    return x * lax.rsqrt(jnp.mean(x * x, axis=-1, keepdims=True) + EPS)


def _sigmoid(x):
    return 1.0 / (1.0 + jnp.exp(-x))


def _silu(x):
    return x * _sigmoid(x)


def _split3f(x):
    hi = _bf(x).astype(F32)
    r = x - hi
    mid = _bf(r).astype(F32)
    return hi, mid, _bf(r - mid).astype(F32)


def _causal_conv_tile(x, w, carry_ref, cols):
    rows = x.shape[0]
    first = lax.broadcasted_iota(jnp.int32, (SUBLANES, x.shape[1]), 0) == 0
    z = x * w[0:1]
    for j in range(1, w.shape[0]):
        rolled = pltpu.roll(z, 1, 0)
        head = jnp.where(first, carry_ref[j - 1:j, cols], rolled[0:SUBLANES])
        carry_ref[j - 1:j, cols] = z[rows - 1:, :]
        z = jnp.concatenate([head, rolled[SUBLANES:]], axis=0) + x * w[j:j + 1]
    return z


def _const_spec(a):
    nd = a.ndim
    return pl.BlockSpec(a.shape, lambda *_: (0,) * nd, pipeline_mode=pl.Buffered(1))


def _params(n_axes=1):
    return pltpu.CompilerParams(dimension_semantics=("arbitrary",) * n_axes,
                                vmem_limit_bytes=VMEM_LIMIT_BYTES)


def _row_call(body, name, m, tm, row_ins, const_ins, row_outs, const_outs=(), scratch=()):
    assert m % tm == 0
    in_specs = [pl.BlockSpec((tm, a.shape[1]), lambda i: (i, 0)) for a in row_ins]
    in_specs += [_const_spec(a) for a in const_ins]
    out_shape = [jax.ShapeDtypeStruct((m, c), dt) for c, dt in row_outs]
    out_specs = [pl.BlockSpec((tm, c), lambda i: (i, 0)) for c, _ in row_outs]
    for shape, dt in const_outs:
        out_shape.append(jax.ShapeDtypeStruct(shape, dt))
        out_specs.append(pl.BlockSpec(shape, lambda i, _n=len(shape): (0,) * _n))
    return pl.pallas_call(body, out_shape=out_shape, grid=(m // tm,), in_specs=in_specs, out_specs=out_specs,
                          scratch_shapes=list(scratch), compiler_params=_params(), name=name)(*row_ins, *const_ins)


def _call(body, name, steps, ins, outs, scratch=()):
    return pl.pallas_call(body, out_shape=[o for o, _ in outs], grid=(steps,), in_specs=[sp for _, sp in ins],
                          out_specs=[sp for _, sp in outs], scratch_shapes=list(scratch),
                          compiler_params=_params(), name=name)(*[a for a, _ in ins])


def _rows(a, tm):
    return a, pl.BlockSpec((tm, a.shape[1]), lambda i: (i, 0))


def _whole(a):
    return a, _const_spec(a)


def _layer(a, layer):
    nd = a.ndim
    return a, pl.BlockSpec((None,) + a.shape[1:], lambda i: (layer,) + (0,) * (nd - 1), pipeline_mode=pl.Buffered(1))


def _layer_rows(a, layer, tm):
    return a, pl.BlockSpec((None, tm, a.shape[2]), lambda i: (layer, i, 0))


def _grouped(a, tm):
    return a, pl.BlockSpec((a.shape[0], tm // a.shape[0], a.shape[2]), lambda i: (0, i, 0))


def _out_rows(m, c, tm):
    return jax.ShapeDtypeStruct((m, c), F32), pl.BlockSpec((tm, c), lambda i: (i, 0))


def _out_whole(shape):
    return jax.ShapeDtypeStruct(shape, F32), pl.BlockSpec(shape, lambda i: (0,) * len(shape))


def _a_in_body(*refs, tm, n_heads, per_row_state):
    if per_row_state:
        (x_ref, g_ref, wqkv_ref, wab_ref, wz_ref, wconv_ref, alog_ref, dtb_ref, buf_ref,
         q_ref, k_ref, v_ref, z_ref, gb_ref, tail_ref) = refs
    else:
        (x_ref, g_ref, wqkv_ref, wab_ref, wz_ref, wconv_ref, alog_ref, dtb_ref,
         q_ref, k_ref, v_ref, z_ref, gb_ref, tail_ref, cbuf) = refs

        @pl.when(pl.program_id(0) == 0)
        def _():
            cbuf[0:SUBLANES, :] = jnp.zeros((SUBLANES, cbuf.shape[1]), F32)

    hn = _bf(_rms(x_ref[...]) * g_ref[...])
    outs = (q_ref, k_ref, v_ref)
    width = q_ref.shape[1]
    per_part = width // MXU_COLS
    for c in range(3 * per_part):
        cols = slice(MXU_COLS * c, MXU_COLS * (c + 1))
        pq = _dot(hn, wqkv_ref[:, cols])
        w = wconv_ref[:, cols]
        if per_row_state:
            y = buf_ref[0, :, cols] * w[0:1]
            y = y + buf_ref[1, :, cols] * w[1:2]
            y = y + buf_ref[2, :, cols] * w[2:3]
            y = y + pq * w[3:4]
            tail_ref[:, cols] = pq
        else:
            y = _causal_conv_tile(pq, w, cbuf, cols)
            tail_ref[:, cols] = pq[tm - SUBLANES:, :]
        a = _silu(y)
        part, sub = divmod(c, per_part)
        if part < 2:
            halves = []
            for hh in range(MXU_COLS // HEAD):
                ah = a[:, HEAD * hh:HEAD * (hh + 1)]
                nrm = ah * lax.rsqrt(jnp.sum(ah * ah, axis=-1, keepdims=True) + EPS)
                if part == 0:
                    nrm = nrm * (HEAD ** -0.5)
                halves.append(nrm)
            a = jnp.concatenate(halves, axis=1)
        outs[part][:, MXU_COLS * sub:MXU_COLS * (sub + 1)] = a

    pab = _dot(hn, wab_ref[...])
    lane = lax.broadcasted_iota(jnp.int32, pab.shape, 1)
    ap = pab + dtb_ref[...]
    softplus = jnp.maximum(ap, 0.0) + jnp.log1p(jnp.exp(-jnp.abs(ap)))
    g = -jnp.exp(alog_ref[...]) * softplus
    gb_ref[...] = jnp.where(lane < n_heads, g, _sigmoid(pab))
    z_ref[...] = _dot(hn, wz_ref[...])


def _delta_prompt_body(q_ref, k_ref, v_ref, gb_ref, o_ref, sout_ref,
                       s_scr, u0_s, wkqg_s, qk_s, kdt_s, gl_s, *, n_heads, n_chunks):
    @pl.when(pl.program_id(0) == 0)
    def _():
        s_scr[...] = jnp.zeros(s_scr.shape, F32)

    group = MXU_COLS // CHUNK
    shift = CHUNK.bit_length() - 1
    lane_p = lax.broadcasted_iota(jnp.int32, (CHUNK, MXU_COLS), 1)
    row_p = lax.broadcasted_iota(jnp.int32, (CHUNK, MXU_COLS), 0)
    col_p = lane_p & (CHUNK - 1)
    blk = [(lane_p >> shift) == u for u in range(group)]
    incl_p = row_p >= col_p
    strict_p = row_p > col_p
    eye_p = jnp.where(row_p == col_p, 1.0, 0.0).astype(F32)
    bd_r = lax.broadcasted_iota(jnp.int32, (MXU_COLS, MXU_COLS), 0)
    bd_c = lax.broadcasted_iota(jnp.int32, (MXU_COLS, MXU_COLS), 1)
    bd_mask = (bd_r >> shift) == (bd_c >> shift)
    ri = lax.broadcasted_iota(jnp.int32, (CHUNK, CHUNK), 0)
    ci = lax.broadcasted_iota(jnp.int32, (CHUNK, CHUNK), 1)
    ltri3 = jnp.concatenate([_bf(jnp.where(ri >= ci, 1.0, 0.0))] * 3, axis=1)
    g_r = lax.broadcasted_iota(jnp.int32, (MXU_COLS, 3 * LANES), 0) >> shift
    g_c = lax.broadcasted_iota(jnp.int32, (MXU_COLS, 3 * LANES), 1) & (LANES - 1)
    ones3 = jnp.ones((CHUNK, 3 * LANES), BF16)
    er = lax.broadcasted_iota(jnp.int32, (HEAD, HEAD), 0)
    ec = lax.broadcasted_iota(jnp.int32, (HEAD, HEAD), 1)
    eye_h = _bf(jnp.where(er == ec, 1.0, 0.0))
    heads = range(n_heads)
    hs = [slice(HEAD * h, HEAD * (h + 1)) for h in heads]

    def block_diag(x):
        return _bf(jnp.where(bd_mask, jnp.concatenate([x] * group, axis=0), 0.0))

    def pack_cols(cols):
        out = cols[-1]
        for u in reversed(range(group - 1)):
            out = jnp.where(blk[u], cols[u], out)
        return out

    def pack_diag(x):
        out = x[CHUNK * (group - 1):]
        for u in reversed(range(group - 1)):
            out = jnp.where(blk[u], x[CHUNK * u:CHUNK * (u + 1)], out)
        return out

    def stack(ref, rows, h0):
        return jnp.concatenate([ref[rows, hs[h]] for h in range(h0, h0 + group)], axis=0)

    def stack_cols(x, lo):
        return jnp.concatenate([x[:, lo + u:lo + u + 1] for u in range(group)], axis=0)

    def prep(i, carry):
        chunks = [i * PREP_CHUNKS + j for j in range(PREP_CHUNKS)]
        rows = [pl.ds(pl.multiple_of(c * CHUNK, CHUNK), CHUNK) for c in chunks]
        gbc = [gb_ref[r, :] for r in rows]
        gc = [_dot(ltri3, _bf(jnp.concatenate(_split3f(x), axis=0))) for x in gbc]
        glast = [x[CHUNK - 1:CHUNK, :] for x in gc]
        for j, c in enumerate(chunks):
            gl_s[c] = jnp.broadcast_to(glast[j], (SUBLANES, LANES))
        groups = [(j, h0) for j in range(PREP_CHUNKS) for h0 in range(0, n_heads, group)]
        ng = range(len(groups))
        kst = [stack(k_ref, rows[j], h0) for j, h0 in groups]
        qst = [stack(q_ref, rows[j], h0) for j, h0 in groups]
        kb = [_bf(x) for x in kst]
        pairs = [_dot_nt(jnp.concatenate([_bf(qst[g]), kb[g]], axis=0), kb[g]) for g in ng]
        gcol_p = [pack_cols([gc[j][:, h:h + 1] for h in range(h0, h0 + group)]) for j, h0 in groups]
        bcol_p = [pack_cols([gbc[j][:, n_heads + h:n_heads + h + 1] for h in range(h0, h0 + group)])
                  for j, h0 in groups]
        grow = [_dot_nt(ones3, _bf(jnp.where(g_c == g_r + h0,
                                               jnp.concatenate([jnp.concatenate([x] * group, axis=0)
                                                                for x in _split3f(gc[j])], axis=1), 0.0)))
                for j, h0 in groups]
        dec = [jnp.exp(jnp.where(incl_p, gcol_p[g] - grow[g], -jnp.inf)) for g in ng]
        a = [jnp.where(strict_p, bcol_p[g] * pack_diag(pairs[g][MXU_COLS:]) * dec[g], 0.0) for g in ng]
        tinv = [eye_p - x for x in a]
        pw = [_dot(_bf(-x), block_diag(-x)) for x in a]
        for _ in range(4):
            res = [_dot(jnp.concatenate([_bf(tinv[g]), _bf(pw[g])], axis=0), block_diag(pw[g])) for g in ng]
            tinv = [tinv[g] + res[g][:CHUNK] for g in ng]
            pw = [r[CHUNK:] for r in res]
        tinv = [tinv[g] + _dot(_bf(tinv[g]), block_diag(pw[g])) for g in ng]
        tbd = [block_diag(x) for x in tinv]
        gcol_s = [stack_cols(gc[j], h0) for j, h0 in groups]
        bcol_s = [stack_cols(gbc[j], n_heads + h0) for j, h0 in groups]
        gam_s = [jnp.exp(x) for x in gcol_s]
        rhs = [jnp.concatenate([stack(v_ref, rows[j], h0) * bcol_s[g], kst[g] * (bcol_s[g] * gam_s[g])], axis=1)
               for g, (j, h0) in enumerate(groups)]
        x0 = [_dot(tbd[g], _bf(rhs[g])) for g in ng]
        resid = [rhs[g] - x0[g] - _dot(block_diag(a[g]), _bf(x0[g])) for g in ng]
        sol = [x0[g] + _dot(tbd[g], _bf(resid[g])) for g in ng]
        kdt = [_dot_nt(eye_h, _bf(kst[g] * jnp.exp(stack_cols(jnp.broadcast_to(glast[j], (CHUNK, LANES)), h0)
                                                   - gcol_s[g])))
               for g, (j, h0) in enumerate(groups)]
        qk = [pack_diag(pairs[g][:MXU_COLS]) * dec[g] for g in ng]
        for g, (j, h0) in enumerate(groups):
            c = chunks[j]
            qg = qst[g] * gam_s[g]
            for u in range(group):
                r = slice(CHUNK * u, CHUNK * (u + 1))
                u0_s[c, h0 + u] = sol[g][r, :HEAD]
                wkqg_s[c, h0 + u] = _bf(jnp.concatenate([sol[g][r, HEAD:], qg[r]], axis=0))
                qk_s[c, h0 + u] = _bf(qk[g][:, r])
                kdt_s[c, h0 + u] = _bf(kdt[g][:, r])
        return carry

    lax.fori_loop(0, n_chunks // PREP_CHUNKS, prep, 0)

    def scan(c, carry):
        rows = pl.ds(pl.multiple_of(c * CHUNK, CHUNK), CHUNK)
        glr = gl_s[c]
        ws = [_dot(wkqg_s[c, h], _bf(s_scr[h])) for h in heads]
        ub = [_bf(u0_s[c, h] - ws[h][:CHUNK]) for h in heads]
        for h in heads:
            o_ref[rows, HEAD * h:HEAD * (h + 1)] = ws[h][CHUNK:] + _dot(qk_s[c, h], ub[h])
        for h in heads:
            s_scr[h] = jnp.exp(glr[0:1, h:h + 1]) * s_scr[h] + _dot(kdt_s[c, h], ub[h])
        return carry

    lax.fori_loop(0, n_chunks, scan, 0)
    sout_ref[...] = s_scr[...]


def _delta_prompt(q, k, v, gb, n_heads):
    t = q.shape[0]
    n_chunks = 8
    tt = n_chunks * CHUNK
    scratch = [pltpu.VMEM((n_heads, HEAD, HEAD), F32),
               pltpu.VMEM((n_chunks, n_heads, CHUNK, HEAD), F32),
               pltpu.VMEM((n_chunks, n_heads, 2 * CHUNK, HEAD), BF16),
               pltpu.VMEM((n_chunks, n_heads, CHUNK, CHUNK), BF16),
               pltpu.VMEM((n_chunks, n_heads, HEAD, CHUNK), BF16),
               pltpu.VMEM((n_chunks, SUBLANES, LANES), F32)]
    return _row_call(functools.partial(_delta_prompt_body, n_heads=n_heads, n_chunks=n_chunks), "delta_prompt",
                     t, tt, [q, k, v, gb], [], [(q.shape[1], F32)],
                     const_outs=[((n_heads, HEAD, HEAD), F32)], scratch=scratch)


def _delta_step_body(qt_ref, kt_ref, v_ref, gbt_ref, s_ref, o_ref, so_ref, *, n_seq, n_heads):
    h = pl.program_id(0)
    decay = jnp.exp(gbt_ref[pl.ds(h, 1), :])
    beta = gbt_ref[pl.ds(n_heads + h, 1), :]
    for b in range(n_seq):
        s = decay[:, b:b + 1] * s_ref[b]
        kc = kt_ref[:, b:b + 1]
        u = beta[:, b:b + 1] * (v_ref[b:b + 1, :] - jnp.sum(kc * s, axis=0, keepdims=True))
        s = s + kc * u
        so_ref[b] = s
        o_ref[b:b + 1, :] = jnp.sum(qt_ref[:, b:b + 1] * s, axis=0, keepdims=True)


def _delta_step(q, k, v, gb, state, n_heads):
    n_seq = q.shape[0]
    col = pl.BlockSpec((HEAD, n_seq), lambda h: (h, 0))
    row = pl.BlockSpec((n_seq, HEAD), lambda h: (0, h))
    st = pl.BlockSpec((n_seq, None, HEAD, HEAD), lambda h: (0, h, 0, 0))
    gbt = gb.T
    return pl.pallas_call(
        functools.partial(_delta_step_body, n_seq=n_seq, n_heads=n_heads),
        out_shape=[jax.ShapeDtypeStruct(q.shape, F32), jax.ShapeDtypeStruct(state.shape, F32)],
        grid=(n_heads,), in_specs=[col, col, row, _const_spec(gbt), st], out_specs=[row, st],
        compiler_params=_params(), name="delta_step")(q.T, k.T, v, gbt, state)


def _delta_out_mix(o_ref, z_ref, res_ref, gout_ref, w_ref):
    parts = []
    for h in range(o_ref.shape[1] // HEAD):
        hs = slice(HEAD * h, HEAD * (h + 1))
        parts.append(_bf(_rms(o_ref[:, hs]) * gout_ref[...] * _silu(z_ref[:, hs])))
    return res_ref[...] + _dot(jnp.concatenate(parts, axis=1), w_ref[...])


def _attn_out_mix(refs, scratch, tm, dils):
    ng = len(dils)
    o_refs, l_refs = refs[:ng], refs[ng:2 * ng]
    res_ref, w_ref = refs[2 * ng:]
    get_o, lses = [], []
    for gi, dil in enumerate(dils):
        if dil == 1:
            get_o.append(lambda hh, r=o_refs[gi]: r[0, :, HEAD * hh:HEAD * (hh + 1)])
            lses.append(l_refs[gi][0])
        else:
            o_s, l_s = scratch.pop(0), scratch.pop(0)
            for r in range(dil):
                dst = pl.ds(r, tm // dil, stride=dil)
                for hh in range(Q_PER_GROUP):
                    o_s.at[hh][dst, :] = o_refs[gi][r, :, HEAD * hh:HEAD * (hh + 1)]
                l_s[dst, :] = l_refs[gi][r]
            get_o.append(lambda hh, r=o_s: r[hh])
            lses.append(l_s[...])
    parts = []
    for hh in range(Q_PER_GROUP):
        lh = [l[:, hh:hh + 1] for l in lses]
        mx = functools.reduce(jnp.maximum, lh)
        ex = [jnp.exp(x - mx) for x in lh]
        den = functools.reduce(lambda a, b: a + b, ex)
        acc = (ex[0] / den) * get_o[0](hh)
        for gi in range(1, ng):
            acc = acc + (ex[gi] / den) * get_o[gi](hh)
        parts.append(_bf(acc))
    return res_ref[...] + _dot(jnp.concatenate(parts, axis=1), w_ref[...])


def _mix_ffn_body(*refs, tm, per_row_state, dils):
    n_mix = 5 if dils is None else 2 * len(dils) + 2
    mix_in, refs = refs[:n_mix], refs[n_mix:]
    if per_row_state:
        p_ref, buf_ref, gffn_ref, wup_ref, wconv_ref, wdown_ref, gple_ref, wgate_ref, wproj_ref = refs[:9]
        out_ref, tail_ref = refs[9:11]
        scratch = list(refs[11:])
    else:
        p_ref, gffn_ref, wup_ref, wconv_ref, wdown_ref, gple_ref, wgate_ref, wproj_ref = refs[:8]
        out_ref, tail_ref = refs[8:10]
        scratch = list(refs[10:])
    act_s = scratch.pop()
    if not per_row_state:
        ubuf = scratch.pop()

        @pl.when(pl.program_id(0) == 0)
        def _():
            ubuf[...] = jnp.zeros(ubuf.shape, F32)

    h = _delta_out_mix(*mix_in) if dils is None else _attn_out_mix(mix_in, scratch, tm, dils)
    hn = _bf(_rms(h) * gffn_ref[...])
    dff = wdown_ref.shape[0]
    for c in range(dff // MXU_COLS):
        ys = []
        for half in range(2):
            lo = half * dff + MXU_COLS * c
            cols = slice(lo, lo + MXU_COLS)
            u = _dot(hn, wup_ref[:, cols])
            w = wconv_ref[:, cols]
            if per_row_state:
                y = buf_ref[:, cols] * w[0:1]
                y = y + buf_ref[:, 2 * dff + lo:2 * dff + lo + MXU_COLS] * w[1:2]
                y = y + u * w[2:3]
                tail_ref[:, cols] = u
            else:
                y = _causal_conv_tile(u, w, ubuf, cols)
                tail_ref[:, cols] = u[tm - SUBLANES:, :]
            ys.append(y)
        act_s[:, MXU_COLS * c:MXU_COLS * (c + 1)] = _bf(_silu(ys[0]) * ys[1])
    h2 = h + _dot(act_s[...], wdown_ref[...])
    gate = _sigmoid(_dot(_bf(_rms(h2) * gple_ref[...]), wgate_ref[...]))
    out_ref[...] = h2 + gate * _dot(_bf(p_ref[...]), wproj_ref[...])


def _mix_ffn(mix_ins, mix_scratch, dils, p, w, layer, m, tm, state):
    per_row = state is not None
    d, dff = w["w_down"].shape[2], w["w_down"].shape[1]
    ins = list(mix_ins) + [_layer_rows(p, layer, tm)] + ([_rows(state.reshape(m, -1), tm)] if per_row else [])
    ins += [_layer(w[name], layer) for name in ("g_ffn", "w_up", "w_fconv", "w_down", "g_ple", "w_gate", "w_proj")]
    outs = [_out_rows(m, d, tm), _out_whole((m if per_row else SUBLANES, 2 * dff))]
    scratch = list(mix_scratch) + ([] if per_row else [pltpu.VMEM((SUBLANES, 2 * dff), F32)])
    scratch.append(pltpu.VMEM((tm, dff), BF16))
    return _call(functools.partial(_mix_ffn_body, tm=tm, per_row_state=per_row, dils=dils), f"mix_ffn{layer}",
                 m // tm, ins, outs, scratch)


def _kvq_body(*refs, tm, dils):
    h_ref, tab_ref, gkv_ref, gmix_ref, wkv_ref, wq_ref, gk_ref, gq_ref = refs[:8]
    if dils is None:
        q_ref, k_ref, v_ref = refs[8:]
    else:
        ng = len(dils)
        k_ref, v_ref = refs[8:10]
        qd, kd, vd = refs[10:10 + ng], refs[10 + ng:10 + 2 * ng], refs[10 + 2 * ng:10 + 3 * ng]
        stage = refs[10 + 3 * ng]
    h = h_ref[...]
    base = _rms(h)
    hkv = _bf(base * gkv_ref[...])
    hq = _bf(base * gmix_ref[...])
    tab = tab_ref[...]
    lane = lax.broadcasted_iota(jnp.int32, tab.shape, 1)
    half = ROPE_DIM // 2
    cos = jnp.where(lane < half, tab, jnp.where(lane < ROPE_DIM, pltpu.roll(tab, half, 1), 1.0))
    sa = jnp.where((lane >= half) & (lane < ROPE_DIM), tab, 0.0)
    sb = jnp.where(lane < half, -pltpu.roll(tab, HEAD - half, 1), 0.0)
    kw = KV_PER_GROUP * HEAD

    def norm_rope(x, g):
        y = _rms(x) * g
        return y * cos + pltpu.roll(y, ROPE_DIM // 2, 1) * sa + pltpu.roll(y, HEAD - ROPE_DIM // 2, 1) * sb

    def heads(src, w_ref, c, g):
        x = _dot(src, w_ref[:, MXU_COLS * c:MXU_COLS * (c + 1)])
        return jnp.concatenate([norm_rope(x[:, HEAD * j:HEAD * (j + 1)], g) for j in range(MXU_COLS // HEAD)], axis=1)

    def by_class(x, dst_ref, dil, cols):
        if dil == 1:
            dst_ref[0, :, cols] = _bf(x)
        else:
            for j in range(MXU_COLS // LANES):
                stage[j] = x[:, LANES * j:LANES * (j + 1)]
            for r in range(dil):
                for j in range(MXU_COLS // LANES):
                    lo = cols.start + LANES * j
                    dst_ref[r, :, lo:lo + LANES] = _bf(stage.at[j][pl.ds(r, tm // dil, stride=dil), :])

    assert kw == MXU_COLS and Q_PER_GROUP * HEAD == 2 * MXU_COLS
    n_kv = k_ref.shape[1]
    for c in range(n_kv // MXU_COLS):
        x = heads(hkv, wkv_ref, c, gk_ref[...])
        k_ref[:, MXU_COLS * c:MXU_COLS * (c + 1)] = x
        if dils is not None:
            by_class(x, kd[c], dils[c], slice(0, kw))
    vv = _dot(hkv, wkv_ref[:, n_kv:])
    v_ref[...] = vv
    if dils is not None:
        for c in range(n_kv // MXU_COLS):
            by_class(vv[:, MXU_COLS * c:MXU_COLS * (c + 1)], vd[c], dils[c], slice(0, kw))
    for c in range(wq_ref.shape[1] // MXU_COLS):
        x = heads(hq, wq_ref, c, gq_ref[...])
        if dils is None:
            q_ref[:, MXU_COLS * c:MXU_COLS * (c + 1)] = x
        else:
            by_class(x, qd[c // 2], dils[c // 2], slice(MXU_COLS * (c % 2), MXU_COLS * (c % 2 + 1)))


def _kvq(h, table, w, tm, dils):
    m, d = h.shape
    n_q, n_kv = w["w_q"].shape[1], w["w_kv"].shape[1] // 2
    qw, kw = Q_PER_GROUP * HEAD, KV_PER_GROUP * HEAD
    consts = [w["g_kv"], w["g_mix"][1], w["w_kv"], w["w_q"], w["g_k"], w["g_q"]]
    rows = [h, table]
    row_spec = lambda c: pl.BlockSpec((tm, c), lambda i: (i, 0))
    in_specs = [row_spec(a.shape[1]) for a in rows] + [_const_spec(a) for a in consts]
    nat = lambda c: jax.ShapeDtypeStruct((m, c), F32)
    if dils is None:
        out_shape = [nat(n_q), nat(n_kv), nat(n_kv)]
        out_specs, scratch = [row_spec(n_q), row_spec(n_kv), row_spec(n_kv)], []
    else:
        out_shape, out_specs = [nat(n_kv), nat(n_kv)], [row_spec(n_kv), row_spec(n_kv)]
        for width in (qw, kw, kw):
            for dil in dils:
                assert tm % (dil * 2 * SUBLANES) == 0
                out_shape.append(jax.ShapeDtypeStruct((dil, m // dil, width), BF16))
                out_specs.append(pl.BlockSpec((dil, tm // dil, width), lambda i: (0, i, 0)))
        scratch = [pltpu.VMEM((MXU_COLS // LANES, tm, LANES), F32)]
    return pl.pallas_call(functools.partial(_kvq_body, tm=tm, dils=dils), out_shape=out_shape, grid=(m // tm,),
                          in_specs=in_specs, out_specs=out_specs, scratch_shapes=scratch,
                          compiler_params=_params(), name="kvq")(*rows, *consts)


def _rope_table(pos):
    half = ROPE_DIM // 2
    inv_freq = ROPE_THETA ** (-jnp.arange(half, dtype=F32) * 2.0 / ROPE_DIM)
    ang = pos.astype(F32)[:, None] * inv_freq[None, :]
    return jnp.concatenate([jnp.cos(ang), jnp.sin(ang), jnp.zeros((pos.shape[0], HEAD - ROPE_DIM), F32)], axis=1)


def _attn_prompt_body(q_ref, kc_ref, kp_ref, vc_ref, vp_ref, o_ref, lse_ref, *, n, blocks):
    first_lo = jnp.where(pl.program_id(1) > 0, 0, n)
    qi = lax.broadcasted_iota(jnp.int32, (Q_REP * n, 2 * n), 0) & (n - 1)
    kj = lax.broadcasted_iota(jnp.int32, (Q_REP * n, 2 * n), 1)
    band = (kj >= qi) & (kj <= qi + n)
    lane = lax.broadcasted_iota(jnp.int32, (n, LANES), 1)
    for j in range(blocks):
        rows = slice(n * j, n * (j + 1))
        lse_tile = jnp.zeros((n, LANES), F32)
        for g in range(KV_PER_GROUP):
            hs = slice(HEAD * g, HEAD * (g + 1))
            if j == 0:
                kprev, vprev = kp_ref[:, hs], vp_ref[:, hs]
                valid = band & (kj >= first_lo)
            else:
                prev = slice(n * (j - 1), n * j)
                kprev, vprev = kc_ref[prev, hs], vc_ref[prev, hs]
                valid = band
            kwin = jnp.concatenate([kprev, kc_ref[rows, hs]], axis=0)
            vwin = jnp.concatenate([vprev, vc_ref[rows, hs]], axis=0)
            q0 = Q_REP * HEAD * g
            qg = jnp.concatenate([q_ref[rows, q0 + HEAD * e:q0 + HEAD * (e + 1)] for e in range(Q_REP)], axis=0)
            s = jnp.where(valid, _dot_nt(qg, kwin) * (HEAD ** -0.5), -jnp.inf)
            mx = jnp.max(s, axis=-1, keepdims=True)
            p = jnp.exp(s - mx)
            den = jnp.sum(p, axis=-1, keepdims=True)
            o = _dot(_bf(p), vwin) / den
            lse = mx + jnp.log(den)
            for e in range(Q_REP):
                o_ref[rows, q0 + HEAD * e:q0 + HEAD * (e + 1)] = o[n * e:n * (e + 1)]
                lse_tile = jnp.where(lane == Q_REP * g + e, lse[n * e:n * (e + 1)], lse_tile)
        lse_ref[rows, :] = lse_tile


def _attn_prompt(qd, kd, vd, gi, window):
    dil, ln, qw = qd.shape
    kw = kd.shape[2]
    n = window // dil
    blocks = min(8, ln // n)
    rows = n * blocks
    assert n == HEAD and ln % rows == 0
    cur = lambda r, b: (r, b, 0)
    prev = lambda r, b: (r, jnp.maximum(b * blocks - 1, 0), 0)
    return pl.pallas_call(
        functools.partial(_attn_prompt_body, n=n, blocks=blocks),
        out_shape=[jax.ShapeDtypeStruct((dil, ln, qw), F32), jax.ShapeDtypeStruct((dil, ln, LANES), F32)],
        grid=(dil, ln // rows),
        in_specs=[pl.BlockSpec((None, rows, qw), cur), pl.BlockSpec((None, rows, kw), cur),
                  pl.BlockSpec((None, n, kw), prev), pl.BlockSpec((None, rows, kw), cur),
                  pl.BlockSpec((None, n, kw), prev)],
        out_specs=[pl.BlockSpec((None, rows, qw), cur), pl.BlockSpec((None, rows, LANES), cur)],
        compiler_params=_params(2), name=f"attn_prompt{gi}")(qd, kd, kd, vd, vd)


def _attn_sample_body(q_ref, kn_ref, vn_ref, *refs, bb):
    n_groups = len(GROUPS)
    caches, outs, lses = refs[:n_groups], refs[n_groups:2 * n_groups], refs[2 * n_groups:]
    lane = lax.broadcasted_iota(jnp.int32, (1, LANES), 1)
    kw = KV_PER_GROUP * HEAD
    for gi in range(n_groups):
        c_ref, o_ref, l_ref = caches[gi], outs[gi], lses[gi]
        for b in range(bb):
            lrow = jnp.zeros((1, LANES), F32)
            for g in range(KV_PER_GROUP):
                kc = c_ref[b, :, 0, g, :]
                vc = c_ref[b, :, 1, g, :]
                kcol = kw * gi + HEAD * g
                kn = kn_ref[b:b + 1, kcol:kcol + HEAD]
                vn = vn_ref[b:b + 1, kcol:kcol + HEAD]
                for e in range(Q_REP):
                    oc = HEAD * (Q_REP * g + e)
                    qv = q_ref[b:b + 1, Q_PER_GROUP * HEAD * gi + oc:Q_PER_GROUP * HEAD * gi + oc + HEAD]
                    s = jnp.sum(kc * qv, axis=1, keepdims=True) * (HEAD ** -0.5)
                    sn = jnp.sum(kn * qv, axis=1, keepdims=True) * (HEAD ** -0.5)
                    mx = jnp.maximum(jnp.max(s, axis=0, keepdims=True), sn)
                    p = jnp.exp(s - mx)
                    pn = jnp.exp(sn - mx)
                    den = jnp.sum(p, axis=0, keepdims=True) + pn
                    o_ref[b:b + 1, oc:oc + HEAD] = (jnp.sum(p * vc, axis=0, keepdims=True) + pn * vn) / den
                    lrow = jnp.where(lane == Q_REP * g + e, mx + jnp.log(den), lrow)
            l_ref[b:b + 1, :] = lrow


def _attn_sample(q, k, v, caches):
    n_seq = q.shape[0]
    bb = SUBLANES
    assert n_seq % bb == 0
    qw, kw = Q_PER_GROUP * HEAD, KV_PER_GROUP * HEAD
    views, cache_specs = [], []
    for cache, (window, dil) in zip(caches, GROUPS):
        n = window // dil
        assert cache.shape[1] == window and n == HEAD
        views.append(cache.reshape(n_seq, n, dil, *cache.shape[2:]))
        cache_specs.append(pl.BlockSpec((bb, n, None, *cache.shape[2:]), lambda i: (i, 0, 0, 0, 0, 0)))
    row = lambda c: pl.BlockSpec((bb, c), lambda i: (i, 0))
    n_groups = len(GROUPS)
    res = pl.pallas_call(
        functools.partial(_attn_sample_body, bb=bb),
        out_shape=[jax.ShapeDtypeStruct((n_seq, qw), F32)] * n_groups
        + [jax.ShapeDtypeStruct((n_seq, LANES), F32)] * n_groups,
        grid=(n_seq // bb,),
        in_specs=[row(q.shape[1]), row(k.shape[1]), row(v.shape[1])] + cache_specs,
        out_specs=[row(qw)] * n_groups + [row(LANES)] * n_groups,
        compiler_params=_params(), name="attn_sample")(q, k, v, *views)
    return res[:n_groups], res[n_groups:]


def _trunk(x, p, pos, state, w, tm, tm_wide):
    m, d = x.shape
    per_row = state is not None
    n_heads = w["n_heads"]
    aw = w["w_z"].shape[1]

    consts = [w["g_mix"][0], w["w_qkv"], w["w_ab"], w["w_z"], w["w_aconv"], w["a_log"], w["dt_bias"]]
    if per_row:
        consts.append(jnp.swapaxes(state["qkv_conv"], 0, 1))
    q, k, v, z, gb, qkv_tail = _row_call(
        functools.partial(_a_in_body, tm=tm, n_heads=n_heads, per_row_state=per_row), "a_in", m, tm, [x], consts,
        [(aw, F32)] * 4 + [(LANES, F32)], const_outs=[((m if per_row else SUBLANES, 3 * aw), F32)],
        scratch=[] if per_row else [pltpu.VMEM((SUBLANES, 3 * aw), F32)])
    if per_row:
        o, s1 = _delta_step(q, k, v, gb, state["delta"], n_heads)
    else:
        o, s1 = _delta_prompt(q, k, v, gb, n_heads)
    mix = [_rows(o, tm_wide), _rows(z, tm_wide), _rows(x, tm_wide), _whole(w["g_aout"]), _whole(w["w_aout"])]
    h, ffn_tail0 = _mix_ffn(mix, [], None, p, w, 0, m, tm_wide, state["ffn_conv"][0] if per_row else None)

    table = _rope_table(pos)
    if per_row:
        qa, ka, va = _kvq(h, table, w, tm_wide, None)
        outs, lses = _attn_sample(qa, ka, va, state["caches"])
        outs, lses = [a[None] for a in outs], [a[None] for a in lses]
        dils = (1,) * len(GROUPS)
    else:
        dils = tuple(dil for _, dil in GROUPS)
        ng = len(dils)
        res = _kvq(h, table, w, tm_wide, dils)
        ka, va = res[:2]
        qd, kd, vd = res[2:2 + ng], res[2 + ng:2 + 2 * ng], res[2 + 2 * ng:]
        att = [_attn_prompt(qd[gi], kd[gi], vd[gi], gi, window) for gi, (window, _) in enumerate(GROUPS)]
        outs, lses = [a[0] for a in att], [a[1] for a in att]
    mix = [_grouped(a, tm_wide) for a in (*outs, *lses)] + [_rows(h, tm_wide), _whole(w["w_o"])]
    mix_scratch = []
    for dil in dils:
        if dil > 1:
            assert tm_wide % (dil * SUBLANES) == 0
            mix_scratch += [pltpu.VMEM((Q_PER_GROUP, tm_wide, HEAD), F32), pltpu.VMEM((tm_wide, LANES), F32)]
    h, ffn_tail1 = _mix_ffn(mix, mix_scratch, dils, p, w, 1, m, tm_wide, state["ffn_conv"][1] if per_row else None)
    return h, s1, qkv_tail, (ffn_tail0, ffn_tail1), ka, va


def _kv_rows(k, v, gi, n_rows):
    kw = KV_PER_GROUP * HEAD
    sl = slice(kw * gi, kw * (gi + 1))
    shape = (n_rows, 1, KV_PER_GROUP, HEAD)
    return jnp.concatenate([k[k.shape[0] - n_rows:, sl].reshape(shape), v[v.shape[0] - n_rows:, sl].reshape(shape)],
                           axis=1)


def kernel(x_prompt, x_sample, p_prompt, p_sample, state_delta, state_qkv_conv, state_ffn_conv, cache_kv_w128, cache_kv_w512, cache_kv_w2048, g_mix_norm, g_ffn_norm, w_ffn_up, w_ffn_conv, w_ffn_down, g_ple_norm, w_ple_gate, w_ple_proj, w_a_in, w_a_conv, a_log, a_dt_bias, g_a_out_norm, w_a_out, g_kv_norm, w_kv, g_k_norm, w_q, g_q_norm, w_o):
    n_heads = a_log.shape[1]
    aw = n_heads * HEAD
    assert w_a_in.shape[0] == 1 and w_q.shape[0] == 1 and w_a_in.shape[2] == 4 * aw + 2 * n_heads
    row = lambda a: a.reshape(1, -1)
    lane_pad = lambda a: jnp.pad(a, ((0, 0), (0, LANES - a.shape[1])))
    w = {
        "n_heads": n_heads,
        "g_mix": [row(g) for g in g_mix_norm], "g_ffn": g_ffn_norm[:, None], "g_ple": g_ple_norm[:, None],
        "w_up": _bf(w_ffn_up), "w_fconv": w_ffn_conv, "w_down": _bf(w_ffn_down),
        "w_gate": _bf(w_ple_gate), "w_proj": _bf(w_ple_proj),
        "w_qkv": _bf(w_a_in[0][:, :3 * aw]), "w_ab": _bf(lane_pad(w_a_in[0][:, 3 * aw:3 * aw + 2 * n_heads])),
        "w_z": _bf(w_a_in[0][:, 3 * aw + 2 * n_heads:]), "w_aconv": w_a_conv[0],
        "a_log": lane_pad(a_log), "dt_bias": lane_pad(a_dt_bias),
        "g_aout": row(g_a_out_norm[0]), "w_aout": _bf(w_a_out[0]),
        "g_kv": row(g_kv_norm), "w_kv": _bf(w_kv), "g_k": row(g_k_norm),
        "w_q": _bf(w_q[0]), "g_q": row(g_q_norm[0]), "w_o": _bf(w_o[0]),
    }
    assert x_prompt.shape[0] == 1 and x_sample.shape[1] == 1
    seq, d = x_prompt.shape[1], x_prompt.shape[2]
    n_seq = x_sample.shape[0]

    yp, sd_p, qkv_tail_p, ffn_tails_p, kp, vp = _trunk(
        x_prompt.reshape(seq, d), p_prompt.reshape(p_prompt.shape[0], seq, -1), jnp.arange(seq, dtype=jnp.int32), None, w,
        tm=256, tm_wide=512)
    a_keep, f_keep = state_qkv_conv.shape[2], state_ffn_conv.shape[2]
    sq_p = qkv_tail_p[SUBLANES - a_keep:][None, None]
    sf_p = jnp.stack([t[SUBLANES - f_keep:] for t in ffn_tails_p])[:, None]
    kv_p = [_kv_rows(kp, vp, gi, min(window, seq))[None] for gi, (window, _) in enumerate(GROUPS)]

    state = {"delta": state_delta[0], "qkv_conv": state_qkv_conv[0], "ffn_conv": state_ffn_conv,
             "caches": (cache_kv_w128, cache_kv_w512, cache_kv_w2048)}
    ys, sd_s, qkv_tail_s, ffn_tails_s, ks, vs = _trunk(
        x_sample.reshape(n_seq, d), p_sample.reshape(p_sample.shape[0], n_seq, -1), jnp.full((n_seq,), PAST_LEN, jnp.int32),
        state, w, tm=n_seq, tm_wide=n_seq)
    sq_s = jnp.concatenate([state_qkv_conv[0][:, 1:], qkv_tail_s[:, None]], axis=1)[None]
    sf_s = jnp.stack([jnp.concatenate([state_ffn_conv[i][:, 1:], ffn_tails_s[i][:, None]], axis=1)
                      for i in range(2)])
    kv_s = [_kv_rows(ks, vs, gi, n_seq).reshape(n_seq, 1, 2, KV_PER_GROUP, HEAD) for gi in range(len(GROUPS))]

    return (yp[None], ys[:, None], sd_p[None, None], sq_p, sf_p, kv_p[0], kv_p[1], kv_p[2],
            sd_s[None], sq_s, sf_s, kv_s[0], kv_s[1], kv_s[2])
```

```python
import functools

import jax
import jax.numpy as jnp
from jax import lax
from jax.experimental import pallas as pl
from jax.experimental.pallas import tpu as pltpu

F32 = jnp.float32
BF16 = jnp.bfloat16

EPS = 1e-6
HEAD = 128
CHUNK = 64
GROUPS = ((128, 1), (512, 4), (2048, 16))
KV_PER_GROUP = 2
Q_REP = 2
Q_PER_GROUP = KV_PER_GROUP * Q_REP
ROPE_DIM = HEAD // 4
ROPE_THETA = 500000.0
PAST_LEN = 16384

LANES = 128
SUBLANES = 8
MXU_COLS = 256
VMEM_LIMIT_BYTES = 56 * 1024 * 1024

PREP_CHUNKS = 4
KVQ_AHEAD = 3


def _dot(a, b):
    return jnp.dot(a, b, preferred_element_type=F32)


def _dot_nt(a, b):
    return lax.dot_general(a, b, (((1,), (1,)), ((), ())), preferred_element_type=F32)


def _bf(x):
    return x.astype(BF16)


def _rms(x):
    return x * lax.rsqrt(jnp.mean(x * x, axis=-1, keepdims=True) + EPS)


def _sigmoid(x):
    return 1.0 / (1.0 + jnp.exp(-x))


def _silu(x):
    return x * _sigmoid(x)


def _split3f(x):
    hi = _bf(x).astype(F32)
    r = x - hi
    mid = _bf(r).astype(F32)
    return hi, mid, _bf(r - mid).astype(F32)


def _causal_conv_tile(x, w, carry_ref, cols):
    rows = x.shape[0]
    first = lax.broadcasted_iota(jnp.int32, (SUBLANES, x.shape[1]), 0) == 0
    z = x * w[0:1]
    for j in range(1, w.shape[0]):
        rolled = pltpu.roll(z, 1, 0)
        head = jnp.where(first, carry_ref[j - 1:j, cols], rolled[0:SUBLANES])
        carry_ref[j - 1:j, cols] = z[rows - 1:, :]
        z = jnp.concatenate([head, rolled[SUBLANES:]], axis=0) + x * w[j:j + 1]
    return z


def _const_spec(a):
    nd = a.ndim
    return pl.BlockSpec(a.shape, lambda *_: (0,) * nd, pipeline_mode=pl.Buffered(1))


def _params(n_axes=1):
    return pltpu.CompilerParams(dimension_semantics=("arbitrary",) * n_axes,
                                vmem_limit_bytes=VMEM_LIMIT_BYTES)


def _row_call(body, name, m, tm, row_ins, const_ins, row_outs, const_outs=(), scratch=()):
    assert m % tm == 0
    in_specs = [pl.BlockSpec((tm, a.shape[1]), lambda i: (i, 0)) for a in row_ins]
    in_specs += [_const_spec(a) for a in const_ins]
    out_shape = [jax.ShapeDtypeStruct((m, c), dt) for c, dt in row_outs]
    out_specs = [pl.BlockSpec((tm, c), lambda i: (i, 0)) for c, _ in row_outs]
    for shape, dt in const_outs:
        out_shape.append(jax.ShapeDtypeStruct(shape, dt))
        out_specs.append(pl.BlockSpec(shape, lambda i, _n=len(shape): (0,) * _n))
    return pl.pallas_call(body, out_shape=out_shape, grid=(m // tm,), in_specs=in_specs, out_specs=out_specs,
                          scratch_shapes=list(scratch), compiler_params=_params(), name=name)(*row_ins, *const_ins)


def _call(body, name, steps, ins, outs, scratch=()):
    return pl.pallas_call(body, out_shape=[o for o, _ in outs], grid=(steps,), in_specs=[sp for _, sp in ins],
                          out_specs=[sp for _, sp in outs], scratch_shapes=list(scratch),
                          compiler_params=_params(), name=name)(*[a for a, _ in ins])


def _rows(a, tm):
    return a, pl.BlockSpec((tm, a.shape[1]), lambda i: (i, 0))


def _whole(a):
    return a, _const_spec(a)


def _layer(a, layer):
    nd = a.ndim
    return a, pl.BlockSpec((None,) + a.shape[1:], lambda i: (layer,) + (0,) * (nd - 1), pipeline_mode=pl.Buffered(1))


def _layer_rows(a, layer, tm):
    return a, pl.BlockSpec((None, tm, a.shape[2]), lambda i: (layer, i, 0))


def _grouped(a, tm):
    return a, pl.BlockSpec((a.shape[0], tm // a.shape[0], a.shape[2]), lambda i: (0, i, 0))


def _out_rows(m, c, tm):
    return jax.ShapeDtypeStruct((m, c), F32), pl.BlockSpec((tm, c), lambda i: (i, 0))


def _out_whole(shape):
    return jax.ShapeDtypeStruct(shape, F32), pl.BlockSpec(shape, lambda i: (0,) * len(shape))


def _a_in_body(*refs, tm, n_heads, per_row_state):
    if per_row_state:
        (x_ref, g_ref, wqkv_ref, wab_ref, wz_ref, wconv_ref, alog_ref, dtb_ref, buf_ref,
         q_ref, k_ref, v_ref, z_ref, gb_ref, tail_ref) = refs
    else:
        (x_ref, g_ref, wqkv_ref, wab_ref, wz_ref, wconv_ref, alog_ref, dtb_ref,
         q_ref, k_ref, v_ref, z_ref, gb_ref, tail_ref, cbuf) = refs

        @pl.when(pl.program_id(0) == 0)
        def _():
            cbuf[0:SUBLANES, :] = jnp.zeros((SUBLANES, cbuf.shape[1]), F32)

    hn = _bf(_rms(x_ref[...]) * g_ref[...])
    outs = (q_ref, k_ref, v_ref)
    width = q_ref.shape[1]
    per_part = width // MXU_COLS
    for c in range(3 * per_part):
        cols = slice(MXU_COLS * c, MXU_COLS * (c + 1))
        pq = _dot(hn, wqkv_ref[:, cols])
        w = wconv_ref[:, cols]
        if per_row_state:
            y = buf_ref[0, :, cols] * w[0:1]
            y = y + buf_ref[1, :, cols] * w[1:2]
            y = y + buf_ref[2, :, cols] * w[2:3]
            y = y + pq * w[3:4]
            tail_ref[:, cols] = pq
        else:
            y = _causal_conv_tile(pq, w, cbuf, cols)
            tail_ref[:, cols] = pq[tm - SUBLANES:, :]
        a = _silu(y)
        part, sub = divmod(c, per_part)
        if part < 2:
            halves = []
            for hh in range(MXU_COLS // HEAD):
                ah = a[:, HEAD * hh:HEAD * (hh + 1)]
                nrm = ah * lax.rsqrt(jnp.sum(ah * ah, axis=-1, keepdims=True) + EPS)
                if part == 0:
                    nrm = nrm * (HEAD ** -0.5)
                halves.append(nrm)
            a = jnp.concatenate(halves, axis=1)
        outs[part][:, MXU_COLS * sub:MXU_COLS * (sub + 1)] = a

    pab = _dot(hn, wab_ref[...])
    lane = lax.broadcasted_iota(jnp.int32, pab.shape, 1)
    ap = pab + dtb_ref[...]
    softplus = jnp.maximum(ap, 0.0) + jnp.log1p(jnp.exp(-jnp.abs(ap)))
    g = -jnp.exp(alog_ref[...]) * softplus
    gb_ref[...] = jnp.where(lane < n_heads, g, _sigmoid(pab))
    z_ref[...] = _dot(hn, wz_ref[...])


def _weave(*stages):
    live = list(stages)
    while live:
        for g in list(live):
            try:
                next(g)
            except StopIteration:
                live.remove(g)


def _delta_prompt_body(q_ref, k_ref, v_ref, gb_ref, o_ref, sout_ref, s_scr, *bufs, n_heads, n_chunks):
    step = pl.program_id(0)
    set_a, set_b = bufs[:len(bufs) // 2], bufs[len(bufs) // 2:]

    @pl.when(step == 0)
    def _():
        s_scr[...] = jnp.zeros(s_scr.shape, F32)
        for ref in set_b:
            ref[...] = jnp.zeros(ref.shape, ref.dtype)

    group = MXU_COLS // CHUNK
    shift = CHUNK.bit_length() - 1
    lane_p = lax.broadcasted_iota(jnp.int32, (CHUNK, MXU_COLS), 1)
    row_p = lax.broadcasted_iota(jnp.int32, (CHUNK, MXU_COLS), 0)
    col_p = lane_p & (CHUNK - 1)
    blk = [(lane_p >> shift) == u for u in range(group)]
    incl_p = row_p >= col_p
    strict_p = row_p > col_p
    eye_p = jnp.where(row_p == col_p, 1.0, 0.0).astype(F32)
    bd_r = lax.broadcasted_iota(jnp.int32, (MXU_COLS, MXU_COLS), 0)
    bd_c = lax.broadcasted_iota(jnp.int32, (MXU_COLS, MXU_COLS), 1)
    bd_mask = (bd_r >> shift) == (bd_c >> shift)
    ri = lax.broadcasted_iota(jnp.int32, (CHUNK, CHUNK), 0)
    ci = lax.broadcasted_iota(jnp.int32, (CHUNK, CHUNK), 1)
    ltri3 = jnp.concatenate([_bf(jnp.where(ri >= ci, 1.0, 0.0))] * 3, axis=1)
    g_r = lax.broadcasted_iota(jnp.int32, (MXU_COLS, 3 * LANES), 0) >> shift
    g_c = lax.broadcasted_iota(jnp.int32, (MXU_COLS, 3 * LANES), 1) & (LANES - 1)
    ones3 = jnp.ones((CHUNK, 3 * LANES), BF16)
    er = lax.broadcasted_iota(jnp.int32, (HEAD, HEAD), 0)
    ec = lax.broadcasted_iota(jnp.int32, (HEAD, HEAD), 1)
    eye_h = _bf(jnp.where(er == ec, 1.0, 0.0))
    heads = range(n_heads)
    hs = [slice(HEAD * h, HEAD * (h + 1)) for h in heads]

    def block_diag(x):
        return _bf(jnp.where(bd_mask, jnp.concatenate([x] * group, axis=0), 0.0))

    def pack_cols(cols):
        out = cols[-1]
        for u in reversed(range(group - 1)):
            out = jnp.where(blk[u], cols[u], out)
        return out

    def pack_diag(x):
        out = x[CHUNK * (group - 1):]
        for u in reversed(range(group - 1)):
            out = jnp.where(blk[u], x[CHUNK * u:CHUNK * (u + 1)], out)
        return out

    def stack(ref, rows, h0):
        return jnp.concatenate([ref[rows, hs[h]] for h in range(h0, h0 + group)], axis=0)

    def stack_cols(x, lo):
        return jnp.concatenate([x[:, lo + u:lo + u + 1] for u in range(group)], axis=0)

    def prep(i, u0_s, wkqg_s, qk_s, kdt_s, gl_s):
        chunks = [i * PREP_CHUNKS + j for j in range(PREP_CHUNKS)]
        rows = [pl.ds(pl.multiple_of(c * CHUNK, CHUNK), CHUNK) for c in chunks]
        gbc = [gb_ref[r, :] for r in rows]
        gc = [_dot(ltri3, _bf(jnp.concatenate(_split3f(x), axis=0))) for x in gbc]
        glast = [x[CHUNK - 1:CHUNK, :] for x in gc]
        for j, c in enumerate(chunks):
            gl_s[c] = jnp.broadcast_to(glast[j], (SUBLANES, LANES))
        groups = [(j, h0) for j in range(PREP_CHUNKS) for h0 in range(0, n_heads, group)]
        ng = range(len(groups))
        kst = [stack(k_ref, rows[j], h0) for j, h0 in groups]
        qst = [stack(q_ref, rows[j], h0) for j, h0 in groups]
        kb = [_bf(x) for x in kst]
        pairs = [_dot_nt(jnp.concatenate([_bf(qst[g]), kb[g]], axis=0), kb[g]) for g in ng]
        yield
        gcol_p = [pack_cols([gc[j][:, h:h + 1] for h in range(h0, h0 + group)]) for j, h0 in groups]
        bcol_p = [pack_cols([gbc[j][:, n_heads + h:n_heads + h + 1] for h in range(h0, h0 + group)])
                  for j, h0 in groups]
        grow = [_dot_nt(ones3, _bf(jnp.where(g_c == g_r + h0,
                                               jnp.concatenate([jnp.concatenate([x] * group, axis=0)
                                                                for x in _split3f(gc[j])], axis=1), 0.0)))
                for j, h0 in groups]
        yield
        dec = [jnp.exp(jnp.where(incl_p, gcol_p[g] - grow[g], -jnp.inf)) for g in ng]
        a = [jnp.where(strict_p, bcol_p[g] * pack_diag(pairs[g][MXU_COLS:]) * dec[g], 0.0) for g in ng]
        tinv = [eye_p - x for x in a]
        pw = [_dot(_bf(-x), block_diag(-x)) for x in a]
        yield
        for _ in range(4):
            res = [_dot(jnp.concatenate([_bf(tinv[g]), _bf(pw[g])], axis=0), block_diag(pw[g])) for g in ng]
            tinv = [tinv[g] + res[g][:CHUNK] for g in ng]
            pw = [r[CHUNK:] for r in res]
            yield
        tinv = [tinv[g] + _dot(_bf(tinv[g]), block_diag(pw[g])) for g in ng]
        yield
        tbd = [block_diag(x) for x in tinv]
        gcol_s = [stack_cols(gc[j], h0) for j, h0 in groups]
        bcol_s = [stack_cols(gbc[j], n_heads + h0) for j, h0 in groups]
        gam_s = [jnp.exp(x) for x in gcol_s]
        rhs = [jnp.concatenate([stack(v_ref, rows[j], h0) * bcol_s[g], kst[g] * (bcol_s[g] * gam_s[g])], axis=1)
               for g, (j, h0) in enumerate(groups)]
        x0 = [_dot(tbd[g], _bf(rhs[g])) for g in ng]
        yield
        resid = [rhs[g] - x0[g] - _dot(block_diag(a[g]), _bf(x0[g])) for g in ng]
        yield
        sol = [x0[g] + _dot(tbd[g], _bf(resid[g])) for g in ng]
        yield
        kdt = [_dot_nt(eye_h, _bf(kst[g] * jnp.exp(stack_cols(jnp.broadcast_to(glast[j], (CHUNK, LANES)), h0)
                                                   - gcol_s[g])))
               for g, (j, h0) in enumerate(groups)]
        qk = [pack_diag(pairs[g][:MXU_COLS]) * dec[g] for g in ng]
        for g, (j, h0) in enumerate(groups):
            c = chunks[j]
            qg = qst[g] * gam_s[g]
            for u in range(group):
                r = slice(CHUNK * u, CHUNK * (u + 1))
                u0_s[c, h0 + u] = sol[g][r, :HEAD]
                wkqg_s[c, h0 + u] = _bf(jnp.concatenate([sol[g][r, HEAD:], qg[r]], axis=0))
                qk_s[c, h0 + u] = _bf(qk[g][:, r])
                kdt_s[c, h0 + u] = _bf(kdt[g][:, r])

    def scan(i, u0_s, wkqg_s, qk_s, kdt_s, gl_s):
        for j in range(PREP_CHUNKS):
            c = i * PREP_CHUNKS + j
            rows = pl.ds(pl.multiple_of(c * CHUNK, CHUNK), CHUNK)
            glr = gl_s[c]
            ws = [_dot(wkqg_s[c, h], _bf(s_scr[h])) for h in heads]
            yield
            ub = [_bf(u0_s[c, h] - ws[h][:CHUNK]) for h in heads]
            for h in heads:
                o_ref[rows, HEAD * h:HEAD * (h + 1)] = ws[h][CHUNK:] + _dot(qk_s[c, h], ub[h])
            yield
            for h in heads:
                s_scr[h] = jnp.exp(glr[0:1, h:h + 1]) * s_scr[h] + _dot(kdt_s[c, h], ub[h])
            yield

    def run(prep_set, scan_set):
        def body(i, carry):
            _weave(prep(i, *prep_set), scan(i, *scan_set))
            return carry
        lax.fori_loop(0, n_chunks // PREP_CHUNKS, body, 0)

    @pl.when(step % 2 == 0)
    def _():
        run(set_a, set_b)

    @pl.when(step % 2 == 1)
    def _():
        run(set_b, set_a)

    sout_ref[...] = s_scr[...]


def _delta_prompt(q, k, v, gb, n_heads):
    t = q.shape[0]
    n_chunks = 8
    tt = n_chunks * CHUNK
    assert t % tt == 0
    n_tiles = t // tt
    prepared = [pltpu.VMEM((n_chunks, n_heads, CHUNK, HEAD), F32),
                pltpu.VMEM((n_chunks, n_heads, 2 * CHUNK, HEAD), BF16),
                pltpu.VMEM((n_chunks, n_heads, CHUNK, CHUNK), BF16),
                pltpu.VMEM((n_chunks, n_heads, HEAD, CHUNK), BF16),
                pltpu.VMEM((n_chunks, SUBLANES, LANES), F32)]
    scratch = [pltpu.VMEM((n_heads, HEAD, HEAD), F32)] + prepared + prepared
    ahead = lambda a: (a, pl.BlockSpec((tt, a.shape[1]), lambda s: (jnp.minimum(s, n_tiles - 1), 0)))
    behind = (jax.ShapeDtypeStruct(q.shape, F32), pl.BlockSpec((tt, q.shape[1]), lambda s: (jnp.maximum(s - 1, 0), 0)))
    return _call(functools.partial(_delta_prompt_body, n_heads=n_heads, n_chunks=n_chunks), "delta_prompt",
                 n_tiles + 1, [ahead(a) for a in (q, k, v, gb)], [behind, _out_whole((n_heads, HEAD, HEAD))], scratch)


def _delta_step_body(qt_ref, kt_ref, v_ref, gbt_ref, s_ref, o_ref, so_ref, *, n_seq, n_heads):
    h = pl.program_id(0)
    decay = jnp.exp(gbt_ref[pl.ds(h, 1), :])
    beta = gbt_ref[pl.ds(n_heads + h, 1), :]
    for b in range(n_seq):
        s = decay[:, b:b + 1] * s_ref[b]
        kc = kt_ref[:, b:b + 1]
        u = beta[:, b:b + 1] * (v_ref[b:b + 1, :] - jnp.sum(kc * s, axis=0, keepdims=True))
        s = s + kc * u
        so_ref[b] = s
        o_ref[b:b + 1, :] = jnp.sum(qt_ref[:, b:b + 1] * s, axis=0, keepdims=True)


def _delta_step(q, k, v, gb, state, n_heads):
    n_seq = q.shape[0]
    col = pl.BlockSpec((HEAD, n_seq), lambda h: (h, 0))
    row = pl.BlockSpec((n_seq, HEAD), lambda h: (0, h))
    st = pl.BlockSpec((n_seq, None, HEAD, HEAD), lambda h: (0, h, 0, 0))
    gbt = gb.T
    return pl.pallas_call(
        functools.partial(_delta_step_body, n_seq=n_seq, n_heads=n_heads),
        out_shape=[jax.ShapeDtypeStruct(q.shape, F32), jax.ShapeDtypeStruct(state.shape, F32)],
        grid=(n_heads,), in_specs=[col, col, row, _const_spec(gbt), st], out_specs=[row, st],
        compiler_params=_params(), name="delta_step")(q.T, k.T, v, gbt, state)


def _delta_out_mix(o_ref, z_ref, res_ref, gout_ref, w_ref):
    parts = []
    for h in range(o_ref.shape[1] // HEAD):
        hs = slice(HEAD * h, HEAD * (h + 1))
        parts.append(_bf(_rms(o_ref[:, hs]) * gout_ref[...] * _silu(z_ref[:, hs])))
    return res_ref[...] + _dot(jnp.concatenate(parts, axis=1), w_ref[...])


def _attn_out_mix(refs, scratch, tm, dils):
    ng = len(dils)
    o_refs, l_refs = refs[:ng], refs[ng:2 * ng]
    res_ref, w_ref = refs[2 * ng:]
    get_o, lses = [], []
    for gi, dil in enumerate(dils):
        if dil == 1:
            get_o.append(lambda hh, r=o_refs[gi]: r[0, :, HEAD * hh:HEAD * (hh + 1)])
            lses.append(l_refs[gi][0])
        else:
            o_s, l_s = scratch.pop(0), scratch.pop(0)
            for r in range(dil):
                dst = pl.ds(r, tm // dil, stride=dil)
                for hh in range(Q_PER_GROUP):
                    o_s.at[hh][dst, :] = o_refs[gi][r, :, HEAD * hh:HEAD * (hh + 1)]
                l_s[dst, :] = l_refs[gi][r]
            get_o.append(lambda hh, r=o_s: r[hh])
            lses.append(l_s[...])
    parts = []
    for hh in range(Q_PER_GROUP):
        lh = [l[:, hh:hh + 1] for l in lses]
        mx = functools.reduce(jnp.maximum, lh)
        ex = [jnp.exp(x - mx) for x in lh]
        den = functools.reduce(lambda a, b: a + b, ex)
        acc = (ex[0] / den) * get_o[0](hh)
        for gi in range(1, ng):
            acc = acc + (ex[gi] / den) * get_o[gi](hh)
        parts.append(_bf(acc))
    return res_ref[...] + _dot(jnp.concatenate(parts, axis=1), w_ref[...])


def _mix_ffn_body(*refs, tm, per_row_state, dils):
    n_mix = 5 if dils is None else 2 * len(dils) + 2
    mix_in, refs = refs[:n_mix], refs[n_mix:]
    if per_row_state:
        p_ref, buf_ref, gffn_ref, wup_ref, wconv_ref, wdown_ref, gple_ref, wgate_ref, wproj_ref = refs[:9]
        out_ref, tail_ref = refs[9:11]
        scratch = list(refs[11:])
    else:
        p_ref, gffn_ref, wup_ref, wconv_ref, wdown_ref, gple_ref, wgate_ref, wproj_ref = refs[:8]
        out_ref, tail_ref = refs[8:10]
        scratch = list(refs[10:])
    act_s = scratch.pop()
    if not per_row_state:
        ubuf = scratch.pop()

        @pl.when(pl.program_id(0) == 0)
        def _():
            ubuf[...] = jnp.zeros(ubuf.shape, F32)

    h = _delta_out_mix(*mix_in) if dils is None else _attn_out_mix(mix_in, scratch, tm, dils)
    hn = _bf(_rms(h) * gffn_ref[...])
    dff = wdown_ref.shape[0]
    for c in range(dff // MXU_COLS):
        ys = []
        for half in range(2):
            lo = half * dff + MXU_COLS * c
            cols = slice(lo, lo + MXU_COLS)
            u = _dot(hn, wup_ref[:, cols])
            w = wconv_ref[:, cols]
            if per_row_state:
                y = buf_ref[:, cols] * w[0:1]
                y = y + buf_ref[:, 2 * dff + lo:2 * dff + lo + MXU_COLS] * w[1:2]
                y = y + u * w[2:3]
                tail_ref[:, cols] = u
            else:
                y = _causal_conv_tile(u, w, ubuf, cols)
                tail_ref[:, cols] = u[tm - SUBLANES:, :]
            ys.append(y)
        act_s[:, MXU_COLS * c:MXU_COLS * (c + 1)] = _bf(_silu(ys[0]) * ys[1])
    h2 = h + _dot(act_s[...], wdown_ref[...])
    gate = _sigmoid(_dot(_bf(_rms(h2) * gple_ref[...]), wgate_ref[...]))
    out_ref[...] = h2 + gate * _dot(_bf(p_ref[...]), wproj_ref[...])


def _mix_ffn(mix_ins, mix_scratch, dils, p, w, layer, m, tm, state):
    per_row = state is not None
    d, dff = w["w_down"].shape[2], w["w_down"].shape[1]
    ins = list(mix_ins) + [_layer_rows(p, layer, tm)] + ([_rows(state.reshape(m, -1), tm)] if per_row else [])
    ins += [_layer(w[name], layer) for name in ("g_ffn", "w_up", "w_fconv", "w_down", "g_ple", "w_gate", "w_proj")]
    outs = [_out_rows(m, d, tm), _out_whole((m if per_row else SUBLANES, 2 * dff))]
    scratch = list(mix_scratch) + ([] if per_row else [pltpu.VMEM((SUBLANES, 2 * dff), F32)])
    scratch.append(pltpu.VMEM((tm, dff), BF16))
    return _call(functools.partial(_mix_ffn_body, tm=tm, per_row_state=per_row, dils=dils), f"mix_ffn{layer}",
                 m // tm, ins, outs, scratch)


def _kvq_body(*refs, tm, dils):
    h_ref, tab_ref, gkv_ref, gmix_ref, wkv_ref, wq_ref, gk_ref, gq_ref = refs[:8]
    if dils is None:
        q_ref, k_ref, v_ref = refs[8:]
    else:
        ng = len(dils)
        k_ref, v_ref = refs[8:10]
        qd, kd, vd = refs[10:10 + ng], refs[10 + ng:10 + 2 * ng], refs[10 + 2 * ng:10 + 3 * ng]
        stage = refs[10 + 3 * ng]
    h = h_ref[...]
    base = _rms(h)
    hkv = _bf(base * gkv_ref[...])
    hq = _bf(base * gmix_ref[...])
    tab = tab_ref[...]
    lane = lax.broadcasted_iota(jnp.int32, tab.shape, 1)
    half = ROPE_DIM // 2
    cos = jnp.where(lane < half, tab, jnp.where(lane < ROPE_DIM, pltpu.roll(tab, half, 1), 1.0))
    sa = jnp.where((lane >= half) & (lane < ROPE_DIM), tab, 0.0)
    sb = jnp.where(lane < half, -pltpu.roll(tab, HEAD - half, 1), 0.0)
    kw = KV_PER_GROUP * HEAD

    def norm_rope(x, g):
        y = _rms(x) * g
        return y * cos + pltpu.roll(y, ROPE_DIM // 2, 1) * sa + pltpu.roll(y, HEAD - ROPE_DIM // 2, 1) * sb

    def heads(x, g):
        return jnp.concatenate([norm_rope(x[:, HEAD * j:HEAD * (j + 1)], g) for j in range(MXU_COLS // HEAD)], axis=1)

    def by_class(x, dst_ref, dil, cols):
        if dil == 1:
            dst_ref[0, :, cols] = _bf(x)
        else:
            for j in range(MXU_COLS // LANES):
                stage[j] = x[:, LANES * j:LANES * (j + 1)]
            for r in range(dil):
                for j in range(MXU_COLS // LANES):
                    lo = cols.start + LANES * j
                    dst_ref[r, :, lo:lo + LANES] = _bf(stage.at[j][pl.ds(r, tm // dil, stride=dil), :])

    assert kw == MXU_COLS and Q_PER_GROUP * HEAD == 2 * MXU_COLS
    n_kv = k_ref.shape[1]

    def project(kind, c):
        lo = MXU_COLS * c + (n_kv if kind == "v" else 0)
        return _dot(hq if kind == "q" else hkv, (wq_ref if kind == "q" else wkv_ref)[:, lo:lo + MXU_COLS])

    def finish(kind, c, x):
        cols = slice(MXU_COLS * c, MXU_COLS * (c + 1))
        if kind == "k":
            x = heads(x, gk_ref[...])
            k_ref[:, cols] = x
        elif kind == "v":
            v_ref[:, cols] = x
        else:
            x = heads(x, gq_ref[...])
            if dils is None:
                q_ref[:, cols] = x
        if dils is not None:
            if kind == "q":
                by_class(x, qd[c // 2], dils[c // 2], slice(MXU_COLS * (c % 2), MXU_COLS * (c % 2 + 1)))
            else:
                by_class(x, (kd if kind == "k" else vd)[c], dils[c], slice(0, kw))

    jobs = [(kind, c) for kind, n in (("k", n_kv), ("v", n_kv), ("q", wq_ref.shape[1])) for c in range(n // MXU_COLS)]
    pending = []
    for kind, c in jobs:
        pending.append((kind, c, project(kind, c)))
        if len(pending) > KVQ_AHEAD:
            finish(*pending.pop(0))
    for job in pending:
        finish(*job)


def _kvq(h, table, w, tm, dils):
    m, d = h.shape
    n_q, n_kv = w["w_q"].shape[1], w["w_kv"].shape[1] // 2
    qw, kw = Q_PER_GROUP * HEAD, KV_PER_GROUP * HEAD
    consts = [w["g_kv"], w["g_mix"][1], w["w_kv"], w["w_q"], w["g_k"], w["g_q"]]
    rows = [h, table]
    row_spec = lambda c: pl.BlockSpec((tm, c), lambda i: (i, 0))
    in_specs = [row_spec(a.shape[1]) for a in rows] + [_const_spec(a) for a in consts]
    nat = lambda c: jax.ShapeDtypeStruct((m, c), F32)
    if dils is None:
        out_shape = [nat(n_q), nat(n_kv), nat(n_kv)]
        out_specs, scratch = [row_spec(n_q), row_spec(n_kv), row_spec(n_kv)], []
    else:
        out_shape, out_specs = [nat(n_kv), nat(n_kv)], [row_spec(n_kv), row_spec(n_kv)]
        for width in (qw, kw, kw):
            for dil in dils:
                assert tm % (dil * 2 * SUBLANES) == 0
                out_shape.append(jax.ShapeDtypeStruct((dil, m // dil, width), BF16))
                out_specs.append(pl.BlockSpec((dil, tm // dil, width), lambda i: (0, i, 0)))
        scratch = [pltpu.VMEM((MXU_COLS // LANES, tm, LANES), F32)]
    return pl.pallas_call(functools.partial(_kvq_body, tm=tm, dils=dils), out_shape=out_shape, grid=(m // tm,),
                          in_specs=in_specs, out_specs=out_specs, scratch_shapes=scratch,
                          compiler_params=_params(), name="kvq")(*rows, *consts)


def _rope_table(pos):
    half = ROPE_DIM // 2
    inv_freq = ROPE_THETA ** (-jnp.arange(half, dtype=F32) * 2.0 / ROPE_DIM)
    ang = pos.astype(F32)[:, None] * inv_freq[None, :]
    return jnp.concatenate([jnp.cos(ang), jnp.sin(ang), jnp.zeros((pos.shape[0], HEAD - ROPE_DIM), F32)], axis=1)


def _attn_prompt_body(q_ref, kc_ref, kp_ref, vc_ref, vp_ref, o_ref, lse_ref, *, n, blocks):
    first_lo = jnp.where(pl.program_id(1) > 0, 0, n)
    qi = lax.broadcasted_iota(jnp.int32, (Q_REP * n, 2 * n), 0) & (n - 1)
    kj = lax.broadcasted_iota(jnp.int32, (Q_REP * n, 2 * n), 1)
    band = (kj >= qi) & (kj <= qi + n)
    lane = lax.broadcasted_iota(jnp.int32, (n, LANES), 1)
    for j in range(blocks):
        rows = slice(n * j, n * (j + 1))
        lse_tile = jnp.zeros((n, LANES), F32)
        for g in range(KV_PER_GROUP):
            hs = slice(HEAD * g, HEAD * (g + 1))
            if j == 0:
                kprev, vprev = kp_ref[:, hs], vp_ref[:, hs]
                valid = band & (kj >= first_lo)
            else:
                prev = slice(n * (j - 1), n * j)
                kprev, vprev = kc_ref[prev, hs], vc_ref[prev, hs]
                valid = band
            kwin = jnp.concatenate([kprev, kc_ref[rows, hs]], axis=0)
            vwin = jnp.concatenate([vprev, vc_ref[rows, hs]], axis=0)
            q0 = Q_REP * HEAD * g
            qg = jnp.concatenate([q_ref[rows, q0 + HEAD * e:q0 + HEAD * (e + 1)] for e in range(Q_REP)], axis=0)
            s = jnp.where(valid, _dot_nt(qg, kwin) * (HEAD ** -0.5), -jnp.inf)
            mx = jnp.max(s, axis=-1, keepdims=True)
            p = jnp.exp(s - mx)
            den = jnp.sum(p, axis=-1, keepdims=True)
            o = _dot(_bf(p), vwin) / den
            lse = mx + jnp.log(den)
            for e in range(Q_REP):
                o_ref[rows, q0 + HEAD * e:q0 + HEAD * (e + 1)] = o[n * e:n * (e + 1)]
                lse_tile = jnp.where(lane == Q_REP * g + e, lse[n * e:n * (e + 1)], lse_tile)
        lse_ref[rows, :] = lse_tile


def _attn_prompt(qd, kd, vd, gi, window):
    dil, ln, qw = qd.shape
    kw = kd.shape[2]
    n = window // dil
    blocks = min(8, ln // n)
    rows = n * blocks
    assert n == HEAD and ln % rows == 0
    cur = lambda r, b: (r, b, 0)
    prev = lambda r, b: (r, jnp.maximum(b * blocks - 1, 0), 0)
    return pl.pallas_call(
        functools.partial(_attn_prompt_body, n=n, blocks=blocks),
        out_shape=[jax.ShapeDtypeStruct((dil, ln, qw), F32), jax.ShapeDtypeStruct((dil, ln, LANES), F32)],
        grid=(dil, ln // rows),
        in_specs=[pl.BlockSpec((None, rows, qw), cur), pl.BlockSpec((None, rows, kw), cur),
                  pl.BlockSpec((None, n, kw), prev), pl.BlockSpec((None, rows, kw), cur),
                  pl.BlockSpec((None, n, kw), prev)],
        out_specs=[pl.BlockSpec((None, rows, qw), cur), pl.BlockSpec((None, rows, LANES), cur)],
        compiler_params=_params(2), name=f"attn_prompt{gi}")(qd, kd, kd, vd, vd)


def _attn_sample_body(q_ref, kn_ref, vn_ref, *refs, bb):
    n_groups = len(GROUPS)
    caches, outs, lses = refs[:n_groups], refs[n_groups:2 * n_groups], refs[2 * n_groups:]
    lane = lax.broadcasted_iota(jnp.int32, (1, LANES), 1)
    kw = KV_PER_GROUP * HEAD
    for gi in range(n_groups):
        c_ref, o_ref, l_ref = caches[gi], outs[gi], lses[gi]
        for b in range(bb):
            lrow = jnp.zeros((1, LANES), F32)
            for g in range(KV_PER_GROUP):
                kc = c_ref[b, :, 0, g, :]
                vc = c_ref[b, :, 1, g, :]
                kcol = kw * gi + HEAD * g
                kn = kn_ref[b:b + 1, kcol:kcol + HEAD]
                vn = vn_ref[b:b + 1, kcol:kcol + HEAD]
                for e in range(Q_REP):
                    oc = HEAD * (Q_REP * g + e)
                    qv = q_ref[b:b + 1, Q_PER_GROUP * HEAD * gi + oc:Q_PER_GROUP * HEAD * gi + oc + HEAD]
                    s = jnp.sum(kc * qv, axis=1, keepdims=True) * (HEAD ** -0.5)
                    sn = jnp.sum(kn * qv, axis=1, keepdims=True) * (HEAD ** -0.5)
                    mx = jnp.maximum(jnp.max(s, axis=0, keepdims=True), sn)
                    p = jnp.exp(s - mx)
                    pn = jnp.exp(sn - mx)
                    den = jnp.sum(p, axis=0, keepdims=True) + pn
                    o_ref[b:b + 1, oc:oc + HEAD] = (jnp.sum(p * vc, axis=0, keepdims=True) + pn * vn) / den
                    lrow = jnp.where(lane == Q_REP * g + e, mx + jnp.log(den), lrow)
            l_ref[b:b + 1, :] = lrow


def _attn_sample(q, k, v, caches):
    n_seq = q.shape[0]
    bb = SUBLANES
    assert n_seq % bb == 0
    qw, kw = Q_PER_GROUP * HEAD, KV_PER_GROUP * HEAD
    views, cache_specs = [], []
    for cache, (window, dil) in zip(caches, GROUPS):
        n = window // dil
        assert cache.shape[1] == window and n == HEAD
        views.append(cache.reshape(n_seq, n, dil, *cache.shape[2:]))
        cache_specs.append(pl.BlockSpec((bb, n, None, *cache.shape[2:]), lambda i: (i, 0, 0, 0, 0, 0)))
    row = lambda c: pl.BlockSpec((bb, c), lambda i: (i, 0))
    n_groups = len(GROUPS)
    res = pl.pallas_call(
        functools.partial(_attn_sample_body, bb=bb),
        out_shape=[jax.ShapeDtypeStruct((n_seq, qw), F32)] * n_groups
        + [jax.ShapeDtypeStruct((n_seq, LANES), F32)] * n_groups,
        grid=(n_seq // bb,),
        in_specs=[row(q.shape[1]), row(k.shape[1]), row(v.shape[1])] + cache_specs,
        out_specs=[row(qw)] * n_groups + [row(LANES)] * n_groups,
        compiler_params=_params(), name="attn_sample")(q, k, v, *views)
    return res[:n_groups], res[n_groups:]


def _trunk(x, p, pos, state, w, tm, tm_wide):
    m, d = x.shape
    per_row = state is not None
    n_heads = w["n_heads"]
    aw = w["w_z"].shape[1]

    consts = [w["g_mix"][0], w["w_qkv"], w["w_ab"], w["w_z"], w["w_aconv"], w["a_log"], w["dt_bias"]]
    if per_row:
        consts.append(jnp.swapaxes(state["qkv_conv"], 0, 1))
    q, k, v, z, gb, qkv_tail = _row_call(
        functools.partial(_a_in_body, tm=tm, n_heads=n_heads, per_row_state=per_row), "a_in", m, tm, [x], consts,
        [(aw, F32)] * 4 + [(LANES, F32)], const_outs=[((m if per_row else SUBLANES, 3 * aw), F32)],
        scratch=[] if per_row else [pltpu.VMEM((SUBLANES, 3 * aw), F32)])
    if per_row:
        o, s1 = _delta_step(q, k, v, gb, state["delta"], n_heads)
    else:
        o, s1 = _delta_prompt(q, k, v, gb, n_heads)
    mix = [_rows(o, tm_wide), _rows(z, tm_wide), _rows(x, tm_wide), _whole(w["g_aout"]), _whole(w["w_aout"])]
    h, ffn_tail0 = _mix_ffn(mix, [], None, p, w, 0, m, tm_wide, state["ffn_conv"][0] if per_row else None)

    table = _rope_table(pos)
    if per_row:
        qa, ka, va = _kvq(h, table, w, tm, None)
        outs, lses = _attn_sample(qa, ka, va, state["caches"])
        outs, lses = [a[None] for a in outs], [a[None] for a in lses]
        dils = (1,) * len(GROUPS)
    else:
        dils = tuple(dil for _, dil in GROUPS)
        ng = len(dils)
        res = _kvq(h, table, w, tm, dils)
        ka, va = res[:2]
        qd, kd, vd = res[2:2 + ng], res[2 + ng:2 + 2 * ng], res[2 + 2 * ng:]
        att = [_attn_prompt(qd[gi], kd[gi], vd[gi], gi, window) for gi, (window, _) in enumerate(GROUPS)]
        outs, lses = [a[0] for a in att], [a[1] for a in att]
    mix = [_grouped(a, tm_wide) for a in (*outs, *lses)] + [_rows(h, tm_wide), _whole(w["w_o"])]
    mix_scratch = []
    for dil in dils:
        if dil > 1:
            assert tm_wide % (dil * SUBLANES) == 0
            mix_scratch += [pltpu.VMEM((Q_PER_GROUP, tm_wide, HEAD), F32), pltpu.VMEM((tm_wide, LANES), F32)]
    h, ffn_tail1 = _mix_ffn(mix, mix_scratch, dils, p, w, 1, m, tm_wide, state["ffn_conv"][1] if per_row else None)
    return h, s1, qkv_tail, (ffn_tail0, ffn_tail1), ka, va


def _kv_rows(k, v, gi, n_rows):
    kw = KV_PER_GROUP * HEAD
    sl = slice(kw * gi, kw * (gi + 1))
    shape = (n_rows, 1, KV_PER_GROUP, HEAD)
    return jnp.concatenate([k[k.shape[0] - n_rows:, sl].reshape(shape), v[v.shape[0] - n_rows:, sl].reshape(shape)],
                           axis=1)


def kernel(x_prompt, x_sample, p_prompt, p_sample, state_delta, state_qkv_conv, state_ffn_conv, cache_kv_w128, cache_kv_w512, cache_kv_w2048, g_mix_norm, g_ffn_norm, w_ffn_up, w_ffn_conv, w_ffn_down, g_ple_norm, w_ple_gate, w_ple_proj, w_a_in, w_a_conv, a_log, a_dt_bias, g_a_out_norm, w_a_out, g_kv_norm, w_kv, g_k_norm, w_q, g_q_norm, w_o):
    n_heads = a_log.shape[1]
    aw = n_heads * HEAD
    assert w_a_in.shape[0] == 1 and w_q.shape[0] == 1 and w_a_in.shape[2] == 4 * aw + 2 * n_heads
    row = lambda a: a.reshape(1, -1)
    lane_pad = lambda a: jnp.pad(a, ((0, 0), (0, LANES - a.shape[1])))
    w = {
        "n_heads": n_heads,
        "g_mix": [row(g) for g in g_mix_norm], "g_ffn": g_ffn_norm[:, None], "g_ple": g_ple_norm[:, None],
        "w_up": _bf(w_ffn_up), "w_fconv": w_ffn_conv, "w_down": _bf(w_ffn_down),
        "w_gate": _bf(w_ple_gate), "w_proj": _bf(w_ple_proj),
        "w_qkv": _bf(w_a_in[0][:, :3 * aw]), "w_ab": _bf(lane_pad(w_a_in[0][:, 3 * aw:3 * aw + 2 * n_heads])),
        "w_z": _bf(w_a_in[0][:, 3 * aw + 2 * n_heads:]), "w_aconv": w_a_conv[0],
        "a_log": lane_pad(a_log), "dt_bias": lane_pad(a_dt_bias),
        "g_aout": row(g_a_out_norm[0]), "w_aout": _bf(w_a_out[0]),
        "g_kv": row(g_kv_norm), "w_kv": _bf(w_kv), "g_k": row(g_k_norm),
        "w_q": _bf(w_q[0]), "g_q": row(g_q_norm[0]), "w_o": _bf(w_o[0]),
    }
    assert x_prompt.shape[0] == 1 and x_sample.shape[1] == 1
    seq, d = x_prompt.shape[1], x_prompt.shape[2]
    n_seq = x_sample.shape[0]

    yp, sd_p, qkv_tail_p, ffn_tails_p, kp, vp = _trunk(
        x_prompt.reshape(seq, d), p_prompt.reshape(p_prompt.shape[0], seq, -1), jnp.arange(seq, dtype=jnp.int32), None, w,
        tm=256, tm_wide=512)
    a_keep, f_keep = state_qkv_conv.shape[2], state_ffn_conv.shape[2]
    sq_p = qkv_tail_p[SUBLANES - a_keep:][None, None]
    sf_p = jnp.stack([t[SUBLANES - f_keep:] for t in ffn_tails_p])[:, None]
    kv_p = [_kv_rows(kp, vp, gi, min(window, seq))[None] for gi, (window, _) in enumerate(GROUPS)]

    state = {"delta": state_delta[0], "qkv_conv": state_qkv_conv[0], "ffn_conv": state_ffn_conv,
             "caches": (cache_kv_w128, cache_kv_w512, cache_kv_w2048)}
    ys, sd_s, qkv_tail_s, ffn_tails_s, ks, vs = _trunk(
        x_sample.reshape(n_seq, d), p_sample.reshape(p_sample.shape[0], n_seq, -1), jnp.full((n_seq,), PAST_LEN, jnp.int32),
        state, w, tm=n_seq, tm_wide=n_seq)
    sq_s = jnp.concatenate([state_qkv_conv[0][:, 1:], qkv_tail_s[:, None]], axis=1)[None]
    sf_s = jnp.stack([jnp.concatenate([state_ffn_conv[i][:, 1:], ffn_tails_s[i][:, None]], axis=1)
                      for i in range(2)])
    kv_s = [_kv_rows(ks, vs, gi, n_seq).reshape(n_seq, 1, 2, KV_PER_GROUP, HEAD) for gi in range(len(GROUPS))]

    return (yp[None], ys[:, None], sd_p[None, None], sq_p, sf_p, kv_p[0], kv_p[1], kv_p[2],
            sd_s[None], sq_s, sf_s, kv_s[0], kv_s[1], kv_s[2])
```

```python
import functools

import jax
import jax.numpy as jnp
from jax import lax
from jax.experimental import pallas as pl
from jax.experimental.pallas import tpu as pltpu

F32 = jnp.float32
BF16 = jnp.bfloat16

EPS = 1e-6
HEAD = 128
CHUNK = 64
GROUPS = ((128, 1), (512, 4), (2048, 16))
KV_PER_GROUP = 2
Q_REP = 2
Q_PER_GROUP = KV_PER_GROUP * Q_REP
ROPE_DIM = HEAD // 4
ROPE_THETA = 500000.0
PAST_LEN = 16384

LANES = 128
SUBLANES = 8
MXU_COLS = 256
VMEM_LIMIT_BYTES = 56 * 1024 * 1024

PREP_CHUNKS = 4
KVQ_AHEAD = 3


def _dot(a, b):
    return jnp.dot(a, b, preferred_element_type=F32)


def _dot_nt(a, b):
    return lax.dot_general(a, b, (((1,), (1,)), ((), ())), preferred_element_type=F32)


def _bf(x):
    return x.astype(BF16)


def _rms(x):
    return x * lax.rsqrt(jnp.mean(x * x, axis=-1, keepdims=True) + EPS)


def _sigmoid(x):
    return 1.0 / (1.0 + jnp.exp(-x))


def _silu(x):
    return x * _sigmoid(x)


def _split3f(x):
    hi = _bf(x).astype(F32)
    r = x - hi
    mid = _bf(r).astype(F32)
    return hi, mid, _bf(r - mid).astype(F32)


def _causal_conv_tile(x, w, carry_ref, cols):
    rows = x.shape[0]
    first = lax.broadcasted_iota(jnp.int32, (SUBLANES, x.shape[1]), 0) == 0
    z = x * w[0:1]
    for j in range(1, w.shape[0]):
        rolled = pltpu.roll(z, 1, 0)
        head = jnp.where(first, carry_ref[j - 1:j, cols], rolled[0:SUBLANES])
        carry_ref[j - 1:j, cols] = z[rows - 1:, :]
        z = jnp.concatenate([head, rolled[SUBLANES:]], axis=0) + x * w[j:j + 1]
    return z


def _const_spec(a):
    nd = a.ndim
    return pl.BlockSpec(a.shape, lambda *_: (0,) * nd, pipeline_mode=pl.Buffered(1))


def _params(n_axes=1):
    return pltpu.CompilerParams(dimension_semantics=("arbitrary",) * n_axes,
                                vmem_limit_bytes=VMEM_LIMIT_BYTES)


def _row_call(body, name, m, tm, row_ins, const_ins, row_outs, const_outs=(), scratch=()):
    assert m % tm == 0
    in_specs = [pl.BlockSpec((tm, a.shape[1]), lambda i: (i, 0)) for a in row_ins]
    in_specs += [_const_spec(a) for a in const_ins]
    out_shape = [jax.ShapeDtypeStruct((m, c), dt) for c, dt in row_outs]
    out_specs = [pl.BlockSpec((tm, c), lambda i: (i, 0)) for c, _ in row_outs]
    for shape, dt in const_outs:
        out_shape.append(jax.ShapeDtypeStruct(shape, dt))
        out_specs.append(pl.BlockSpec(shape, lambda i, _n=len(shape): (0,) * _n))
    return pl.pallas_call(body, out_shape=out_shape, grid=(m // tm,), in_specs=in_specs, out_specs=out_specs,
                          scratch_shapes=list(scratch), compiler_params=_params(), name=name)(*row_ins, *const_ins)


def _call(body, name, steps, ins, outs, scratch=()):
    return pl.pallas_call(body, out_shape=[o for o, _ in outs], grid=(steps,), in_specs=[sp for _, sp in ins],
                          out_specs=[sp for _, sp in outs], scratch_shapes=list(scratch),
                          compiler_params=_params(), name=name)(*[a for a, _ in ins])


def _rows(a, tm):
    return a, pl.BlockSpec((tm, a.shape[1]), lambda i: (i, 0))


def _whole(a):
    return a, _const_spec(a)


def _layer(a, layer):
    nd = a.ndim
    return a, pl.BlockSpec((None,) + a.shape[1:], lambda i: (layer,) + (0,) * (nd - 1), pipeline_mode=pl.Buffered(1))


def _layer_rows(a, layer, tm):
    return a, pl.BlockSpec((None, tm, a.shape[2]), lambda i: (layer, i, 0))


def _grouped(a, tm):
    return a, pl.BlockSpec((a.shape[0], tm // a.shape[0], a.shape[2]), lambda i: (0, i, 0))


def _out_rows(m, c, tm):
    return jax.ShapeDtypeStruct((m, c), F32), pl.BlockSpec((tm, c), lambda i: (i, 0))


def _out_whole(shape):
    return jax.ShapeDtypeStruct(shape, F32), pl.BlockSpec(shape, lambda i: (0,) * len(shape))


def _qkv_activation(y, part):
    a = _silu(y)
    if part == 2:
        return a
    halves = []
    for hh in range(MXU_COLS // HEAD):
        ah = a[:, HEAD * hh:HEAD * (hh + 1)]
        nrm = ah * lax.rsqrt(jnp.sum(ah * ah, axis=-1, keepdims=True) + EPS)
        halves.append(nrm * (HEAD ** -0.5) if part == 0 else nrm)
    return jnp.concatenate(halves, axis=1)


def _decay_and_beta(pab, alog_ref, dtb_ref, n_heads):
    lane = lax.broadcasted_iota(jnp.int32, pab.shape, 1)
    ap = pab + dtb_ref[...]
    softplus = jnp.maximum(ap, 0.0) + jnp.log1p(jnp.exp(-jnp.abs(ap)))
    return jnp.where(lane < n_heads, -jnp.exp(alog_ref[...]) * softplus, _sigmoid(pab))


def _a_in_step_body(x_ref, g_ref, wqkv_ref, wab_ref, wz_ref, wconv_ref, alog_ref, dtb_ref, buf_ref,
                    q_ref, k_ref, v_ref, z_ref, gb_ref, tail_ref, *, n_heads):
    hn = _bf(_rms(x_ref[...]) * g_ref[...])
    outs = (q_ref, k_ref, v_ref)
    per_part = q_ref.shape[1] // MXU_COLS
    for c in range(3 * per_part):
        cols = slice(MXU_COLS * c, MXU_COLS * (c + 1))
        pq = _dot(hn, wqkv_ref[:, cols])
        w = wconv_ref[:, cols]
        y = buf_ref[0, :, cols] * w[0:1]
        y = y + buf_ref[1, :, cols] * w[1:2]
        y = y + buf_ref[2, :, cols] * w[2:3]
        y = y + pq * w[3:4]
        tail_ref[:, cols] = pq
        part, sub = divmod(c, per_part)
        outs[part][:, MXU_COLS * sub:MXU_COLS * (sub + 1)] = _qkv_activation(y, part)
    gb_ref[...] = _decay_and_beta(_dot(hn, wab_ref[...]), alog_ref, dtb_ref, n_heads)
    z_ref[...] = _dot(hn, wz_ref[...])


def _weave(*stages):
    live = list(stages)
    while live:
        for g in list(live):
            try:
                next(g)
            except StopIteration:
                live.remove(g)


def _mixer_a_prompt_body(x_ref, g_ref, wqkv_ref, wab_ref, wz_ref, wconv_ref, alog_ref, dtb_ref,
                         z_ref, tail_ref, o_ref, sout_ref, cbuf, s_scr, *bufs, n_heads, n_chunks):
    step = pl.program_id(0)
    n_proj = 4
    proj_sets = bufs[:n_proj], bufs[n_proj:2 * n_proj]
    rest = bufs[2 * n_proj:]
    prep_sets = rest[:len(rest) // 2], rest[len(rest) // 2:]

    @pl.when(step == 0)
    def _():
        cbuf[...] = jnp.zeros(cbuf.shape, F32)
        s_scr[...] = jnp.zeros(s_scr.shape, F32)
        for ref in (*proj_sets[1], *prep_sets[0]):
            ref[...] = jnp.zeros(ref.shape, ref.dtype)

    sub = n_chunks // 2 * CHUNK

    def project(i, q_s, k_s, v_s, gb_s):
        rows = pl.ds(pl.multiple_of(i * sub, sub), sub)
        hn = _bf(_rms(x_ref[rows, :]) * g_ref[...])
        outs = (q_s, k_s, v_s)
        per_part = q_s.shape[1] // MXU_COLS
        for c in range(3 * per_part):
            cols = slice(MXU_COLS * c, MXU_COLS * (c + 1))
            pq = _dot(hn, wqkv_ref[:, cols])
            y = _causal_conv_tile(pq, wconv_ref[:, cols], cbuf, cols)
            tail_ref[:, cols] = pq[sub - SUBLANES:, :]
            part, sc = divmod(c, per_part)
            outs[part][rows, MXU_COLS * sc:MXU_COLS * (sc + 1)] = _qkv_activation(y, part)
            yield
        gb_s[rows, :] = _decay_and_beta(_dot(hn, wab_ref[...]), alog_ref, dtb_ref, n_heads)
        z_ref[rows, :] = _dot(hn, wz_ref[...])
        yield

    group = MXU_COLS // CHUNK
    shift = CHUNK.bit_length() - 1
    lane_p = lax.broadcasted_iota(jnp.int32, (CHUNK, MXU_COLS), 1)
    row_p = lax.broadcasted_iota(jnp.int32, (CHUNK, MXU_COLS), 0)
    col_p = lane_p & (CHUNK - 1)
    blk = [(lane_p >> shift) == u for u in range(group)]
    incl_p = row_p >= col_p
    strict_p = row_p > col_p
    eye_p = jnp.where(row_p == col_p, 1.0, 0.0).astype(F32)
    bd_r = lax.broadcasted_iota(jnp.int32, (MXU_COLS, MXU_COLS), 0)
    bd_c = lax.broadcasted_iota(jnp.int32, (MXU_COLS, MXU_COLS), 1)
    bd_mask = (bd_r >> shift) == (bd_c >> shift)
    ri = lax.broadcasted_iota(jnp.int32, (CHUNK, CHUNK), 0)
    ci = lax.broadcasted_iota(jnp.int32, (CHUNK, CHUNK), 1)
    ltri3 = jnp.concatenate([_bf(jnp.where(ri >= ci, 1.0, 0.0))] * 3, axis=1)
    g_r = lax.broadcasted_iota(jnp.int32, (MXU_COLS, 3 * LANES), 0) >> shift
    g_c = lax.broadcasted_iota(jnp.int32, (MXU_COLS, 3 * LANES), 1) & (LANES - 1)
    ones3 = jnp.ones((CHUNK, 3 * LANES), BF16)
    er = lax.broadcasted_iota(jnp.int32, (HEAD, HEAD), 0)
    ec = lax.broadcasted_iota(jnp.int32, (HEAD, HEAD), 1)
    eye_h = _bf(jnp.where(er == ec, 1.0, 0.0))
    heads = range(n_heads)
    hs = [slice(HEAD * h, HEAD * (h + 1)) for h in heads]

    def block_diag(x):
        return _bf(jnp.where(bd_mask, jnp.concatenate([x] * group, axis=0), 0.0))

    def pack_cols(cols):
        out = cols[-1]
        for u in reversed(range(group - 1)):
            out = jnp.where(blk[u], cols[u], out)
        return out

    def pack_diag(x):
        out = x[CHUNK * (group - 1):]
        for u in reversed(range(group - 1)):
            out = jnp.where(blk[u], x[CHUNK * u:CHUNK * (u + 1)], out)
        return out

    def stack(ref, rows, h0):
        return jnp.concatenate([ref[rows, hs[h]] for h in range(h0, h0 + group)], axis=0)

    def stack_cols(x, lo):
        return jnp.concatenate([x[:, lo + u:lo + u + 1] for u in range(group)], axis=0)

    def prep(i, q_ref, k_ref, v_ref, gb_ref, u0_s, wkqg_s, qk_s, kdt_s, gl_s):
        chunks = [i * PREP_CHUNKS + j for j in range(PREP_CHUNKS)]
        rows = [pl.ds(pl.multiple_of(c * CHUNK, CHUNK), CHUNK) for c in chunks]
        gbc = [gb_ref[r, :] for r in rows]
        gc = [_dot(ltri3, _bf(jnp.concatenate(_split3f(x), axis=0))) for x in gbc]
        glast = [x[CHUNK - 1:CHUNK, :] for x in gc]
        for j, c in enumerate(chunks):
            gl_s[c] = jnp.broadcast_to(glast[j], (SUBLANES, LANES))
        groups = [(j, h0) for j in range(PREP_CHUNKS) for h0 in range(0, n_heads, group)]
        ng = range(len(groups))
        kst = [stack(k_ref, rows[j], h0) for j, h0 in groups]
        qst = [stack(q_ref, rows[j], h0) for j, h0 in groups]
        kb = [_bf(x) for x in kst]
        pairs = [_dot_nt(jnp.concatenate([_bf(qst[g]), kb[g]], axis=0), kb[g]) for g in ng]
        yield
        gcol_p = [pack_cols([gc[j][:, h:h + 1] for h in range(h0, h0 + group)]) for j, h0 in groups]
        bcol_p = [pack_cols([gbc[j][:, n_heads + h:n_heads + h + 1] for h in range(h0, h0 + group)])
                  for j, h0 in groups]
        grow = [_dot_nt(ones3, _bf(jnp.where(g_c == g_r + h0,
                                               jnp.concatenate([jnp.concatenate([x] * group, axis=0)
                                                                for x in _split3f(gc[j])], axis=1), 0.0)))
                for j, h0 in groups]
        yield
        dec = [jnp.exp(jnp.where(incl_p, gcol_p[g] - grow[g], -jnp.inf)) for g in ng]
        a = [jnp.where(strict_p, bcol_p[g] * pack_diag(pairs[g][MXU_COLS:]) * dec[g], 0.0) for g in ng]
        tinv = [eye_p - x for x in a]
        pw = [_dot(_bf(-x), block_diag(-x)) for x in a]
        yield
        for _ in range(4):
            res = [_dot(jnp.concatenate([_bf(tinv[g]), _bf(pw[g])], axis=0), block_diag(pw[g])) for g in ng]
            tinv = [tinv[g] + res[g][:CHUNK] for g in ng]
            pw = [r[CHUNK:] for r in res]
            yield
        tinv = [tinv[g] + _dot(_bf(tinv[g]), block_diag(pw[g])) for g in ng]
        yield
        tbd = [block_diag(x) for x in tinv]
        gcol_s = [stack_cols(gc[j], h0) for j, h0 in groups]
        bcol_s = [stack_cols(gbc[j], n_heads + h0) for j, h0 in groups]
        gam_s = [jnp.exp(x) for x in gcol_s]
        rhs = [jnp.concatenate([stack(v_ref, rows[j], h0) * bcol_s[g], kst[g] * (bcol_s[g] * gam_s[g])], axis=1)
               for g, (j, h0) in enumerate(groups)]
        x0 = [_dot(tbd[g], _bf(rhs[g])) for g in ng]
        yield
        resid = [rhs[g] - x0[g] - _dot(block_diag(a[g]), _bf(x0[g])) for g in ng]
        yield
        sol = [x0[g] + _dot(tbd[g], _bf(resid[g])) for g in ng]
        yield
        kdt = [_dot_nt(eye_h, _bf(kst[g] * jnp.exp(stack_cols(jnp.broadcast_to(glast[j], (CHUNK, LANES)), h0)
                                                   - gcol_s[g])))
               for g, (j, h0) in enumerate(groups)]
        qk = [pack_diag(pairs[g][:MXU_COLS]) * dec[g] for g in ng]
        for g, (j, h0) in enumerate(groups):
            c = chunks[j]
            qg = qst[g] * gam_s[g]
            for u in range(group):
                r = slice(CHUNK * u, CHUNK * (u + 1))
                u0_s[c, h0 + u] = sol[g][r, :HEAD]
                wkqg_s[c, h0 + u] = _bf(jnp.concatenate([sol[g][r, HEAD:], qg[r]], axis=0))
                qk_s[c, h0 + u] = _bf(qk[g][:, r])
                kdt_s[c, h0 + u] = _bf(kdt[g][:, r])

    def scan(i, u0_s, wkqg_s, qk_s, kdt_s, gl_s):
        for j in range(PREP_CHUNKS):
            c = i * PREP_CHUNKS + j
            rows = pl.ds(pl.multiple_of(c * CHUNK, CHUNK), CHUNK)
            glr = gl_s[c]
            ws = [_dot(wkqg_s[c, h], _bf(s_scr[h])) for h in heads]
            yield
            ub = [_bf(u0_s[c, h] - ws[h][:CHUNK]) for h in heads]
            for h in heads:
                o_ref[rows, HEAD * h:HEAD * (h + 1)] = ws[h][CHUNK:] + _dot(qk_s[c, h], ub[h])
            yield
            for h in heads:
                s_scr[h] = jnp.exp(glr[0:1, h:h + 1]) * s_scr[h] + _dot(kdt_s[c, h], ub[h])
            yield

    def run(mine, other):
        def body(i, carry):
            _weave(project(i, *proj_sets[mine]), prep(i, *proj_sets[other], *prep_sets[other]),
                   scan(i, *prep_sets[mine]))
            return carry
        lax.fori_loop(0, n_chunks // PREP_CHUNKS, body, 0)

    @pl.when(step % 2 == 0)
    def _():
        run(0, 1)

    @pl.when(step % 2 == 1)
    def _():
        run(1, 0)

    sout_ref[...] = s_scr[...]


def _mixer_a_prompt(x, w, n_heads):
    t, d = x.shape
    aw = w["w_z"].shape[1]
    n_chunks = 2 * PREP_CHUNKS
    tt = n_chunks * CHUNK
    assert t % tt == 0 and n_chunks % PREP_CHUNKS == 0
    n_tiles = t // tt
    projected = [pltpu.VMEM((tt, aw), F32)] * 3 + [pltpu.VMEM((tt, LANES), F32)]
    prepared = [pltpu.VMEM((n_chunks, n_heads, CHUNK, HEAD), F32),
                pltpu.VMEM((n_chunks, n_heads, 2 * CHUNK, HEAD), BF16),
                pltpu.VMEM((n_chunks, n_heads, CHUNK, CHUNK), BF16),
                pltpu.VMEM((n_chunks, n_heads, HEAD, CHUNK), BF16),
                pltpu.VMEM((n_chunks, SUBLANES, LANES), F32)]
    scratch = [pltpu.VMEM((SUBLANES, 3 * aw), F32), pltpu.VMEM((n_heads, HEAD, HEAD), F32)]
    scratch += projected + projected + prepared + prepared
    first = lambda s: (jnp.minimum(s, n_tiles - 1), 0)
    last = lambda s: (jnp.maximum(s - 2, 0), 0)
    ins = [(x, pl.BlockSpec((tt, d), first))] + [_whole(w[name]) for name in
                                                  ("g_mix0", "w_qkv", "w_ab", "w_z", "w_aconv", "a_log", "dt_bias")]
    outs = [(jax.ShapeDtypeStruct((t, aw), F32), pl.BlockSpec((tt, aw), first)), _out_whole((SUBLANES, 3 * aw)),
            (jax.ShapeDtypeStruct((t, aw), F32), pl.BlockSpec((tt, aw), last)), _out_whole((n_heads, HEAD, HEAD))]
    z, tail, o, s1 = _call(functools.partial(_mixer_a_prompt_body, n_heads=n_heads, n_chunks=n_chunks),
                           "mixer_a_prompt", n_tiles + 2, ins, outs, scratch)
    return o, z, s1, tail


def _delta_step_body(qt_ref, kt_ref, v_ref, gbt_ref, s_ref, o_ref, so_ref, *, n_seq, n_heads):
    h = pl.program_id(0)
    decay = jnp.exp(gbt_ref[pl.ds(h, 1), :])
    beta = gbt_ref[pl.ds(n_heads + h, 1), :]
    for b in range(n_seq):
        s = decay[:, b:b + 1] * s_ref[b]
        kc = kt_ref[:, b:b + 1]
        u = beta[:, b:b + 1] * (v_ref[b:b + 1, :] - jnp.sum(kc * s, axis=0, keepdims=True))
        s = s + kc * u
        so_ref[b] = s
        o_ref[b:b + 1, :] = jnp.sum(qt_ref[:, b:b + 1] * s, axis=0, keepdims=True)


def _delta_step(q, k, v, gb, state, n_heads):
    n_seq = q.shape[0]
    col = pl.BlockSpec((HEAD, n_seq), lambda h: (h, 0))
    row = pl.BlockSpec((n_seq, HEAD), lambda h: (0, h))
    st = pl.BlockSpec((n_seq, None, HEAD, HEAD), lambda h: (0, h, 0, 0))
    gbt = gb.T
    return pl.pallas_call(
        functools.partial(_delta_step_body, n_seq=n_seq, n_heads=n_heads),
        out_shape=[jax.ShapeDtypeStruct(q.shape, F32), jax.ShapeDtypeStruct(state.shape, F32)],
        grid=(n_heads,), in_specs=[col, col, row, _const_spec(gbt), st], out_specs=[row, st],
        compiler_params=_params(), name="delta_step")(q.T, k.T, v, gbt, state)


def _delta_out_mix(o_ref, z_ref, res_ref, gout_ref, w_ref):
    parts = []
    for h in range(o_ref.shape[1] // HEAD):
        hs = slice(HEAD * h, HEAD * (h + 1))
        parts.append(_bf(_rms(o_ref[:, hs]) * gout_ref[...] * _silu(z_ref[:, hs])))
    return res_ref[...] + _dot(jnp.concatenate(parts, axis=1), w_ref[...])


def _attn_out_mix(refs, scratch, tm, dils):
    ng = len(dils)
    o_refs, l_refs = refs[:ng], refs[ng:2 * ng]
    res_ref, w_ref = refs[2 * ng:]
    get_o, lses = [], []
    for gi, dil in enumerate(dils):
        if dil == 1:
            get_o.append(lambda hh, r=o_refs[gi]: r[0, :, HEAD * hh:HEAD * (hh + 1)])
            lses.append(l_refs[gi][0])
        else:
            o_s, l_s = scratch.pop(0), scratch.pop(0)
            for r in range(dil):
                dst = pl.ds(r, tm // dil, stride=dil)
                for hh in range(Q_PER_GROUP):
                    o_s.at[hh][dst, :] = o_refs[gi][r, :, HEAD * hh:HEAD * (hh + 1)]
                l_s[dst, :] = l_refs[gi][r]
            get_o.append(lambda hh, r=o_s: r[hh])
            lses.append(l_s[...])
    parts = []
    for hh in range(Q_PER_GROUP):
        lh = [l[:, hh:hh + 1] for l in lses]
        mx = functools.reduce(jnp.maximum, lh)
        ex = [jnp.exp(x - mx) for x in lh]
        den = functools.reduce(lambda a, b: a + b, ex)
        acc = (ex[0] / den) * get_o[0](hh)
        for gi in range(1, ng):
            acc = acc + (ex[gi] / den) * get_o[gi](hh)
        parts.append(_bf(acc))
    return res_ref[...] + _dot(jnp.concatenate(parts, axis=1), w_ref[...])


def _mix_ffn_body(*refs, tm, per_row_state, dils):
    n_mix = 5 if dils is None else 2 * len(dils) + 2
    mix_in, refs = refs[:n_mix], refs[n_mix:]
    if per_row_state:
        p_ref, buf_ref, gffn_ref, wup_ref, wconv_ref, wdown_ref, gple_ref, wgate_ref, wproj_ref = refs[:9]
        out_ref, tail_ref = refs[9:11]
        scratch = list(refs[11:])
    else:
        p_ref, gffn_ref, wup_ref, wconv_ref, wdown_ref, gple_ref, wgate_ref, wproj_ref = refs[:8]
        out_ref, tail_ref = refs[8:10]
        scratch = list(refs[10:])
    act_s = scratch.pop()
    if not per_row_state:
        ubuf = scratch.pop()

        @pl.when(pl.program_id(0) == 0)
        def _():
            ubuf[...] = jnp.zeros(ubuf.shape, F32)

    h = _delta_out_mix(*mix_in) if dils is None else _attn_out_mix(mix_in, scratch, tm, dils)
    hn = _bf(_rms(h) * gffn_ref[...])
    dff = wdown_ref.shape[0]
    for c in range(dff // MXU_COLS):
        ys = []
        for half in range(2):
            lo = half * dff + MXU_COLS * c
            cols = slice(lo, lo + MXU_COLS)
            u = _dot(hn, wup_ref[:, cols])
            w = wconv_ref[:, cols]
            if per_row_state:
                y = buf_ref[:, cols] * w[0:1]
                y = y + buf_ref[:, 2 * dff + lo:2 * dff + lo + MXU_COLS] * w[1:2]
                y = y + u * w[2:3]
                tail_ref[:, cols] = u
            else:
                y = _causal_conv_tile(u, w, ubuf, cols)
                tail_ref[:, cols] = u[tm - SUBLANES:, :]
            ys.append(y)
        act_s[:, MXU_COLS * c:MXU_COLS * (c + 1)] = _bf(_silu(ys[0]) * ys[1])
    h2 = h + _dot(act_s[...], wdown_ref[...])
    gate = _sigmoid(_dot(_bf(_rms(h2) * gple_ref[...]), wgate_ref[...]))
    out_ref[...] = h2 + gate * _dot(_bf(p_ref[...]), wproj_ref[...])


def _mix_ffn(mix_ins, mix_scratch, dils, p, w, layer, m, tm, state):
    per_row = state is not None
    d, dff = w["w_down"].shape[2], w["w_down"].shape[1]
    ins = list(mix_ins) + [_layer_rows(p, layer, tm)] + ([_rows(state.reshape(m, -1), tm)] if per_row else [])
    ins += [_layer(w[name], layer) for name in ("g_ffn", "w_up", "w_fconv", "w_down", "g_ple", "w_gate", "w_proj")]
    outs = [_out_rows(m, d, tm), _out_whole((m if per_row else SUBLANES, 2 * dff))]
    scratch = list(mix_scratch) + ([] if per_row else [pltpu.VMEM((SUBLANES, 2 * dff), F32)])
    scratch.append(pltpu.VMEM((tm, dff), BF16))
    return _call(functools.partial(_mix_ffn_body, tm=tm, per_row_state=per_row, dils=dils), f"mix_ffn{layer}",
                 m // tm, ins, outs, scratch)


def _kvq_body(*refs, tm, dils):
    h_ref, tab_ref, gkv_ref, gmix_ref, wkv_ref, wq_ref, gk_ref, gq_ref = refs[:8]
    if dils is None:
        q_ref, k_ref, v_ref = refs[8:]
    else:
        ng = len(dils)
        k_ref, v_ref = refs[8:10]
        qd, kd, vd = refs[10:10 + ng], refs[10 + ng:10 + 2 * ng], refs[10 + 2 * ng:10 + 3 * ng]
        stage = refs[10 + 3 * ng]
    h = h_ref[...]
    base = _rms(h)
    hkv = _bf(base * gkv_ref[...])
    hq = _bf(base * gmix_ref[...])
    tab = tab_ref[...]
    lane = lax.broadcasted_iota(jnp.int32, tab.shape, 1)
    half = ROPE_DIM // 2
    cos = jnp.where(lane < half, tab, jnp.where(lane < ROPE_DIM, pltpu.roll(tab, half, 1), 1.0))
    sa = jnp.where((lane >= half) & (lane < ROPE_DIM), tab, 0.0)
    sb = jnp.where(lane < half, -pltpu.roll(tab, HEAD - half, 1), 0.0)
    kw = KV_PER_GROUP * HEAD

    def norm_rope(x, g):
        y = _rms(x) * g
        return y * cos + pltpu.roll(y, ROPE_DIM // 2, 1) * sa + pltpu.roll(y, HEAD - ROPE_DIM // 2, 1) * sb

    def heads(x, g):
        return jnp.concatenate([norm_rope(x[:, HEAD * j:HEAD * (j + 1)], g) for j in range(MXU_COLS // HEAD)], axis=1)

    def by_class(x, dst_ref, dil, cols):
        if dil == 1:
            dst_ref[0, :, cols] = _bf(x)
        else:
            for j in range(MXU_COLS // LANES):
                stage[j] = x[:, LANES * j:LANES * (j + 1)]
            for r in range(dil):
                for j in range(MXU_COLS // LANES):
                    lo = cols.start + LANES * j
                    dst_ref[r, :, lo:lo + LANES] = _bf(stage.at[j][pl.ds(r, tm // dil, stride=dil), :])

    assert kw == MXU_COLS and Q_PER_GROUP * HEAD == 2 * MXU_COLS
    n_kv = k_ref.shape[1]

    def project(kind, c):
        lo = MXU_COLS * c + (n_kv if kind == "v" else 0)
        return _dot(hq if kind == "q" else hkv, (wq_ref if kind == "q" else wkv_ref)[:, lo:lo + MXU_COLS])

    def finish(kind, c, x):
        cols = slice(MXU_COLS * c, MXU_COLS * (c + 1))
        if kind == "k":
            x = heads(x, gk_ref[...])
            k_ref[:, cols] = x
        elif kind == "v":
            v_ref[:, cols] = x
        else:
            x = heads(x, gq_ref[...])
            if dils is None:
                q_ref[:, cols] = x
        if dils is not None:
            if kind == "q":
                by_class(x, qd[c // 2], dils[c // 2], slice(MXU_COLS * (c % 2), MXU_COLS * (c % 2 + 1)))
            else:
                by_class(x, (kd if kind == "k" else vd)[c], dils[c], slice(0, kw))

    jobs = [(kind, c) for kind, n in (("k", n_kv), ("v", n_kv), ("q", wq_ref.shape[1])) for c in range(n // MXU_COLS)]
    pending = []
    for kind, c in jobs:
        pending.append((kind, c, project(kind, c)))
        if len(pending) > KVQ_AHEAD:
            finish(*pending.pop(0))
    for job in pending:
        finish(*job)


def _kvq(h, table, w, tm, dils):
    m, d = h.shape
    n_q, n_kv = w["w_q"].shape[1], w["w_kv"].shape[1] // 2
    qw, kw = Q_PER_GROUP * HEAD, KV_PER_GROUP * HEAD
    consts = [w["g_kv"], w["g_mix1"], w["w_kv"], w["w_q"], w["g_k"], w["g_q"]]
    rows = [h, table]
    row_spec = lambda c: pl.BlockSpec((tm, c), lambda i: (i, 0))
    in_specs = [row_spec(a.shape[1]) for a in rows] + [_const_spec(a) for a in consts]
    nat = lambda c: jax.ShapeDtypeStruct((m, c), F32)
    if dils is None:
        out_shape = [nat(n_q), nat(n_kv), nat(n_kv)]
        out_specs, scratch = [row_spec(n_q), row_spec(n_kv), row_spec(n_kv)], []
    else:
        out_shape, out_specs = [nat(n_kv), nat(n_kv)], [row_spec(n_kv), row_spec(n_kv)]
        for width in (qw, kw, kw):
            for dil in dils:
                assert tm % (dil * 2 * SUBLANES) == 0
                out_shape.append(jax.ShapeDtypeStruct((dil, m // dil, width), BF16))
                out_specs.append(pl.BlockSpec((dil, tm // dil, width), lambda i: (0, i, 0)))
        scratch = [pltpu.VMEM((MXU_COLS // LANES, tm, LANES), F32)]
    return pl.pallas_call(functools.partial(_kvq_body, tm=tm, dils=dils), out_shape=out_shape, grid=(m // tm,),
                          in_specs=in_specs, out_specs=out_specs, scratch_shapes=scratch,
                          compiler_params=_params(), name="kvq")(*rows, *consts)


def _rope_table(pos):
    half = ROPE_DIM // 2
    inv_freq = ROPE_THETA ** (-jnp.arange(half, dtype=F32) * 2.0 / ROPE_DIM)
    ang = pos.astype(F32)[:, None] * inv_freq[None, :]
    return jnp.concatenate([jnp.cos(ang), jnp.sin(ang), jnp.zeros((pos.shape[0], HEAD - ROPE_DIM), F32)], axis=1)


def _attn_prompt_body(q_ref, kc_ref, kp_ref, vc_ref, vp_ref, o_ref, lse_ref, *, n, blocks):
    first_lo = jnp.where(pl.program_id(1) > 0, 0, n)
    qi = lax.broadcasted_iota(jnp.int32, (Q_REP * n, 2 * n), 0) & (n - 1)
    kj = lax.broadcasted_iota(jnp.int32, (Q_REP * n, 2 * n), 1)
    band = (kj >= qi) & (kj <= qi + n)
    lane = lax.broadcasted_iota(jnp.int32, (n, LANES), 1)
    for j in range(blocks):
        rows = slice(n * j, n * (j + 1))
        lse_tile = jnp.zeros((n, LANES), F32)
        for g in range(KV_PER_GROUP):
            hs = slice(HEAD * g, HEAD * (g + 1))
            if j == 0:
                kprev, vprev = kp_ref[:, hs], vp_ref[:, hs]
                valid = band & (kj >= first_lo)
            else:
                prev = slice(n * (j - 1), n * j)
                kprev, vprev = kc_ref[prev, hs], vc_ref[prev, hs]
                valid = band
            kwin = jnp.concatenate([kprev, kc_ref[rows, hs]], axis=0)
            vwin = jnp.concatenate([vprev, vc_ref[rows, hs]], axis=0)
            q0 = Q_REP * HEAD * g
            qg = jnp.concatenate([q_ref[rows, q0 + HEAD * e:q0 + HEAD * (e + 1)] for e in range(Q_REP)], axis=0)
            s = jnp.where(valid, _dot_nt(qg, kwin) * (HEAD ** -0.5), -jnp.inf)
            mx = jnp.max(s, axis=-1, keepdims=True)
            p = jnp.exp(s - mx)
            den = jnp.sum(p, axis=-1, keepdims=True)
            o = _dot(_bf(p), vwin) / den
            lse = mx + jnp.log(den)
            for e in range(Q_REP):
                o_ref[rows, q0 + HEAD * e:q0 + HEAD * (e + 1)] = o[n * e:n * (e + 1)]
                lse_tile = jnp.where(lane == Q_REP * g + e, lse[n * e:n * (e + 1)], lse_tile)
        lse_ref[rows, :] = lse_tile


def _attn_prompt(qd, kd, vd, gi, window):
    dil, ln, qw = qd.shape
    kw = kd.shape[2]
    n = window // dil
    blocks = min(8, ln // n)
    rows = n * blocks
    assert n == HEAD and ln % rows == 0
    cur = lambda r, b: (r, b, 0)
    prev = lambda r, b: (r, jnp.maximum(b * blocks - 1, 0), 0)
    return pl.pallas_call(
        functools.partial(_attn_prompt_body, n=n, blocks=blocks),
        out_shape=[jax.ShapeDtypeStruct((dil, ln, qw), F32), jax.ShapeDtypeStruct((dil, ln, LANES), F32)],
        grid=(dil, ln // rows),
        in_specs=[pl.BlockSpec((None, rows, qw), cur), pl.BlockSpec((None, rows, kw), cur),
                  pl.BlockSpec((None, n, kw), prev), pl.BlockSpec((None, rows, kw), cur),
                  pl.BlockSpec((None, n, kw), prev)],
        out_specs=[pl.BlockSpec((None, rows, qw), cur), pl.BlockSpec((None, rows, LANES), cur)],
        compiler_params=_params(2), name=f"attn_prompt{gi}")(qd, kd, kd, vd, vd)


def _attn_sample_body(q_ref, kn_ref, vn_ref, *refs, bb):
    n_groups = len(GROUPS)
    caches, outs, lses = refs[:n_groups], refs[n_groups:2 * n_groups], refs[2 * n_groups:]
    lane = lax.broadcasted_iota(jnp.int32, (1, LANES), 1)
    kw = KV_PER_GROUP * HEAD
    for gi in range(n_groups):
        c_ref, o_ref, l_ref = caches[gi], outs[gi], lses[gi]
        for b in range(bb):
            lrow = jnp.zeros((1, LANES), F32)
            for g in range(KV_PER_GROUP):
                kc = c_ref[b, :, 0, g, :]
                vc = c_ref[b, :, 1, g, :]
                kcol = kw * gi + HEAD * g
                kn = kn_ref[b:b + 1, kcol:kcol + HEAD]
                vn = vn_ref[b:b + 1, kcol:kcol + HEAD]
                for e in range(Q_REP):
                    oc = HEAD * (Q_REP * g + e)
                    qv = q_ref[b:b + 1, Q_PER_GROUP * HEAD * gi + oc:Q_PER_GROUP * HEAD * gi + oc + HEAD]
                    s = jnp.sum(kc * qv, axis=1, keepdims=True) * (HEAD ** -0.5)
                    sn = jnp.sum(kn * qv, axis=1, keepdims=True) * (HEAD ** -0.5)
                    mx = jnp.maximum(jnp.max(s, axis=0, keepdims=True), sn)
                    p = jnp.exp(s - mx)
                    pn = jnp.exp(sn - mx)
                    den = jnp.sum(p, axis=0, keepdims=True) + pn
                    o_ref[b:b + 1, oc:oc + HEAD] = (jnp.sum(p * vc, axis=0, keepdims=True) + pn * vn) / den
                    lrow = jnp.where(lane == Q_REP * g + e, mx + jnp.log(den), lrow)
            l_ref[b:b + 1, :] = lrow


def _attn_sample(q, k, v, caches):
    n_seq = q.shape[0]
    bb = SUBLANES
    assert n_seq % bb == 0
    qw, kw = Q_PER_GROUP * HEAD, KV_PER_GROUP * HEAD
    views, cache_specs = [], []
    for cache, (window, dil) in zip(caches, GROUPS):
        n = window // dil
        assert cache.shape[1] == window and n == HEAD
        views.append(cache.reshape(n_seq, n, dil, *cache.shape[2:]))
        cache_specs.append(pl.BlockSpec((bb, n, None, *cache.shape[2:]), lambda i: (i, 0, 0, 0, 0, 0)))
    row = lambda c: pl.BlockSpec((bb, c), lambda i: (i, 0))
    n_groups = len(GROUPS)
    res = pl.pallas_call(
        functools.partial(_attn_sample_body, bb=bb),
        out_shape=[jax.ShapeDtypeStruct((n_seq, qw), F32)] * n_groups
        + [jax.ShapeDtypeStruct((n_seq, LANES), F32)] * n_groups,
        grid=(n_seq // bb,),
        in_specs=[row(q.shape[1]), row(k.shape[1]), row(v.shape[1])] + cache_specs,
        out_specs=[row(qw)] * n_groups + [row(LANES)] * n_groups,
        compiler_params=_params(), name="attn_sample")(q, k, v, *views)
    return res[:n_groups], res[n_groups:]


def _trunk(x, p, pos, state, w, tm, tm_wide):
    m, d = x.shape
    per_row = state is not None
    n_heads = w["n_heads"]
    aw = w["w_z"].shape[1]

    if per_row:
        consts = [w["g_mix0"], w["w_qkv"], w["w_ab"], w["w_z"], w["w_aconv"], w["a_log"], w["dt_bias"],
                  jnp.swapaxes(state["qkv_conv"], 0, 1)]
        q, k, v, z, gb, qkv_tail = _row_call(functools.partial(_a_in_step_body, n_heads=n_heads), "a_in_step", m, tm,
                                             [x], consts, [(aw, F32)] * 4 + [(LANES, F32)],
                                             const_outs=[((m, 3 * aw), F32)])
        o, s1 = _delta_step(q, k, v, gb, state["delta"], n_heads)
    else:
        o, z, s1, qkv_tail = _mixer_a_prompt(x, w, n_heads)
    mix = [_rows(o, tm_wide), _rows(z, tm_wide), _rows(x, tm_wide), _whole(w["g_aout"]), _whole(w["w_aout"])]
    h, ffn_tail0 = _mix_ffn(mix, [], None, p, w, 0, m, tm_wide, state["ffn_conv"][0] if per_row else None)

    table = _rope_table(pos)
    if per_row:
        qa, ka, va = _kvq(h, table, w, tm, None)
        outs, lses = _attn_sample(qa, ka, va, state["caches"])
        outs, lses = [a[None] for a in outs], [a[None] for a in lses]
        dils = (1,) * len(GROUPS)
    else:
        dils = tuple(dil for _, dil in GROUPS)
        ng = len(dils)
        res = _kvq(h, table, w, tm, dils)
        ka, va = res[:2]
        qd, kd, vd = res[2:2 + ng], res[2 + ng:2 + 2 * ng], res[2 + 2 * ng:]
        att = [_attn_prompt(qd[gi], kd[gi], vd[gi], gi, window) for gi, (window, _) in enumerate(GROUPS)]
        outs, lses = [a[0] for a in att], [a[1] for a in att]
    mix = [_grouped(a, tm_wide) for a in (*outs, *lses)] + [_rows(h, tm_wide), _whole(w["w_o"])]
    mix_scratch = []
    for dil in dils:
        if dil > 1:
            assert tm_wide % (dil * SUBLANES) == 0
            mix_scratch += [pltpu.VMEM((Q_PER_GROUP, tm_wide, HEAD), F32), pltpu.VMEM((tm_wide, LANES), F32)]
    h, ffn_tail1 = _mix_ffn(mix, mix_scratch, dils, p, w, 1, m, tm_wide, state["ffn_conv"][1] if per_row else None)
    return h, s1, qkv_tail, (ffn_tail0, ffn_tail1), ka, va


def _kv_rows(k, v, gi, n_rows):
    kw = KV_PER_GROUP * HEAD
    sl = slice(kw * gi, kw * (gi + 1))
    shape = (n_rows, 1, KV_PER_GROUP, HEAD)
    return jnp.concatenate([k[k.shape[0] - n_rows:, sl].reshape(shape), v[v.shape[0] - n_rows:, sl].reshape(shape)],
                           axis=1)


def kernel(x_prompt, x_sample, p_prompt, p_sample, state_delta, state_qkv_conv, state_ffn_conv, cache_kv_w128, cache_kv_w512, cache_kv_w2048, g_mix_norm, g_ffn_norm, w_ffn_up, w_ffn_conv, w_ffn_down, g_ple_norm, w_ple_gate, w_ple_proj, w_a_in, w_a_conv, a_log, a_dt_bias, g_a_out_norm, w_a_out, g_kv_norm, w_kv, g_k_norm, w_q, g_q_norm, w_o):
    n_heads = a_log.shape[1]
    aw = n_heads * HEAD
    assert w_a_in.shape[0] == 1 and w_q.shape[0] == 1 and w_a_in.shape[2] == 4 * aw + 2 * n_heads
    row = lambda a: a.reshape(1, -1)
    lane_pad = lambda a: jnp.pad(a, ((0, 0), (0, LANES - a.shape[1])))
    w = {
        "n_heads": n_heads,
        "g_mix0": row(g_mix_norm[0]), "g_mix1": row(g_mix_norm[1]), "g_ffn": g_ffn_norm[:, None], "g_ple": g_ple_norm[:, None],
        "w_up": _bf(w_ffn_up), "w_fconv": w_ffn_conv, "w_down": _bf(w_ffn_down),
        "w_gate": _bf(w_ple_gate), "w_proj": _bf(w_ple_proj),
        "w_qkv": _bf(w_a_in[0][:, :3 * aw]), "w_ab": _bf(lane_pad(w_a_in[0][:, 3 * aw:3 * aw + 2 * n_heads])),
        "w_z": _bf(w_a_in[0][:, 3 * aw + 2 * n_heads:]), "w_aconv": w_a_conv[0],
        "a_log": lane_pad(a_log), "dt_bias": lane_pad(a_dt_bias),
        "g_aout": row(g_a_out_norm[0]), "w_aout": _bf(w_a_out[0]),
        "g_kv": row(g_kv_norm), "w_kv": _bf(w_kv), "g_k": row(g_k_norm),
        "w_q": _bf(w_q[0]), "g_q": row(g_q_norm[0]), "w_o": _bf(w_o[0]),
    }
    assert x_prompt.shape[0] == 1 and x_sample.shape[1] == 1
    seq, d = x_prompt.shape[1], x_prompt.shape[2]
    n_seq = x_sample.shape[0]

    yp, sd_p, qkv_tail_p, ffn_tails_p, kp, vp = _trunk(
        x_prompt.reshape(seq, d), p_prompt.reshape(p_prompt.shape[0], seq, -1), jnp.arange(seq, dtype=jnp.int32), None, w,
        tm=256, tm_wide=512)
    a_keep, f_keep = state_qkv_conv.shape[2], state_ffn_conv.shape[2]
    sq_p = qkv_tail_p[SUBLANES - a_keep:][None, None]
    sf_p = jnp.stack([t[SUBLANES - f_keep:] for t in ffn_tails_p])[:, None]
    kv_p = [_kv_rows(kp, vp, gi, min(window, seq))[None] for gi, (window, _) in enumerate(GROUPS)]

    state = {"delta": state_delta[0], "qkv_conv": state_qkv_conv[0], "ffn_conv": state_ffn_conv,
             "caches": (cache_kv_w128, cache_kv_w512, cache_kv_w2048)}
    ys, sd_s, qkv_tail_s, ffn_tails_s, ks, vs = _trunk(
        x_sample.reshape(n_seq, d), p_sample.reshape(p_sample.shape[0], n_seq, -1), jnp.full((n_seq,), PAST_LEN, jnp.int32),
        state, w, tm=n_seq, tm_wide=n_seq)
    sq_s = jnp.concatenate([state_qkv_conv[0][:, 1:], qkv_tail_s[:, None]], axis=1)[None]
    sf_s = jnp.stack([jnp.concatenate([state_ffn_conv[i][:, 1:], ffn_tails_s[i][:, None]], axis=1)
                      for i in range(2)])
    kv_s = [_kv_rows(ks, vs, gi, n_seq).reshape(n_seq, 1, 2, KV_PER_GROUP, HEAD) for gi in range(len(GROUPS))]

    return (yp[None], ys[:, None], sd_p[None, None], sq_p, sf_p, kv_p[0], kv_p[1], kv_p[2],
            sd_s[None], sq_s, sf_s, kv_s[0], kv_s[1], kv_s[2])
```

```python
import functools

import jax
import jax.numpy as jnp
from jax import lax
from jax.experimental import pallas as pl
from jax.experimental.pallas import tpu as pltpu

F32 = jnp.float32
BF16 = jnp.bfloat16

EPS = 1e-6
HEAD = 128
CHUNK = 64
GROUPS = ((128, 1), (512, 4), (2048, 16))
KV_PER_GROUP = 2
Q_REP = 2
Q_PER_GROUP = KV_PER_GROUP * Q_REP
ROPE_DIM = HEAD // 4
ROPE_THETA = 500000.0
PAST_LEN = 16384

LANES = 128
SUBLANES = 8
MXU_COLS = 256
VMEM_LIMIT_BYTES = 56 * 1024 * 1024

PREP_CHUNKS = 4


def _dot(a, b):
    return jnp.dot(a, b, preferred_element_type=F32)


def _dot_nt(a, b):
    return lax.dot_general(a, b, (((1,), (1,)), ((), ())), preferred_element_type=F32)


def _bf(x):
    return x.astype(BF16)


def _rms(x):
    return x * lax.rsqrt(jnp.mean(x * x, axis=-1, keepdims=True) + EPS)


def _sigmoid(x):
    return 1.0 / (1.0 + jnp.exp(-x))


def _silu(x):
    return x * _sigmoid(x)


def _split3f(x):
    hi = _bf(x).astype(F32)
    r = x - hi
    mid = _bf(r).astype(F32)
    return hi, mid, _bf(r - mid).astype(F32)


def _causal_conv_tile(x, w, carry_ref, cols):
    rows = x.shape[0]
    first = lax.broadcasted_iota(jnp.int32, (SUBLANES, x.shape[1]), 0) == 0
    z = x * w[0:1]
    for j in range(1, w.shape[0]):
        rolled = pltpu.roll(z, 1, 0)
        head = jnp.where(first, carry_ref[j - 1:j, cols], rolled[0:SUBLANES])
        carry_ref[j - 1:j, cols] = z[rows - 1:, :]
        z = jnp.concatenate([head, rolled[SUBLANES:]], axis=0) + x * w[j:j + 1]
    return z


def _const_spec(a):
    nd = a.ndim
    return pl.BlockSpec(a.shape, lambda *_: (0,) * nd, pipeline_mode=pl.Buffered(1))


def _params(n_axes=1):
    return pltpu.CompilerParams(dimension_semantics=("arbitrary",) * n_axes,
                                vmem_limit_bytes=VMEM_LIMIT_BYTES)


def _row_call(body, name, m, tm, row_ins, const_ins, row_outs, const_outs=(), scratch=()):
    assert m % tm == 0
    in_specs = [pl.BlockSpec((tm, a.shape[1]), lambda i: (i, 0)) for a in row_ins]
    in_specs += [_const_spec(a) for a in const_ins]
    out_shape = [jax.ShapeDtypeStruct((m, c), dt) for c, dt in row_outs]
    out_specs = [pl.BlockSpec((tm, c), lambda i: (i, 0)) for c, _ in row_outs]
    for shape, dt in const_outs:
        out_shape.append(jax.ShapeDtypeStruct(shape, dt))
        out_specs.append(pl.BlockSpec(shape, lambda i, _n=len(shape): (0,) * _n))
    return pl.pallas_call(body, out_shape=out_shape, grid=(m // tm,), in_specs=in_specs, out_specs=out_specs,
                          scratch_shapes=list(scratch), compiler_params=_params(), name=name)(*row_ins, *const_ins)


def _call(body, name, steps, ins, outs, scratch=()):
    return pl.pallas_call(body, out_shape=[o for o, _ in outs], grid=(steps,), in_specs=[sp for _, sp in ins],
                          out_specs=[sp for _, sp in outs], scratch_shapes=list(scratch),
                          compiler_params=_params(), name=name)(*[a for a, _ in ins])


def _rows(a, tm):
    return a, pl.BlockSpec((tm, a.shape[1]), lambda i: (i, 0))


def _whole(a):
    return a, _const_spec(a)


def _layer(a, layer):
    nd = a.ndim
    return a, pl.BlockSpec((None,) + a.shape[1:], lambda i: (layer,) + (0,) * (nd - 1), pipeline_mode=pl.Buffered(1))


def _layer_rows(a, layer, tm):
    return a, pl.BlockSpec((None, tm, a.shape[2]), lambda i: (layer, i, 0))


def _grouped(a, tm):
    return a, pl.BlockSpec((a.shape[0], tm // a.shape[0], a.shape[2]), lambda i: (0, i, 0))


def _out_rows(m, c, tm):
    return jax.ShapeDtypeStruct((m, c), F32), pl.BlockSpec((tm, c), lambda i: (i, 0))


def _out_whole(shape):
    return jax.ShapeDtypeStruct(shape, F32), pl.BlockSpec(shape, lambda i: (0,) * len(shape))


def _qkv_activation(y, part):
    a = _silu(y)
    if part == 2:
        return a
    halves = []
    for hh in range(MXU_COLS // HEAD):
        ah = a[:, HEAD * hh:HEAD * (hh + 1)]
        nrm = ah * lax.rsqrt(jnp.sum(ah * ah, axis=-1, keepdims=True) + EPS)
        halves.append(nrm * (HEAD ** -0.5) if part == 0 else nrm)
    return jnp.concatenate(halves, axis=1)


def _decay_and_beta(pab, alog_ref, dtb_ref, n_heads):
    lane = lax.broadcasted_iota(jnp.int32, pab.shape, 1)
    ap = pab + dtb_ref[...]
    softplus = jnp.maximum(ap, 0.0) + jnp.log1p(jnp.exp(-jnp.abs(ap)))
    return jnp.where(lane < n_heads, -jnp.exp(alog_ref[...]) * softplus, _sigmoid(pab))


def _a_in_step_body(x_ref, g_ref, wqkv_ref, wab_ref, wz_ref, wconv_ref, alog_ref, dtb_ref, buf_ref,
                    q_ref, k_ref, v_ref, z_ref, gb_ref, tail_ref, *, n_heads):
    hn = _bf(_rms(x_ref[...]) * g_ref[...])
    outs = (q_ref, k_ref, v_ref)
    per_part = q_ref.shape[1] // MXU_COLS
    for c in range(3 * per_part):
        cols = slice(MXU_COLS * c, MXU_COLS * (c + 1))
        pq = _dot(hn, wqkv_ref[:, cols])
        w = wconv_ref[:, cols]
        y = buf_ref[0, :, cols] * w[0:1]
        y = y + buf_ref[1, :, cols] * w[1:2]
        y = y + buf_ref[2, :, cols] * w[2:3]
        y = y + pq * w[3:4]
        tail_ref[:, cols] = pq
        part, sub = divmod(c, per_part)
        outs[part][:, MXU_COLS * sub:MXU_COLS * (sub + 1)] = _qkv_activation(y, part)
    gb_ref[...] = _decay_and_beta(_dot(hn, wab_ref[...]), alog_ref, dtb_ref, n_heads)
    z_ref[...] = _dot(hn, wz_ref[...])


def _weave(*stages):
    live = list(stages)
    while live:
        for g in list(live):
            try:
                next(g)
            except StopIteration:
                live.remove(g)


def _mixer_a_prompt_body(x_ref, g_ref, wqkv_ref, wab_ref, wz_ref, wconv_ref, alog_ref, dtb_ref,
                         z_ref, tail_ref, o_ref, sout_ref, cbuf, s_scr, *bufs, n_heads, n_chunks):
    step = pl.program_id(0)
    n_proj = 4
    proj_sets = bufs[:n_proj], bufs[n_proj:2 * n_proj]
    rest = bufs[2 * n_proj:]
    prep_sets = rest[:len(rest) // 2], rest[len(rest) // 2:]

    @pl.when(step == 0)
    def _():
        cbuf[...] = jnp.zeros(cbuf.shape, F32)
        s_scr[...] = jnp.zeros(s_scr.shape, F32)
        for ref in (*proj_sets[1], *prep_sets[0]):
            ref[...] = jnp.zeros(ref.shape, ref.dtype)

    sub = n_chunks // 2 * CHUNK

    def project(i, q_s, k_s, v_s, gb_s):
        rows = pl.ds(pl.multiple_of(i * sub, sub), sub)
        hn = _bf(_rms(x_ref[rows, :]) * g_ref[...])
        outs = (q_s, k_s, v_s)
        per_part = q_s.shape[1] // MXU_COLS
        for c in range(3 * per_part):
            cols = slice(MXU_COLS * c, MXU_COLS * (c + 1))
            pq = _dot(hn, wqkv_ref[:, cols])
            y = _causal_conv_tile(pq, wconv_ref[:, cols], cbuf, cols)
            tail_ref[:, cols] = pq[sub - SUBLANES:, :]
            part, sc = divmod(c, per_part)
            outs[part][rows, MXU_COLS * sc:MXU_COLS * (sc + 1)] = _qkv_activation(y, part)
            yield
        gb_s[rows, :] = _decay_and_beta(_dot(hn, wab_ref[...]), alog_ref, dtb_ref, n_heads)
        z_ref[rows, :] = _dot(hn, wz_ref[...])
        yield

    group = MXU_COLS // CHUNK
    shift = CHUNK.bit_length() - 1
    lane_p = lax.broadcasted_iota(jnp.int32, (CHUNK, MXU_COLS), 1)
    row_p = lax.broadcasted_iota(jnp.int32, (CHUNK, MXU_COLS), 0)
    col_p = lane_p & (CHUNK - 1)
    blk = [(lane_p >> shift) == u for u in range(group)]
    incl_p = row_p >= col_p
    strict_p = row_p > col_p
    eye_p = jnp.where(row_p == col_p, 1.0, 0.0).astype(F32)
    bd_r = lax.broadcasted_iota(jnp.int32, (MXU_COLS, MXU_COLS), 0)
    bd_c = lax.broadcasted_iota(jnp.int32, (MXU_COLS, MXU_COLS), 1)
    bd_mask = (bd_r >> shift) == (bd_c >> shift)
    ri = lax.broadcasted_iota(jnp.int32, (CHUNK, CHUNK), 0)
    ci = lax.broadcasted_iota(jnp.int32, (CHUNK, CHUNK), 1)
    ltri3 = jnp.concatenate([_bf(jnp.where(ri >= ci, 1.0, 0.0))] * 3, axis=1)
    g_r = lax.broadcasted_iota(jnp.int32, (MXU_COLS, 3 * LANES), 0) >> shift
    g_c = lax.broadcasted_iota(jnp.int32, (MXU_COLS, 3 * LANES), 1) & (LANES - 1)
    ones3 = jnp.ones((CHUNK, 3 * LANES), BF16)
    er = lax.broadcasted_iota(jnp.int32, (HEAD, HEAD), 0)
    ec = lax.broadcasted_iota(jnp.int32, (HEAD, HEAD), 1)
    eye_h = _bf(jnp.where(er == ec, 1.0, 0.0))
    heads = range(n_heads)
    hs = [slice(HEAD * h, HEAD * (h + 1)) for h in heads]

    def block_diag(x):
        return _bf(jnp.where(bd_mask, jnp.concatenate([x] * group, axis=0), 0.0))

    def pack_cols(cols):
        out = cols[-1]
        for u in reversed(range(group - 1)):
            out = jnp.where(blk[u], cols[u], out)
        return out

    def pack_diag(x):
        out = x[CHUNK * (group - 1):]
        for u in reversed(range(group - 1)):
            out = jnp.where(blk[u], x[CHUNK * u:CHUNK * (u + 1)], out)
        return out

    def stack(ref, rows, h0):
        return jnp.concatenate([ref[rows, hs[h]] for h in range(h0, h0 + group)], axis=0)

    def stack_cols(x, lo):
        return jnp.concatenate([x[:, lo + u:lo + u + 1] for u in range(group)], axis=0)

    def prep(i, q_ref, k_ref, v_ref, gb_ref, u0_s, wkqg_s, qk_s, kdt_s, gl_s):
        chunks = [i * PREP_CHUNKS + j for j in range(PREP_CHUNKS)]
        rows = [pl.ds(pl.multiple_of(c * CHUNK, CHUNK), CHUNK) for c in chunks]
        gbc = [gb_ref[r, :] for r in rows]
        gc = [_dot(ltri3, _bf(jnp.concatenate(_split3f(x), axis=0))) for x in gbc]
        glast = [x[CHUNK - 1:CHUNK, :] for x in gc]
        for j, c in enumerate(chunks):
            gl_s[c] = jnp.broadcast_to(glast[j], (SUBLANES, LANES))
        groups = [(j, h0) for j in range(PREP_CHUNKS) for h0 in range(0, n_heads, group)]
        ng = range(len(groups))
        kst = [stack(k_ref, rows[j], h0) for j, h0 in groups]
        qst = [stack(q_ref, rows[j], h0) for j, h0 in groups]
        kb = [_bf(x) for x in kst]
        pairs = [_dot_nt(jnp.concatenate([_bf(qst[g]), kb[g]], axis=0), kb[g]) for g in ng]
        yield
        gcol_p = [pack_cols([gc[j][:, h:h + 1] for h in range(h0, h0 + group)]) for j, h0 in groups]
        bcol_p = [pack_cols([gbc[j][:, n_heads + h:n_heads + h + 1] for h in range(h0, h0 + group)])
                  for j, h0 in groups]
        grow = [_dot_nt(ones3, _bf(jnp.where(g_c == g_r + h0,
                                               jnp.concatenate([jnp.concatenate([x] * group, axis=0)
                                                                for x in _split3f(gc[j])], axis=1), 0.0)))
                for j, h0 in groups]
        yield
        dec = [jnp.exp(jnp.where(incl_p, gcol_p[g] - grow[g], -jnp.inf)) for g in ng]
        a = [jnp.where(strict_p, bcol_p[g] * pack_diag(pairs[g][MXU_COLS:]) * dec[g], 0.0) for g in ng]
        tinv = [eye_p - x for x in a]
        pw = [_dot(_bf(-x), block_diag(-x)) for x in a]
        yield
        for _ in range(4):
            res = [_dot(jnp.concatenate([_bf(tinv[g]), _bf(pw[g])], axis=0), block_diag(pw[g])) for g in ng]
            tinv = [tinv[g] + res[g][:CHUNK] for g in ng]
            pw = [r[CHUNK:] for r in res]
            yield
        tinv = [tinv[g] + _dot(_bf(tinv[g]), block_diag(pw[g])) for g in ng]
        yield
        tbd = [block_diag(x) for x in tinv]
        gcol_s = [stack_cols(gc[j], h0) for j, h0 in groups]
        bcol_s = [stack_cols(gbc[j], n_heads + h0) for j, h0 in groups]
        gam_s = [jnp.exp(x) for x in gcol_s]
        rhs = [jnp.concatenate([stack(v_ref, rows[j], h0) * bcol_s[g], kst[g] * (bcol_s[g] * gam_s[g])], axis=1)
               for g, (j, h0) in enumerate(groups)]
        x0 = [_dot(tbd[g], _bf(rhs[g])) for g in ng]
        yield
        resid = [rhs[g] - x0[g] - _dot(block_diag(a[g]), _bf(x0[g])) for g in ng]
        yield
        sol = [x0[g] + _dot(tbd[g], _bf(resid[g])) for g in ng]
        yield
        kdt = [_dot_nt(eye_h, _bf(kst[g] * jnp.exp(stack_cols(jnp.broadcast_to(glast[j], (CHUNK, LANES)), h0)
                                                   - gcol_s[g])))
               for g, (j, h0) in enumerate(groups)]
        qk = [pack_diag(pairs[g][:MXU_COLS]) * dec[g] for g in ng]
        for g, (j, h0) in enumerate(groups):
            c = chunks[j]
            qg = qst[g] * gam_s[g]
            for u in range(group):
                r = slice(CHUNK * u, CHUNK * (u + 1))
                u0_s[c, h0 + u] = sol[g][r, :HEAD]
                wkqg_s[c, h0 + u] = _bf(jnp.concatenate([sol[g][r, HEAD:], qg[r]], axis=0))
                qk_s[c, h0 + u] = _bf(qk[g][:, r])
                kdt_s[c, h0 + u] = _bf(kdt[g][:, r])

    def scan(i, u0_s, wkqg_s, qk_s, kdt_s, gl_s):
        for j in range(PREP_CHUNKS):
            c = i * PREP_CHUNKS + j
            rows = pl.ds(pl.multiple_of(c * CHUNK, CHUNK), CHUNK)
            glr = gl_s[c]
            ws = [_dot(wkqg_s[c, h], _bf(s_scr[h])) for h in heads]
            yield
            ub = [_bf(u0_s[c, h] - ws[h][:CHUNK]) for h in heads]
            for h in heads:
                o_ref[rows, HEAD * h:HEAD * (h + 1)] = ws[h][CHUNK:] + _dot(qk_s[c, h], ub[h])
            yield
            for h in heads:
                s_scr[h] = jnp.exp(glr[0:1, h:h + 1]) * s_scr[h] + _dot(kdt_s[c, h], ub[h])
            yield

    def run(mine, other):
        def body(i, carry):
            _weave(project(i, *proj_sets[mine]), prep(i, *proj_sets[other], *prep_sets[other]),
                   scan(i, *prep_sets[mine]))
            return carry
        lax.fori_loop(0, n_chunks // PREP_CHUNKS, body, 0)

    @pl.when(step % 2 == 0)
    def _():
        run(0, 1)

    @pl.when(step % 2 == 1)
    def _():
        run(1, 0)

    sout_ref[...] = s_scr[...]


def _mixer_a_prompt(x, w, n_heads):
    t, d = x.shape
    aw = w["w_z"].shape[1]
    n_chunks = 2 * PREP_CHUNKS
    tt = n_chunks * CHUNK
    assert t % tt == 0 and n_chunks % PREP_CHUNKS == 0
    n_tiles = t // tt
    projected = [pltpu.VMEM((tt, aw), F32)] * 3 + [pltpu.VMEM((tt, LANES), F32)]
    prepared = [pltpu.VMEM((n_chunks, n_heads, CHUNK, HEAD), F32),
                pltpu.VMEM((n_chunks, n_heads, 2 * CHUNK, HEAD), BF16),
                pltpu.VMEM((n_chunks, n_heads, CHUNK, CHUNK), BF16),
                pltpu.VMEM((n_chunks, n_heads, HEAD, CHUNK), BF16),
                pltpu.VMEM((n_chunks, SUBLANES, LANES), F32)]
    scratch = [pltpu.VMEM((SUBLANES, 3 * aw), F32), pltpu.VMEM((n_heads, HEAD, HEAD), F32)]
    scratch += projected + projected + prepared + prepared
    first = lambda s: (jnp.minimum(s, n_tiles - 1), 0)
    last = lambda s: (jnp.maximum(s - 2, 0), 0)
    ins = [(x, pl.BlockSpec((tt, d), first))] + [_whole(w[name]) for name in
                                                  ("g_mix0", "w_qkv", "w_ab", "w_z", "w_aconv", "a_log", "dt_bias")]
    outs = [(jax.ShapeDtypeStruct((t, aw), F32), pl.BlockSpec((tt, aw), first)), _out_whole((SUBLANES, 3 * aw)),
            (jax.ShapeDtypeStruct((t, aw), F32), pl.BlockSpec((tt, aw), last)), _out_whole((n_heads, HEAD, HEAD))]
    z, tail, o, s1 = _call(functools.partial(_mixer_a_prompt_body, n_heads=n_heads, n_chunks=n_chunks),
                           "mixer_a_prompt", n_tiles + 2, ins, outs, scratch)
    return o, z, s1, tail


def _delta_step_body(qt_ref, kt_ref, v_ref, gbt_ref, s_ref, o_ref, so_ref, *, n_seq, n_heads):
    h = pl.program_id(0)
    decay = jnp.exp(gbt_ref[pl.ds(h, 1), :])
    beta = gbt_ref[pl.ds(n_heads + h, 1), :]
    for b in range(n_seq):
        s = decay[:, b:b + 1] * s_ref[b]
        kc = kt_ref[:, b:b + 1]
        u = beta[:, b:b + 1] * (v_ref[b:b + 1, :] - jnp.sum(kc * s, axis=0, keepdims=True))
        s = s + kc * u
        so_ref[b] = s
        o_ref[b:b + 1, :] = jnp.sum(qt_ref[:, b:b + 1] * s, axis=0, keepdims=True)


def _delta_step(q, k, v, gb, state, n_heads):
    n_seq = q.shape[0]
    col = pl.BlockSpec((HEAD, n_seq), lambda h: (h, 0))
    row = pl.BlockSpec((n_seq, HEAD), lambda h: (0, h))
    st = pl.BlockSpec((n_seq, None, HEAD, HEAD), lambda h: (0, h, 0, 0))
    gbt = gb.T
    return pl.pallas_call(
        functools.partial(_delta_step_body, n_seq=n_seq, n_heads=n_heads),
        out_shape=[jax.ShapeDtypeStruct(q.shape, F32), jax.ShapeDtypeStruct(state.shape, F32)],
        grid=(n_heads,), in_specs=[col, col, row, _const_spec(gbt), st], out_specs=[row, st],
        compiler_params=_params(), name="delta_step")(q.T, k.T, v, gbt, state)


def _delta_out_mix(o_ref, z_ref, res_ref, gout_ref, w_ref):
    parts = []
    for h in range(o_ref.shape[1] // HEAD):
        hs = slice(HEAD * h, HEAD * (h + 1))
        parts.append(_bf(_rms(o_ref[:, hs]) * gout_ref[...] * _silu(z_ref[:, hs])))
    return res_ref[...] + _dot(jnp.concatenate(parts, axis=1), w_ref[...])


def _attn_out_mix(refs, scratch, tm, dils):
    ng = len(dils)
    o_refs, l_refs = refs[:ng], refs[ng:2 * ng]
    res_ref, w_ref = refs[2 * ng:]
    get_o, lses = [], []
    for gi, dil in enumerate(dils):
        if dil == 1:
            get_o.append(lambda hh, r=o_refs[gi]: r[0, :, HEAD * hh:HEAD * (hh + 1)])
            lses.append(l_refs[gi][0])
        else:
            o_s, l_s = scratch.pop(0), scratch.pop(0)
            for r in range(dil):
                dst = pl.ds(r, tm // dil, stride=dil)
                for hh in range(Q_PER_GROUP):
                    o_s.at[hh][dst, :] = o_refs[gi][r, :, HEAD * hh:HEAD * (hh + 1)]
                l_s[dst, :] = l_refs[gi][r]
            get_o.append(lambda hh, r=o_s: r[hh])
            lses.append(l_s[...])
    parts = []
    for hh in range(Q_PER_GROUP):
        lh = [l[:, hh:hh + 1] for l in lses]
        mx = functools.reduce(jnp.maximum, lh)
        ex = [jnp.exp(x - mx) for x in lh]
        den = functools.reduce(lambda a, b: a + b, ex)
        acc = (ex[0] / den) * get_o[0](hh)
        for gi in range(1, ng):
            acc = acc + (ex[gi] / den) * get_o[gi](hh)
        parts.append(_bf(acc))
    return res_ref[...] + _dot(jnp.concatenate(parts, axis=1), w_ref[...])


def _ffn_stages(h, p_ref, buf_ref, gffn_ref, wup_ref, wconv_ref, wdown_ref, gple_ref, wgate_ref, wproj_ref,
                tail_ref, ubuf, act_s, result, *, tm):
    hn = _bf(_rms(h) * gffn_ref[...])
    dff = wdown_ref.shape[0]
    for c in range(dff // MXU_COLS):
        ys = []
        for half in range(2):
            lo = half * dff + MXU_COLS * c
            cols = slice(lo, lo + MXU_COLS)
            u = _dot(hn, wup_ref[:, cols])
            w = wconv_ref[:, cols]
            if buf_ref is not None:
                y = buf_ref[:, cols] * w[0:1]
                y = y + buf_ref[:, 2 * dff + lo:2 * dff + lo + MXU_COLS] * w[1:2]
                y = y + u * w[2:3]
                tail_ref[:, cols] = u
            else:
                y = _causal_conv_tile(u, w, ubuf, cols)
                tail_ref[:, cols] = u[tm - SUBLANES:, :]
            ys.append(y)
        act_s[:, MXU_COLS * c:MXU_COLS * (c + 1)] = _bf(_silu(ys[0]) * ys[1])
        yield
    h2 = h + _dot(act_s[...], wdown_ref[...])
    yield
    gate = _sigmoid(_dot(_bf(_rms(h2) * gple_ref[...]), wgate_ref[...]))
    result.append(h2 + gate * _dot(_bf(p_ref[...]), wproj_ref[...]))
    yield


def _mix_ffn_body(*refs, tm, per_row_state, dils):
    n_mix = 5 if dils is None else 2 * len(dils) + 2
    mix_in, refs = refs[:n_mix], refs[n_mix:]
    p_ref, refs = refs[0], refs[1:]
    buf_ref = None
    if per_row_state:
        buf_ref, refs = refs[0], refs[1:]
    ffn_w, (out_ref, tail_ref), scratch = refs[:7], refs[7:9], list(refs[9:])
    act_s = scratch.pop()
    ubuf = None
    if not per_row_state:
        ubuf = scratch.pop()

        @pl.when(pl.program_id(0) == 0)
        def _():
            ubuf[...] = jnp.zeros(ubuf.shape, F32)

    h = _delta_out_mix(*mix_in) if dils is None else _attn_out_mix(mix_in, scratch, tm, dils)
    result = []
    for _ in _ffn_stages(h, p_ref, buf_ref, *ffn_w, tail_ref, ubuf, act_s, result, tm=tm):
        pass
    out_ref[...] = result[0]


def _mix_ffn(mix_ins, mix_scratch, dils, p, w, layer, m, tm, state):
    per_row = state is not None
    d, dff = w["w_down"].shape[2], w["w_down"].shape[1]
    ins = list(mix_ins) + [_layer_rows(p, layer, tm)] + ([_rows(state.reshape(m, -1), tm)] if per_row else [])
    ins += [_layer(w[name], layer) for name in ("g_ffn", "w_up", "w_fconv", "w_down", "g_ple", "w_gate", "w_proj")]
    outs = [_out_rows(m, d, tm), _out_whole((m if per_row else SUBLANES, 2 * dff))]
    scratch = list(mix_scratch) + ([] if per_row else [pltpu.VMEM((SUBLANES, 2 * dff), F32)])
    scratch.append(pltpu.VMEM((tm, dff), BF16))
    return _call(functools.partial(_mix_ffn_body, tm=tm, per_row_state=per_row, dils=dils), f"mix_ffn{layer}",
                 m // tm, ins, outs, scratch)


def _kvq_stages(h, tab_ref, gkv_ref, gmix_ref, wkv_ref, wq_ref, gk_ref, gq_ref, q_ref, k_ref, v_ref, qd, kd, vd,
                stage, *, tm, dils):
    base = _rms(h)
    hkv = _bf(base * gkv_ref[...])
    hq = _bf(base * gmix_ref[...])
    tab = tab_ref[...]
    lane = lax.broadcasted_iota(jnp.int32, tab.shape, 1)
    half = ROPE_DIM // 2
    cos = jnp.where(lane < half, tab, jnp.where(lane < ROPE_DIM, pltpu.roll(tab, half, 1), 1.0))
    sa = jnp.where((lane >= half) & (lane < ROPE_DIM), tab, 0.0)
    sb = jnp.where(lane < half, -pltpu.roll(tab, HEAD - half, 1), 0.0)
    kw = KV_PER_GROUP * HEAD

    def norm_rope(x, g):
        y = _rms(x) * g
        return y * cos + pltpu.roll(y, ROPE_DIM // 2, 1) * sa + pltpu.roll(y, HEAD - ROPE_DIM // 2, 1) * sb

    def heads(x, g):
        return jnp.concatenate([norm_rope(x[:, HEAD * j:HEAD * (j + 1)], g) for j in range(MXU_COLS // HEAD)], axis=1)

    def by_class(x, dst_ref, dil, cols):
        if dil == 1:
            dst_ref[0, :, cols] = _bf(x)
        else:
            for j in range(MXU_COLS // LANES):
                stage[j] = x[:, LANES * j:LANES * (j + 1)]
            for r in range(dil):
                for j in range(MXU_COLS // LANES):
                    lo = cols.start + LANES * j
                    dst_ref[r, :, lo:lo + LANES] = _bf(stage.at[j][pl.ds(r, tm // dil, stride=dil), :])

    assert kw == MXU_COLS and Q_PER_GROUP * HEAD == 2 * MXU_COLS
    n_kv = k_ref.shape[1]

    def project(kind, c):
        lo = MXU_COLS * c + (n_kv if kind == "v" else 0)
        return _dot(hq if kind == "q" else hkv, (wq_ref if kind == "q" else wkv_ref)[:, lo:lo + MXU_COLS])

    def finish(kind, c, x):
        cols = slice(MXU_COLS * c, MXU_COLS * (c + 1))
        if kind == "k":
            x = heads(x, gk_ref[...])
            k_ref[:, cols] = x
        elif kind == "v":
            v_ref[:, cols] = x
        else:
            x = heads(x, gq_ref[...])
            if dils is None:
                q_ref[:, cols] = x
        if dils is not None:
            if kind == "q":
                by_class(x, qd[c // 2], dils[c // 2], slice(MXU_COLS * (c % 2), MXU_COLS * (c % 2 + 1)))
            else:
                by_class(x, (kd if kind == "k" else vd)[c], dils[c], slice(0, kw))

    yield
    for kind, n in (("k", n_kv), ("v", n_kv), ("q", wq_ref.shape[1])):
        for c in range(n // MXU_COLS):
            finish(kind, c, project(kind, c))
            yield


def _kvq_body(*refs, tm, dils):
    h_ref, consts, outs = refs[0], refs[1:8], refs[8:]
    if dils is None:
        q_ref, k_ref, v_ref = outs
        qd = kd = vd = stage = None
    else:
        ng = len(dils)
        q_ref, (k_ref, v_ref) = None, outs[:2]
        qd, kd, vd, stage = outs[2:2 + ng], outs[2 + ng:2 + 2 * ng], outs[2 + 2 * ng:2 + 3 * ng], outs[2 + 3 * ng]
    for _ in _kvq_stages(h_ref[...], *consts, q_ref, k_ref, v_ref, qd, kd, vd, stage, tm=tm, dils=dils):
        pass


def _kvq(h, table, w, tm, dils):
    m, d = h.shape
    n_q, n_kv = w["w_q"].shape[1], w["w_kv"].shape[1] // 2
    qw, kw = Q_PER_GROUP * HEAD, KV_PER_GROUP * HEAD
    consts = [w["g_kv"], w["g_mix1"], w["w_kv"], w["w_q"], w["g_k"], w["g_q"]]
    rows = [h, table]
    row_spec = lambda c: pl.BlockSpec((tm, c), lambda i: (i, 0))
    in_specs = [row_spec(a.shape[1]) for a in rows] + [_const_spec(a) for a in consts]
    nat = lambda c: jax.ShapeDtypeStruct((m, c), F32)
    if dils is None:
        out_shape = [nat(n_q), nat(n_kv), nat(n_kv)]
        out_specs, scratch = [row_spec(n_q), row_spec(n_kv), row_spec(n_kv)], []
    else:
        out_shape, out_specs = [nat(n_kv), nat(n_kv)], [row_spec(n_kv), row_spec(n_kv)]
        for width in (qw, kw, kw):
            for dil in dils:
                assert tm % (dil * 2 * SUBLANES) == 0
                out_shape.append(jax.ShapeDtypeStruct((dil, m // dil, width), BF16))
                out_specs.append(pl.BlockSpec((dil, tm // dil, width), lambda i: (0, i, 0)))
        scratch = [pltpu.VMEM((MXU_COLS // LANES, tm, LANES), F32)]
    return pl.pallas_call(functools.partial(_kvq_body, tm=tm, dils=dils), out_shape=out_shape, grid=(m // tm,),
                          in_specs=in_specs, out_specs=out_specs, scratch_shapes=scratch,
                          compiler_params=_params(), name="kvq")(*rows, *consts)


def _rope_table(pos):
    half = ROPE_DIM // 2
    inv_freq = ROPE_THETA ** (-jnp.arange(half, dtype=F32) * 2.0 / ROPE_DIM)
    ang = pos.astype(F32)[:, None] * inv_freq[None, :]
    return jnp.concatenate([jnp.cos(ang), jnp.sin(ang), jnp.zeros((pos.shape[0], HEAD - ROPE_DIM), F32)], axis=1)


def _attn_prompt_body(q_ref, kc_ref, kp_ref, vc_ref, vp_ref, o_ref, lse_ref, *, n, blocks):
    first_lo = jnp.where(pl.program_id(1) > 0, 0, n)
    qi = lax.broadcasted_iota(jnp.int32, (Q_REP * n, 2 * n), 0) & (n - 1)
    kj = lax.broadcasted_iota(jnp.int32, (Q_REP * n, 2 * n), 1)
    band = (kj >= qi) & (kj <= qi + n)
    lane = lax.broadcasted_iota(jnp.int32, (n, LANES), 1)
    for j in range(blocks):
        rows = slice(n * j, n * (j + 1))
        lse_tile = jnp.zeros((n, LANES), F32)
        for g in range(KV_PER_GROUP):
            hs = slice(HEAD * g, HEAD * (g + 1))
            if j == 0:
                kprev, vprev = kp_ref[:, hs], vp_ref[:, hs]
                valid = band & (kj >= first_lo)
            else:
                prev = slice(n * (j - 1), n * j)
                kprev, vprev = kc_ref[prev, hs], vc_ref[prev, hs]
                valid = band
            kwin = jnp.concatenate([kprev, kc_ref[rows, hs]], axis=0)
            vwin = jnp.concatenate([vprev, vc_ref[rows, hs]], axis=0)
            q0 = Q_REP * HEAD * g
            qg = jnp.concatenate([q_ref[rows, q0 + HEAD * e:q0 + HEAD * (e + 1)] for e in range(Q_REP)], axis=0)
            s = jnp.where(valid, _dot_nt(qg, kwin) * (HEAD ** -0.5), -jnp.inf)
            mx = jnp.max(s, axis=-1, keepdims=True)
            p = jnp.exp(s - mx)
            den = jnp.sum(p, axis=-1, keepdims=True)
            o = _dot(_bf(p), vwin) / den
            lse = mx + jnp.log(den)
            for e in range(Q_REP):
                o_ref[rows, q0 + HEAD * e:q0 + HEAD * (e + 1)] = o[n * e:n * (e + 1)]
                lse_tile = jnp.where(lane == Q_REP * g + e, lse[n * e:n * (e + 1)], lse_tile)
        lse_ref[rows, :] = lse_tile


def _attn_prompt(qd, kd, vd, gi, window):
    dil, ln, qw = qd.shape
    kw = kd.shape[2]
    n = window // dil
    blocks = min(8, ln // n)
    rows = n * blocks
    assert n == HEAD and ln % rows == 0
    cur = lambda r, b: (r, b, 0)
    prev = lambda r, b: (r, jnp.maximum(b * blocks - 1, 0), 0)
    return pl.pallas_call(
        functools.partial(_attn_prompt_body, n=n, blocks=blocks),
        out_shape=[jax.ShapeDtypeStruct((dil, ln, qw), F32), jax.ShapeDtypeStruct((dil, ln, LANES), F32)],
        grid=(dil, ln // rows),
        in_specs=[pl.BlockSpec((None, rows, qw), cur), pl.BlockSpec((None, rows, kw), cur),
                  pl.BlockSpec((None, n, kw), prev), pl.BlockSpec((None, rows, kw), cur),
                  pl.BlockSpec((None, n, kw), prev)],
        out_specs=[pl.BlockSpec((None, rows, qw), cur), pl.BlockSpec((None, rows, LANES), cur)],
        compiler_params=_params(2), name=f"attn_prompt{gi}")(qd, kd, kd, vd, vd)


def _attn_sample_body(q_ref, kn_ref, vn_ref, *refs, bb):
    n_groups = len(GROUPS)
    caches, outs, lses = refs[:n_groups], refs[n_groups:2 * n_groups], refs[2 * n_groups:]
    lane = lax.broadcasted_iota(jnp.int32, (1, LANES), 1)
    kw = KV_PER_GROUP * HEAD
    for gi in range(n_groups):
        c_ref, o_ref, l_ref = caches[gi], outs[gi], lses[gi]
        for b in range(bb):
            lrow = jnp.zeros((1, LANES), F32)
            for g in range(KV_PER_GROUP):
                kc = c_ref[b, :, 0, g, :]
                vc = c_ref[b, :, 1, g, :]
                kcol = kw * gi + HEAD * g
                kn = kn_ref[b:b + 1, kcol:kcol + HEAD]
                vn = vn_ref[b:b + 1, kcol:kcol + HEAD]
                for e in range(Q_REP):
                    oc = HEAD * (Q_REP * g + e)
                    qv = q_ref[b:b + 1, Q_PER_GROUP * HEAD * gi + oc:Q_PER_GROUP * HEAD * gi + oc + HEAD]
                    s = jnp.sum(kc * qv, axis=1, keepdims=True) * (HEAD ** -0.5)
                    sn = jnp.sum(kn * qv, axis=1, keepdims=True) * (HEAD ** -0.5)
                    mx = jnp.maximum(jnp.max(s, axis=0, keepdims=True), sn)
                    p = jnp.exp(s - mx)
                    pn = jnp.exp(sn - mx)
                    den = jnp.sum(p, axis=0, keepdims=True) + pn
                    o_ref[b:b + 1, oc:oc + HEAD] = (jnp.sum(p * vc, axis=0, keepdims=True) + pn * vn) / den
                    lrow = jnp.where(lane == Q_REP * g + e, mx + jnp.log(den), lrow)
            l_ref[b:b + 1, :] = lrow


def _attn_sample(q, k, v, caches):
    n_seq = q.shape[0]
    bb = SUBLANES
    assert n_seq % bb == 0
    qw, kw = Q_PER_GROUP * HEAD, KV_PER_GROUP * HEAD
    views, cache_specs = [], []
    for cache, (window, dil) in zip(caches, GROUPS):
        n = window // dil
        assert cache.shape[1] == window and n == HEAD
        views.append(cache.reshape(n_seq, n, dil, *cache.shape[2:]))
        cache_specs.append(pl.BlockSpec((bb, n, None, *cache.shape[2:]), lambda i: (i, 0, 0, 0, 0, 0)))
    row = lambda c: pl.BlockSpec((bb, c), lambda i: (i, 0))
    n_groups = len(GROUPS)
    res = pl.pallas_call(
        functools.partial(_attn_sample_body, bb=bb),
        out_shape=[jax.ShapeDtypeStruct((n_seq, qw), F32)] * n_groups
        + [jax.ShapeDtypeStruct((n_seq, LANES), F32)] * n_groups,
        grid=(n_seq // bb,),
        in_specs=[row(q.shape[1]), row(k.shape[1]), row(v.shape[1])] + cache_specs,
        out_specs=[row(qw)] * n_groups + [row(LANES)] * n_groups,
        compiler_params=_params(), name="attn_sample")(q, k, v, *views)
    return res[:n_groups], res[n_groups:]


def _ffn0_kvq_body(*refs, tm, dils, n_tiles):
    mix_in, p_ref, ffn_w, tab_ref, kvq_w = refs[:5], refs[5], refs[6:13], refs[13], refs[14:20]
    out_ref, tail_ref, k_ref, v_ref = refs[20:24]
    ng = len(dils)
    qd, kd, vd = refs[24:24 + ng], refs[24 + ng:24 + 2 * ng], refs[24 + 2 * ng:24 + 3 * ng]
    ubuf, act_s, tail_s, hbuf, stage = refs[24 + 3 * ng:]
    step = pl.program_id(0)

    @pl.when(step == 0)
    def _():
        ubuf[...] = jnp.zeros(ubuf.shape, F32)
        hbuf[...] = jnp.zeros(hbuf.shape, F32)

    h_prev = hbuf[...]
    result = []
    _weave(_ffn_stages(_delta_out_mix(*mix_in), p_ref, None, *ffn_w, tail_s, ubuf, act_s, result, tm=tm),
           _kvq_stages(h_prev, tab_ref, *kvq_w, None, k_ref, v_ref, qd, kd, vd, stage, tm=tm, dils=dils))
    hbuf[...] = result[0]

    @pl.when(step < n_tiles)
    def _():
        out_ref[...] = result[0]
        tail_ref[...] = tail_s[...]


def _ffn0_kvq(o, z, x, p, table, w, tm, dils):
    m, d = x.shape
    dff = w["w_down"].shape[1]
    n_kv = w["w_kv"].shape[1] // 2
    qw, kw = Q_PER_GROUP * HEAD, KV_PER_GROUP * HEAD
    assert m % tm == 0
    n_tiles = m // tm
    cur = lambda s: (jnp.minimum(s, n_tiles - 1), 0)
    prev = lambda s: (jnp.maximum(s - 1, 0), 0)
    at = lambda a, imap: (a, pl.BlockSpec((tm, a.shape[1]), imap))
    ins = [at(o, cur), at(z, cur), at(x, cur), _whole(w["g_aout"]), _whole(w["w_aout"]),
           (p, pl.BlockSpec((None, tm, p.shape[2]), lambda s: (0, jnp.minimum(s, n_tiles - 1), 0)))]
    ins += [_layer(w[name], 0) for name in ("g_ffn", "w_up", "w_fconv", "w_down", "g_ple", "w_gate", "w_proj")]
    ins += [at(table, prev)] + [_whole(w[name]) for name in ("g_kv", "g_mix1", "w_kv", "w_q", "g_k", "g_q")]
    nat = lambda c, imap: (jax.ShapeDtypeStruct((m, c), F32), pl.BlockSpec((tm, c), imap))
    outs = [nat(d, cur), _out_whole((SUBLANES, 2 * dff)), nat(n_kv, prev), nat(n_kv, prev)]
    for width in (qw, kw, kw):
        for dil in dils:
            assert tm % (dil * 2 * SUBLANES) == 0
            outs.append((jax.ShapeDtypeStruct((dil, m // dil, width), BF16),
                         pl.BlockSpec((dil, tm // dil, width), lambda s: (0, jnp.maximum(s - 1, 0), 0))))
    scratch = [pltpu.VMEM((SUBLANES, 2 * dff), F32), pltpu.VMEM((tm, dff), BF16), pltpu.VMEM((SUBLANES, 2 * dff), F32),
               pltpu.VMEM((tm, d), F32), pltpu.VMEM((MXU_COLS // LANES, tm, LANES), F32)]
    return _call(functools.partial(_ffn0_kvq_body, tm=tm, dils=dils, n_tiles=n_tiles), "ffn0_kvq", n_tiles + 1,
                 ins, outs, scratch)


def _trunk(x, p, pos, state, w, tm, tm_wide):
    m, d = x.shape
    per_row = state is not None
    n_heads = w["n_heads"]
    aw = w["w_z"].shape[1]

    if per_row:
        consts = [w["g_mix0"], w["w_qkv"], w["w_ab"], w["w_z"], w["w_aconv"], w["a_log"], w["dt_bias"],
                  jnp.swapaxes(state["qkv_conv"], 0, 1)]
        q, k, v, z, gb, qkv_tail = _row_call(functools.partial(_a_in_step_body, n_heads=n_heads), "a_in_step", m, tm,
                                             [x], consts, [(aw, F32)] * 4 + [(LANES, F32)],
                                             const_outs=[((m, 3 * aw), F32)])
        o, s1 = _delta_step(q, k, v, gb, state["delta"], n_heads)
    else:
        o, z, s1, qkv_tail = _mixer_a_prompt(x, w, n_heads)
    table = _rope_table(pos)
    if per_row:
        mix = [_rows(o, tm_wide), _rows(z, tm_wide), _rows(x, tm_wide), _whole(w["g_aout"]), _whole(w["w_aout"])]
        h, ffn_tail0 = _mix_ffn(mix, [], None, p, w, 0, m, tm_wide, state["ffn_conv"][0])
        qa, ka, va = _kvq(h, table, w, tm, None)
        outs, lses = _attn_sample(qa, ka, va, state["caches"])
        outs, lses = [a[None] for a in outs], [a[None] for a in lses]
        dils = (1,) * len(GROUPS)
    else:
        dils = tuple(dil for _, dil in GROUPS)
        ng = len(dils)
        res = _ffn0_kvq(o, z, x, p, table, w, tm, dils)
        h, ffn_tail0, ka, va = res[:4]
        qd, kd, vd = res[4:4 + ng], res[4 + ng:4 + 2 * ng], res[4 + 2 * ng:]
        att = [_attn_prompt(qd[gi], kd[gi], vd[gi], gi, window) for gi, (window, _) in enumerate(GROUPS)]
        outs, lses = [a[0] for a in att], [a[1] for a in att]
    mix = [_grouped(a, tm_wide) for a in (*outs, *lses)] + [_rows(h, tm_wide), _whole(w["w_o"])]
    mix_scratch = []
    for dil in dils:
        if dil > 1:
            assert tm_wide % (dil * SUBLANES) == 0
            mix_scratch += [pltpu.VMEM((Q_PER_GROUP, tm_wide, HEAD), F32), pltpu.VMEM((tm_wide, LANES), F32)]
    h, ffn_tail1 = _mix_ffn(mix, mix_scratch, dils, p, w, 1, m, tm_wide, state["ffn_conv"][1] if per_row else None)
    return h, s1, qkv_tail, (ffn_tail0, ffn_tail1), ka, va


def _kv_rows(k, v, gi, n_rows):
    kw = KV_PER_GROUP * HEAD
    sl = slice(kw * gi, kw * (gi + 1))
    shape = (n_rows, 1, KV_PER_GROUP, HEAD)
    return jnp.concatenate([k[k.shape[0] - n_rows:, sl].reshape(shape), v[v.shape[0] - n_rows:, sl].reshape(shape)],
                           axis=1)


def kernel(x_prompt, x_sample, p_prompt, p_sample, state_delta, state_qkv_conv, state_ffn_conv, cache_kv_w128, cache_kv_w512, cache_kv_w2048, g_mix_norm, g_ffn_norm, w_ffn_up, w_ffn_conv, w_ffn_down, g_ple_norm, w_ple_gate, w_ple_proj, w_a_in, w_a_conv, a_log, a_dt_bias, g_a_out_norm, w_a_out, g_kv_norm, w_kv, g_k_norm, w_q, g_q_norm, w_o):
    n_heads = a_log.shape[1]
    aw = n_heads * HEAD
    assert w_a_in.shape[0] == 1 and w_q.shape[0] == 1 and w_a_in.shape[2] == 4 * aw + 2 * n_heads
    row = lambda a: a.reshape(1, -1)
    lane_pad = lambda a: jnp.pad(a, ((0, 0), (0, LANES - a.shape[1])))
    w = {
        "n_heads": n_heads,
        "g_mix0": row(g_mix_norm[0]), "g_mix1": row(g_mix_norm[1]), "g_ffn": g_ffn_norm[:, None], "g_ple": g_ple_norm[:, None],
        "w_up": _bf(w_ffn_up), "w_fconv": w_ffn_conv, "w_down": _bf(w_ffn_down),
        "w_gate": _bf(w_ple_gate), "w_proj": _bf(w_ple_proj),
        "w_qkv": _bf(w_a_in[0][:, :3 * aw]), "w_ab": _bf(lane_pad(w_a_in[0][:, 3 * aw:3 * aw + 2 * n_heads])),
        "w_z": _bf(w_a_in[0][:, 3 * aw + 2 * n_heads:]), "w_aconv": w_a_conv[0],
        "a_log": lane_pad(a_log), "dt_bias": lane_pad(a_dt_bias),
        "g_aout": row(g_a_out_norm[0]), "w_aout": _bf(w_a_out[0]),
        "g_kv": row(g_kv_norm), "w_kv": _bf(w_kv), "g_k": row(g_k_norm),
        "w_q": _bf(w_q[0]), "g_q": row(g_q_norm[0]), "w_o": _bf(w_o[0]),
    }
    assert x_prompt.shape[0] == 1 and x_sample.shape[1] == 1
    seq, d = x_prompt.shape[1], x_prompt.shape[2]
    n_seq = x_sample.shape[0]

    yp, sd_p, qkv_tail_p, ffn_tails_p, kp, vp = _trunk(
        x_prompt.reshape(seq, d), p_prompt.reshape(p_prompt.shape[0], seq, -1), jnp.arange(seq, dtype=jnp.int32), None, w,
        tm=256, tm_wide=512)
    a_keep, f_keep = state_qkv_conv.shape[2], state_ffn_conv.shape[2]
    sq_p = qkv_tail_p[SUBLANES - a_keep:][None, None]
    sf_p = jnp.stack([t[SUBLANES - f_keep:] for t in ffn_tails_p])[:, None]
    kv_p = [_kv_rows(kp, vp, gi, min(window, seq))[None] for gi, (window, _) in enumerate(GROUPS)]

    state = {"delta": state_delta[0], "qkv_conv": state_qkv_conv[0], "ffn_conv": state_ffn_conv,
             "caches": (cache_kv_w128, cache_kv_w512, cache_kv_w2048)}
    ys, sd_s, qkv_tail_s, ffn_tails_s, ks, vs = _trunk(
        x_sample.reshape(n_seq, d), p_sample.reshape(p_sample.shape[0], n_seq, -1), jnp.full((n_seq,), PAST_LEN, jnp.int32),
        state, w, tm=n_seq, tm_wide=n_seq)
    sq_s = jnp.concatenate([state_qkv_conv[0][:, 1:], qkv_tail_s[:, None]], axis=1)[None]
    sf_s = jnp.stack([jnp.concatenate([state_ffn_conv[i][:, 1:], ffn_tails_s[i][:, None]], axis=1)
                      for i in range(2)])
    kv_s = [_kv_rows(ks, vs, gi, n_seq).reshape(n_seq, 1, 2, KV_PER_GROUP, HEAD) for gi in range(len(GROUPS))]

    return (yp[None], ys[:, None], sd_p[None, None], sq_p, sf_p, kv_p[0], kv_p[1], kv_p[2],
            sd_s[None], sq_s, sf_s, kv_s[0], kv_s[1], kv_s[2])
```

```python
import functools

import jax
import jax.numpy as jnp
from jax import lax
from jax.experimental import pallas as pl
from jax.experimental.pallas import tpu as pltpu

F32 = jnp.float32
BF16 = jnp.bfloat16

EPS = 1e-6
HEAD = 128
CHUNK = 64
GROUPS = ((128, 1), (512, 4), (2048, 16))
KV_PER_GROUP = 2
Q_REP = 2
Q_PER_GROUP = KV_PER_GROUP * Q_REP
ROPE_DIM = HEAD // 4
ROPE_THETA = 500000.0
PAST_LEN = 16384

LANES = 128
SUBLANES = 8
MXU_COLS = 256
VMEM_LIMIT_BYTES = 56 * 1024 * 1024

PREP_CHUNKS = 4


def _dot(a, b):
    return jnp.dot(a, b, preferred_element_type=F32)


def _dot_nt(a, b):
    return lax.dot_general(a, b, (((1,), (1,)), ((), ())), preferred_element_type=F32)


def _bf(x):
    return x.astype(BF16)


def _rms(x):
    return x * lax.rsqrt(jnp.mean(x * x, axis=-1, keepdims=True) + EPS)


def _sigmoid(x):
    return 1.0 / (1.0 + jnp.exp(-x))


def _silu(x):
    return x * _sigmoid(x)


def _split3f(x):
    hi = _bf(x).astype(F32)
    r = x - hi
    mid = _bf(r).astype(F32)
    return hi, mid, _bf(r - mid).astype(F32)


def _causal_conv_tile(x, w, carry_ref, cols):
    rows = x.shape[0]
    first = lax.broadcasted_iota(jnp.int32, (SUBLANES, x.shape[1]), 0) == 0
    z = x * w[0:1]
    for j in range(1, w.shape[0]):
        rolled = pltpu.roll(z, 1, 0)
        head = jnp.where(first, carry_ref[j - 1:j, cols], rolled[0:SUBLANES])
        carry_ref[j - 1:j, cols] = z[rows - 1:, :]
        z = jnp.concatenate([head, rolled[SUBLANES:]], axis=0) + x * w[j:j + 1]
    return z


def _const_spec(a):
    nd = a.ndim
    return pl.BlockSpec(a.shape, lambda *_: (0,) * nd, pipeline_mode=pl.Buffered(1))


def _params(n_axes=1):
    return pltpu.CompilerParams(dimension_semantics=("arbitrary",) * n_axes,
                                vmem_limit_bytes=VMEM_LIMIT_BYTES)


def _row_call(body, name, m, tm, row_ins, const_ins, row_outs, const_outs=(), scratch=()):
    assert m % tm == 0
    in_specs = [pl.BlockSpec((tm, a.shape[1]), lambda i: (i, 0)) for a in row_ins]
    in_specs += [_const_spec(a) for a in const_ins]
    out_shape = [jax.ShapeDtypeStruct((m, c), dt) for c, dt in row_outs]
    out_specs = [pl.BlockSpec((tm, c), lambda i: (i, 0)) for c, _ in row_outs]
    for shape, dt in const_outs:
        out_shape.append(jax.ShapeDtypeStruct(shape, dt))
        out_specs.append(pl.BlockSpec(shape, lambda i, _n=len(shape): (0,) * _n))
    return pl.pallas_call(body, out_shape=out_shape, grid=(m // tm,), in_specs=in_specs, out_specs=out_specs,
                          scratch_shapes=list(scratch), compiler_params=_params(), name=name)(*row_ins, *const_ins)


def _call(body, name, steps, ins, outs, scratch=()):
    return pl.pallas_call(body, out_shape=[o for o, _ in outs], grid=(steps,), in_specs=[sp for _, sp in ins],
                          out_specs=[sp for _, sp in outs], scratch_shapes=list(scratch),
                          compiler_params=_params(), name=name)(*[a for a, _ in ins])


def _rows(a, tm):
    return a, pl.BlockSpec((tm, a.shape[1]), lambda i: (i, 0))


def _whole(a):
    return a, _const_spec(a)


def _layer(a, layer):
    nd = a.ndim
    return a, pl.BlockSpec((None,) + a.shape[1:], lambda i: (layer,) + (0,) * (nd - 1), pipeline_mode=pl.Buffered(1))


def _layer_rows(a, layer, tm):
    return a, pl.BlockSpec((None, tm, a.shape[2]), lambda i: (layer, i, 0))


def _grouped(a, tm):
    return a, pl.BlockSpec((a.shape[0], tm // a.shape[0], a.shape[2]), lambda i: (0, i, 0))


def _out_rows(m, c, tm):
    return jax.ShapeDtypeStruct((m, c), F32), pl.BlockSpec((tm, c), lambda i: (i, 0))


def _out_whole(shape):
    return jax.ShapeDtypeStruct(shape, F32), pl.BlockSpec(shape, lambda i: (0,) * len(shape))


def _qkv_activation(y, part):
    a = _silu(y)
    if part == 2:
        return a
    halves = []
    for hh in range(MXU_COLS // HEAD):
        ah = a[:, HEAD * hh:HEAD * (hh + 1)]
        nrm = ah * lax.rsqrt(jnp.sum(ah * ah, axis=-1, keepdims=True) + EPS)
        halves.append(nrm * (HEAD ** -0.5) if part == 0 else nrm)
    return jnp.concatenate(halves, axis=1)


def _decay_and_beta(pab, alog_ref, dtb_ref, n_heads):
    lane = lax.broadcasted_iota(jnp.int32, pab.shape, 1)
    ap = pab + dtb_ref[...]
    softplus = jnp.maximum(ap, 0.0) + jnp.log1p(jnp.exp(-jnp.abs(ap)))
    return jnp.where(lane < n_heads, -jnp.exp(alog_ref[...]) * softplus, _sigmoid(pab))


def _a_in_step_body(x_ref, g_ref, wqkv_ref, wab_ref, wz_ref, wconv_ref, alog_ref, dtb_ref, buf_ref,
                    q_ref, k_ref, v_ref, z_ref, gb_ref, tail_ref, *, n_heads):
    hn = _bf(_rms(x_ref[...]) * g_ref[...])
    outs = (q_ref, k_ref, v_ref)
    per_part = q_ref.shape[1] // MXU_COLS
    for c in range(3 * per_part):
        cols = slice(MXU_COLS * c, MXU_COLS * (c + 1))
        pq = _dot(hn, wqkv_ref[:, cols])
        w = wconv_ref[:, cols]
        y = buf_ref[0, :, cols] * w[0:1]
        y = y + buf_ref[1, :, cols] * w[1:2]
        y = y + buf_ref[2, :, cols] * w[2:3]
        y = y + pq * w[3:4]
        tail_ref[:, cols] = pq
        part, sub = divmod(c, per_part)
        outs[part][:, MXU_COLS * sub:MXU_COLS * (sub + 1)] = _qkv_activation(y, part)
    gb_ref[...] = _decay_and_beta(_dot(hn, wab_ref[...]), alog_ref, dtb_ref, n_heads)
    z_ref[...] = _dot(hn, wz_ref[...])


def _weave(*stages):
    live = list(stages)
    while live:
        for g in list(live):
            try:
                next(g)
            except StopIteration:
                live.remove(g)


def _mixer_a_prompt_body(x_ref, g_ref, wqkv_ref, wab_ref, wz_ref, wconv_ref, alog_ref, dtb_ref,
                         z_ref, tail_ref, o_ref, sout_ref, cbuf, s_scr, *bufs, n_heads, n_chunks):
    step = pl.program_id(0)
    n_proj = 4
    proj_sets = bufs[:n_proj], bufs[n_proj:2 * n_proj]
    rest = bufs[2 * n_proj:]
    prep_sets = rest[:len(rest) // 2], rest[len(rest) // 2:]

    @pl.when(step == 0)
    def _():
        cbuf[...] = jnp.zeros(cbuf.shape, F32)
        s_scr[...] = jnp.zeros(s_scr.shape, F32)
        for ref in (*proj_sets[1], *prep_sets[0]):
            ref[...] = jnp.zeros(ref.shape, ref.dtype)

    sub = n_chunks // 2 * CHUNK

    def project(i, q_s, k_s, v_s, gb_s):
        rows = pl.ds(pl.multiple_of(i * sub, sub), sub)
        hn = _bf(_rms(x_ref[rows, :]) * g_ref[...])
        outs = (q_s, k_s, v_s)
        per_part = q_s.shape[1] // MXU_COLS
        for c in range(3 * per_part):
            cols = slice(MXU_COLS * c, MXU_COLS * (c + 1))
            pq = _dot(hn, wqkv_ref[:, cols])
            y = _causal_conv_tile(pq, wconv_ref[:, cols], cbuf, cols)
            tail_ref[:, cols] = pq[sub - SUBLANES:, :]
            part, sc = divmod(c, per_part)
            outs[part][rows, MXU_COLS * sc:MXU_COLS * (sc + 1)] = _qkv_activation(y, part)
            yield
        gb_s[rows, :] = _decay_and_beta(_dot(hn, wab_ref[...]), alog_ref, dtb_ref, n_heads)
        z_ref[rows, :] = _dot(hn, wz_ref[...])
        yield

    group = MXU_COLS // CHUNK
    shift = CHUNK.bit_length() - 1
    lane_p = lax.broadcasted_iota(jnp.int32, (CHUNK, MXU_COLS), 1)
    row_p = lax.broadcasted_iota(jnp.int32, (CHUNK, MXU_COLS), 0)
    col_p = lane_p & (CHUNK - 1)
    blk = [(lane_p >> shift) == u for u in range(group)]
    incl_p = row_p >= col_p
    strict_p = row_p > col_p
    eye_p = jnp.where(row_p == col_p, 1.0, 0.0).astype(F32)
    bd_r = lax.broadcasted_iota(jnp.int32, (MXU_COLS, MXU_COLS), 0)
    bd_c = lax.broadcasted_iota(jnp.int32, (MXU_COLS, MXU_COLS), 1)
    bd_mask = (bd_r >> shift) == (bd_c >> shift)
    ri = lax.broadcasted_iota(jnp.int32, (CHUNK, CHUNK), 0)
    ci = lax.broadcasted_iota(jnp.int32, (CHUNK, CHUNK), 1)
    ltri3 = jnp.concatenate([_bf(jnp.where(ri >= ci, 1.0, 0.0))] * 3, axis=1)
    g_r = lax.broadcasted_iota(jnp.int32, (MXU_COLS, 3 * LANES), 0) >> shift
    g_c = lax.broadcasted_iota(jnp.int32, (MXU_COLS, 3 * LANES), 1) & (LANES - 1)
    ones3 = jnp.ones((CHUNK, 3 * LANES), BF16)
    er = lax.broadcasted_iota(jnp.int32, (HEAD, HEAD), 0)
    ec = lax.broadcasted_iota(jnp.int32, (HEAD, HEAD), 1)
    eye_h = _bf(jnp.where(er == ec, 1.0, 0.0))
    heads = range(n_heads)
    hs = [slice(HEAD * h, HEAD * (h + 1)) for h in heads]

    def block_diag(x):
        return _bf(jnp.where(bd_mask, jnp.concatenate([x] * group, axis=0), 0.0))

    def pack_cols(cols):
        out = cols[-1]
        for u in reversed(range(group - 1)):
            out = jnp.where(blk[u], cols[u], out)
        return out

    wide = group * HEAD
    kd_r = lax.broadcasted_iota(jnp.int32, (MXU_COLS, wide), 0) >> shift
    kd_c = lax.broadcasted_iota(jnp.int32, (MXU_COLS, wide), 1) >> (HEAD.bit_length() - 1)
    kd_mask = kd_r == kd_c

    def key_blocks(kc):
        return _bf(jnp.where(kd_mask, jnp.concatenate([kc] * group, axis=0), 0.0))

    def stack(ref, rows, h0):
        return jnp.concatenate([ref[rows, hs[h]] for h in range(h0, h0 + group)], axis=0)

    def stack_cols(x, lo):
        return jnp.concatenate([x[:, lo + u:lo + u + 1] for u in range(group)], axis=0)

    def prep(i, q_ref, k_ref, v_ref, gb_ref, u0_s, wkqg_s, qk_s, kdt_s, gl_s):
        chunks = [i * PREP_CHUNKS + j for j in range(PREP_CHUNKS)]
        rows = [pl.ds(pl.multiple_of(c * CHUNK, CHUNK), CHUNK) for c in chunks]
        gbc = [gb_ref[r, :] for r in rows]
        gc = [_dot(ltri3, _bf(jnp.concatenate(_split3f(x), axis=0))) for x in gbc]
        glast = [x[CHUNK - 1:CHUNK, :] for x in gc]
        for j, c in enumerate(chunks):
            gl_s[c] = jnp.broadcast_to(glast[j], (SUBLANES, LANES))
        groups = [(j, h0) for j in range(PREP_CHUNKS) for h0 in range(0, n_heads, group)]
        ng = range(len(groups))
        kst = [stack(k_ref, rows[j], h0) for j, h0 in groups]
        qst = [stack(q_ref, rows[j], h0) for j, h0 in groups]
        kc = [k_ref[rows[j], HEAD * h0:HEAD * (h0 + group)] for j, h0 in groups]
        qc = [q_ref[rows[j], HEAD * h0:HEAD * (h0 + group)] for j, h0 in groups]
        pairs = [_dot_nt(_bf(jnp.concatenate([qc[g], kc[g]], axis=0)), key_blocks(kc[g])) for g in ng]
        yield
        gcol_p = [pack_cols([gc[j][:, h:h + 1] for h in range(h0, h0 + group)]) for j, h0 in groups]
        bcol_p = [pack_cols([gbc[j][:, n_heads + h:n_heads + h + 1] for h in range(h0, h0 + group)])
                  for j, h0 in groups]
        grow = [_dot_nt(ones3, _bf(jnp.where(g_c == g_r + h0,
                                               jnp.concatenate([jnp.concatenate([x] * group, axis=0)
                                                                for x in _split3f(gc[j])], axis=1), 0.0)))
                for j, h0 in groups]
        yield
        dec = [jnp.exp(jnp.where(incl_p, gcol_p[g] - grow[g], -jnp.inf)) for g in ng]
        a = [jnp.where(strict_p, bcol_p[g] * pairs[g][CHUNK:] * dec[g], 0.0) for g in ng]
        tinv = [eye_p - x for x in a]
        pw = [_dot(_bf(-x), block_diag(-x)) for x in a]
        yield
        for _ in range(4):
            res = [_dot(jnp.concatenate([_bf(tinv[g]), _bf(pw[g])], axis=0), block_diag(pw[g])) for g in ng]
            tinv = [tinv[g] + res[g][:CHUNK] for g in ng]
            pw = [r[CHUNK:] for r in res]
            yield
        tinv = [tinv[g] + _dot(_bf(tinv[g]), block_diag(pw[g])) for g in ng]
        yield
        tbd = [block_diag(x) for x in tinv]
        gcol_s = [stack_cols(gc[j], h0) for j, h0 in groups]
        bcol_s = [stack_cols(gbc[j], n_heads + h0) for j, h0 in groups]
        gam_s = [jnp.exp(x) for x in gcol_s]
        rhs = [jnp.concatenate([stack(v_ref, rows[j], h0) * bcol_s[g], kst[g] * (bcol_s[g] * gam_s[g])], axis=1)
               for g, (j, h0) in enumerate(groups)]
        x0 = [_dot(tbd[g], _bf(rhs[g])) for g in ng]
        yield
        resid = [rhs[g] - x0[g] - _dot(block_diag(a[g]), _bf(x0[g])) for g in ng]
        yield
        sol = [x0[g] + _dot(tbd[g], _bf(resid[g])) for g in ng]
        yield
        kdt = [_dot_nt(eye_h, _bf(kst[g] * jnp.exp(stack_cols(jnp.broadcast_to(glast[j], (CHUNK, LANES)), h0)
                                                   - gcol_s[g])))
               for g, (j, h0) in enumerate(groups)]
        qk = [pairs[g][:CHUNK] * dec[g] for g in ng]
        for g, (j, h0) in enumerate(groups):
            c = chunks[j]
            qg = qst[g] * gam_s[g]
            for u in range(group):
                r = slice(CHUNK * u, CHUNK * (u + 1))
                u0_s[c, h0 + u] = sol[g][r, :HEAD]
                wkqg_s[c, h0 + u] = _bf(jnp.concatenate([sol[g][r, HEAD:], qg[r]], axis=0))
                qk_s[c, h0 + u] = _bf(qk[g][:, r])
                kdt_s[c, h0 + u] = _bf(kdt[g][:, r])

    def scan(i, u0_s, wkqg_s, qk_s, kdt_s, gl_s):
        for j in range(PREP_CHUNKS):
            c = i * PREP_CHUNKS + j
            rows = pl.ds(pl.multiple_of(c * CHUNK, CHUNK), CHUNK)
            glr = gl_s[c]
            ws = [_dot(wkqg_s[c, h], _bf(s_scr[h])) for h in heads]
            yield
            ub = [_bf(u0_s[c, h] - ws[h][:CHUNK]) for h in heads]
            for h in heads:
                o_ref[rows, HEAD * h:HEAD * (h + 1)] = ws[h][CHUNK:] + _dot(qk_s[c, h], ub[h])
            yield
            for h in heads:
                s_scr[h] = jnp.exp(glr[0:1, h:h + 1]) * s_scr[h] + _dot(kdt_s[c, h], ub[h])
            yield

    def run(mine, other):
        def body(i, carry):
            _weave(project(i, *proj_sets[mine]), prep(i, *proj_sets[other], *prep_sets[other]),
                   scan(i, *prep_sets[mine]))
            return carry
        lax.fori_loop(0, n_chunks // PREP_CHUNKS, body, 0)

    @pl.when(step % 2 == 0)
    def _():
        run(0, 1)

    @pl.when(step % 2 == 1)
    def _():
        run(1, 0)

    sout_ref[...] = s_scr[...]


def _mixer_a_prompt(x, w, n_heads):
    t, d = x.shape
    aw = w["w_z"].shape[1]
    n_chunks = 2 * PREP_CHUNKS
    tt = n_chunks * CHUNK
    assert t % tt == 0 and n_chunks % PREP_CHUNKS == 0
    n_tiles = t // tt
    projected = [pltpu.VMEM((tt, aw), F32)] * 3 + [pltpu.VMEM((tt, LANES), F32)]
    prepared = [pltpu.VMEM((n_chunks, n_heads, CHUNK, HEAD), F32),
                pltpu.VMEM((n_chunks, n_heads, 2 * CHUNK, HEAD), BF16),
                pltpu.VMEM((n_chunks, n_heads, CHUNK, CHUNK), BF16),
                pltpu.VMEM((n_chunks, n_heads, HEAD, CHUNK), BF16),
                pltpu.VMEM((n_chunks, SUBLANES, LANES), F32)]
    scratch = [pltpu.VMEM((SUBLANES, 3 * aw), F32), pltpu.VMEM((n_heads, HEAD, HEAD), F32)]
    scratch += projected + projected + prepared + prepared
    first = lambda s: (jnp.minimum(s, n_tiles - 1), 0)
    last = lambda s: (jnp.maximum(s - 2, 0), 0)
    ins = [(x, pl.BlockSpec((tt, d), first))] + [_whole(w[name]) for name in
                                                  ("g_mix0", "w_qkv", "w_ab", "w_z", "w_aconv", "a_log", "dt_bias")]
    outs = [(jax.ShapeDtypeStruct((t, aw), F32), pl.BlockSpec((tt, aw), first)), _out_whole((SUBLANES, 3 * aw)),
            (jax.ShapeDtypeStruct((t, aw), F32), pl.BlockSpec((tt, aw), last)), _out_whole((n_heads, HEAD, HEAD))]
    z, tail, o, s1 = _call(functools.partial(_mixer_a_prompt_body, n_heads=n_heads, n_chunks=n_chunks),
                           "mixer_a_prompt", n_tiles + 2, ins, outs, scratch)
    return o, z, s1, tail


def _delta_step_body(qt_ref, kt_ref, v_ref, gbt_ref, s_ref, o_ref, so_ref, *, n_seq, n_heads):
    h = pl.program_id(0)
    decay = jnp.exp(gbt_ref[pl.ds(h, 1), :])
    beta = gbt_ref[pl.ds(n_heads + h, 1), :]
    for b in range(n_seq):
        s = decay[:, b:b + 1] * s_ref[b]
        kc = kt_ref[:, b:b + 1]
        u = beta[:, b:b + 1] * (v_ref[b:b + 1, :] - jnp.sum(kc * s, axis=0, keepdims=True))
        s = s + kc * u
        so_ref[b] = s
        o_ref[b:b + 1, :] = jnp.sum(qt_ref[:, b:b + 1] * s, axis=0, keepdims=True)


def _delta_step(q, k, v, gb, state, n_heads):
    n_seq = q.shape[0]
    col = pl.BlockSpec((HEAD, n_seq), lambda h: (h, 0))
    row = pl.BlockSpec((n_seq, HEAD), lambda h: (0, h))
    st = pl.BlockSpec((n_seq, None, HEAD, HEAD), lambda h: (0, h, 0, 0))
    gbt = gb.T
    return pl.pallas_call(
        functools.partial(_delta_step_body, n_seq=n_seq, n_heads=n_heads),
        out_shape=[jax.ShapeDtypeStruct(q.shape, F32), jax.ShapeDtypeStruct(state.shape, F32)],
        grid=(n_heads,), in_specs=[col, col, row, _const_spec(gbt), st], out_specs=[row, st],
        compiler_params=_params(), name="delta_step")(q.T, k.T, v, gbt, state)


def _delta_out_mix(o_ref, z_ref, res_ref, gout_ref, w_ref):
    parts = []
    for h in range(o_ref.shape[1] // HEAD):
        hs = slice(HEAD * h, HEAD * (h + 1))
        parts.append(_bf(_rms(o_ref[:, hs]) * gout_ref[...] * _silu(z_ref[:, hs])))
    return res_ref[...] + _dot(jnp.concatenate(parts, axis=1), w_ref[...])


def _attn_out_mix(refs, scratch, tm, dils):
    ng = len(dils)
    o_refs, l_refs = refs[:ng], refs[ng:2 * ng]
    res_ref, w_ref = refs[2 * ng:]
    get_o, lses = [], []
    for gi, dil in enumerate(dils):
        if dil == 1:
            get_o.append(lambda hh, r=o_refs[gi]: r[0, :, HEAD * hh:HEAD * (hh + 1)])
            lses.append(l_refs[gi][0])
        else:
            o_s, l_s = scratch.pop(0), scratch.pop(0)
            for r in range(dil):
                dst = pl.ds(r, tm // dil, stride=dil)
                for hh in range(Q_PER_GROUP):
                    o_s.at[hh][dst, :] = o_refs[gi][r, :, HEAD * hh:HEAD * (hh + 1)]
                l_s[dst, :] = l_refs[gi][r]
            get_o.append(lambda hh, r=o_s: r[hh])
            lses.append(l_s[...])
    parts = []
    for hh in range(Q_PER_GROUP):
        lh = [l[:, hh:hh + 1] for l in lses]
        mx = functools.reduce(jnp.maximum, lh)
        ex = [jnp.exp(x - mx) for x in lh]
        den = functools.reduce(lambda a, b: a + b, ex)
        acc = (ex[0] / den) * get_o[0](hh)
        for gi in range(1, ng):
            acc = acc + (ex[gi] / den) * get_o[gi](hh)
        parts.append(_bf(acc))
    return res_ref[...] + _dot(jnp.concatenate(parts, axis=1), w_ref[...])


def _ffn_stages(h, p_ref, buf_ref, gffn_ref, wup_ref, wconv_ref, wdown_ref, gple_ref, wgate_ref, wproj_ref,
                tail_ref, ubuf, act_s, result, *, tm):
    hn = _bf(_rms(h) * gffn_ref[...])
    dff = wdown_ref.shape[0]
    for c in range(dff // MXU_COLS):
        ys = []
        for half in range(2):
            lo = half * dff + MXU_COLS * c
            cols = slice(lo, lo + MXU_COLS)
            u = _dot(hn, wup_ref[:, cols])
            w = wconv_ref[:, cols]
            if buf_ref is not None:
                y = buf_ref[:, cols] * w[0:1]
                y = y + buf_ref[:, 2 * dff + lo:2 * dff + lo + MXU_COLS] * w[1:2]
                y = y + u * w[2:3]
                tail_ref[:, cols] = u
            else:
                y = _causal_conv_tile(u, w, ubuf, cols)
                tail_ref[:, cols] = u[tm - SUBLANES:, :]
            ys.append(y)
        act_s[:, MXU_COLS * c:MXU_COLS * (c + 1)] = _bf(_silu(ys[0]) * ys[1])
        yield
    h2 = h + _dot(act_s[...], wdown_ref[...])
    yield
    gate = _sigmoid(_dot(_bf(_rms(h2) * gple_ref[...]), wgate_ref[...]))
    result.append(h2 + gate * _dot(_bf(p_ref[...]), wproj_ref[...]))
    yield


def _mix_ffn_body(*refs, tm, per_row_state, dils):
    n_mix = 5 if dils is None else 2 * len(dils) + 2
    mix_in, refs = refs[:n_mix], refs[n_mix:]
    p_ref, refs = refs[0], refs[1:]
    buf_ref = None
    if per_row_state:
        buf_ref, refs = refs[0], refs[1:]
    ffn_w, (out_ref, tail_ref), scratch = refs[:7], refs[7:9], list(refs[9:])
    act_s = scratch.pop()
    ubuf = None
    if not per_row_state:
        ubuf = scratch.pop()

        @pl.when(pl.program_id(0) == 0)
        def _():
            ubuf[...] = jnp.zeros(ubuf.shape, F32)

    h = _delta_out_mix(*mix_in) if dils is None else _attn_out_mix(mix_in, scratch, tm, dils)
    result = []
    for _ in _ffn_stages(h, p_ref, buf_ref, *ffn_w, tail_ref, ubuf, act_s, result, tm=tm):
        pass
    out_ref[...] = result[0]


def _mix_ffn(mix_ins, mix_scratch, dils, p, w, layer, m, tm, state):
    per_row = state is not None
    d, dff = w["w_down"].shape[2], w["w_down"].shape[1]
    ins = list(mix_ins) + [_layer_rows(p, layer, tm)] + ([_rows(state.reshape(m, -1), tm)] if per_row else [])
    ins += [_layer(w[name], layer) for name in ("g_ffn", "w_up", "w_fconv", "w_down", "g_ple", "w_gate", "w_proj")]
    outs = [_out_rows(m, d, tm), _out_whole((m if per_row else SUBLANES, 2 * dff))]
    scratch = list(mix_scratch) + ([] if per_row else [pltpu.VMEM((SUBLANES, 2 * dff), F32)])
    scratch.append(pltpu.VMEM((tm, dff), BF16))
    return _call(functools.partial(_mix_ffn_body, tm=tm, per_row_state=per_row, dils=dils), f"mix_ffn{layer}",
                 m // tm, ins, outs, scratch)


def _kvq_stages(h, tab_ref, gkv_ref, gmix_ref, wkv_ref, wq_ref, gk_ref, gq_ref, q_ref, k_ref, v_ref, qd, kd, vd,
                stage, *, tm, dils):
    base = _rms(h)
    hkv = _bf(base * gkv_ref[...])
    hq = _bf(base * gmix_ref[...])
    tab = tab_ref[...]
    lane = lax.broadcasted_iota(jnp.int32, tab.shape, 1)
    half = ROPE_DIM // 2
    cos = jnp.where(lane < half, tab, jnp.where(lane < ROPE_DIM, pltpu.roll(tab, half, 1), 1.0))
    sa = jnp.where((lane >= half) & (lane < ROPE_DIM), tab, 0.0)
    sb = jnp.where(lane < half, -pltpu.roll(tab, HEAD - half, 1), 0.0)
    kw = KV_PER_GROUP * HEAD

    def norm_rope(x, g):
        y = _rms(x) * g
        return y * cos + pltpu.roll(y, ROPE_DIM // 2, 1) * sa + pltpu.roll(y, HEAD - ROPE_DIM // 2, 1) * sb

    def heads(x, g):
        return jnp.concatenate([norm_rope(x[:, HEAD * j:HEAD * (j + 1)], g) for j in range(MXU_COLS // HEAD)], axis=1)

    def by_class(x, dst_ref, dil, cols):
        if dil == 1:
            dst_ref[0, :, cols] = _bf(x)
        else:
            for j in range(MXU_COLS // LANES):
                stage[j] = x[:, LANES * j:LANES * (j + 1)]
            for r in range(dil):
                for j in range(MXU_COLS // LANES):
                    lo = cols.start + LANES * j
                    dst_ref[r, :, lo:lo + LANES] = _bf(stage.at[j][pl.ds(r, tm // dil, stride=dil), :])

    assert kw == MXU_COLS and Q_PER_GROUP * HEAD == 2 * MXU_COLS
    n_kv = k_ref.shape[1]

    def project(kind, c):
        lo = MXU_COLS * c + (n_kv if kind == "v" else 0)
        return _dot(hq if kind == "q" else hkv, (wq_ref if kind == "q" else wkv_ref)[:, lo:lo + MXU_COLS])

    def finish(kind, c, x):
        cols = slice(MXU_COLS * c, MXU_COLS * (c + 1))
        if kind == "k":
            x = heads(x, gk_ref[...])
            k_ref[:, cols] = x
        elif kind == "v":
            v_ref[:, cols] = x
        else:
            x = heads(x, gq_ref[...])
            if dils is None:
                q_ref[:, cols] = x
        if dils is not None:
            if kind == "q":
                by_class(x, qd[c // 2], dils[c // 2], slice(MXU_COLS * (c % 2), MXU_COLS * (c % 2 + 1)))
            else:
                by_class(x, (kd if kind == "k" else vd)[c], dils[c], slice(0, kw))

    yield
    for kind, n in (("k", n_kv), ("v", n_kv), ("q", wq_ref.shape[1])):
        for c in range(n // MXU_COLS):
            finish(kind, c, project(kind, c))
            yield


def _kvq_body(*refs, tm, dils):
    h_ref, consts, outs = refs[0], refs[1:8], refs[8:]
    if dils is None:
        q_ref, k_ref, v_ref = outs
        qd = kd = vd = stage = None
    else:
        ng = len(dils)
        q_ref, (k_ref, v_ref) = None, outs[:2]
        qd, kd, vd, stage = outs[2:2 + ng], outs[2 + ng:2 + 2 * ng], outs[2 + 2 * ng:2 + 3 * ng], outs[2 + 3 * ng]
    for _ in _kvq_stages(h_ref[...], *consts, q_ref, k_ref, v_ref, qd, kd, vd, stage, tm=tm, dils=dils):
        pass


def _kvq(h, table, w, tm, dils):
    m, d = h.shape
    n_q, n_kv = w["w_q"].shape[1], w["w_kv"].shape[1] // 2
    qw, kw = Q_PER_GROUP * HEAD, KV_PER_GROUP * HEAD
    consts = [w["g_kv"], w["g_mix1"], w["w_kv"], w["w_q"], w["g_k"], w["g_q"]]
    rows = [h, table]
    row_spec = lambda c: pl.BlockSpec((tm, c), lambda i: (i, 0))
    in_specs = [row_spec(a.shape[1]) for a in rows] + [_const_spec(a) for a in consts]
    nat = lambda c: jax.ShapeDtypeStruct((m, c), F32)
    if dils is None:
        out_shape = [nat(n_q), nat(n_kv), nat(n_kv)]
        out_specs, scratch = [row_spec(n_q), row_spec(n_kv), row_spec(n_kv)], []
    else:
        out_shape, out_specs = [nat(n_kv), nat(n_kv)], [row_spec(n_kv), row_spec(n_kv)]
        for width in (qw, kw, kw):
            for dil in dils:
                assert tm % (dil * 2 * SUBLANES) == 0
                out_shape.append(jax.ShapeDtypeStruct((dil, m // dil, width), BF16))
                out_specs.append(pl.BlockSpec((dil, tm // dil, width), lambda i: (0, i, 0)))
        scratch = [pltpu.VMEM((MXU_COLS // LANES, tm, LANES), F32)]
    return pl.pallas_call(functools.partial(_kvq_body, tm=tm, dils=dils), out_shape=out_shape, grid=(m // tm,),
                          in_specs=in_specs, out_specs=out_specs, scratch_shapes=scratch,
                          compiler_params=_params(), name="kvq")(*rows, *consts)


def _rope_table(first, count, consecutive):
    half = ROPE_DIM // 2
    inv_freq = ROPE_THETA ** (-jnp.arange(half, dtype=F32) * 2.0 / ROPE_DIM)
    pad = jnp.zeros((count, HEAD - ROPE_DIM), F32)
    if not consecutive:
        ang = jnp.full((count, 1), first, F32) * inv_freq[None, :]
        return jnp.concatenate([jnp.cos(ang), jnp.sin(ang), pad], axis=1)
    assert count % LANES == 0
    a = (first + LANES * jnp.arange(count // LANES, dtype=F32))[:, None] * inv_freq[None, :]
    b = jnp.arange(LANES, dtype=F32)[:, None] * inv_freq[None, :]
    ca, sa, cb, sb = jnp.cos(a)[:, None], jnp.sin(a)[:, None], jnp.cos(b)[None], jnp.sin(b)[None]
    cos = (ca * cb - sa * sb).reshape(count, half)
    sin = (sa * cb + ca * sb).reshape(count, half)
    return jnp.concatenate([cos, sin, pad], axis=1)


def _attn_prompt_body(q_ref, kc_ref, kp_ref, vc_ref, vp_ref, o_ref, lse_ref, *, n, blocks):
    first_lo = jnp.where(pl.program_id(1) > 0, 0, n)
    qi = lax.broadcasted_iota(jnp.int32, (Q_REP * n, 2 * n), 0) & (n - 1)
    kj = lax.broadcasted_iota(jnp.int32, (Q_REP * n, 2 * n), 1)
    band = (kj >= qi) & (kj <= qi + n)
    lane = lax.broadcasted_iota(jnp.int32, (n, LANES), 1)
    for j in range(blocks):
        rows = slice(n * j, n * (j + 1))
        lse_tile = jnp.zeros((n, LANES), F32)
        for g in range(KV_PER_GROUP):
            hs = slice(HEAD * g, HEAD * (g + 1))
            if j == 0:
                kprev, vprev = kp_ref[:, hs], vp_ref[:, hs]
                valid = band & (kj >= first_lo)
            else:
                prev = slice(n * (j - 1), n * j)
                kprev, vprev = kc_ref[prev, hs], vc_ref[prev, hs]
                valid = band
            kwin = jnp.concatenate([kprev, kc_ref[rows, hs]], axis=0)
            vwin = jnp.concatenate([vprev, vc_ref[rows, hs]], axis=0)
            q0 = Q_REP * HEAD * g
            qg = jnp.concatenate([q_ref[rows, q0 + HEAD * e:q0 + HEAD * (e + 1)] for e in range(Q_REP)], axis=0)
            s = jnp.where(valid, _dot_nt(qg, kwin) * (HEAD ** -0.5), -jnp.inf)
            mx = jnp.max(s, axis=-1, keepdims=True)
            p = jnp.exp(s - mx)
            den = jnp.sum(p, axis=-1, keepdims=True)
            o = _dot(_bf(p), vwin) / den
            lse = mx + jnp.log(den)
            for e in range(Q_REP):
                o_ref[rows, q0 + HEAD * e:q0 + HEAD * (e + 1)] = o[n * e:n * (e + 1)]
                lse_tile = jnp.where(lane == Q_REP * g + e, lse[n * e:n * (e + 1)], lse_tile)
        lse_ref[rows, :] = lse_tile


def _attn_prompt(qd, kd, vd, gi, window):
    dil, ln, qw = qd.shape
    kw = kd.shape[2]
    n = window // dil
    blocks = min(8, ln // n)
    rows = n * blocks
    assert n == HEAD and ln % rows == 0
    cur = lambda r, b: (r, b, 0)
    prev = lambda r, b: (r, jnp.maximum(b * blocks - 1, 0), 0)
    return pl.pallas_call(
        functools.partial(_attn_prompt_body, n=n, blocks=blocks),
        out_shape=[jax.ShapeDtypeStruct((dil, ln, qw), F32), jax.ShapeDtypeStruct((dil, ln, LANES), F32)],
        grid=(dil, ln // rows),
        in_specs=[pl.BlockSpec((None, rows, qw), cur), pl.BlockSpec((None, rows, kw), cur),
                  pl.BlockSpec((None, n, kw), prev), pl.BlockSpec((None, rows, kw), cur),
                  pl.BlockSpec((None, n, kw), prev)],
        out_specs=[pl.BlockSpec((None, rows, qw), cur), pl.BlockSpec((None, rows, LANES), cur)],
        compiler_params=_params(2), name=f"attn_prompt{gi}")(qd, kd, kd, vd, vd)


def _attn_sample_body(q_ref, kn_ref, vn_ref, *refs, bb):
    n_groups = len(GROUPS)
    caches, outs, lses = refs[:n_groups], refs[n_groups:2 * n_groups], refs[2 * n_groups:]
    lane = lax.broadcasted_iota(jnp.int32, (1, LANES), 1)
    kw = KV_PER_GROUP * HEAD
    for gi in range(n_groups):
        c_ref, o_ref, l_ref = caches[gi], outs[gi], lses[gi]
        for b in range(bb):
            lrow = jnp.zeros((1, LANES), F32)
            for g in range(KV_PER_GROUP):
                kc = c_ref[b, :, g, :]
                vc = c_ref[b, :, KV_PER_GROUP + g, :]
                kcol = kw * gi + HEAD * g
                kn = kn_ref[b:b + 1, kcol:kcol + HEAD]
                vn = vn_ref[b:b + 1, kcol:kcol + HEAD]
                for e in range(Q_REP):
                    oc = HEAD * (Q_REP * g + e)
                    qv = q_ref[b:b + 1, Q_PER_GROUP * HEAD * gi + oc:Q_PER_GROUP * HEAD * gi + oc + HEAD]
                    s = jnp.sum(kc * qv, axis=1, keepdims=True) * (HEAD ** -0.5)
                    sn = jnp.sum(kn * qv, axis=1, keepdims=True) * (HEAD ** -0.5)
                    mx = jnp.maximum(jnp.max(s, axis=0, keepdims=True), sn)
                    p = jnp.exp(s - mx)
                    pn = jnp.exp(sn - mx)
                    den = jnp.sum(p, axis=0, keepdims=True) + pn
                    o_ref[b:b + 1, oc:oc + HEAD] = (jnp.sum(p * vc, axis=0, keepdims=True) + pn * vn) / den
                    lrow = jnp.where(lane == Q_REP * g + e, mx + jnp.log(den), lrow)
            l_ref[b:b + 1, :] = lrow


def _attn_sample(q, k, v, caches):
    n_seq = q.shape[0]
    bb = SUBLANES
    assert n_seq % bb == 0
    qw, kw = Q_PER_GROUP * HEAD, KV_PER_GROUP * HEAD
    views, cache_specs = [], []
    for cache, (window, dil) in zip(caches, GROUPS):
        n = window // dil
        assert cache.shape[1] == window and n == HEAD
        views.append(cache.reshape(n_seq, n, dil, 2 * KV_PER_GROUP, HEAD))
        cache_specs.append(pl.BlockSpec((bb, n, None, 2 * KV_PER_GROUP, HEAD), lambda i: (i, 0, 0, 0, 0)))
    row = lambda c: pl.BlockSpec((bb, c), lambda i: (i, 0))
    n_groups = len(GROUPS)
    res = pl.pallas_call(
        functools.partial(_attn_sample_body, bb=bb),
        out_shape=[jax.ShapeDtypeStruct((n_seq, qw), F32)] * n_groups
        + [jax.ShapeDtypeStruct((n_seq, LANES), F32)] * n_groups,
        grid=(n_seq // bb,),
        in_specs=[row(q.shape[1]), row(k.shape[1]), row(v.shape[1])] + cache_specs,
        out_specs=[row(qw)] * n_groups + [row(LANES)] * n_groups,
        compiler_params=_params(), name="attn_sample")(q, k, v, *views)
    return res[:n_groups], res[n_groups:]


def _ffn0_kvq_body(*refs, tm, dils, n_tiles):
    mix_in, p_ref, ffn_w, tab_ref, kvq_w = refs[:5], refs[5], refs[6:13], refs[13], refs[14:20]
    out_ref, tail_ref, k_ref, v_ref = refs[20:24]
    ng = len(dils)
    qd, kd, vd = refs[24:24 + ng], refs[24 + ng:24 + 2 * ng], refs[24 + 2 * ng:24 + 3 * ng]
    ubuf, act_s, tail_s, hbuf, stage = refs[24 + 3 * ng:]
    step = pl.program_id(0)

    @pl.when(step == 0)
    def _():
        ubuf[...] = jnp.zeros(ubuf.shape, F32)
        hbuf[...] = jnp.zeros(hbuf.shape, F32)

    h_prev = hbuf[...]
    result = []
    _weave(_ffn_stages(_delta_out_mix(*mix_in), p_ref, None, *ffn_w, tail_s, ubuf, act_s, result, tm=tm),
           _kvq_stages(h_prev, tab_ref, *kvq_w, None, k_ref, v_ref, qd, kd, vd, stage, tm=tm, dils=dils))
    hbuf[...] = result[0]

    @pl.when(step < n_tiles)
    def _():
        out_ref[...] = result[0]
        tail_ref[...] = tail_s[...]


def _ffn0_kvq(o, z, x, p, table, w, tm, dils):
    m, d = x.shape
    dff = w["w_down"].shape[1]
    n_kv = w["w_kv"].shape[1] // 2
    qw, kw = Q_PER_GROUP * HEAD, KV_PER_GROUP * HEAD
    assert m % tm == 0
    n_tiles = m // tm
    cur = lambda s: (jnp.minimum(s, n_tiles - 1), 0)
    prev = lambda s: (jnp.maximum(s - 1, 0), 0)
    at = lambda a, imap: (a, pl.BlockSpec((tm, a.shape[1]), imap))
    ins = [at(o, cur), at(z, cur), at(x, cur), _whole(w["g_aout"]), _whole(w["w_aout"]),
           (p, pl.BlockSpec((None, tm, p.shape[2]), lambda s: (0, jnp.minimum(s, n_tiles - 1), 0)))]
    ins += [_layer(w[name], 0) for name in ("g_ffn", "w_up", "w_fconv", "w_down", "g_ple", "w_gate", "w_proj")]
    ins += [at(table, prev)] + [_whole(w[name]) for name in ("g_kv", "g_mix1", "w_kv", "w_q", "g_k", "g_q")]
    nat = lambda c, imap: (jax.ShapeDtypeStruct((m, c), F32), pl.BlockSpec((tm, c), imap))
    outs = [nat(d, cur), _out_whole((SUBLANES, 2 * dff)), nat(n_kv, prev), nat(n_kv, prev)]
    for width in (qw, kw, kw):
        for dil in dils:
            assert tm % (dil * 2 * SUBLANES) == 0
            outs.append((jax.ShapeDtypeStruct((dil, m // dil, width), BF16),
                         pl.BlockSpec((dil, tm // dil, width), lambda s: (0, jnp.maximum(s - 1, 0), 0))))
    scratch = [pltpu.VMEM((SUBLANES, 2 * dff), F32), pltpu.VMEM((tm, dff), BF16), pltpu.VMEM((SUBLANES, 2 * dff), F32),
               pltpu.VMEM((tm, d), F32), pltpu.VMEM((MXU_COLS // LANES, tm, LANES), F32)]
    return _call(functools.partial(_ffn0_kvq_body, tm=tm, dils=dils, n_tiles=n_tiles), "ffn0_kvq", n_tiles + 1,
                 ins, outs, scratch)


def _trunk(x, p, first_pos, state, w, tm, tm_wide):
    m, d = x.shape
    per_row = state is not None
    n_heads = w["n_heads"]
    aw = w["w_z"].shape[1]

    if per_row:
        consts = [w["g_mix0"], w["w_qkv"], w["w_ab"], w["w_z"], w["w_aconv"], w["a_log"], w["dt_bias"],
                  jnp.swapaxes(state["qkv_conv"], 0, 1)]
        q, k, v, z, gb, qkv_tail = _row_call(functools.partial(_a_in_step_body, n_heads=n_heads), "a_in_step", m, tm,
                                             [x], consts, [(aw, F32)] * 4 + [(LANES, F32)],
                                             const_outs=[((m, 3 * aw), F32)])
        o, s1 = _delta_step(q, k, v, gb, state["delta"], n_heads)
    else:
        o, z, s1, qkv_tail = _mixer_a_prompt(x, w, n_heads)
    table = _rope_table(first_pos, m, consecutive=not per_row)
    if per_row:
        mix = [_rows(o, tm_wide), _rows(z, tm_wide), _rows(x, tm_wide), _whole(w["g_aout"]), _whole(w["w_aout"])]
        h, ffn_tail0 = _mix_ffn(mix, [], None, p, w, 0, m, tm_wide, state["ffn_conv"][0])
        qa, ka, va = _kvq(h, table, w, tm, None)
        outs, lses = _attn_sample(qa, ka, va, state["caches"])
        outs, lses = [a[None] for a in outs], [a[None] for a in lses]
        dils = (1,) * len(GROUPS)
    else:
        dils = tuple(dil for _, dil in GROUPS)
        ng = len(dils)
        res = _ffn0_kvq(o, z, x, p, table, w, tm, dils)
        h, ffn_tail0, ka, va = res[:4]
        qd, kd, vd = res[4:4 + ng], res[4 + ng:4 + 2 * ng], res[4 + 2 * ng:]
        att = [_attn_prompt(qd[gi], kd[gi], vd[gi], gi, window) for gi, (window, _) in enumerate(GROUPS)]
        outs, lses = [a[0] for a in att], [a[1] for a in att]
    mix = [_grouped(a, tm_wide) for a in (*outs, *lses)] + [_rows(h, tm_wide), _whole(w["w_o"])]
    mix_scratch = []
    for dil in dils:
        if dil > 1:
            assert tm_wide % (dil * SUBLANES) == 0
            mix_scratch += [pltpu.VMEM((Q_PER_GROUP, tm_wide, HEAD), F32), pltpu.VMEM((tm_wide, LANES), F32)]
    h, ffn_tail1 = _mix_ffn(mix, mix_scratch, dils, p, w, 1, m, tm_wide, state["ffn_conv"][1] if per_row else None)
    return h, s1, qkv_tail, (ffn_tail0, ffn_tail1), ka, va


def _kv_rows(k, v, gi, n_rows):
    kw = KV_PER_GROUP * HEAD
    sl = slice(kw * gi, kw * (gi + 1))
    shape = (n_rows, 1, KV_PER_GROUP, HEAD)
    return jnp.concatenate([k[k.shape[0] - n_rows:, sl].reshape(shape), v[v.shape[0] - n_rows:, sl].reshape(shape)],
                           axis=1)


def kernel(x_prompt, x_sample, p_prompt, p_sample, state_delta, state_qkv_conv, state_ffn_conv, cache_kv_w128, cache_kv_w512, cache_kv_w2048, g_mix_norm, g_ffn_norm, w_ffn_up, w_ffn_conv, w_ffn_down, g_ple_norm, w_ple_gate, w_ple_proj, w_a_in, w_a_conv, a_log, a_dt_bias, g_a_out_norm, w_a_out, g_kv_norm, w_kv, g_k_norm, w_q, g_q_norm, w_o):
    n_heads = a_log.shape[1]
    aw = n_heads * HEAD
    assert w_a_in.shape[0] == 1 and w_q.shape[0] == 1 and w_a_in.shape[2] == 4 * aw + 2 * n_heads
    row = lambda a: a.reshape(1, -1)
    lane_pad = lambda a: jnp.pad(a, ((0, 0), (0, LANES - a.shape[1])))
    w = {
        "n_heads": n_heads,
        "g_mix0": row(g_mix_norm[0]), "g_mix1": row(g_mix_norm[1]), "g_ffn": g_ffn_norm[:, None], "g_ple": g_ple_norm[:, None],
        "w_up": _bf(w_ffn_up), "w_fconv": w_ffn_conv, "w_down": _bf(w_ffn_down),
        "w_gate": _bf(w_ple_gate), "w_proj": _bf(w_ple_proj),
        "w_qkv": _bf(w_a_in[0][:, :3 * aw]), "w_ab": _bf(lane_pad(w_a_in[0][:, 3 * aw:3 * aw + 2 * n_heads])),
        "w_z": _bf(w_a_in[0][:, 3 * aw + 2 * n_heads:]), "w_aconv": w_a_conv[0],
        "a_log": lane_pad(a_log), "dt_bias": lane_pad(a_dt_bias),
        "g_aout": row(g_a_out_norm[0]), "w_aout": _bf(w_a_out[0]),
        "g_kv": row(g_kv_norm), "w_kv": _bf(w_kv), "g_k": row(g_k_norm),
        "w_q": _bf(w_q[0]), "g_q": row(g_q_norm[0]), "w_o": _bf(w_o[0]),
    }
    assert x_prompt.shape[0] == 1 and x_sample.shape[1] == 1
    seq, d = x_prompt.shape[1], x_prompt.shape[2]
    n_seq = x_sample.shape[0]

    yp, sd_p, qkv_tail_p, ffn_tails_p, kp, vp = _trunk(
        x_prompt.reshape(seq, d), p_prompt.reshape(p_prompt.shape[0], seq, -1), 0, None, w,
        tm=256, tm_wide=512)
    a_keep, f_keep = state_qkv_conv.shape[2], state_ffn_conv.shape[2]
    sq_p = qkv_tail_p[SUBLANES - a_keep:][None, None]
    sf_p = jnp.stack([t[SUBLANES - f_keep:] for t in ffn_tails_p])[:, None]
    kv_p = [_kv_rows(kp, vp, gi, min(window, seq))[None] for gi, (window, _) in enumerate(GROUPS)]

    state = {"delta": state_delta[0], "qkv_conv": state_qkv_conv[0], "ffn_conv": state_ffn_conv,
             "caches": (cache_kv_w128, cache_kv_w512, cache_kv_w2048)}
    ys, sd_s, qkv_tail_s, ffn_tails_s, ks, vs = _trunk(
        x_sample.reshape(n_seq, d), p_sample.reshape(p_sample.shape[0], n_seq, -1), PAST_LEN,
        state, w, tm=n_seq, tm_wide=n_seq)
    sq_s = jnp.concatenate([state_qkv_conv[0][:, 1:], qkv_tail_s[:, None]], axis=1)[None]
    sf_s = jnp.stack([jnp.concatenate([state_ffn_conv[i][:, 1:], ffn_tails_s[i][:, None]], axis=1)
                      for i in range(2)])
    kv_s = [_kv_rows(ks, vs, gi, n_seq).reshape(n_seq, 1, 2, KV_PER_GROUP, HEAD) for gi in range(len(GROUPS))]

    return (yp[None], ys[:, None], sd_p[None, None], sq_p, sf_p, kv_p[0], kv_p[1], kv_p[2],
            sd_s[None], sq_s, sf_s, kv_s[0], kv_s[1], kv_s[2])
```

```python
import functools

import jax
import jax.numpy as jnp
from jax import lax
from jax.experimental import pallas as pl
from jax.experimental.pallas import tpu as pltpu

F32 = jnp.float32
BF16 = jnp.bfloat16

EPS = 1e-6
HEAD = 128
CHUNK = 64
GROUPS = ((128, 1), (512, 4), (2048, 16))
KV_PER_GROUP = 2
Q_REP = 2
Q_PER_GROUP = KV_PER_GROUP * Q_REP
ROPE_DIM = HEAD // 4
ROPE_THETA = 500000.0
PAST_LEN = 16384

LANES = 128
SUBLANES = 8
MXU_COLS = 256
VMEM_LIMIT_BYTES = 56 * 1024 * 1024

PREP_CHUNKS = 4


def _dot(a, b):
    return jnp.dot(a, b, preferred_element_type=F32)


def _dot_nt(a, b):
    return lax.dot_general(a, b, (((1,), (1,)), ((), ())), preferred_element_type=F32)


def _bf(x):
    return x.astype(BF16)


def _rms(x):
    return x * lax.rsqrt(jnp.mean(x * x, axis=-1, keepdims=True) + EPS)


def _sigmoid(x):
    return 1.0 / (1.0 + jnp.exp(-x))


def _silu(x):
    return x * _sigmoid(x)


def _split3f(x):
    hi = _bf(x).astype(F32)
    r = x - hi
    mid = _bf(r).astype(F32)
    return hi, mid, _bf(r - mid).astype(F32)


def _causal_conv_tile(x, w, carry_ref, cols):
    rows = x.shape[0]
    first = lax.broadcasted_iota(jnp.int32, (SUBLANES, x.shape[1]), 0) == 0
    z = x * w[0:1]
    for j in range(1, w.shape[0]):
        rolled = pltpu.roll(z, 1, 0)
        head = jnp.where(first, carry_ref[j - 1:j, cols], rolled[0:SUBLANES])
        carry_ref[j - 1:j, cols] = z[rows - 1:, :]
        z = jnp.concatenate([head, rolled[SUBLANES:]], axis=0) + x * w[j:j + 1]
    return z


def _const_spec(a):
    nd = a.ndim
    return pl.BlockSpec(a.shape, lambda *_: (0,) * nd, pipeline_mode=pl.Buffered(1))


def _params(n_axes=1):
    return pltpu.CompilerParams(dimension_semantics=("arbitrary",) * n_axes,
                                vmem_limit_bytes=VMEM_LIMIT_BYTES)


def _row_call(body, name, m, tm, row_ins, const_ins, row_outs, const_outs=(), scratch=()):
    assert m % tm == 0
    in_specs = [pl.BlockSpec((tm, a.shape[1]), lambda i: (i, 0)) for a in row_ins]
    in_specs += [_const_spec(a) for a in const_ins]
    out_shape = [jax.ShapeDtypeStruct((m, c), dt) for c, dt in row_outs]
    out_specs = [pl.BlockSpec((tm, c), lambda i: (i, 0)) for c, _ in row_outs]
    for shape, dt in const_outs:
        out_shape.append(jax.ShapeDtypeStruct(shape, dt))
        out_specs.append(pl.BlockSpec(shape, lambda i, _n=len(shape): (0,) * _n))
    return pl.pallas_call(body, out_shape=out_shape, grid=(m // tm,), in_specs=in_specs, out_specs=out_specs,
                          scratch_shapes=list(scratch), compiler_params=_params(), name=name)(*row_ins, *const_ins)


def _call(body, name, steps, ins, outs, scratch=()):
    return pl.pallas_call(body, out_shape=[o for o, _ in outs], grid=(steps,), in_specs=[sp for _, sp in ins],
                          out_specs=[sp for _, sp in outs], scratch_shapes=list(scratch),
                          compiler_params=_params(), name=name)(*[a for a, _ in ins])


def _rows(a, tm):
    return a, pl.BlockSpec((tm, a.shape[1]), lambda i: (i, 0))


def _whole(a):
    return a, _const_spec(a)


def _layer(a, layer):
    nd = a.ndim
    return a, pl.BlockSpec((None,) + a.shape[1:], lambda i: (layer,) + (0,) * (nd - 1), pipeline_mode=pl.Buffered(1))


def _layer_rows(a, layer, tm):
    return a, pl.BlockSpec((None, tm, a.shape[2]), lambda i: (layer, i, 0))


def _grouped(a, tm):
    return a, pl.BlockSpec((a.shape[0], tm // a.shape[0], a.shape[2]), lambda i: (0, i, 0))


def _out_rows(m, c, tm):
    return jax.ShapeDtypeStruct((m, c), F32), pl.BlockSpec((tm, c), lambda i: (i, 0))


def _out_whole(shape):
    return jax.ShapeDtypeStruct(shape, F32), pl.BlockSpec(shape, lambda i: (0,) * len(shape))


def _qkv_activation(y, part):
    a = _silu(y)
    if part == 2:
        return a
    halves = []
    for hh in range(MXU_COLS // HEAD):
        ah = a[:, HEAD * hh:HEAD * (hh + 1)]
        nrm = ah * lax.rsqrt(jnp.sum(ah * ah, axis=-1, keepdims=True) + EPS)
        halves.append(nrm * (HEAD ** -0.5) if part == 0 else nrm)
    return jnp.concatenate(halves, axis=1)


def _decay_and_beta(pab, alog_ref, dtb_ref, n_heads):
    lane = lax.broadcasted_iota(jnp.int32, pab.shape, 1)
    ap = pab + dtb_ref[...]
    softplus = jnp.maximum(ap, 0.0) + jnp.log1p(jnp.exp(-jnp.abs(ap)))
    return jnp.where(lane < n_heads, -jnp.exp(alog_ref[...]) * softplus, _sigmoid(pab))


def _a_in_step_body(x_ref, g_ref, wqkv_ref, wab_ref, wz_ref, wconv_ref, alog_ref, dtb_ref, buf_ref,
                    q_ref, k_ref, v_ref, z_ref, gb_ref, tail_ref, *, n_heads):
    hn = _bf(_rms(x_ref[...]) * g_ref[...])
    outs = (q_ref, k_ref, v_ref)
    per_part = q_ref.shape[1] // MXU_COLS
    for c in range(3 * per_part):
        cols = slice(MXU_COLS * c, MXU_COLS * (c + 1))
        pq = _dot(hn, wqkv_ref[:, cols])
        w = wconv_ref[:, cols]
        y = buf_ref[0, :, cols] * w[0:1]
        y = y + buf_ref[1, :, cols] * w[1:2]
        y = y + buf_ref[2, :, cols] * w[2:3]
        y = y + pq * w[3:4]
        tail_ref[:, cols] = pq
        part, sub = divmod(c, per_part)
        outs[part][:, MXU_COLS * sub:MXU_COLS * (sub + 1)] = _qkv_activation(y, part)
    gb_ref[...] = _decay_and_beta(_dot(hn, wab_ref[...]), alog_ref, dtb_ref, n_heads)
    z_ref[...] = _dot(hn, wz_ref[...])


def _weave(*stages):
    live = list(stages)
    while live:
        for g in list(live):
            try:
                next(g)
            except StopIteration:
                live.remove(g)


def _mixer_a_prompt_body(x_ref, g_ref, wqkv_ref, wab_ref, wz_ref, wconv_ref, alog_ref, dtb_ref,
                         z_ref, tail_ref, o_ref, sout_ref, cbuf, s_scr, *bufs, n_heads, n_chunks):
    step = pl.program_id(0)
    n_proj = 4
    proj_sets = bufs[:n_proj], bufs[n_proj:2 * n_proj]
    rest = bufs[2 * n_proj:]
    prep_sets = rest[:len(rest) // 2], rest[len(rest) // 2:]

    @pl.when(step == 0)
    def _():
        cbuf[...] = jnp.zeros(cbuf.shape, F32)
        s_scr[...] = jnp.zeros(s_scr.shape, F32)
        for ref in (*proj_sets[1], *prep_sets[0]):
            ref[...] = jnp.zeros(ref.shape, ref.dtype)

    sub = n_chunks // 2 * CHUNK

    def project(i, q_s, k_s, v_s, gb_s):
        rows = pl.ds(pl.multiple_of(i * sub, sub), sub)
        hn = _bf(_rms(x_ref[rows, :]) * g_ref[...])
        outs = (q_s, k_s, v_s)
        per_part = q_s.shape[1] // MXU_COLS
        for c in range(3 * per_part):
            cols = slice(MXU_COLS * c, MXU_COLS * (c + 1))
            pq = _dot(hn, wqkv_ref[:, cols])
            y = _causal_conv_tile(pq, wconv_ref[:, cols], cbuf, cols)
            tail_ref[:, cols] = pq[sub - SUBLANES:, :]
            part, sc = divmod(c, per_part)
            outs[part][rows, MXU_COLS * sc:MXU_COLS * (sc + 1)] = _qkv_activation(y, part)
            yield
        gb_s[rows, :] = _decay_and_beta(_dot(hn, wab_ref[...]), alog_ref, dtb_ref, n_heads)
        z_ref[rows, :] = _dot(hn, wz_ref[...])
        yield

    group = MXU_COLS // CHUNK
    shift = CHUNK.bit_length() - 1
    lane_p = lax.broadcasted_iota(jnp.int32, (CHUNK, MXU_COLS), 1)
    row_p = lax.broadcasted_iota(jnp.int32, (CHUNK, MXU_COLS), 0)
    col_p = lane_p & (CHUNK - 1)
    blk = [(lane_p >> shift) == u for u in range(group)]
    incl_p = row_p >= col_p
    strict_p = row_p > col_p
    eye_p = jnp.where(row_p == col_p, 1.0, 0.0).astype(F32)
    bd_r = lax.broadcasted_iota(jnp.int32, (MXU_COLS, MXU_COLS), 0)
    bd_c = lax.broadcasted_iota(jnp.int32, (MXU_COLS, MXU_COLS), 1)
    bd_mask = (bd_r >> shift) == (bd_c >> shift)
    ri = lax.broadcasted_iota(jnp.int32, (CHUNK, CHUNK), 0)
    ci = lax.broadcasted_iota(jnp.int32, (CHUNK, CHUNK), 1)
    ltri3 = jnp.concatenate([_bf(jnp.where(ri >= ci, 1.0, 0.0))] * 3, axis=1)
    g_r = lax.broadcasted_iota(jnp.int32, (MXU_COLS, 3 * LANES), 0) >> shift
    g_c = lax.broadcasted_iota(jnp.int32, (MXU_COLS, 3 * LANES), 1) & (LANES - 1)
    ones3 = jnp.ones((CHUNK, 3 * LANES), BF16)
    er = lax.broadcasted_iota(jnp.int32, (HEAD, HEAD), 0)
    ec = lax.broadcasted_iota(jnp.int32, (HEAD, HEAD), 1)
    eye_h = _bf(jnp.where(er == ec, 1.0, 0.0))
    heads = range(n_heads)
    hs = [slice(HEAD * h, HEAD * (h + 1)) for h in heads]

    def block_diag(x):
        return _bf(jnp.where(bd_mask, jnp.concatenate([x] * group, axis=0), 0.0))

    def pack_cols(cols):
        out = cols[-1]
        for u in reversed(range(group - 1)):
            out = jnp.where(blk[u], cols[u], out)
        return out

    wide = group * HEAD
    kd_r = lax.broadcasted_iota(jnp.int32, (MXU_COLS, wide), 0) >> shift
    kd_c = lax.broadcasted_iota(jnp.int32, (MXU_COLS, wide), 1) >> (HEAD.bit_length() - 1)
    kd_mask = kd_r == kd_c

    def key_blocks(kc):
        return _bf(jnp.where(kd_mask, jnp.concatenate([kc] * group, axis=0), 0.0))

    def stack(ref, rows, h0):
        return jnp.concatenate([ref[rows, hs[h]] for h in range(h0, h0 + group)], axis=0)

    def stack_cols(x, lo):
        return jnp.concatenate([x[:, lo + u:lo + u + 1] for u in range(group)], axis=0)

    def prep(i, q_ref, k_ref, v_ref, gb_ref, u0_s, wkqg_s, qk_s, kdt_s, gl_s):
        chunks = [i * PREP_CHUNKS + j for j in range(PREP_CHUNKS)]
        rows = [pl.ds(pl.multiple_of(c * CHUNK, CHUNK), CHUNK) for c in chunks]
        gbc = [gb_ref[r, :] for r in rows]
        gc = [_dot(ltri3, _bf(jnp.concatenate(_split3f(x), axis=0))) for x in gbc]
        glast = [x[CHUNK - 1:CHUNK, :] for x in gc]
        for j, c in enumerate(chunks):
            gl_s[c] = jnp.broadcast_to(glast[j], (SUBLANES, LANES))
        groups = [(j, h0) for j in range(PREP_CHUNKS) for h0 in range(0, n_heads, group)]
        ng = range(len(groups))
        kc = [k_ref[rows[j], HEAD * h0:HEAD * (h0 + group)] for j, h0 in groups]
        qc = [q_ref[rows[j], HEAD * h0:HEAD * (h0 + group)] for j, h0 in groups]
        pairs = [_dot_nt(_bf(jnp.concatenate([qc[g], kc[g]], axis=0)), key_blocks(kc[g])) for g in ng]
        yield
        gcol_p = [pack_cols([gc[j][:, h:h + 1] for h in range(h0, h0 + group)]) for j, h0 in groups]
        bcol_p = [pack_cols([gbc[j][:, n_heads + h:n_heads + h + 1] for h in range(h0, h0 + group)])
                  for j, h0 in groups]
        grow = [_dot_nt(ones3, _bf(jnp.where(g_c == g_r + h0,
                                               jnp.concatenate([jnp.concatenate([x] * group, axis=0)
                                                                for x in _split3f(gc[j])], axis=1), 0.0)))
                for j, h0 in groups]
        yield
        dec = [jnp.exp(jnp.where(incl_p, gcol_p[g] - grow[g], -jnp.inf)) for g in ng]
        a = [jnp.where(strict_p, bcol_p[g] * pairs[g][CHUNK:] * dec[g], 0.0) for g in ng]
        for g, (j, h0) in enumerate(groups):
            qk = _bf(pairs[g][:CHUNK] * dec[g])
            for u in range(group):
                qk_s[chunks[j], h0 + u] = qk[:, CHUNK * u:CHUNK * (u + 1)]
        tinv = [eye_p - x for x in a]
        pw = [_dot(_bf(-x), block_diag(-x)) for x in a]
        yield
        for _ in range(4):
            res = [_dot(jnp.concatenate([_bf(tinv[g]), _bf(pw[g])], axis=0), block_diag(pw[g])) for g in ng]
            tinv = [tinv[g] + res[g][:CHUNK] for g in ng]
            pw = [r[CHUNK:] for r in res]
            yield
        tinv = [tinv[g] + _dot(_bf(tinv[g]), block_diag(pw[g])) for g in ng]
        yield
        tbd = [block_diag(x) for x in tinv]
        kst = [stack(k_ref, rows[j], h0) for j, h0 in groups]
        gcol_s = [stack_cols(gc[j], h0) for j, h0 in groups]
        bcol_s = [stack_cols(gbc[j], n_heads + h0) for j, h0 in groups]
        gam_s = [jnp.exp(x) for x in gcol_s]
        rhs = [jnp.concatenate([stack(v_ref, rows[j], h0) * bcol_s[g], kst[g] * (bcol_s[g] * gam_s[g])], axis=1)
               for g, (j, h0) in enumerate(groups)]
        x0 = [_dot(tbd[g], _bf(rhs[g])) for g in ng]
        yield
        resid = [rhs[g] - x0[g] - _dot(block_diag(a[g]), _bf(x0[g])) for g in ng]
        yield
        sol = [x0[g] + _dot(tbd[g], _bf(resid[g])) for g in ng]
        yield
        kdt = [_dot_nt(eye_h, _bf(kst[g] * jnp.exp(stack_cols(jnp.broadcast_to(glast[j], (CHUNK, LANES)), h0)
                                                   - gcol_s[g])))
               for g, (j, h0) in enumerate(groups)]
        for g, (j, h0) in enumerate(groups):
            c = chunks[j]
            qg = stack(q_ref, rows[j], h0) * gam_s[g]
            for u in range(group):
                r = slice(CHUNK * u, CHUNK * (u + 1))
                u0_s[c, h0 + u] = sol[g][r, :HEAD]
                wkqg_s[c, h0 + u] = _bf(jnp.concatenate([sol[g][r, HEAD:], qg[r]], axis=0))
                kdt_s[c, h0 + u] = _bf(kdt[g][:, r])

    def scan(i, u0_s, wkqg_s, qk_s, kdt_s, gl_s):
        for j in range(PREP_CHUNKS):
            c = i * PREP_CHUNKS + j
            rows = pl.ds(pl.multiple_of(c * CHUNK, CHUNK), CHUNK)
            glr = gl_s[c]
            ws = [_dot(wkqg_s[c, h], _bf(s_scr[h])) for h in heads]
            yield
            ub = [_bf(u0_s[c, h] - ws[h][:CHUNK]) for h in heads]
            for h in heads:
                o_ref[rows, HEAD * h:HEAD * (h + 1)] = ws[h][CHUNK:] + _dot(qk_s[c, h], ub[h])
            yield
            for h in heads:
                s_scr[h] = jnp.exp(glr[0:1, h:h + 1]) * s_scr[h] + _dot(kdt_s[c, h], ub[h])
            yield

    def run(mine, other):
        def body(i, carry):
            _weave(project(i, *proj_sets[mine]), prep(i, *proj_sets[other], *prep_sets[other]),
                   scan(i, *prep_sets[mine]))
            return carry
        lax.fori_loop(0, n_chunks // PREP_CHUNKS, body, 0)

    @pl.when(step % 2 == 0)
    def _():
        run(0, 1)

    @pl.when(step % 2 == 1)
    def _():
        run(1, 0)

    sout_ref[...] = s_scr[...]


def _mixer_a_prompt(x, w, n_heads):
    t, d = x.shape
    aw = w["w_z"].shape[1]
    n_chunks = 2 * PREP_CHUNKS
    tt = n_chunks * CHUNK
    assert t % tt == 0 and n_chunks % PREP_CHUNKS == 0
    n_tiles = t // tt
    projected = [pltpu.VMEM((tt, aw), F32)] * 3 + [pltpu.VMEM((tt, LANES), F32)]
    prepared = [pltpu.VMEM((n_chunks, n_heads, CHUNK, HEAD), F32),
                pltpu.VMEM((n_chunks, n_heads, 2 * CHUNK, HEAD), BF16),
                pltpu.VMEM((n_chunks, n_heads, CHUNK, CHUNK), BF16),
                pltpu.VMEM((n_chunks, n_heads, HEAD, CHUNK), BF16),
                pltpu.VMEM((n_chunks, SUBLANES, LANES), F32)]
    scratch = [pltpu.VMEM((SUBLANES, 3 * aw), F32), pltpu.VMEM((n_heads, HEAD, HEAD), F32)]
    scratch += projected + projected + prepared + prepared
    first = lambda s: (jnp.minimum(s, n_tiles - 1), 0)
    last = lambda s: (jnp.maximum(s - 2, 0), 0)
    ins = [(x, pl.BlockSpec((tt, d), first))] + [_whole(w[name]) for name in
                                                  ("g_mix0", "w_qkv", "w_ab", "w_z", "w_aconv", "a_log", "dt_bias")]
    outs = [(jax.ShapeDtypeStruct((t, aw), F32), pl.BlockSpec((tt, aw), first)), _out_whole((SUBLANES, 3 * aw)),
            (jax.ShapeDtypeStruct((t, aw), F32), pl.BlockSpec((tt, aw), last)), _out_whole((n_heads, HEAD, HEAD))]
    z, tail, o, s1 = _call(functools.partial(_mixer_a_prompt_body, n_heads=n_heads, n_chunks=n_chunks),
                           "mixer_a_prompt", n_tiles + 2, ins, outs, scratch)
    return o, z, s1, tail


def _delta_step_body(qt_ref, kt_ref, v_ref, gbt_ref, s_ref, o_ref, so_ref, *, n_seq, n_heads):
    h = pl.program_id(0)
    decay = jnp.exp(gbt_ref[pl.ds(h, 1), :])
    beta = gbt_ref[pl.ds(n_heads + h, 1), :]
    for b in range(n_seq):
        s = decay[:, b:b + 1] * s_ref[b]
        kc = kt_ref[:, b:b + 1]
        u = beta[:, b:b + 1] * (v_ref[b:b + 1, :] - jnp.sum(kc * s, axis=0, keepdims=True))
        s = s + kc * u
        so_ref[b] = s
        o_ref[b:b + 1, :] = jnp.sum(qt_ref[:, b:b + 1] * s, axis=0, keepdims=True)


def _delta_step(q, k, v, gb, state, n_heads):
    n_seq = q.shape[0]
    col = pl.BlockSpec((HEAD, n_seq), lambda h: (h, 0))
    row = pl.BlockSpec((n_seq, HEAD), lambda h: (0, h))
    st = pl.BlockSpec((n_seq, None, HEAD, HEAD), lambda h: (0, h, 0, 0))
    gbt = gb.T
    return pl.pallas_call(
        functools.partial(_delta_step_body, n_seq=n_seq, n_heads=n_heads),
        out_shape=[jax.ShapeDtypeStruct(q.shape, F32), jax.ShapeDtypeStruct(state.shape, F32)],
        grid=(n_heads,), in_specs=[col, col, row, _const_spec(gbt), st], out_specs=[row, st],
        compiler_params=_params(), name="delta_step")(q.T, k.T, v, gbt, state)


def _delta_out_mix(o_ref, z_ref, res_ref, gout_ref, w_ref):
    parts = []
    for h in range(o_ref.shape[1] // HEAD):
        hs = slice(HEAD * h, HEAD * (h + 1))
        parts.append(_bf(_rms(o_ref[:, hs]) * gout_ref[...] * _silu(z_ref[:, hs])))
    return res_ref[...] + _dot(jnp.concatenate(parts, axis=1), w_ref[...])


def _attn_out_mix(refs, scratch, tm, dils):
    ng = len(dils)
    o_refs, l_refs = refs[:ng], refs[ng:2 * ng]
    res_ref, w_ref = refs[2 * ng:]
    get_o, lses = [], []
    for gi, dil in enumerate(dils):
        if dil == 1:
            get_o.append(lambda hh, r=o_refs[gi]: r[0, :, HEAD * hh:HEAD * (hh + 1)])
            lses.append(l_refs[gi][0])
        else:
            o_s, l_s = scratch.pop(0), scratch.pop(0)
            for r in range(dil):
                dst = pl.ds(r, tm // dil, stride=dil)
                for hh in range(Q_PER_GROUP):
                    o_s.at[hh][dst, :] = o_refs[gi][r, :, HEAD * hh:HEAD * (hh + 1)]
                l_s[dst, :] = l_refs[gi][r]
            get_o.append(lambda hh, r=o_s: r[hh])
            lses.append(l_s[...])
    parts = []
    for hh in range(Q_PER_GROUP):
        lh = [l[:, hh:hh + 1] for l in lses]
        mx = functools.reduce(jnp.maximum, lh)
        ex = [jnp.exp(x - mx) for x in lh]
        den = functools.reduce(lambda a, b: a + b, ex)
        acc = (ex[0] / den) * get_o[0](hh)
        for gi in range(1, ng):
            acc = acc + (ex[gi] / den) * get_o[gi](hh)
        parts.append(_bf(acc))
    return res_ref[...] + _dot(jnp.concatenate(parts, axis=1), w_ref[...])


def _ffn_stages(h, p_ref, buf_ref, gffn_ref, wup_ref, wconv_ref, wdown_ref, gple_ref, wgate_ref, wproj_ref,
                tail_ref, ubuf, act_s, result, *, tm):
    hn = _bf(_rms(h) * gffn_ref[...])
    dff = wdown_ref.shape[0]
    for c in range(dff // MXU_COLS):
        ys = []
        for half in range(2):
            lo = half * dff + MXU_COLS * c
            cols = slice(lo, lo + MXU_COLS)
            u = _dot(hn, wup_ref[:, cols])
            w = wconv_ref[:, cols]
            if buf_ref is not None:
                y = buf_ref[:, cols] * w[0:1]
                y = y + buf_ref[:, 2 * dff + lo:2 * dff + lo + MXU_COLS] * w[1:2]
                y = y + u * w[2:3]
                tail_ref[:, cols] = u
            else:
                y = _causal_conv_tile(u, w, ubuf, cols)
                tail_ref[:, cols] = u[tm - SUBLANES:, :]
            ys.append(y)
        act_s[:, MXU_COLS * c:MXU_COLS * (c + 1)] = _bf(_silu(ys[0]) * ys[1])
        yield
    h2 = h + _dot(act_s[...], wdown_ref[...])
    yield
    gate = _sigmoid(_dot(_bf(_rms(h2) * gple_ref[...]), wgate_ref[...]))
    result.append(h2 + gate * _dot(_bf(p_ref[...]), wproj_ref[...]))
    yield


def _mix_ffn_body(*refs, tm, per_row_state, dils):
    n_mix = 5 if dils is None else 2 * len(dils) + 2
    mix_in, refs = refs[:n_mix], refs[n_mix:]
    p_ref, refs = refs[0], refs[1:]
    buf_ref = None
    if per_row_state:
        buf_ref, refs = refs[0], refs[1:]
    ffn_w, (out_ref, tail_ref), scratch = refs[:7], refs[7:9], list(refs[9:])
    act_s = scratch.pop()
    ubuf = None
    if not per_row_state:
        ubuf = scratch.pop()

        @pl.when(pl.program_id(0) == 0)
        def _():
            ubuf[...] = jnp.zeros(ubuf.shape, F32)

    h = _delta_out_mix(*mix_in) if dils is None else _attn_out_mix(mix_in, scratch, tm, dils)
    result = []
    for _ in _ffn_stages(h, p_ref, buf_ref, *ffn_w, tail_ref, ubuf, act_s, result, tm=tm):
        pass
    out_ref[...] = result[0]


def _mix_ffn(mix_ins, mix_scratch, dils, p, w, layer, m, tm, state):
    per_row = state is not None
    d, dff = w["w_down"].shape[2], w["w_down"].shape[1]
    ins = list(mix_ins) + [_layer_rows(p, layer, tm)] + ([_rows(state.reshape(m, -1), tm)] if per_row else [])
    ins += [_layer(w[name], layer) for name in ("g_ffn", "w_up", "w_fconv", "w_down", "g_ple", "w_gate", "w_proj")]
    outs = [_out_rows(m, d, tm), _out_whole((m if per_row else SUBLANES, 2 * dff))]
    scratch = list(mix_scratch) + ([] if per_row else [pltpu.VMEM((SUBLANES, 2 * dff), F32)])
    scratch.append(pltpu.VMEM((tm, dff), BF16))
    return _call(functools.partial(_mix_ffn_body, tm=tm, per_row_state=per_row, dils=dils), f"mix_ffn{layer}",
                 m // tm, ins, outs, scratch)


def _kvq_stages(h, tab_ref, gkv_ref, gmix_ref, wkv_ref, wq_ref, gk_ref, gq_ref, q_ref, k_ref, v_ref, qd, kd, vd,
                stage, *, tm, dils):
    base = _rms(h)
    hkv = _bf(base * gkv_ref[...])
    hq = _bf(base * gmix_ref[...])
    tab = tab_ref[...]
    lane = lax.broadcasted_iota(jnp.int32, tab.shape, 1)
    half = ROPE_DIM // 2
    cos = jnp.where(lane < half, tab, jnp.where(lane < ROPE_DIM, pltpu.roll(tab, half, 1), 1.0))
    sa = jnp.where((lane >= half) & (lane < ROPE_DIM), tab, 0.0)
    sb = jnp.where(lane < half, -pltpu.roll(tab, HEAD - half, 1), 0.0)
    kw = KV_PER_GROUP * HEAD

    def norm_rope(x, g):
        y = _rms(x) * g
        return y * cos + pltpu.roll(y, ROPE_DIM // 2, 1) * sa + pltpu.roll(y, HEAD - ROPE_DIM // 2, 1) * sb

    def heads(x, g):
        return jnp.concatenate([norm_rope(x[:, HEAD * j:HEAD * (j + 1)], g) for j in range(MXU_COLS // HEAD)], axis=1)

    def by_class(x, dst_ref, dil, cols):
        if dil == 1:
            dst_ref[0, :, cols] = _bf(x)
        else:
            for j in range(MXU_COLS // LANES):
                stage[j] = x[:, LANES * j:LANES * (j + 1)]
            for r in range(dil):
                for j in range(MXU_COLS // LANES):
                    lo = cols.start + LANES * j
                    dst_ref[r, :, lo:lo + LANES] = _bf(stage.at[j][pl.ds(r, tm // dil, stride=dil), :])

    assert kw == MXU_COLS and Q_PER_GROUP * HEAD == 2 * MXU_COLS
    n_kv = k_ref.shape[1]

    def project(kind, c):
        lo = MXU_COLS * c + (n_kv if kind == "v" else 0)
        return _dot(hq if kind == "q" else hkv, (wq_ref if kind == "q" else wkv_ref)[:, lo:lo + MXU_COLS])

    def finish(kind, c, x):
        cols = slice(MXU_COLS * c, MXU_COLS * (c + 1))
        if kind == "k":
            x = heads(x, gk_ref[...])
            k_ref[:, cols] = x
        elif kind == "v":
            v_ref[:, cols] = x
        else:
            x = heads(x, gq_ref[...])
            if dils is None:
                q_ref[:, cols] = x
        if dils is not None:
            if kind == "q":
                by_class(x, qd[c // 2], dils[c // 2], slice(MXU_COLS * (c % 2), MXU_COLS * (c % 2 + 1)))
            else:
                by_class(x, (kd if kind == "k" else vd)[c], dils[c], slice(0, kw))

    yield
    for kind, n in (("k", n_kv), ("v", n_kv), ("q", wq_ref.shape[1])):
        for c in range(n // MXU_COLS):
            finish(kind, c, project(kind, c))
            yield


def _kvq_body(*refs, tm, dils):
    h_ref, consts, outs = refs[0], refs[1:8], refs[8:]
    if dils is None:
        q_ref, k_ref, v_ref = outs
        qd = kd = vd = stage = None
    else:
        ng = len(dils)
        q_ref, (k_ref, v_ref) = None, outs[:2]
        qd, kd, vd, stage = outs[2:2 + ng], outs[2 + ng:2 + 2 * ng], outs[2 + 2 * ng:2 + 3 * ng], outs[2 + 3 * ng]
    for _ in _kvq_stages(h_ref[...], *consts, q_ref, k_ref, v_ref, qd, kd, vd, stage, tm=tm, dils=dils):
        pass


def _kvq(h, table, w, tm, dils):
    m, d = h.shape
    n_q, n_kv = w["w_q"].shape[1], w["w_kv"].shape[1] // 2
    qw, kw = Q_PER_GROUP * HEAD, KV_PER_GROUP * HEAD
    consts = [w["g_kv"], w["g_mix1"], w["w_kv"], w["w_q"], w["g_k"], w["g_q"]]
    rows = [h, table]
    row_spec = lambda c: pl.BlockSpec((tm, c), lambda i: (i, 0))
    in_specs = [row_spec(a.shape[1]) for a in rows] + [_const_spec(a) for a in consts]
    nat = lambda c: jax.ShapeDtypeStruct((m, c), F32)
    if dils is None:
        out_shape = [nat(n_q), nat(n_kv), nat(n_kv)]
        out_specs, scratch = [row_spec(n_q), row_spec(n_kv), row_spec(n_kv)], []
    else:
        out_shape, out_specs = [nat(n_kv), nat(n_kv)], [row_spec(n_kv), row_spec(n_kv)]
        for width in (qw, kw, kw):
            for dil in dils:
                assert tm % (dil * 2 * SUBLANES) == 0
                out_shape.append(jax.ShapeDtypeStruct((dil, m // dil, width), BF16))
                out_specs.append(pl.BlockSpec((dil, tm // dil, width), lambda i: (0, i, 0)))
        scratch = [pltpu.VMEM((MXU_COLS // LANES, tm, LANES), F32)]
    return pl.pallas_call(functools.partial(_kvq_body, tm=tm, dils=dils), out_shape=out_shape, grid=(m // tm,),
                          in_specs=in_specs, out_specs=out_specs, scratch_shapes=scratch,
                          compiler_params=_params(), name="kvq")(*rows, *consts)


def _rope_table(first, count, consecutive):
    half = ROPE_DIM // 2
    inv_freq = ROPE_THETA ** (-jnp.arange(half, dtype=F32) * 2.0 / ROPE_DIM)
    pad = jnp.zeros((count, HEAD - ROPE_DIM), F32)
    if not consecutive:
        ang = jnp.full((count, 1), first, F32) * inv_freq[None, :]
        return jnp.concatenate([jnp.cos(ang), jnp.sin(ang), pad], axis=1)
    assert count % LANES == 0
    a = (first + LANES * jnp.arange(count // LANES, dtype=F32))[:, None] * inv_freq[None, :]
    b = jnp.arange(LANES, dtype=F32)[:, None] * inv_freq[None, :]
    ca, sa, cb, sb = jnp.cos(a)[:, None], jnp.sin(a)[:, None], jnp.cos(b)[None], jnp.sin(b)[None]
    cos = (ca * cb - sa * sb).reshape(count, half)
    sin = (sa * cb + ca * sb).reshape(count, half)
    return jnp.concatenate([cos, sin, pad], axis=1)


def _attn_prompt_body(q_ref, kc_ref, kp_ref, vc_ref, vp_ref, o_ref, lse_ref, *, n, blocks):
    first_lo = jnp.where(pl.program_id(1) > 0, 0, n)
    qi = lax.broadcasted_iota(jnp.int32, (Q_REP * n, 2 * n), 0) & (n - 1)
    kj = lax.broadcasted_iota(jnp.int32, (Q_REP * n, 2 * n), 1)
    band = (kj >= qi) & (kj <= qi + n)
    lane = lax.broadcasted_iota(jnp.int32, (n, LANES), 1)
    for j in range(blocks):
        rows = slice(n * j, n * (j + 1))
        lse_tile = jnp.zeros((n, LANES), F32)
        for g in range(KV_PER_GROUP):
            hs = slice(HEAD * g, HEAD * (g + 1))
            if j == 0:
                kprev, vprev = kp_ref[:, hs], vp_ref[:, hs]
                valid = band & (kj >= first_lo)
            else:
                prev = slice(n * (j - 1), n * j)
                kprev, vprev = kc_ref[prev, hs], vc_ref[prev, hs]
                valid = band
            kwin = jnp.concatenate([kprev, kc_ref[rows, hs]], axis=0)
            vwin = jnp.concatenate([vprev, vc_ref[rows, hs]], axis=0)
            q0 = Q_REP * HEAD * g
            qg = jnp.concatenate([q_ref[rows, q0 + HEAD * e:q0 + HEAD * (e + 1)] for e in range(Q_REP)], axis=0)
            s = jnp.where(valid, _dot_nt(qg, kwin) * (HEAD ** -0.5), -jnp.inf)
            mx = jnp.max(s, axis=-1, keepdims=True)
            p = jnp.exp(s - mx)
            den = jnp.sum(p, axis=-1, keepdims=True)
            o = _dot(_bf(p), vwin) / den
            lse = mx + jnp.log(den)
            for e in range(Q_REP):
                o_ref[rows, q0 + HEAD * e:q0 + HEAD * (e + 1)] = o[n * e:n * (e + 1)]
                lse_tile = jnp.where(lane == Q_REP * g + e, lse[n * e:n * (e + 1)], lse_tile)
        lse_ref[rows, :] = lse_tile


def _attn_prompt(qd, kd, vd, gi, window):
    dil, ln, qw = qd.shape
    kw = kd.shape[2]
    n = window // dil
    blocks = min(8, ln // n)
    rows = n * blocks
    assert n == HEAD and ln % rows == 0
    cur = lambda r, b: (r, b, 0)
    prev = lambda r, b: (r, jnp.maximum(b * blocks - 1, 0), 0)
    return pl.pallas_call(
        functools.partial(_attn_prompt_body, n=n, blocks=blocks),
        out_shape=[jax.ShapeDtypeStruct((dil, ln, qw), F32), jax.ShapeDtypeStruct((dil, ln, LANES), F32)],
        grid=(dil, ln // rows),
        in_specs=[pl.BlockSpec((None, rows, qw), cur), pl.BlockSpec((None, rows, kw), cur),
                  pl.BlockSpec((None, n, kw), prev), pl.BlockSpec((None, rows, kw), cur),
                  pl.BlockSpec((None, n, kw), prev)],
        out_specs=[pl.BlockSpec((None, rows, qw), cur), pl.BlockSpec((None, rows, LANES), cur)],
        compiler_params=_params(2), name=f"attn_prompt{gi}")(qd, kd, kd, vd, vd)


def _attn_sample_body(q_ref, kn_ref, vn_ref, *refs, bb):
    n_groups = len(GROUPS)
    caches, outs, lses = refs[:n_groups], refs[n_groups:2 * n_groups], refs[2 * n_groups:]
    lane = lax.broadcasted_iota(jnp.int32, (1, LANES), 1)
    kw = KV_PER_GROUP * HEAD
    units = [(gi, b, g) for gi in range(n_groups) for b in range(bb) for g in range(KV_PER_GROUP)]
    scale = HEAD ** -0.5
    qm, kn, vn = [], [], []
    for gi, b, g in units:
        q0 = Q_PER_GROUP * HEAD * gi + Q_REP * HEAD * g
        qm.append(jnp.concatenate([q_ref[b:b + 1, q0 + HEAD * e:q0 + HEAD * (e + 1)] for e in range(Q_REP)]
                                  + [jnp.zeros((SUBLANES - Q_REP, HEAD), F32)], axis=0))
        kn.append(kn_ref[b:b + 1, kw * gi + HEAD * g:kw * gi + HEAD * (g + 1)])
        vn.append(vn_ref[b:b + 1, kw * gi + HEAD * g:kw * gi + HEAD * (g + 1)])
    n = range(len(units))
    s = [_dot_nt(_bf(qm[u]), _bf(caches[gi][b, :, g, :])) * scale for u, (gi, b, g) in enumerate(units)]
    sn = [jnp.sum(qm[u] * kn[u], axis=1, keepdims=True) * scale for u in n]
    mx = [jnp.maximum(jnp.max(s[u], axis=1, keepdims=True), sn[u]) for u in n]
    p = [jnp.exp(s[u] - mx[u]) for u in n]
    pn = [jnp.exp(sn[u] - mx[u]) for u in n]
    den = [jnp.sum(p[u], axis=1, keepdims=True) + pn[u] for u in n]
    pv = [_dot(_bf(p[u]), _bf(caches[gi][b, :, KV_PER_GROUP + g, :])) for u, (gi, b, g) in enumerate(units)]
    lrows = {}
    for u, (gi, b, g) in enumerate(units):
        o = (pv[u] + pn[u] * vn[u]) / den[u]
        lse = mx[u] + jnp.log(den[u])
        for e in range(Q_REP):
            oc = HEAD * (Q_REP * g + e)
            outs[gi][b:b + 1, oc:oc + HEAD] = o[e:e + 1]
            lrows[gi, b] = jnp.where(lane == Q_REP * g + e, lse[e:e + 1], lrows.get((gi, b), jnp.zeros((1, LANES), F32)))
    for (gi, b), lrow in lrows.items():
        lses[gi][b:b + 1, :] = lrow


def _attn_sample(q, k, v, caches):
    n_seq = q.shape[0]
    bb = SUBLANES
    assert n_seq % bb == 0
    qw, kw = Q_PER_GROUP * HEAD, KV_PER_GROUP * HEAD
    views, cache_specs = [], []
    for cache, (window, dil) in zip(caches, GROUPS):
        n = window // dil
        assert cache.shape[1] == window and n == HEAD
        views.append(cache.reshape(n_seq, n, dil, 2 * KV_PER_GROUP, HEAD))
        cache_specs.append(pl.BlockSpec((bb, n, None, 2 * KV_PER_GROUP, HEAD), lambda i: (i, 0, 0, 0, 0)))
    row = lambda c: pl.BlockSpec((bb, c), lambda i: (i, 0))
    n_groups = len(GROUPS)
    res = pl.pallas_call(
        functools.partial(_attn_sample_body, bb=bb),
        out_shape=[jax.ShapeDtypeStruct((n_seq, qw), F32)] * n_groups
        + [jax.ShapeDtypeStruct((n_seq, LANES), F32)] * n_groups,
        grid=(n_seq // bb,),
        in_specs=[row(q.shape[1]), row(k.shape[1]), row(v.shape[1])] + cache_specs,
        out_specs=[row(qw)] * n_groups + [row(LANES)] * n_groups,
        compiler_params=_params(), name="attn_sample")(q, k, v, *views)
    return res[:n_groups], res[n_groups:]


def _ffn0_kvq_body(*refs, tm, dils, n_tiles):
    mix_in, p_ref, ffn_w, tab_ref, kvq_w = refs[:5], refs[5], refs[6:13], refs[13], refs[14:20]
    out_ref, tail_ref, k_ref, v_ref = refs[20:24]
    ng = len(dils)
    qd, kd, vd = refs[24:24 + ng], refs[24 + ng:24 + 2 * ng], refs[24 + 2 * ng:24 + 3 * ng]
    ubuf, act_s, tail_s, hbuf, stage = refs[24 + 3 * ng:]
    step = pl.program_id(0)

    @pl.when(step == 0)
    def _():
        ubuf[...] = jnp.zeros(ubuf.shape, F32)
        hbuf[...] = jnp.zeros(hbuf.shape, F32)

    h_prev = hbuf[...]
    result = []
    _weave(_ffn_stages(_delta_out_mix(*mix_in), p_ref, None, *ffn_w, tail_s, ubuf, act_s, result, tm=tm),
           _kvq_stages(h_prev, tab_ref, *kvq_w, None, k_ref, v_ref, qd, kd, vd, stage, tm=tm, dils=dils))
    hbuf[...] = result[0]

    @pl.when(step < n_tiles)
    def _():
        out_ref[...] = result[0]
        tail_ref[...] = tail_s[...]


def _ffn0_kvq(o, z, x, p, table, w, tm, dils):
    m, d = x.shape
    dff = w["w_down"].shape[1]
    n_kv = w["w_kv"].shape[1] // 2
    qw, kw = Q_PER_GROUP * HEAD, KV_PER_GROUP * HEAD
    assert m % tm == 0
    n_tiles = m // tm
    cur = lambda s: (jnp.minimum(s, n_tiles - 1), 0)
    prev = lambda s: (jnp.maximum(s - 1, 0), 0)
    at = lambda a, imap: (a, pl.BlockSpec((tm, a.shape[1]), imap))
    ins = [at(o, cur), at(z, cur), at(x, cur), _whole(w["g_aout"]), _whole(w["w_aout"]),
           (p, pl.BlockSpec((None, tm, p.shape[2]), lambda s: (0, jnp.minimum(s, n_tiles - 1), 0)))]
    ins += [_layer(w[name], 0) for name in ("g_ffn", "w_up", "w_fconv", "w_down", "g_ple", "w_gate", "w_proj")]
    ins += [at(table, prev)] + [_whole(w[name]) for name in ("g_kv", "g_mix1", "w_kv", "w_q", "g_k", "g_q")]
    nat = lambda c, imap: (jax.ShapeDtypeStruct((m, c), F32), pl.BlockSpec((tm, c), imap))
    outs = [nat(d, cur), _out_whole((SUBLANES, 2 * dff)), nat(n_kv, prev), nat(n_kv, prev)]
    for width in (qw, kw, kw):
        for dil in dils:
            assert tm % (dil * 2 * SUBLANES) == 0
            outs.append((jax.ShapeDtypeStruct((dil, m // dil, width), BF16),
                         pl.BlockSpec((dil, tm // dil, width), lambda s: (0, jnp.maximum(s - 1, 0), 0))))
    scratch = [pltpu.VMEM((SUBLANES, 2 * dff), F32), pltpu.VMEM((tm, dff), BF16), pltpu.VMEM((SUBLANES, 2 * dff), F32),
               pltpu.VMEM((tm, d), F32), pltpu.VMEM((MXU_COLS // LANES, tm, LANES), F32)]
    return _call(functools.partial(_ffn0_kvq_body, tm=tm, dils=dils, n_tiles=n_tiles), "ffn0_kvq", n_tiles + 1,
                 ins, outs, scratch)


def _trunk(x, p, first_pos, state, w, tm, tm_wide):
    m, d = x.shape
    per_row = state is not None
    n_heads = w["n_heads"]
    aw = w["w_z"].shape[1]

    if per_row:
        consts = [w["g_mix0"], w["w_qkv"], w["w_ab"], w["w_z"], w["w_aconv"], w["a_log"], w["dt_bias"],
                  jnp.swapaxes(state["qkv_conv"], 0, 1)]
        q, k, v, z, gb, qkv_tail = _row_call(functools.partial(_a_in_step_body, n_heads=n_heads), "a_in_step", m, tm,
                                             [x], consts, [(aw, F32)] * 4 + [(LANES, F32)],
                                             const_outs=[((m, 3 * aw), F32)])
        o, s1 = _delta_step(q, k, v, gb, state["delta"], n_heads)
    else:
        o, z, s1, qkv_tail = _mixer_a_prompt(x, w, n_heads)
    table = _rope_table(first_pos, m, consecutive=not per_row)
    if per_row:
        mix = [_rows(o, tm_wide), _rows(z, tm_wide), _rows(x, tm_wide), _whole(w["g_aout"]), _whole(w["w_aout"])]
        h, ffn_tail0 = _mix_ffn(mix, [], None, p, w, 0, m, tm_wide, state["ffn_conv"][0])
        qa, ka, va = _kvq(h, table, w, tm, None)
        outs, lses = _attn_sample(qa, ka, va, state["caches"])
        outs, lses = [a[None] for a in outs], [a[None] for a in lses]
        dils = (1,) * len(GROUPS)
    else:
        dils = tuple(dil for _, dil in GROUPS)
        ng = len(dils)
        res = _ffn0_kvq(o, z, x, p, table, w, tm, dils)
        h, ffn_tail0, ka, va = res[:4]
        qd, kd, vd = res[4:4 + ng], res[4 + ng:4 + 2 * ng], res[4 + 2 * ng:]
        att = [_attn_prompt(qd[gi], kd[gi], vd[gi], gi, window) for gi, (window, _) in enumerate(GROUPS)]
        outs, lses = [a[0] for a in att], [a[1] for a in att]
    mix = [_grouped(a, tm_wide) for a in (*outs, *lses)] + [_rows(h, tm_wide), _whole(w["w_o"])]
    mix_scratch = []
    for dil in dils:
        if dil > 1:
            assert tm_wide % (dil * SUBLANES) == 0
            mix_scratch += [pltpu.VMEM((Q_PER_GROUP, tm_wide, HEAD), F32), pltpu.VMEM((tm_wide, LANES), F32)]
    h, ffn_tail1 = _mix_ffn(mix, mix_scratch, dils, p, w, 1, m, tm_wide, state["ffn_conv"][1] if per_row else None)
    return h, s1, qkv_tail, (ffn_tail0, ffn_tail1), ka, va


def _kv_rows(k, v, gi, n_rows):
    kw = KV_PER_GROUP * HEAD
    sl = slice(kw * gi, kw * (gi + 1))
    shape = (n_rows, 1, KV_PER_GROUP, HEAD)
    return jnp.concatenate([k[k.shape[0] - n_rows:, sl].reshape(shape), v[v.shape[0] - n_rows:, sl].reshape(shape)],
                           axis=1)


def kernel(x_prompt, x_sample, p_prompt, p_sample, state_delta, state_qkv_conv, state_ffn_conv, cache_kv_w128, cache_kv_w512, cache_kv_w2048, g_mix_norm, g_ffn_norm, w_ffn_up, w_ffn_conv, w_ffn_down, g_ple_norm, w_ple_gate, w_ple_proj, w_a_in, w_a_conv, a_log, a_dt_bias, g_a_out_norm, w_a_out, g_kv_norm, w_kv, g_k_norm, w_q, g_q_norm, w_o):
    n_heads = a_log.shape[1]
    aw = n_heads * HEAD
    assert w_a_in.shape[0] == 1 and w_q.shape[0] == 1 and w_a_in.shape[2] == 4 * aw + 2 * n_heads
    row = lambda a: a.reshape(1, -1)
    lane_pad = lambda a: jnp.pad(a, ((0, 0), (0, LANES - a.shape[1])))
    w = {
        "n_heads": n_heads,
        "g_mix0": row(g_mix_norm[0]), "g_mix1": row(g_mix_norm[1]), "g_ffn": g_ffn_norm[:, None], "g_ple": g_ple_norm[:, None],
        "w_up": _bf(w_ffn_up), "w_fconv": w_ffn_conv, "w_down": _bf(w_ffn_down),
        "w_gate": _bf(w_ple_gate), "w_proj": _bf(w_ple_proj),
        "w_qkv": _bf(w_a_in[0][:, :3 * aw]), "w_ab": _bf(lane_pad(w_a_in[0][:, 3 * aw:3 * aw + 2 * n_heads])),
        "w_z": _bf(w_a_in[0][:, 3 * aw + 2 * n_heads:]), "w_aconv": w_a_conv[0],
        "a_log": lane_pad(a_log), "dt_bias": lane_pad(a_dt_bias),
        "g_aout": row(g_a_out_norm[0]), "w_aout": _bf(w_a_out[0]),
        "g_kv": row(g_kv_norm), "w_kv": _bf(w_kv), "g_k": row(g_k_norm),
        "w_q": _bf(w_q[0]), "g_q": row(g_q_norm[0]), "w_o": _bf(w_o[0]),
    }
    assert x_prompt.shape[0] == 1 and x_sample.shape[1] == 1
    seq, d = x_prompt.shape[1], x_prompt.shape[2]
    n_seq = x_sample.shape[0]

    yp, sd_p, qkv_tail_p, ffn_tails_p, kp, vp = _trunk(
        x_prompt.reshape(seq, d), p_prompt.reshape(p_prompt.shape[0], seq, -1), 0, None, w,
        tm=256, tm_wide=512)
    a_keep, f_keep = state_qkv_conv.shape[2], state_ffn_conv.shape[2]
    sq_p = qkv_tail_p[SUBLANES - a_keep:][None, None]
    sf_p = jnp.stack([t[SUBLANES - f_keep:] for t in ffn_tails_p])[:, None]
    kv_p = [_kv_rows(kp, vp, gi, min(window, seq))[None] for gi, (window, _) in enumerate(GROUPS)]

    state = {"delta": state_delta[0], "qkv_conv": state_qkv_conv[0], "ffn_conv": state_ffn_conv,
             "caches": (cache_kv_w128, cache_kv_w512, cache_kv_w2048)}
    ys, sd_s, qkv_tail_s, ffn_tails_s, ks, vs = _trunk(
        x_sample.reshape(n_seq, d), p_sample.reshape(p_sample.shape[0], n_seq, -1), PAST_LEN,
        state, w, tm=n_seq, tm_wide=n_seq)
    sq_s = jnp.concatenate([state_qkv_conv[0][:, 1:], qkv_tail_s[:, None]], axis=1)[None]
    sf_s = jnp.stack([jnp.concatenate([state_ffn_conv[i][:, 1:], ffn_tails_s[i][:, None]], axis=1)
                      for i in range(2)])
    kv_s = [_kv_rows(ks, vs, gi, n_seq).reshape(n_seq, 1, 2, KV_PER_GROUP, HEAD) for gi in range(len(GROUPS))]

    return (yp[None], ys[:, None], sd_p[None, None], sq_p, sf_p, kv_p[0], kv_p[1], kv_p[2],
            sd_s[None], sq_s, sf_s, kv_s[0], kv_s[1], kv_s[2])
```

```python
import functools

import jax
import jax.numpy as jnp
from jax import lax
from jax.experimental import pallas as pl
from jax.experimental.pallas import tpu as pltpu

F32 = jnp.float32
BF16 = jnp.bfloat16

EPS = 1e-6
HEAD = 128
CHUNK = 64
GROUPS = ((128, 1), (512, 4), (2048, 16))
KV_PER_GROUP = 2
Q_REP = 2
Q_PER_GROUP = KV_PER_GROUP * Q_REP
ROPE_DIM = HEAD // 4
ROPE_THETA = 500000.0
PAST_LEN = 16384

LANES = 128
SUBLANES = 8
MXU_COLS = 256
VMEM_LIMIT_BYTES = 56 * 1024 * 1024

ROW_TILE = 256
WIDE_ROW_TILE = 512
PREP_CHUNKS = 4
ATTN_BLOCKS = 8
SAMPLE_SEQS = SUBLANES


def _dot(a, b):
    return jnp.dot(a, b, preferred_element_type=F32)


def _dot_nt(a, b):
    return lax.dot_general(a, b, (((1,), (1,)), ((), ())), preferred_element_type=F32)


def _bf(x):
    return x.astype(BF16)


def _rms(x):
    return x * lax.rsqrt(jnp.mean(x * x, axis=-1, keepdims=True) + EPS)


def _sigmoid(x):
    return 1.0 / (1.0 + jnp.exp(-x))


def _silu(x):
    return x * _sigmoid(x)


def _split3f(x):
    hi = _bf(x).astype(F32)
    r = x - hi
    mid = _bf(r).astype(F32)
    return hi, mid, _bf(r - mid).astype(F32)


def _causal_conv_tile(x, w, carry_ref, cols):
    rows = x.shape[0]
    first = lax.broadcasted_iota(jnp.int32, (SUBLANES, x.shape[1]), 0) == 0
    z = x * w[0:1]
    for j in range(1, w.shape[0]):
        rolled = pltpu.roll(z, 1, 0)
        head = jnp.where(first, carry_ref[j - 1:j, cols], rolled[0:SUBLANES])
        carry_ref[j - 1:j, cols] = z[rows - 1:, :]
        z = jnp.concatenate([head, rolled[SUBLANES:]], axis=0) + x * w[j:j + 1]
    return z


def _const_spec(a):
    nd = a.ndim
    return pl.BlockSpec(a.shape, lambda *_: (0,) * nd, pipeline_mode=pl.Buffered(1))


def _params(n_axes=1):
    return pltpu.CompilerParams(dimension_semantics=("arbitrary",) * n_axes,
                                vmem_limit_bytes=VMEM_LIMIT_BYTES)


def _call(body, name, steps, ins, outs, scratch=()):
    return pl.pallas_call(body, out_shape=[o for o, _ in outs], grid=(steps,), in_specs=[sp for _, sp in ins],
                          out_specs=[sp for _, sp in outs], scratch_shapes=list(scratch),
                          compiler_params=_params(), name=name)(*[a for a, _ in ins])


def _rows(a, tm):
    return a, pl.BlockSpec((tm, a.shape[1]), lambda i: (i, 0))


def _whole(a):
    return a, _const_spec(a)


def _layer(a, layer):
    nd = a.ndim
    return a, pl.BlockSpec((None,) + a.shape[1:], lambda i: (layer,) + (0,) * (nd - 1), pipeline_mode=pl.Buffered(1))


def _layer_rows(a, layer, tm):
    return a, pl.BlockSpec((None, tm, a.shape[2]), lambda i: (layer, i, 0))


def _grouped(a, tm):
    return a, pl.BlockSpec((a.shape[0], tm // a.shape[0], a.shape[2]), lambda i: (0, i, 0))


def _out_rows(m, c, tm):
    return jax.ShapeDtypeStruct((m, c), F32), pl.BlockSpec((tm, c), lambda i: (i, 0))


def _out_whole(shape):
    return jax.ShapeDtypeStruct(shape, F32), pl.BlockSpec(shape, lambda i: (0,) * len(shape))


def _qkv_activation(y, part):
    a = _silu(y)
    if part == 2:
        return a
    halves = []
    for hh in range(MXU_COLS // HEAD):
        ah = a[:, HEAD * hh:HEAD * (hh + 1)]
        nrm = ah * lax.rsqrt(jnp.sum(ah * ah, axis=-1, keepdims=True) + EPS)
        halves.append(nrm * (HEAD ** -0.5) if part == 0 else nrm)
    return jnp.concatenate(halves, axis=1)


def _decay_and_beta(pab, alog_ref, dtb_ref, n_heads):
    lane = lax.broadcasted_iota(jnp.int32, pab.shape, 1)
    ap = pab + dtb_ref[...]
    softplus = jnp.maximum(ap, 0.0) + jnp.log1p(jnp.exp(-jnp.abs(ap)))
    return jnp.where(lane < n_heads, -jnp.exp(alog_ref[...]) * softplus, _sigmoid(pab))


def _a_in_step_body(x_ref, g_ref, wqkv_ref, wab_ref, wz_ref, wconv_ref, alog_ref, dtb_ref, buf_ref,
                    q_ref, k_ref, v_ref, z_ref, gb_ref, tail_ref, *, n_heads):
    hn = _bf(_rms(x_ref[...]) * g_ref[...])
    outs = (q_ref, k_ref, v_ref)
    per_part = q_ref.shape[1] // MXU_COLS
    for c in range(3 * per_part):
        cols = slice(MXU_COLS * c, MXU_COLS * (c + 1))
        pq = _dot(hn, wqkv_ref[:, cols])
        w = wconv_ref[:, cols]
        y = buf_ref[0, :, cols] * w[0:1]
        y = y + buf_ref[1, :, cols] * w[1:2]
        y = y + buf_ref[2, :, cols] * w[2:3]
        y = y + pq * w[3:4]
        tail_ref[:, cols] = pq
        part, sub = divmod(c, per_part)
        outs[part][:, MXU_COLS * sub:MXU_COLS * (sub + 1)] = _qkv_activation(y, part)
    gb_ref[...] = _decay_and_beta(_dot(hn, wab_ref[...]), alog_ref, dtb_ref, n_heads)
    z_ref[...] = _dot(hn, wz_ref[...])


def _weave(*stages):
    live = list(stages)
    while live:
        for g in list(live):
            try:
                next(g)
            except StopIteration:
                live.remove(g)


def _mixer_a_prompt_body(x_ref, g_ref, wqkv_ref, wab_ref, wz_ref, wconv_ref, alog_ref, dtb_ref,
                         z_ref, tail_ref, o_ref, sout_ref, cbuf, s_scr, *bufs, n_heads, n_chunks):
    step = pl.program_id(0)
    n_proj = 4
    proj_sets = bufs[:n_proj], bufs[n_proj:2 * n_proj]
    rest = bufs[2 * n_proj:]
    prep_sets = rest[:len(rest) // 2], rest[len(rest) // 2:]

    @pl.when(step == 0)
    def _():
        cbuf[...] = jnp.zeros(cbuf.shape, F32)
        s_scr[...] = jnp.zeros(s_scr.shape, F32)
        for ref in (*proj_sets[1], *prep_sets[0]):
            ref[...] = jnp.zeros(ref.shape, ref.dtype)

    sub = n_chunks // 2 * CHUNK

    def project(i, q_s, k_s, v_s, gb_s):
        rows = pl.ds(pl.multiple_of(i * sub, sub), sub)
        hn = _bf(_rms(x_ref[rows, :]) * g_ref[...])
        outs = (q_s, k_s, v_s)
        per_part = q_s.shape[1] // MXU_COLS
        for c in range(3 * per_part):
            cols = slice(MXU_COLS * c, MXU_COLS * (c + 1))
            pq = _dot(hn, wqkv_ref[:, cols])
            y = _causal_conv_tile(pq, wconv_ref[:, cols], cbuf, cols)
            tail_ref[:, cols] = pq[sub - SUBLANES:, :]
            part, sc = divmod(c, per_part)
            outs[part][rows, MXU_COLS * sc:MXU_COLS * (sc + 1)] = _qkv_activation(y, part)
            yield
        gb_s[rows, :] = _decay_and_beta(_dot(hn, wab_ref[...]), alog_ref, dtb_ref, n_heads)
        z_ref[rows, :] = _dot(hn, wz_ref[...])
        yield

    group = MXU_COLS // CHUNK
    shift = CHUNK.bit_length() - 1
    lane_p = lax.broadcasted_iota(jnp.int32, (CHUNK, MXU_COLS), 1)
    row_p = lax.broadcasted_iota(jnp.int32, (CHUNK, MXU_COLS), 0)
    col_p = lane_p & (CHUNK - 1)
    blk = [(lane_p >> shift) == u for u in range(group)]
    incl_p = row_p >= col_p
    strict_p = row_p > col_p
    eye_p = jnp.where(row_p == col_p, 1.0, 0.0).astype(F32)
    bd_r = lax.broadcasted_iota(jnp.int32, (MXU_COLS, MXU_COLS), 0)
    bd_c = lax.broadcasted_iota(jnp.int32, (MXU_COLS, MXU_COLS), 1)
    bd_mask = (bd_r >> shift) == (bd_c >> shift)
    ri = lax.broadcasted_iota(jnp.int32, (CHUNK, CHUNK), 0)
    ci = lax.broadcasted_iota(jnp.int32, (CHUNK, CHUNK), 1)
    ltri3 = jnp.concatenate([_bf(jnp.where(ri >= ci, 1.0, 0.0))] * 3, axis=1)
    g_r = lax.broadcasted_iota(jnp.int32, (MXU_COLS, 3 * LANES), 0) >> shift
    g_c = lax.broadcasted_iota(jnp.int32, (MXU_COLS, 3 * LANES), 1) & (LANES - 1)
    ones3 = jnp.ones((CHUNK, 3 * LANES), BF16)
    er = lax.broadcasted_iota(jnp.int32, (HEAD, HEAD), 0)
    ec = lax.broadcasted_iota(jnp.int32, (HEAD, HEAD), 1)
    eye_h = _bf(jnp.where(er == ec, 1.0, 0.0))
    heads = range(n_heads)
    hs = [slice(HEAD * h, HEAD * (h + 1)) for h in heads]

    def block_diag(x):
        return _bf(jnp.where(bd_mask, jnp.concatenate([x] * group, axis=0), 0.0))

    def pack_cols(cols):
        out = cols[-1]
        for u in reversed(range(group - 1)):
            out = jnp.where(blk[u], cols[u], out)
        return out

    wide = group * HEAD
    kd_r = lax.broadcasted_iota(jnp.int32, (MXU_COLS, wide), 0) >> shift
    kd_c = lax.broadcasted_iota(jnp.int32, (MXU_COLS, wide), 1) >> (HEAD.bit_length() - 1)
    kd_mask = kd_r == kd_c

    def key_blocks(kc):
        return _bf(jnp.where(kd_mask, jnp.concatenate([kc] * group, axis=0), 0.0))

    def stack(ref, rows, h0):
        return jnp.concatenate([ref[rows, hs[h]] for h in range(h0, h0 + group)], axis=0)

    def stack_cols(x, lo):
        return jnp.concatenate([x[:, lo + u:lo + u + 1] for u in range(group)], axis=0)

    def prep(i, q_ref, k_ref, v_ref, gb_ref, u0_s, wkqg_s, qk_s, kdt_s, gl_s):
        chunks = [i * PREP_CHUNKS + j for j in range(PREP_CHUNKS)]
        rows = [pl.ds(pl.multiple_of(c * CHUNK, CHUNK), CHUNK) for c in chunks]
        gbc = [gb_ref[r, :] for r in rows]
        gc = [_dot(ltri3, _bf(jnp.concatenate(_split3f(x), axis=0))) for x in gbc]
        glast = [x[CHUNK - 1:CHUNK, :] for x in gc]
        for j, c in enumerate(chunks):
            gl_s[c] = jnp.broadcast_to(glast[j], (SUBLANES, LANES))
        groups = [(j, h0) for j in range(PREP_CHUNKS) for h0 in range(0, n_heads, group)]
        ng = range(len(groups))
        kc = [k_ref[rows[j], HEAD * h0:HEAD * (h0 + group)] for j, h0 in groups]
        qc = [q_ref[rows[j], HEAD * h0:HEAD * (h0 + group)] for j, h0 in groups]
        pairs = [_dot_nt(_bf(jnp.concatenate([qc[g], kc[g]], axis=0)), key_blocks(kc[g])) for g in ng]
        yield
        gcol_p = [pack_cols([gc[j][:, h:h + 1] for h in range(h0, h0 + group)]) for j, h0 in groups]
        bcol_p = [pack_cols([gbc[j][:, n_heads + h:n_heads + h + 1] for h in range(h0, h0 + group)])
                  for j, h0 in groups]
        grow = [_dot_nt(ones3, _bf(jnp.where(g_c == g_r + h0,
                                               jnp.concatenate([jnp.concatenate([x] * group, axis=0)
                                                                for x in _split3f(gc[j])], axis=1), 0.0)))
                for j, h0 in groups]
        yield
        dec = [jnp.exp(jnp.where(incl_p, gcol_p[g] - grow[g], -jnp.inf)) for g in ng]
        a = [jnp.where(strict_p, bcol_p[g] * pairs[g][CHUNK:] * dec[g], 0.0) for g in ng]
        for g, (j, h0) in enumerate(groups):
            qk = _bf(pairs[g][:CHUNK] * dec[g])
            for u in range(group):
                qk_s[chunks[j], h0 + u] = qk[:, CHUNK * u:CHUNK * (u + 1)]
        tinv = [eye_p - x for x in a]
        pw = [_dot(_bf(-x), block_diag(-x)) for x in a]
        yield
        for _ in range(4):
            res = [_dot(jnp.concatenate([_bf(tinv[g]), _bf(pw[g])], axis=0), block_diag(pw[g])) for g in ng]
            tinv = [tinv[g] + res[g][:CHUNK] for g in ng]
            pw = [r[CHUNK:] for r in res]
            yield
        tinv = [tinv[g] + _dot(_bf(tinv[g]), block_diag(pw[g])) for g in ng]
        yield
        tbd = [block_diag(x) for x in tinv]
        kst = [stack(k_ref, rows[j], h0) for j, h0 in groups]
        gcol_s = [stack_cols(gc[j], h0) for j, h0 in groups]
        bcol_s = [stack_cols(gbc[j], n_heads + h0) for j, h0 in groups]
        gam_s = [jnp.exp(x) for x in gcol_s]
        rhs = [jnp.concatenate([stack(v_ref, rows[j], h0) * bcol_s[g], kst[g] * (bcol_s[g] * gam_s[g])], axis=1)
               for g, (j, h0) in enumerate(groups)]
        x0 = [_dot(tbd[g], _bf(rhs[g])) for g in ng]
        yield
        resid = [rhs[g] - x0[g] - _dot(block_diag(a[g]), _bf(x0[g])) for g in ng]
        yield
        sol = [x0[g] + _dot(tbd[g], _bf(resid[g])) for g in ng]
        yield
        kdt = [_dot_nt(eye_h, _bf(kst[g] * jnp.exp(stack_cols(jnp.broadcast_to(glast[j], (CHUNK, LANES)), h0)
                                                   - gcol_s[g])))
               for g, (j, h0) in enumerate(groups)]
        for g, (j, h0) in enumerate(groups):
            c = chunks[j]
            qg = stack(q_ref, rows[j], h0) * gam_s[g]
            for u in range(group):
                r = slice(CHUNK * u, CHUNK * (u + 1))
                u0_s[c, h0 + u] = sol[g][r, :HEAD]
                wkqg_s[c, h0 + u] = _bf(jnp.concatenate([sol[g][r, HEAD:], qg[r]], axis=0))
                kdt_s[c, h0 + u] = _bf(kdt[g][:, r])

    def scan(i, u0_s, wkqg_s, qk_s, kdt_s, gl_s):
        for j in range(PREP_CHUNKS):
            c = i * PREP_CHUNKS + j
            rows = pl.ds(pl.multiple_of(c * CHUNK, CHUNK), CHUNK)
            glr = gl_s[c]
            ws = [_dot(wkqg_s[c, h], _bf(s_scr[h])) for h in heads]
            yield
            ub = [_bf(u0_s[c, h] - ws[h][:CHUNK]) for h in heads]
            for h in heads:
                o_ref[rows, HEAD * h:HEAD * (h + 1)] = ws[h][CHUNK:] + _dot(qk_s[c, h], ub[h])
            yield
            for h in heads:
                s_scr[h] = jnp.exp(glr[0:1, h:h + 1]) * s_scr[h] + _dot(kdt_s[c, h], ub[h])
            yield

    def run(mine, other):
        def body(i, carry):
            _weave(project(i, *proj_sets[mine]), prep(i, *proj_sets[other], *prep_sets[other]),
                   scan(i, *prep_sets[mine]))
            return carry
        lax.fori_loop(0, n_chunks // PREP_CHUNKS, body, 0)

    @pl.when(step % 2 == 0)
    def _():
        run(0, 1)

    @pl.when(step % 2 == 1)
    def _():
        run(1, 0)

    sout_ref[...] = s_scr[...]


def _mixer_a_prompt(x, w, n_heads):
    t, d = x.shape
    aw = w["w_z"].shape[1]
    n_chunks = 2 * PREP_CHUNKS
    tt = n_chunks * CHUNK
    assert t % tt == 0 and n_chunks % PREP_CHUNKS == 0
    n_tiles = t // tt
    projected = [pltpu.VMEM((tt, aw), F32)] * 3 + [pltpu.VMEM((tt, LANES), F32)]
    prepared = [pltpu.VMEM((n_chunks, n_heads, CHUNK, HEAD), F32),
                pltpu.VMEM((n_chunks, n_heads, 2 * CHUNK, HEAD), BF16),
                pltpu.VMEM((n_chunks, n_heads, CHUNK, CHUNK), BF16),
                pltpu.VMEM((n_chunks, n_heads, HEAD, CHUNK), BF16),
                pltpu.VMEM((n_chunks, SUBLANES, LANES), F32)]
    scratch = [pltpu.VMEM((SUBLANES, 3 * aw), F32), pltpu.VMEM((n_heads, HEAD, HEAD), F32)]
    scratch += projected + projected + prepared + prepared
    first = lambda s: (jnp.minimum(s, n_tiles - 1), 0)
    last = lambda s: (jnp.maximum(s - 2, 0), 0)
    ins = [(x, pl.BlockSpec((tt, d), first))] + [_whole(w[name]) for name in
                                                  ("g_mix0", "w_qkv", "w_ab", "w_z", "w_aconv", "a_log", "dt_bias")]
    outs = [(jax.ShapeDtypeStruct((t, aw), F32), pl.BlockSpec((tt, aw), first)), _out_whole((SUBLANES, 3 * aw)),
            (jax.ShapeDtypeStruct((t, aw), F32), pl.BlockSpec((tt, aw), last)), _out_whole((n_heads, HEAD, HEAD))]
    z, tail, o, s1 = _call(functools.partial(_mixer_a_prompt_body, n_heads=n_heads, n_chunks=n_chunks),
                           "mixer_a_prompt", n_tiles + 2, ins, outs, scratch)
    return o, z, s1, tail


def _delta_step_body(qt_ref, kt_ref, v_ref, gbt_ref, s_ref, o_ref, so_ref, *, n_seq, n_heads):
    h = pl.program_id(0)
    decay = jnp.exp(gbt_ref[pl.ds(h, 1), :])
    beta = gbt_ref[pl.ds(n_heads + h, 1), :]
    for b in range(n_seq):
        s = decay[:, b:b + 1] * s_ref[b]
        kc = kt_ref[:, b:b + 1]
        u = beta[:, b:b + 1] * (v_ref[b:b + 1, :] - jnp.sum(kc * s, axis=0, keepdims=True))
        s = s + kc * u
        so_ref[b] = s
        o_ref[b:b + 1, :] = jnp.sum(qt_ref[:, b:b + 1] * s, axis=0, keepdims=True)


def _delta_step(q, k, v, gb, state, n_heads):
    n_seq = q.shape[0]
    col = pl.BlockSpec((HEAD, n_seq), lambda h: (h, 0))
    row = pl.BlockSpec((n_seq, HEAD), lambda h: (0, h))
    st = pl.BlockSpec((n_seq, None, HEAD, HEAD), lambda h: (0, h, 0, 0))
    gbt = gb.T
    return pl.pallas_call(
        functools.partial(_delta_step_body, n_seq=n_seq, n_heads=n_heads),
        out_shape=[jax.ShapeDtypeStruct(q.shape, F32), jax.ShapeDtypeStruct(state.shape, F32)],
        grid=(n_heads,), in_specs=[col, col, row, _const_spec(gbt), st], out_specs=[row, st],
        compiler_params=_params(), name="delta_step")(q.T, k.T, v, gbt, state)


def _delta_out_mix(o_ref, z_ref, res_ref, gout_ref, w_ref):
    parts = []
    for h in range(o_ref.shape[1] // HEAD):
        hs = slice(HEAD * h, HEAD * (h + 1))
        parts.append(_bf(_rms(o_ref[:, hs]) * gout_ref[...] * _silu(z_ref[:, hs])))
    return res_ref[...] + _dot(jnp.concatenate(parts, axis=1), w_ref[...])


def _attn_out_mix(refs, scratch, tm, dils):
    ng = len(dils)
    o_refs, l_refs = refs[:ng], refs[ng:2 * ng]
    res_ref, w_ref = refs[2 * ng:]
    get_o, lses = [], []
    for gi, dil in enumerate(dils):
        if dil == 1:
            get_o.append(lambda hh, r=o_refs[gi]: r[0, :, HEAD * hh:HEAD * (hh + 1)])
            lses.append(l_refs[gi][0])
        else:
            o_s, l_s = scratch.pop(0), scratch.pop(0)
            for r in range(dil):
                dst = pl.ds(r, tm // dil, stride=dil)
                for hh in range(Q_PER_GROUP):
                    o_s.at[hh][dst, :] = o_refs[gi][r, :, HEAD * hh:HEAD * (hh + 1)]
                l_s[dst, :] = l_refs[gi][r]
            get_o.append(lambda hh, r=o_s: r[hh])
            lses.append(l_s[...])
    parts = []
    for hh in range(Q_PER_GROUP):
        lh = [l[:, hh:hh + 1] for l in lses]
        mx = functools.reduce(jnp.maximum, lh)
        ex = [jnp.exp(x - mx) for x in lh]
        den = functools.reduce(lambda a, b: a + b, ex)
        acc = (ex[0] / den) * get_o[0](hh)
        for gi in range(1, ng):
            acc = acc + (ex[gi] / den) * get_o[gi](hh)
        parts.append(_bf(acc))
    return res_ref[...] + _dot(jnp.concatenate(parts, axis=1), w_ref[...])


def _ffn_stages(h, p_ref, buf_ref, gffn_ref, wup_ref, wconv_ref, wdown_ref, gple_ref, wgate_ref, wproj_ref,
                tail_ref, ubuf, act_s, result, *, tm):
    hn = _bf(_rms(h) * gffn_ref[...])
    dff = wdown_ref.shape[0]
    for c in range(dff // MXU_COLS):
        ys = []
        for half in range(2):
            lo = half * dff + MXU_COLS * c
            cols = slice(lo, lo + MXU_COLS)
            u = _dot(hn, wup_ref[:, cols])
            w = wconv_ref[:, cols]
            if buf_ref is not None:
                y = buf_ref[:, cols] * w[0:1]
                y = y + buf_ref[:, 2 * dff + lo:2 * dff + lo + MXU_COLS] * w[1:2]
                y = y + u * w[2:3]
                tail_ref[:, cols] = u
            else:
                y = _causal_conv_tile(u, w, ubuf, cols)
                tail_ref[:, cols] = u[tm - SUBLANES:, :]
            ys.append(y)
        act_s[:, MXU_COLS * c:MXU_COLS * (c + 1)] = _bf(_silu(ys[0]) * ys[1])
        yield
    h2 = h + _dot(act_s[...], wdown_ref[...])
    yield
    gate = _sigmoid(_dot(_bf(_rms(h2) * gple_ref[...]), wgate_ref[...]))
    result.append(h2 + gate * _dot(_bf(p_ref[...]), wproj_ref[...]))
    yield


def _mix_ffn_body(*refs, tm, per_row_state, dils):
    n_mix = 5 if dils is None else 2 * len(dils) + 2
    mix_in, refs = refs[:n_mix], refs[n_mix:]
    p_ref, refs = refs[0], refs[1:]
    buf_ref = None
    if per_row_state:
        buf_ref, refs = refs[0], refs[1:]
    ffn_w, (out_ref, tail_ref), scratch = refs[:7], refs[7:9], list(refs[9:])
    act_s = scratch.pop()
    ubuf = None
    if not per_row_state:
        ubuf = scratch.pop()

        @pl.when(pl.program_id(0) == 0)
        def _():
            ubuf[...] = jnp.zeros(ubuf.shape, F32)

    h = _delta_out_mix(*mix_in) if dils is None else _attn_out_mix(mix_in, scratch, tm, dils)
    result = []
    for _ in _ffn_stages(h, p_ref, buf_ref, *ffn_w, tail_ref, ubuf, act_s, result, tm=tm):
        pass
    out_ref[...] = result[0]


def _mix_ffn(mix_ins, mix_scratch, dils, p, w, layer, m, tm, state):
    per_row = state is not None
    d, dff = w["w_down"].shape[2], w["w_down"].shape[1]
    ins = list(mix_ins) + [_layer_rows(p, layer, tm)] + ([_rows(state.reshape(m, -1), tm)] if per_row else [])
    ins += [_layer(w[name], layer) for name in ("g_ffn", "w_up", "w_fconv", "w_down", "g_ple", "w_gate", "w_proj")]
    outs = [_out_rows(m, d, tm), _out_whole((m if per_row else SUBLANES, 2 * dff))]
    scratch = list(mix_scratch) + ([] if per_row else [pltpu.VMEM((SUBLANES, 2 * dff), F32)])
    scratch.append(pltpu.VMEM((tm, dff), BF16))
    return _call(functools.partial(_mix_ffn_body, tm=tm, per_row_state=per_row, dils=dils), f"mix_ffn{layer}",
                 m // tm, ins, outs, scratch)


def _kvq_stages(h, tab_ref, gkv_ref, gmix_ref, wkv_ref, wq_ref, gk_ref, gq_ref, q_ref, k_ref, v_ref, qd, kd, vd,
                stage, *, tm, dils):
    base = _rms(h)
    hkv = _bf(base * gkv_ref[...])
    hq = _bf(base * gmix_ref[...])
    tab = tab_ref[...]
    lane = lax.broadcasted_iota(jnp.int32, tab.shape, 1)
    half = ROPE_DIM // 2
    cos = jnp.where(lane < half, tab, jnp.where(lane < ROPE_DIM, pltpu.roll(tab, half, 1), 1.0))
    sa = jnp.where((lane >= half) & (lane < ROPE_DIM), tab, 0.0)
    sb = jnp.where(lane < half, -pltpu.roll(tab, HEAD - half, 1), 0.0)
    kw = KV_PER_GROUP * HEAD

    def norm_rope(x, g):
        y = _rms(x) * g
        return y * cos + pltpu.roll(y, ROPE_DIM // 2, 1) * sa + pltpu.roll(y, HEAD - ROPE_DIM // 2, 1) * sb

    def heads(x, g):
        return jnp.concatenate([norm_rope(x[:, HEAD * j:HEAD * (j + 1)], g) for j in range(MXU_COLS // HEAD)], axis=1)

    def by_class(x, dst_ref, dil, cols):
        if dil == 1:
            dst_ref[0, :, cols] = _bf(x)
        else:
            for j in range(MXU_COLS // LANES):
                stage[j] = x[:, LANES * j:LANES * (j + 1)]
            for r in range(dil):
                for j in range(MXU_COLS // LANES):
                    lo = cols.start + LANES * j
                    dst_ref[r, :, lo:lo + LANES] = _bf(stage.at[j][pl.ds(r, tm // dil, stride=dil), :])

    assert kw == MXU_COLS and Q_PER_GROUP * HEAD == 2 * MXU_COLS
    n_kv = k_ref.shape[1]

    def project(kind, c):
        lo = MXU_COLS * c + (n_kv if kind == "v" else 0)
        return _dot(hq if kind == "q" else hkv, (wq_ref if kind == "q" else wkv_ref)[:, lo:lo + MXU_COLS])

    def finish(kind, c, x):
        cols = slice(MXU_COLS * c, MXU_COLS * (c + 1))
        if kind == "k":
            x = heads(x, gk_ref[...])
            k_ref[:, cols] = x
        elif kind == "v":
            v_ref[:, cols] = x
        else:
            x = heads(x, gq_ref[...])
            if dils is None:
                q_ref[:, cols] = x
        if dils is not None:
            if kind == "q":
                by_class(x, qd[c // 2], dils[c // 2], slice(MXU_COLS * (c % 2), MXU_COLS * (c % 2 + 1)))
            else:
                by_class(x, (kd if kind == "k" else vd)[c], dils[c], slice(0, kw))

    yield
    for kind, n in (("k", n_kv), ("v", n_kv), ("q", wq_ref.shape[1])):
        for c in range(n // MXU_COLS):
            finish(kind, c, project(kind, c))
            yield


def _kvq_body(*refs, tm, dils):
    h_ref, consts, outs = refs[0], refs[1:8], refs[8:]
    if dils is None:
        q_ref, k_ref, v_ref = outs
        qd = kd = vd = stage = None
    else:
        ng = len(dils)
        q_ref, (k_ref, v_ref) = None, outs[:2]
        qd, kd, vd, stage = outs[2:2 + ng], outs[2 + ng:2 + 2 * ng], outs[2 + 2 * ng:2 + 3 * ng], outs[2 + 3 * ng]
    for _ in _kvq_stages(h_ref[...], *consts, q_ref, k_ref, v_ref, qd, kd, vd, stage, tm=tm, dils=dils):
        pass


def _kvq(h, table, w, tm, dils):
    m = h.shape[0]
    n_q, n_kv = w["w_q"].shape[1], w["w_kv"].shape[1] // 2
    qw, kw = Q_PER_GROUP * HEAD, KV_PER_GROUP * HEAD
    consts = [w["g_kv"], w["g_mix1"], w["w_kv"], w["w_q"], w["g_k"], w["g_q"]]
    rows = [h, table]
    row_spec = lambda c: pl.BlockSpec((tm, c), lambda i: (i, 0))
    in_specs = [row_spec(a.shape[1]) for a in rows] + [_const_spec(a) for a in consts]
    nat = lambda c: jax.ShapeDtypeStruct((m, c), F32)
    if dils is None:
        out_shape = [nat(n_q), nat(n_kv), nat(n_kv)]
        out_specs, scratch = [row_spec(n_q), row_spec(n_kv), row_spec(n_kv)], []
    else:
        out_shape, out_specs = [nat(n_kv), nat(n_kv)], [row_spec(n_kv), row_spec(n_kv)]
        for width in (qw, kw, kw):
            for dil in dils:
                assert tm % (dil * 2 * SUBLANES) == 0
                out_shape.append(jax.ShapeDtypeStruct((dil, m // dil, width), BF16))
                out_specs.append(pl.BlockSpec((dil, tm // dil, width), lambda i: (0, i, 0)))
        scratch = [pltpu.VMEM((MXU_COLS // LANES, tm, LANES), F32)]
    return pl.pallas_call(functools.partial(_kvq_body, tm=tm, dils=dils), out_shape=out_shape, grid=(m // tm,),
                          in_specs=in_specs, out_specs=out_specs, scratch_shapes=scratch,
                          compiler_params=_params(), name="kvq")(*rows, *consts)


def _rope_table(first, count, consecutive):
    half = ROPE_DIM // 2
    inv_freq = ROPE_THETA ** (-jnp.arange(half, dtype=F32) * 2.0 / ROPE_DIM)
    pad = jnp.zeros((count, HEAD - ROPE_DIM), F32)
    if not consecutive:
        ang = jnp.full((count, 1), first, F32) * inv_freq[None, :]
        return jnp.concatenate([jnp.cos(ang), jnp.sin(ang), pad], axis=1)
    assert count % LANES == 0
    a = (first + LANES * jnp.arange(count // LANES, dtype=F32))[:, None] * inv_freq[None, :]
    b = jnp.arange(LANES, dtype=F32)[:, None] * inv_freq[None, :]
    ca, sa, cb, sb = jnp.cos(a)[:, None], jnp.sin(a)[:, None], jnp.cos(b)[None], jnp.sin(b)[None]
    cos = (ca * cb - sa * sb).reshape(count, half)
    sin = (sa * cb + ca * sb).reshape(count, half)
    return jnp.concatenate([cos, sin, pad], axis=1)


def _attn_prompt_body(q_ref, kc_ref, kp_ref, vc_ref, vp_ref, o_ref, lse_ref, *, n, blocks):
    first_lo = jnp.where(pl.program_id(1) > 0, 0, n)
    qi = lax.broadcasted_iota(jnp.int32, (Q_REP * n, 2 * n), 0) & (n - 1)
    kj = lax.broadcasted_iota(jnp.int32, (Q_REP * n, 2 * n), 1)
    band = (kj >= qi) & (kj <= qi + n)
    lane = lax.broadcasted_iota(jnp.int32, (n, LANES), 1)
    for j in range(blocks):
        rows = slice(n * j, n * (j + 1))
        lse_tile = jnp.zeros((n, LANES), F32)
        for g in range(KV_PER_GROUP):
            hs = slice(HEAD * g, HEAD * (g + 1))
            if j == 0:
                kprev, vprev = kp_ref[:, hs], vp_ref[:, hs]
                valid = band & (kj >= first_lo)
            else:
                prev = slice(n * (j - 1), n * j)
                kprev, vprev = kc_ref[prev, hs], vc_ref[prev, hs]
                valid = band
            kwin = jnp.concatenate([kprev, kc_ref[rows, hs]], axis=0)
            vwin = jnp.concatenate([vprev, vc_ref[rows, hs]], axis=0)
            q0 = Q_REP * HEAD * g
            qg = jnp.concatenate([q_ref[rows, q0 + HEAD * e:q0 + HEAD * (e + 1)] for e in range(Q_REP)], axis=0)
            s = jnp.where(valid, _dot_nt(qg, kwin) * (HEAD ** -0.5), -jnp.inf)
            mx = jnp.max(s, axis=-1, keepdims=True)
            p = jnp.exp(s - mx)
            den = jnp.sum(p, axis=-1, keepdims=True)
            o = _dot(_bf(p), vwin) / den
            lse = mx + jnp.log(den)
            for e in range(Q_REP):
                o_ref[rows, q0 + HEAD * e:q0 + HEAD * (e + 1)] = o[n * e:n * (e + 1)]
                lse_tile = jnp.where(lane == Q_REP * g + e, lse[n * e:n * (e + 1)], lse_tile)
        lse_ref[rows, :] = lse_tile


def _attn_prompt(qd, kd, vd, gi, window):
    dil, ln, qw = qd.shape
    kw = kd.shape[2]
    n = window // dil
    blocks = min(ATTN_BLOCKS, ln // n)
    rows = n * blocks
    assert n == HEAD and ln % rows == 0
    cur = lambda r, b: (r, b, 0)
    prev = lambda r, b: (r, jnp.maximum(b * blocks - 1, 0), 0)
    return pl.pallas_call(
        functools.partial(_attn_prompt_body, n=n, blocks=blocks),
        out_shape=[jax.ShapeDtypeStruct((dil, ln, qw), F32), jax.ShapeDtypeStruct((dil, ln, LANES), F32)],
        grid=(dil, ln // rows),
        in_specs=[pl.BlockSpec((None, rows, qw), cur), pl.BlockSpec((None, rows, kw), cur),
                  pl.BlockSpec((None, n, kw), prev), pl.BlockSpec((None, rows, kw), cur),
                  pl.BlockSpec((None, n, kw), prev)],
        out_specs=[pl.BlockSpec((None, rows, qw), cur), pl.BlockSpec((None, rows, LANES), cur)],
        compiler_params=_params(2), name=f"attn_prompt{gi}")(qd, kd, kd, vd, vd)


def _attn_sample_body(q_ref, kn_ref, vn_ref, *refs, bb):
    n_groups = len(GROUPS)
    caches, outs, lses = refs[:n_groups], refs[n_groups:2 * n_groups], refs[2 * n_groups:]
    lane = lax.broadcasted_iota(jnp.int32, (1, LANES), 1)
    kw = KV_PER_GROUP * HEAD
    units = [(gi, b, g) for gi in range(n_groups) for b in range(bb) for g in range(KV_PER_GROUP)]
    scale = HEAD ** -0.5
    qm, kn, vn = [], [], []
    for gi, b, g in units:
        q0 = Q_PER_GROUP * HEAD * gi + Q_REP * HEAD * g
        qm.append(jnp.concatenate([q_ref[b:b + 1, q0 + HEAD * e:q0 + HEAD * (e + 1)] for e in range(Q_REP)]
                                  + [jnp.zeros((SUBLANES - Q_REP, HEAD), F32)], axis=0))
        kn.append(kn_ref[b:b + 1, kw * gi + HEAD * g:kw * gi + HEAD * (g + 1)])
        vn.append(vn_ref[b:b + 1, kw * gi + HEAD * g:kw * gi + HEAD * (g + 1)])
    n = range(len(units))
    s = [_dot_nt(_bf(qm[u]), _bf(caches[gi][b, :, g, :])) * scale for u, (gi, b, g) in enumerate(units)]
    sn = [jnp.sum(qm[u] * kn[u], axis=1, keepdims=True) * scale for u in n]
    mx = [jnp.maximum(jnp.max(s[u], axis=1, keepdims=True), sn[u]) for u in n]
    p = [jnp.exp(s[u] - mx[u]) for u in n]
    pn = [jnp.exp(sn[u] - mx[u]) for u in n]
    den = [jnp.sum(p[u], axis=1, keepdims=True) + pn[u] for u in n]
    pv = [_dot(_bf(p[u]), _bf(caches[gi][b, :, KV_PER_GROUP + g, :])) for u, (gi, b, g) in enumerate(units)]
    lrows = {}
    for u, (gi, b, g) in enumerate(units):
        o = (pv[u] + pn[u] * vn[u]) / den[u]
        lse = mx[u] + jnp.log(den[u])
        for e in range(Q_REP):
            oc = HEAD * (Q_REP * g + e)
            outs[gi][b:b + 1, oc:oc + HEAD] = o[e:e + 1]
            lrows[gi, b] = jnp.where(lane == Q_REP * g + e, lse[e:e + 1], lrows.get((gi, b), jnp.zeros((1, LANES), F32)))
    for (gi, b), lrow in lrows.items():
        lses[gi][b:b + 1, :] = lrow


def _attn_sample(q, k, v, caches):
    n_seq = q.shape[0]
    bb = SAMPLE_SEQS
    assert n_seq % bb == 0
    qw = Q_PER_GROUP * HEAD
    views, cache_specs = [], []
    for cache, (window, dil) in zip(caches, GROUPS):
        n = window // dil
        assert cache.shape[1] == window and n == HEAD
        views.append(cache.reshape(n_seq, n, dil, 2 * KV_PER_GROUP, HEAD))
        cache_specs.append(pl.BlockSpec((bb, n, None, 2 * KV_PER_GROUP, HEAD), lambda i: (i, 0, 0, 0, 0)))
    row = lambda c: pl.BlockSpec((bb, c), lambda i: (i, 0))
    n_groups = len(GROUPS)
    res = pl.pallas_call(
        functools.partial(_attn_sample_body, bb=bb),
        out_shape=[jax.ShapeDtypeStruct((n_seq, qw), F32)] * n_groups
        + [jax.ShapeDtypeStruct((n_seq, LANES), F32)] * n_groups,
        grid=(n_seq // bb,),
        in_specs=[row(q.shape[1]), row(k.shape[1]), row(v.shape[1])] + cache_specs,
        out_specs=[row(qw)] * n_groups + [row(LANES)] * n_groups,
        compiler_params=_params(), name="attn_sample")(q, k, v, *views)
    return res[:n_groups], res[n_groups:]


def _ffn0_kvq_body(*refs, tm, dils, n_tiles):
    mix_in, p_ref, ffn_w, tab_ref, kvq_w = refs[:5], refs[5], refs[6:13], refs[13], refs[14:20]
    out_ref, tail_ref, k_ref, v_ref = refs[20:24]
    ng = len(dils)
    qd, kd, vd = refs[24:24 + ng], refs[24 + ng:24 + 2 * ng], refs[24 + 2 * ng:24 + 3 * ng]
    ubuf, act_s, tail_s, hbuf, stage = refs[24 + 3 * ng:]
    step = pl.program_id(0)

    @pl.when(step == 0)
    def _():
        ubuf[...] = jnp.zeros(ubuf.shape, F32)
        hbuf[...] = jnp.zeros(hbuf.shape, F32)

    h_prev = hbuf[...]
    result = []
    _weave(_ffn_stages(_delta_out_mix(*mix_in), p_ref, None, *ffn_w, tail_s, ubuf, act_s, result, tm=tm),
           _kvq_stages(h_prev, tab_ref, *kvq_w, None, k_ref, v_ref, qd, kd, vd, stage, tm=tm, dils=dils))
    hbuf[...] = result[0]

    @pl.when(step < n_tiles)
    def _():
        out_ref[...] = result[0]
        tail_ref[...] = tail_s[...]


def _ffn0_kvq(o, z, x, p, table, w, tm, dils):
    m, d = x.shape
    dff = w["w_down"].shape[1]
    n_kv = w["w_kv"].shape[1] // 2
    qw, kw = Q_PER_GROUP * HEAD, KV_PER_GROUP * HEAD
    assert m % tm == 0
    n_tiles = m // tm
    cur = lambda s: (jnp.minimum(s, n_tiles - 1), 0)
    prev = lambda s: (jnp.maximum(s - 1, 0), 0)
    at = lambda a, imap: (a, pl.BlockSpec((tm, a.shape[1]), imap))
    ins = [at(o, cur), at(z, cur), at(x, cur), _whole(w["g_aout"]), _whole(w["w_aout"]),
           (p, pl.BlockSpec((None, tm, p.shape[2]), lambda s: (0, jnp.minimum(s, n_tiles - 1), 0)))]
    ins += [_layer(w[name], 0) for name in ("g_ffn", "w_up", "w_fconv", "w_down", "g_ple", "w_gate", "w_proj")]
    ins += [at(table, prev)] + [_whole(w[name]) for name in ("g_kv", "g_mix1", "w_kv", "w_q", "g_k", "g_q")]
    nat = lambda c, imap: (jax.ShapeDtypeStruct((m, c), F32), pl.BlockSpec((tm, c), imap))
    outs = [nat(d, cur), _out_whole((SUBLANES, 2 * dff)), nat(n_kv, prev), nat(n_kv, prev)]
    for width in (qw, kw, kw):
        for dil in dils:
            assert tm % (dil * 2 * SUBLANES) == 0
            outs.append((jax.ShapeDtypeStruct((dil, m // dil, width), BF16),
                         pl.BlockSpec((dil, tm // dil, width), lambda s: (0, jnp.maximum(s - 1, 0), 0))))
    scratch = [pltpu.VMEM((SUBLANES, 2 * dff), F32), pltpu.VMEM((tm, dff), BF16), pltpu.VMEM((SUBLANES, 2 * dff), F32),
               pltpu.VMEM((tm, d), F32), pltpu.VMEM((MXU_COLS // LANES, tm, LANES), F32)]
    return _call(functools.partial(_ffn0_kvq_body, tm=tm, dils=dils, n_tiles=n_tiles), "ffn0_kvq", n_tiles + 1,
                 ins, outs, scratch)


def _trunk(x, p, first_pos, state, w, tm, tm_wide):
    m = x.shape[0]
    per_row = state is not None
    n_heads = w["n_heads"]
    aw = w["w_z"].shape[1]

    if per_row:
        consts = [w["g_mix0"], w["w_qkv"], w["w_ab"], w["w_z"], w["w_aconv"], w["a_log"], w["dt_bias"],
                  jnp.swapaxes(state["qkv_conv"], 0, 1)]
        q, k, v, z, gb, qkv_tail = _call(
            functools.partial(_a_in_step_body, n_heads=n_heads), "a_in_step", m // tm,
            [_rows(x, tm)] + [_whole(c) for c in consts],
            [_out_rows(m, aw, tm)] * 4 + [_out_rows(m, LANES, tm), _out_whole((m, 3 * aw))])
        o, s1 = _delta_step(q, k, v, gb, state["delta"], n_heads)
    else:
        o, z, s1, qkv_tail = _mixer_a_prompt(x, w, n_heads)
    table = _rope_table(first_pos, m, consecutive=not per_row)
    if per_row:
        mix = [_rows(o, tm_wide), _rows(z, tm_wide), _rows(x, tm_wide), _whole(w["g_aout"]), _whole(w["w_aout"])]
        h, ffn_tail0 = _mix_ffn(mix, [], None, p, w, 0, m, tm_wide, state["ffn_conv"][0])
        qa, ka, va = _kvq(h, table, w, tm, None)
        outs, lses = _attn_sample(qa, ka, va, state["caches"])
        outs, lses = [a[None] for a in outs], [a[None] for a in lses]
        dils = (1,) * len(GROUPS)
    else:
        dils = tuple(dil for _, dil in GROUPS)
        ng = len(dils)
        res = _ffn0_kvq(o, z, x, p, table, w, tm, dils)
        h, ffn_tail0, ka, va = res[:4]
        qd, kd, vd = res[4:4 + ng], res[4 + ng:4 + 2 * ng], res[4 + 2 * ng:]
        att = [_attn_prompt(qd[gi], kd[gi], vd[gi], gi, window) for gi, (window, _) in enumerate(GROUPS)]
        outs, lses = [a[0] for a in att], [a[1] for a in att]
    mix = [_grouped(a, tm_wide) for a in (*outs, *lses)] + [_rows(h, tm_wide), _whole(w["w_o"])]
    mix_scratch = []
    for dil in dils:
        if dil > 1:
            assert tm_wide % (dil * SUBLANES) == 0
            mix_scratch += [pltpu.VMEM((Q_PER_GROUP, tm_wide, HEAD), F32), pltpu.VMEM((tm_wide, LANES), F32)]
    h, ffn_tail1 = _mix_ffn(mix, mix_scratch, dils, p, w, 1, m, tm_wide, state["ffn_conv"][1] if per_row else None)
    return h, s1, qkv_tail, (ffn_tail0, ffn_tail1), ka, va


def _kv_rows(k, v, gi, n_rows):
    kw = KV_PER_GROUP * HEAD
    sl = slice(kw * gi, kw * (gi + 1))
    shape = (n_rows, 1, KV_PER_GROUP, HEAD)
    return jnp.concatenate([k[k.shape[0] - n_rows:, sl].reshape(shape), v[v.shape[0] - n_rows:, sl].reshape(shape)],
                           axis=1)


def kernel(x_prompt, x_sample, p_prompt, p_sample, state_delta, state_qkv_conv, state_ffn_conv, cache_kv_w128, cache_kv_w512, cache_kv_w2048, g_mix_norm, g_ffn_norm, w_ffn_up, w_ffn_conv, w_ffn_down, g_ple_norm, w_ple_gate, w_ple_proj, w_a_in, w_a_conv, a_log, a_dt_bias, g_a_out_norm, w_a_out, g_kv_norm, w_kv, g_k_norm, w_q, g_q_norm, w_o):
    n_heads = a_log.shape[1]
    aw = n_heads * HEAD
    assert w_a_in.shape[0] == 1 and w_q.shape[0] == 1 and w_a_in.shape[2] == 4 * aw + 2 * n_heads
    row = lambda a: a.reshape(1, -1)
    lane_pad = lambda a: jnp.pad(a, ((0, 0), (0, LANES - a.shape[1])))
    w = {
        "n_heads": n_heads,
        "g_mix0": row(g_mix_norm[0]), "g_mix1": row(g_mix_norm[1]), "g_ffn": g_ffn_norm[:, None], "g_ple": g_ple_norm[:, None],
        "w_up": _bf(w_ffn_up), "w_fconv": w_ffn_conv, "w_down": _bf(w_ffn_down),
        "w_gate": _bf(w_ple_gate), "w_proj": _bf(w_ple_proj),
        "w_qkv": _bf(w_a_in[0][:, :3 * aw]), "w_ab": _bf(lane_pad(w_a_in[0][:, 3 * aw:3 * aw + 2 * n_heads])),
        "w_z": _bf(w_a_in[0][:, 3 * aw + 2 * n_heads:]), "w_aconv": w_a_conv[0],
        "a_log": lane_pad(a_log), "dt_bias": lane_pad(a_dt_bias),
        "g_aout": row(g_a_out_norm[0]), "w_aout": _bf(w_a_out[0]),
        "g_kv": row(g_kv_norm), "w_kv": _bf(w_kv), "g_k": row(g_k_norm),
        "w_q": _bf(w_q[0]), "g_q": row(g_q_norm[0]), "w_o": _bf(w_o[0]),
    }
    assert x_prompt.shape[0] == 1 and x_sample.shape[1] == 1
    seq, d = x_prompt.shape[1], x_prompt.shape[2]
    n_seq = x_sample.shape[0]

    yp, sd_p, qkv_tail_p, ffn_tails_p, kp, vp = _trunk(
        x_prompt.reshape(seq, d), p_prompt.reshape(p_prompt.shape[0], seq, -1), 0, None, w,
        tm=ROW_TILE, tm_wide=WIDE_ROW_TILE)
    a_keep, f_keep = state_qkv_conv.shape[2], state_ffn_conv.shape[2]
    sq_p = qkv_tail_p[SUBLANES - a_keep:][None, None]
    sf_p = jnp.stack([t[SUBLANES - f_keep:] for t in ffn_tails_p])[:, None]
    kv_p = [_kv_rows(kp, vp, gi, min(window, seq))[None] for gi, (window, _) in enumerate(GROUPS)]

    state = {"delta": state_delta[0], "qkv_conv": state_qkv_conv[0], "ffn_conv": state_ffn_conv,
             "caches": (cache_kv_w128, cache_kv_w512, cache_kv_w2048)}
    ys, sd_s, qkv_tail_s, ffn_tails_s, ks, vs = _trunk(
        x_sample.reshape(n_seq, d), p_sample.reshape(p_sample.shape[0], n_seq, -1), PAST_LEN,
        state, w, tm=n_seq, tm_wide=n_seq)
    sq_s = jnp.concatenate([state_qkv_conv[0][:, 1:], qkv_tail_s[:, None]], axis=1)[None]
    sf_s = jnp.stack([jnp.concatenate([state_ffn_conv[i][:, 1:], ffn_tails_s[i][:, None]], axis=1)
                      for i in range(2)])
    kv_s = [_kv_rows(ks, vs, gi, n_seq).reshape(n_seq, 1, 2, KV_PER_GROUP, HEAD) for gi in range(len(GROUPS))]

    return (yp[None], ys[:, None], sd_p[None, None], sq_p, sf_p, kv_p[0], kv_p[1], kv_p[2],
            sd_s[None], sq_s, sf_s, kv_s[0], kv_s[1], kv_s[2])
```

```python
import functools

import jax
import jax.numpy as jnp
from jax import lax
from jax.experimental import pallas as pl
from jax.experimental.pallas import tpu as pltpu

F32 = jnp.float32
BF16 = jnp.bfloat16

EPS = 1e-6
HEAD = 128
CHUNK = 64
GROUPS = ((128, 1), (512, 4), (2048, 16))
KV_PER_GROUP = 2
Q_REP = 2
Q_PER_GROUP = KV_PER_GROUP * Q_REP
ROPE_DIM = HEAD // 4
ROPE_THETA = 500000.0
PAST_LEN = 16384

LANES = 128
SUBLANES = 8
MXU_COLS = 256
VMEM_LIMIT_BYTES = 56 * 1024 * 1024

ROW_TILE = 256
WIDE_ROW_TILE = 512
PREP_CHUNKS = 4
ATTN_BLOCKS = 8
SAMPLE_SEQS = SUBLANES


def _dot(a, b):
    return jnp.dot(a, b, preferred_element_type=F32)


def _dot_nt(a, b):
    return lax.dot_general(a, b, (((1,), (1,)), ((), ())), preferred_element_type=F32)


def _bf(x):
    return x.astype(BF16)


def _rms(x):
    return x * lax.rsqrt(jnp.mean(x * x, axis=-1, keepdims=True) + EPS)


def _sigmoid(x):
    return 1.0 / (1.0 + jnp.exp(-x))


def _silu(x):
    return x * _sigmoid(x)


def _split3f(x):
    hi = _bf(x).astype(F32)
    r = x - hi
    mid = _bf(r).astype(F32)
    return hi, mid, _bf(r - mid).astype(F32)


def _causal_conv_tile(x, w, carry_ref, cols):
    rows = x.shape[0]
    first = lax.broadcasted_iota(jnp.int32, (SUBLANES, x.shape[1]), 0) == 0
    z = x * w[0:1]
    for j in range(1, w.shape[0]):
        rolled = pltpu.roll(z, 1, 0)
        head = jnp.where(first, carry_ref[j - 1:j, cols], rolled[0:SUBLANES])
        carry_ref[j - 1:j, cols] = z[rows - 1:, :]
        z = jnp.concatenate([head, rolled[SUBLANES:]], axis=0) + x * w[j:j + 1]
    return z


def _const_spec(a):
    nd = a.ndim
    return pl.BlockSpec(a.shape, lambda *_: (0,) * nd, pipeline_mode=pl.Buffered(1))


def _params(n_axes=1):
    return pltpu.CompilerParams(dimension_semantics=("arbitrary",) * n_axes,
                                vmem_limit_bytes=VMEM_LIMIT_BYTES)


def _call(body, name, steps, ins, outs, scratch=()):
    return pl.pallas_call(body, out_shape=[o for o, _ in outs], grid=(steps,), in_specs=[sp for _, sp in ins],
                          out_specs=[sp for _, sp in outs], scratch_shapes=list(scratch),
                          compiler_params=_params(), name=name)(*[a for a, _ in ins])


def _rows(a, tm):
    return a, pl.BlockSpec((tm, a.shape[1]), lambda i: (i, 0))


def _whole(a):
    return a, _const_spec(a)


def _layer(a, layer):
    nd = a.ndim
    return a, pl.BlockSpec((None,) + a.shape[1:], lambda i: (layer,) + (0,) * (nd - 1), pipeline_mode=pl.Buffered(1))


def _layer_rows(a, layer, tm):
    return a, pl.BlockSpec((None, tm, a.shape[2]), lambda i: (layer, i, 0))


def _grouped(a, tm):
    return a, pl.BlockSpec((a.shape[0], tm // a.shape[0], a.shape[2]), lambda i: (0, i, 0))


def _out_rows(m, c, tm):
    return jax.ShapeDtypeStruct((m, c), F32), pl.BlockSpec((tm, c), lambda i: (i, 0))


def _out_whole(shape):
    return jax.ShapeDtypeStruct(shape, F32), pl.BlockSpec(shape, lambda i: (0,) * len(shape))


def _qkv_activation(y, part):
    a = _silu(y)
    if part == 2:
        return a
    halves = []
    for hh in range(MXU_COLS // HEAD):
        ah = a[:, HEAD * hh:HEAD * (hh + 1)]
        nrm = ah * lax.rsqrt(jnp.sum(ah * ah, axis=-1, keepdims=True) + EPS)
        halves.append(nrm * (HEAD ** -0.5) if part == 0 else nrm)
    return jnp.concatenate(halves, axis=1)


def _decay_and_beta(pab, alog_ref, dtb_ref, n_heads):
    lane = lax.broadcasted_iota(jnp.int32, pab.shape, 1)
    ap = pab + dtb_ref[...]
    softplus = jnp.maximum(ap, 0.0) + jnp.log1p(jnp.exp(-jnp.abs(ap)))
    return jnp.where(lane < n_heads, -jnp.exp(alog_ref[...]) * softplus, _sigmoid(pab))


def _a_in_step_body(x_ref, g_ref, wqkv_ref, wab_ref, wz_ref, wconv_ref, alog_ref, dtb_ref, buf_ref,
                    q_ref, k_ref, v_ref, z_ref, gb_ref, tail_ref, *, n_heads):
    hn = _bf(_rms(x_ref[...]) * g_ref[...])
    outs = (q_ref, k_ref, v_ref)
    per_part = q_ref.shape[1] // MXU_COLS
    for c in range(3 * per_part):
        cols = slice(MXU_COLS * c, MXU_COLS * (c + 1))
        pq = _dot(hn, wqkv_ref[:, cols])
        w = wconv_ref[:, cols]
        y = buf_ref[0, :, cols] * w[0:1]
        y = y + buf_ref[1, :, cols] * w[1:2]
        y = y + buf_ref[2, :, cols] * w[2:3]
        y = y + pq * w[3:4]
        tail_ref[:, cols] = pq
        part, sub = divmod(c, per_part)
        outs[part][:, MXU_COLS * sub:MXU_COLS * (sub + 1)] = _qkv_activation(y, part)
    gb_ref[...] = _decay_and_beta(_dot(hn, wab_ref[...]), alog_ref, dtb_ref, n_heads)
    z_ref[...] = _dot(hn, wz_ref[...])


def _weave(*stages):
    live = list(stages)
    while live:
        for g in list(live):
            try:
                next(g)
            except StopIteration:
                live.remove(g)


def _mixer_a_prompt_body(x_ref, g_ref, wqkv_ref, wab_ref, wz_ref, wconv_ref, alog_ref, dtb_ref,
                         z_ref, tail_ref, o_ref, sout_ref, cbuf, s_scr, *bufs, n_heads, n_chunks):
    step = pl.program_id(0)
    n_proj = 4
    proj_sets = bufs[:n_proj], bufs[n_proj:2 * n_proj]
    rest = bufs[2 * n_proj:]
    prep_sets = rest[:len(rest) // 2], rest[len(rest) // 2:]

    @pl.when(step == 0)
    def _():
        cbuf[...] = jnp.zeros(cbuf.shape, F32)
        s_scr[...] = jnp.zeros(s_scr.shape, F32)
        for ref in (*proj_sets[1], *prep_sets[0]):
            ref[...] = jnp.zeros(ref.shape, ref.dtype)

    sub = n_chunks // 2 * CHUNK

    def project(i, q_s, k_s, v_s, gb_s):
        rows = pl.ds(pl.multiple_of(i * sub, sub), sub)
        hn = _bf(_rms(x_ref[rows, :]) * g_ref[...])
        outs = (q_s, k_s, v_s)
        per_part = q_s.shape[1] // MXU_COLS
        for c in range(3 * per_part):
            cols = slice(MXU_COLS * c, MXU_COLS * (c + 1))
            pq = _dot(hn, wqkv_ref[:, cols])
            y = _causal_conv_tile(pq, wconv_ref[:, cols], cbuf, cols)
            tail_ref[:, cols] = pq[sub - SUBLANES:, :]
            part, sc = divmod(c, per_part)
            outs[part][rows, MXU_COLS * sc:MXU_COLS * (sc + 1)] = _qkv_activation(y, part)
            yield
        gb_s[rows, :] = _decay_and_beta(_dot(hn, wab_ref[...]), alog_ref, dtb_ref, n_heads)
        z_ref[rows, :] = _dot(hn, wz_ref[...])
        yield

    group = MXU_COLS // CHUNK
    shift = CHUNK.bit_length() - 1
    lane_p = lax.broadcasted_iota(jnp.int32, (CHUNK, MXU_COLS), 1)
    row_p = lax.broadcasted_iota(jnp.int32, (CHUNK, MXU_COLS), 0)
    col_p = lane_p & (CHUNK - 1)
    blk = [(lane_p >> shift) == u for u in range(group)]
    incl_p = row_p >= col_p
    strict_p = row_p > col_p
    eye_p = jnp.where(row_p == col_p, 1.0, 0.0).astype(F32)
    bd_r = lax.broadcasted_iota(jnp.int32, (MXU_COLS, MXU_COLS), 0)
    bd_c = lax.broadcasted_iota(jnp.int32, (MXU_COLS, MXU_COLS), 1)
    bd_mask = (bd_r >> shift) == (bd_c >> shift)
    ri = lax.broadcasted_iota(jnp.int32, (CHUNK, CHUNK), 0)
    ci = lax.broadcasted_iota(jnp.int32, (CHUNK, CHUNK), 1)
    ltri3 = jnp.concatenate([_bf(jnp.where(ri >= ci, 1.0, 0.0))] * 3, axis=1)
    g_r = lax.broadcasted_iota(jnp.int32, (MXU_COLS, 3 * LANES), 0) >> shift
    g_c = lax.broadcasted_iota(jnp.int32, (MXU_COLS, 3 * LANES), 1) & (LANES - 1)
    ones3 = jnp.ones((CHUNK, 3 * LANES), BF16)
    er = lax.broadcasted_iota(jnp.int32, (HEAD, HEAD), 0)
    ec = lax.broadcasted_iota(jnp.int32, (HEAD, HEAD), 1)
    eye_h = _bf(jnp.where(er == ec, 1.0, 0.0))
    heads = range(n_heads)
    hs = [slice(HEAD * h, HEAD * (h + 1)) for h in heads]

    def block_diag(x):
        return _bf(jnp.where(bd_mask, jnp.concatenate([x] * group, axis=0), 0.0))

    def pack_cols(cols):
        out = cols[-1]
        for u in reversed(range(group - 1)):
            out = jnp.where(blk[u], cols[u], out)
        return out

    wide = group * HEAD
    kd_r = lax.broadcasted_iota(jnp.int32, (MXU_COLS, wide), 0) >> shift
    kd_c = lax.broadcasted_iota(jnp.int32, (MXU_COLS, wide), 1) >> (HEAD.bit_length() - 1)
    kd_mask = kd_r == kd_c

    def key_blocks(kc):
        return _bf(jnp.where(kd_mask, jnp.concatenate([kc] * group, axis=0), 0.0))

    def stack(ref, rows, h0):
        return jnp.concatenate([ref[rows, hs[h]] for h in range(h0, h0 + group)], axis=0)

    def stack_cols(x, lo):
        return jnp.concatenate([x[:, lo + u:lo + u + 1] for u in range(group)], axis=0)

    def prep(i, q_ref, k_ref, v_ref, gb_ref, u0_s, wkqg_s, qk_s, kdt_s, gl_s):
        chunks = [i * PREP_CHUNKS + j for j in range(PREP_CHUNKS)]
        rows = [pl.ds(pl.multiple_of(c * CHUNK, CHUNK), CHUNK) for c in chunks]
        gbc = [gb_ref[r, :] for r in rows]
        gc = [_dot(ltri3, _bf(jnp.concatenate(_split3f(x), axis=0))) for x in gbc]
        glast = [x[CHUNK - 1:CHUNK, :] for x in gc]
        for j, c in enumerate(chunks):
            gl_s[c] = jnp.broadcast_to(glast[j], (SUBLANES, LANES))
        groups = [(j, h0) for j in range(PREP_CHUNKS) for h0 in range(0, n_heads, group)]
        ng = range(len(groups))
        kc = [k_ref[rows[j], HEAD * h0:HEAD * (h0 + group)] for j, h0 in groups]
        qc = [q_ref[rows[j], HEAD * h0:HEAD * (h0 + group)] for j, h0 in groups]
        pairs = [_dot_nt(_bf(jnp.concatenate([qc[g], kc[g]], axis=0)), key_blocks(kc[g])) for g in ng]
        yield
        gcol_p = [pack_cols([gc[j][:, h:h + 1] for h in range(h0, h0 + group)]) for j, h0 in groups]
        bcol_p = [pack_cols([gbc[j][:, n_heads + h:n_heads + h + 1] for h in range(h0, h0 + group)])
                  for j, h0 in groups]
        grow = [_dot_nt(ones3, _bf(jnp.where(g_c == g_r + h0,
                                               jnp.concatenate([jnp.concatenate([x] * group, axis=0)
                                                                for x in _split3f(gc[j])], axis=1), 0.0)))
                for j, h0 in groups]
        yield
        dec = [jnp.exp(jnp.where(incl_p, gcol_p[g] - grow[g], -jnp.inf)) for g in ng]
        a = [jnp.where(strict_p, bcol_p[g] * pairs[g][CHUNK:] * dec[g], 0.0) for g in ng]
        for g, (j, h0) in enumerate(groups):
            qk = _bf(pairs[g][:CHUNK] * dec[g])
            for u in range(group):
                qk_s[chunks[j], h0 + u] = qk[:, CHUNK * u:CHUNK * (u + 1)]
        tinv = [eye_p - x for x in a]
        pw = [_dot(_bf(-x), block_diag(-x)) for x in a]
        yield
        for _ in range(4):
            res = [_dot(jnp.concatenate([_bf(tinv[g]), _bf(pw[g])], axis=0), block_diag(pw[g])) for g in ng]
            tinv = [tinv[g] + res[g][:CHUNK] for g in ng]
            pw = [r[CHUNK:] for r in res]
            yield
        tinv = [tinv[g] + _dot(_bf(tinv[g]), block_diag(pw[g])) for g in ng]
        yield
        tbd = [block_diag(x) for x in tinv]
        kst = [stack(k_ref, rows[j], h0) for j, h0 in groups]
        gcol_s = [stack_cols(gc[j], h0) for j, h0 in groups]
        bcol_s = [stack_cols(gbc[j], n_heads + h0) for j, h0 in groups]
        gam_s = [jnp.exp(x) for x in gcol_s]
        rhs = [jnp.concatenate([stack(v_ref, rows[j], h0) * bcol_s[g], kst[g] * (bcol_s[g] * gam_s[g])], axis=1)
               for g, (j, h0) in enumerate(groups)]
        sol = [_dot(tbd[g], _bf(rhs[g])) for g in ng]
        yield
        kdt = [_dot_nt(eye_h, _bf(kst[g] * jnp.exp(stack_cols(jnp.broadcast_to(glast[j], (CHUNK, LANES)), h0)
                                                   - gcol_s[g])))
               for g, (j, h0) in enumerate(groups)]
        for g, (j, h0) in enumerate(groups):
            c = chunks[j]
            qg = stack(q_ref, rows[j], h0) * gam_s[g]
            for u in range(group):
                r = slice(CHUNK * u, CHUNK * (u + 1))
                u0_s[c, h0 + u] = sol[g][r, :HEAD]
                wkqg_s[c, h0 + u] = _bf(jnp.concatenate([sol[g][r, HEAD:], qg[r]], axis=0))
                kdt_s[c, h0 + u] = _bf(kdt[g][:, r])

    def scan(i, u0_s, wkqg_s, qk_s, kdt_s, gl_s):
        for j in range(PREP_CHUNKS):
            c = i * PREP_CHUNKS + j
            rows = pl.ds(pl.multiple_of(c * CHUNK, CHUNK), CHUNK)
            glr = gl_s[c]
            ws = [_dot(wkqg_s[c, h], _bf(s_scr[h])) for h in heads]
            yield
            ub = [_bf(u0_s[c, h] - ws[h][:CHUNK]) for h in heads]
            for h in heads:
                o_ref[rows, HEAD * h:HEAD * (h + 1)] = ws[h][CHUNK:] + _dot(qk_s[c, h], ub[h])
            yield
            for h in heads:
                s_scr[h] = jnp.exp(glr[0:1, h:h + 1]) * s_scr[h] + _dot(kdt_s[c, h], ub[h])
            yield

    def run(mine, other):
        def body(i, carry):
            _weave(project(i, *proj_sets[mine]), prep(i, *proj_sets[other], *prep_sets[other]),
                   scan(i, *prep_sets[mine]))
            return carry
        lax.fori_loop(0, n_chunks // PREP_CHUNKS, body, 0)

    @pl.when(step % 2 == 0)
    def _():
        run(0, 1)

    @pl.when(step % 2 == 1)
    def _():
        run(1, 0)

    sout_ref[...] = s_scr[...]


def _mixer_a_prompt(x, w, n_heads):
    t, d = x.shape
    aw = w["w_z"].shape[1]
    n_chunks = 2 * PREP_CHUNKS
    tt = n_chunks * CHUNK
    assert t % tt == 0 and n_chunks % PREP_CHUNKS == 0
    n_tiles = t // tt
    projected = [pltpu.VMEM((tt, aw), F32)] * 3 + [pltpu.VMEM((tt, LANES), F32)]
    prepared = [pltpu.VMEM((n_chunks, n_heads, CHUNK, HEAD), F32),
                pltpu.VMEM((n_chunks, n_heads, 2 * CHUNK, HEAD), BF16),
                pltpu.VMEM((n_chunks, n_heads, CHUNK, CHUNK), BF16),
                pltpu.VMEM((n_chunks, n_heads, HEAD, CHUNK), BF16),
                pltpu.VMEM((n_chunks, SUBLANES, LANES), F32)]
    scratch = [pltpu.VMEM((SUBLANES, 3 * aw), F32), pltpu.VMEM((n_heads, HEAD, HEAD), F32)]
    scratch += projected + projected + prepared + prepared
    first = lambda s: (jnp.minimum(s, n_tiles - 1), 0)
    last = lambda s: (jnp.maximum(s - 2, 0), 0)
    ins = [(x, pl.BlockSpec((tt, d), first))] + [_whole(w[name]) for name in
                                                  ("g_mix0", "w_qkv", "w_ab", "w_z", "w_aconv", "a_log", "dt_bias")]
    outs = [(jax.ShapeDtypeStruct((t, aw), F32), pl.BlockSpec((tt, aw), first)), _out_whole((SUBLANES, 3 * aw)),
            (jax.ShapeDtypeStruct((t, aw), F32), pl.BlockSpec((tt, aw), last)), _out_whole((n_heads, HEAD, HEAD))]
    z, tail, o, s1 = _call(functools.partial(_mixer_a_prompt_body, n_heads=n_heads, n_chunks=n_chunks),
                           "mixer_a_prompt", n_tiles + 2, ins, outs, scratch)
    return o, z, s1, tail


def _delta_step_body(qt_ref, kt_ref, v_ref, gbt_ref, s_ref, o_ref, so_ref, *, n_seq, n_heads):
    h = pl.program_id(0)
    decay = jnp.exp(gbt_ref[pl.ds(h, 1), :])
    beta = gbt_ref[pl.ds(n_heads + h, 1), :]
    for b in range(n_seq):
        s = decay[:, b:b + 1] * s_ref[b]
        kc = kt_ref[:, b:b + 1]
        u = beta[:, b:b + 1] * (v_ref[b:b + 1, :] - jnp.sum(kc * s, axis=0, keepdims=True))
        s = s + kc * u
        so_ref[b] = s
        o_ref[b:b + 1, :] = jnp.sum(qt_ref[:, b:b + 1] * s, axis=0, keepdims=True)


def _delta_step(q, k, v, gb, state, n_heads):
    n_seq = q.shape[0]
    col = pl.BlockSpec((HEAD, n_seq), lambda h: (h, 0))
    row = pl.BlockSpec((n_seq, HEAD), lambda h: (0, h))
    st = pl.BlockSpec((n_seq, None, HEAD, HEAD), lambda h: (0, h, 0, 0))
    gbt = gb.T
    return pl.pallas_call(
        functools.partial(_delta_step_body, n_seq=n_seq, n_heads=n_heads),
        out_shape=[jax.ShapeDtypeStruct(q.shape, F32), jax.ShapeDtypeStruct(state.shape, F32)],
        grid=(n_heads,), in_specs=[col, col, row, _const_spec(gbt), st], out_specs=[row, st],
        compiler_params=_params(), name="delta_step")(q.T, k.T, v, gbt, state)


def _delta_out_mix(o_ref, z_ref, res_ref, gout_ref, w_ref):
    parts = []
    for h in range(o_ref.shape[1] // HEAD):
        hs = slice(HEAD * h, HEAD * (h + 1))
        parts.append(_bf(_rms(o_ref[:, hs]) * gout_ref[...] * _silu(z_ref[:, hs])))
    return res_ref[...] + _dot(jnp.concatenate(parts, axis=1), w_ref[...])


def _attn_out_mix(refs, scratch, tm, dils):
    ng = len(dils)
    o_refs, l_refs = refs[:ng], refs[ng:2 * ng]
    res_ref, w_ref = refs[2 * ng:]
    get_o, lses = [], []
    for gi, dil in enumerate(dils):
        if dil == 1:
            get_o.append(lambda hh, r=o_refs[gi]: r[0, :, HEAD * hh:HEAD * (hh + 1)])
            lses.append(l_refs[gi][0])
        else:
            o_s, l_s = scratch.pop(0), scratch.pop(0)
            for r in range(dil):
                dst = pl.ds(r, tm // dil, stride=dil)
                for hh in range(Q_PER_GROUP):
                    o_s.at[hh][dst, :] = o_refs[gi][r, :, HEAD * hh:HEAD * (hh + 1)]
                l_s[dst, :] = l_refs[gi][r]
            get_o.append(lambda hh, r=o_s: r[hh])
            lses.append(l_s[...])
    parts = []
    for hh in range(Q_PER_GROUP):
        lh = [l[:, hh:hh + 1] for l in lses]
        mx = functools.reduce(jnp.maximum, lh)
        ex = [jnp.exp(x - mx) for x in lh]
        den = functools.reduce(lambda a, b: a + b, ex)
        acc = (ex[0] / den) * get_o[0](hh)
        for gi in range(1, ng):
            acc = acc + (ex[gi] / den) * get_o[gi](hh)
        parts.append(_bf(acc))
    return res_ref[...] + _dot(jnp.concatenate(parts, axis=1), w_ref[...])


def _ffn_stages(h, p_ref, buf_ref, gffn_ref, wup_ref, wconv_ref, wdown_ref, gple_ref, wgate_ref, wproj_ref,
                tail_ref, ubuf, act_s, result, *, tm):
    hn = _bf(_rms(h) * gffn_ref[...])
    dff = wdown_ref.shape[0]
    for c in range(dff // MXU_COLS):
        ys = []
        for half in range(2):
            lo = half * dff + MXU_COLS * c
            cols = slice(lo, lo + MXU_COLS)
            u = _dot(hn, wup_ref[:, cols])
            w = wconv_ref[:, cols]
            if buf_ref is not None:
                y = buf_ref[:, cols] * w[0:1]
                y = y + buf_ref[:, 2 * dff + lo:2 * dff + lo + MXU_COLS] * w[1:2]
                y = y + u * w[2:3]
                tail_ref[:, cols] = u
            else:
                y = _causal_conv_tile(u, w, ubuf, cols)
                tail_ref[:, cols] = u[tm - SUBLANES:, :]
            ys.append(y)
        act_s[:, MXU_COLS * c:MXU_COLS * (c + 1)] = _bf(_silu(ys[0]) * ys[1])
        yield
    h2 = h + _dot(act_s[...], wdown_ref[...])
    yield
    gate = _sigmoid(_dot(_bf(_rms(h2) * gple_ref[...]), wgate_ref[...]))
    result.append(h2 + gate * _dot(_bf(p_ref[...]), wproj_ref[...]))
    yield


def _mix_ffn_body(*refs, tm, per_row_state, dils):
    n_mix = 5 if dils is None else 2 * len(dils) + 2
    mix_in, refs = refs[:n_mix], refs[n_mix:]
    p_ref, refs = refs[0], refs[1:]
    buf_ref = None
    if per_row_state:
        buf_ref, refs = refs[0], refs[1:]
    ffn_w, (out_ref, tail_ref), scratch = refs[:7], refs[7:9], list(refs[9:])
    act_s = scratch.pop()
    ubuf = None
    if not per_row_state:
        ubuf = scratch.pop()

        @pl.when(pl.program_id(0) == 0)
        def _():
            ubuf[...] = jnp.zeros(ubuf.shape, F32)

    h = _delta_out_mix(*mix_in) if dils is None else _attn_out_mix(mix_in, scratch, tm, dils)
    result = []
    for _ in _ffn_stages(h, p_ref, buf_ref, *ffn_w, tail_ref, ubuf, act_s, result, tm=tm):
        pass
    out_ref[...] = result[0]


def _mix_ffn(mix_ins, mix_scratch, dils, p, w, layer, m, tm, state):
    per_row = state is not None
    d, dff = w["w_down"].shape[2], w["w_down"].shape[1]
    ins = list(mix_ins) + [_layer_rows(p, layer, tm)] + ([_rows(state.reshape(m, -1), tm)] if per_row else [])
    ins += [_layer(w[name], layer) for name in ("g_ffn", "w_up", "w_fconv", "w_down", "g_ple", "w_gate", "w_proj")]
    outs = [_out_rows(m, d, tm), _out_whole((m if per_row else SUBLANES, 2 * dff))]
    scratch = list(mix_scratch) + ([] if per_row else [pltpu.VMEM((SUBLANES, 2 * dff), F32)])
    scratch.append(pltpu.VMEM((tm, dff), BF16))
    return _call(functools.partial(_mix_ffn_body, tm=tm, per_row_state=per_row, dils=dils), f"mix_ffn{layer}",
                 m // tm, ins, outs, scratch)


def _kvq_stages(h, tab_ref, gkv_ref, gmix_ref, wkv_ref, wq_ref, gk_ref, gq_ref, q_ref, k_ref, v_ref, qd, kd, vd,
                stage, *, tm, dils):
    base = _rms(h)
    hkv = _bf(base * gkv_ref[...])
    hq = _bf(base * gmix_ref[...])
    tab = tab_ref[...]
    lane = lax.broadcasted_iota(jnp.int32, tab.shape, 1)
    half = ROPE_DIM // 2
    cos = jnp.where(lane < half, tab, jnp.where(lane < ROPE_DIM, pltpu.roll(tab, half, 1), 1.0))
    sa = jnp.where((lane >= half) & (lane < ROPE_DIM), tab, 0.0)
    sb = jnp.where(lane < half, -pltpu.roll(tab, HEAD - half, 1), 0.0)
    kw = KV_PER_GROUP * HEAD

    def norm_rope(x, g):
        y = _rms(x) * g
        return y * cos + pltpu.roll(y, ROPE_DIM // 2, 1) * sa + pltpu.roll(y, HEAD - ROPE_DIM // 2, 1) * sb

    def heads(x, g):
        return jnp.concatenate([norm_rope(x[:, HEAD * j:HEAD * (j + 1)], g) for j in range(MXU_COLS // HEAD)], axis=1)

    def by_class(x, dst_ref, dil, cols):
        if dil == 1:
            dst_ref[0, :, cols] = _bf(x)
        else:
            for j in range(MXU_COLS // LANES):
                stage[j] = x[:, LANES * j:LANES * (j + 1)]
            for r in range(dil):
                for j in range(MXU_COLS // LANES):
                    lo = cols.start + LANES * j
                    dst_ref[r, :, lo:lo + LANES] = _bf(stage.at[j][pl.ds(r, tm // dil, stride=dil), :])

    assert kw == MXU_COLS and Q_PER_GROUP * HEAD == 2 * MXU_COLS
    n_kv = k_ref.shape[1]

    def project(kind, c):
        lo = MXU_COLS * c + (n_kv if kind == "v" else 0)
        return _dot(hq if kind == "q" else hkv, (wq_ref if kind == "q" else wkv_ref)[:, lo:lo + MXU_COLS])

    def finish(kind, c, x):
        cols = slice(MXU_COLS * c, MXU_COLS * (c + 1))
        if kind == "k":
            x = heads(x, gk_ref[...])
            k_ref[:, cols] = x
        elif kind == "v":
            v_ref[:, cols] = x
        else:
            x = heads(x, gq_ref[...])
            if dils is None:
                q_ref[:, cols] = x
        if dils is not None:
            if kind == "q":
                by_class(x, qd[c // 2], dils[c // 2], slice(MXU_COLS * (c % 2), MXU_COLS * (c % 2 + 1)))
            else:
                by_class(x, (kd if kind == "k" else vd)[c], dils[c], slice(0, kw))

    yield
    for kind, n in (("k", n_kv), ("v", n_kv), ("q", wq_ref.shape[1])):
        for c in range(n // MXU_COLS):
            finish(kind, c, project(kind, c))
            yield


def _kvq_body(*refs, tm, dils):
    h_ref, consts, outs = refs[0], refs[1:8], refs[8:]
    if dils is None:
        q_ref, k_ref, v_ref = outs
        qd = kd = vd = stage = None
    else:
        ng = len(dils)
        q_ref, (k_ref, v_ref) = None, outs[:2]
        qd, kd, vd, stage = outs[2:2 + ng], outs[2 + ng:2 + 2 * ng], outs[2 + 2 * ng:2 + 3 * ng], outs[2 + 3 * ng]
    for _ in _kvq_stages(h_ref[...], *consts, q_ref, k_ref, v_ref, qd, kd, vd, stage, tm=tm, dils=dils):
        pass


def _kvq(h, table, w, tm, dils):
    m = h.shape[0]
    n_q, n_kv = w["w_q"].shape[1], w["w_kv"].shape[1] // 2
    qw, kw = Q_PER_GROUP * HEAD, KV_PER_GROUP * HEAD
    consts = [w["g_kv"], w["g_mix1"], w["w_kv"], w["w_q"], w["g_k"], w["g_q"]]
    rows = [h, table]
    row_spec = lambda c: pl.BlockSpec((tm, c), lambda i: (i, 0))
    in_specs = [row_spec(a.shape[1]) for a in rows] + [_const_spec(a) for a in consts]
    nat = lambda c: jax.ShapeDtypeStruct((m, c), F32)
    if dils is None:
        out_shape = [nat(n_q), nat(n_kv), nat(n_kv)]
        out_specs, scratch = [row_spec(n_q), row_spec(n_kv), row_spec(n_kv)], []
    else:
        out_shape, out_specs = [nat(n_kv), nat(n_kv)], [row_spec(n_kv), row_spec(n_kv)]
        for width in (qw, kw, kw):
            for dil in dils:
                assert tm % (dil * 2 * SUBLANES) == 0
                out_shape.append(jax.ShapeDtypeStruct((dil, m // dil, width), BF16))
                out_specs.append(pl.BlockSpec((dil, tm // dil, width), lambda i: (0, i, 0)))
        scratch = [pltpu.VMEM((MXU_COLS // LANES, tm, LANES), F32)]
    return pl.pallas_call(functools.partial(_kvq_body, tm=tm, dils=dils), out_shape=out_shape, grid=(m // tm,),
                          in_specs=in_specs, out_specs=out_specs, scratch_shapes=scratch,
                          compiler_params=_params(), name="kvq")(*rows, *consts)


def _rope_table(first, count, consecutive):
    half = ROPE_DIM // 2
    inv_freq = ROPE_THETA ** (-jnp.arange(half, dtype=F32) * 2.0 / ROPE_DIM)
    pad = jnp.zeros((count, HEAD - ROPE_DIM), F32)
    if not consecutive:
        ang = jnp.full((count, 1), first, F32) * inv_freq[None, :]
        return jnp.concatenate([jnp.cos(ang), jnp.sin(ang), pad], axis=1)
    assert count % LANES == 0
    a = (first + LANES * jnp.arange(count // LANES, dtype=F32))[:, None] * inv_freq[None, :]
    b = jnp.arange(LANES, dtype=F32)[:, None] * inv_freq[None, :]
    ca, sa, cb, sb = jnp.cos(a)[:, None], jnp.sin(a)[:, None], jnp.cos(b)[None], jnp.sin(b)[None]
    cos = (ca * cb - sa * sb).reshape(count, half)
    sin = (sa * cb + ca * sb).reshape(count, half)
    return jnp.concatenate([cos, sin, pad], axis=1)


def _attn_prompt_body(q_ref, kc_ref, kp_ref, vc_ref, vp_ref, o_ref, lse_ref, *, n, blocks):
    first_lo = jnp.where(pl.program_id(1) > 0, 0, n)
    qi = lax.broadcasted_iota(jnp.int32, (Q_REP * n, 2 * n), 0) & (n - 1)
    kj = lax.broadcasted_iota(jnp.int32, (Q_REP * n, 2 * n), 1)
    band = (kj >= qi) & (kj <= qi + n)
    lane = lax.broadcasted_iota(jnp.int32, (n, LANES), 1)
    for j in range(blocks):
        rows = slice(n * j, n * (j + 1))
        lse_tile = jnp.zeros((n, LANES), F32)
        for g in range(KV_PER_GROUP):
            hs = slice(HEAD * g, HEAD * (g + 1))
            if j == 0:
                kprev, vprev = kp_ref[:, hs], vp_ref[:, hs]
                valid = band & (kj >= first_lo)
            else:
                prev = slice(n * (j - 1), n * j)
                kprev, vprev = kc_ref[prev, hs], vc_ref[prev, hs]
                valid = band
            kwin = jnp.concatenate([kprev, kc_ref[rows, hs]], axis=0)
            vwin = jnp.concatenate([vprev, vc_ref[rows, hs]], axis=0)
            q0 = Q_REP * HEAD * g
            qg = jnp.concatenate([q_ref[rows, q0 + HEAD * e:q0 + HEAD * (e + 1)] for e in range(Q_REP)], axis=0)
            s = jnp.where(valid, _dot_nt(qg, kwin) * (HEAD ** -0.5), -jnp.inf)
            mx = jnp.max(s, axis=-1, keepdims=True)
            p = jnp.exp(s - mx)
            den = jnp.sum(p, axis=-1, keepdims=True)
            o = _dot(_bf(p), vwin) / den
            lse = mx + jnp.log(den)
            for e in range(Q_REP):
                o_ref[rows, q0 + HEAD * e:q0 + HEAD * (e + 1)] = o[n * e:n * (e + 1)]
                lse_tile = jnp.where(lane == Q_REP * g + e, lse[n * e:n * (e + 1)], lse_tile)
        lse_ref[rows, :] = lse_tile


def _attn_prompt(qd, kd, vd, gi, window):
    dil, ln, qw = qd.shape
    kw = kd.shape[2]
    n = window // dil
    blocks = min(ATTN_BLOCKS, ln // n)
    rows = n * blocks
    assert n == HEAD and ln % rows == 0
    cur = lambda r, b: (r, b, 0)
    prev = lambda r, b: (r, jnp.maximum(b * blocks - 1, 0), 0)
    return pl.pallas_call(
        functools.partial(_attn_prompt_body, n=n, blocks=blocks),
        out_shape=[jax.ShapeDtypeStruct((dil, ln, qw), F32), jax.ShapeDtypeStruct((dil, ln, LANES), F32)],
        grid=(dil, ln // rows),
        in_specs=[pl.BlockSpec((None, rows, qw), cur), pl.BlockSpec((None, rows, kw), cur),
                  pl.BlockSpec((None, n, kw), prev), pl.BlockSpec((None, rows, kw), cur),
                  pl.BlockSpec((None, n, kw), prev)],
        out_specs=[pl.BlockSpec((None, rows, qw), cur), pl.BlockSpec((None, rows, LANES), cur)],
        compiler_params=_params(2), name=f"attn_prompt{gi}")(qd, kd, kd, vd, vd)


def _attn_sample_body(q_ref, kn_ref, vn_ref, *refs, bb):
    n_groups = len(GROUPS)
    caches, outs, lses = refs[:n_groups], refs[n_groups:2 * n_groups], refs[2 * n_groups:]
    lane = lax.broadcasted_iota(jnp.int32, (1, LANES), 1)
    kw = KV_PER_GROUP * HEAD
    units = [(gi, b, g) for gi in range(n_groups) for b in range(bb) for g in range(KV_PER_GROUP)]
    scale = HEAD ** -0.5
    qm, kn, vn = [], [], []
    for gi, b, g in units:
        q0 = Q_PER_GROUP * HEAD * gi + Q_REP * HEAD * g
        qm.append(jnp.concatenate([q_ref[b:b + 1, q0 + HEAD * e:q0 + HEAD * (e + 1)] for e in range(Q_REP)]
                                  + [jnp.zeros((SUBLANES - Q_REP, HEAD), F32)], axis=0))
        kn.append(kn_ref[b:b + 1, kw * gi + HEAD * g:kw * gi + HEAD * (g + 1)])
        vn.append(vn_ref[b:b + 1, kw * gi + HEAD * g:kw * gi + HEAD * (g + 1)])
    n = range(len(units))
    s = [_dot_nt(_bf(qm[u]), _bf(caches[gi][b, :, g, :])) * scale for u, (gi, b, g) in enumerate(units)]
    sn = [jnp.sum(qm[u] * kn[u], axis=1, keepdims=True) * scale for u in n]
    mx = [jnp.maximum(jnp.max(s[u], axis=1, keepdims=True), sn[u]) for u in n]
    p = [jnp.exp(s[u] - mx[u]) for u in n]
    pn = [jnp.exp(sn[u] - mx[u]) for u in n]
    den = [jnp.sum(p[u], axis=1, keepdims=True) + pn[u] for u in n]
    pv = [_dot(_bf(p[u]), _bf(caches[gi][b, :, KV_PER_GROUP + g, :])) for u, (gi, b, g) in enumerate(units)]
    lrows = {}
    for u, (gi, b, g) in enumerate(units):
        o = (pv[u] + pn[u] * vn[u]) / den[u]
        lse = mx[u] + jnp.log(den[u])
        for e in range(Q_REP):
            oc = HEAD * (Q_REP * g + e)
            outs[gi][b:b + 1, oc:oc + HEAD] = o[e:e + 1]
            lrows[gi, b] = jnp.where(lane == Q_REP * g + e, lse[e:e + 1], lrows.get((gi, b), jnp.zeros((1, LANES), F32)))
    for (gi, b), lrow in lrows.items():
        lses[gi][b:b + 1, :] = lrow


def _attn_sample(q, k, v, caches):
    n_seq = q.shape[0]
    bb = SAMPLE_SEQS
    assert n_seq % bb == 0
    qw = Q_PER_GROUP * HEAD
    views, cache_specs = [], []
    for cache, (window, dil) in zip(caches, GROUPS):
        n = window // dil
        assert cache.shape[1] == window and n == HEAD
        views.append(cache.reshape(n_seq, n, dil, 2 * KV_PER_GROUP, HEAD))
        cache_specs.append(pl.BlockSpec((bb, n, None, 2 * KV_PER_GROUP, HEAD), lambda i: (i, 0, 0, 0, 0)))
    row = lambda c: pl.BlockSpec((bb, c), lambda i: (i, 0))
    n_groups = len(GROUPS)
    res = pl.pallas_call(
        functools.partial(_attn_sample_body, bb=bb),
        out_shape=[jax.ShapeDtypeStruct((n_seq, qw), F32)] * n_groups
        + [jax.ShapeDtypeStruct((n_seq, LANES), F32)] * n_groups,
        grid=(n_seq // bb,),
        in_specs=[row(q.shape[1]), row(k.shape[1]), row(v.shape[1])] + cache_specs,
        out_specs=[row(qw)] * n_groups + [row(LANES)] * n_groups,
        compiler_params=_params(), name="attn_sample")(q, k, v, *views)
    return res[:n_groups], res[n_groups:]


def _ffn0_kvq_body(*refs, tm, dils, n_tiles):
    mix_in, p_ref, ffn_w, tab_ref, kvq_w = refs[:5], refs[5], refs[6:13], refs[13], refs[14:20]
    out_ref, tail_ref, k_ref, v_ref = refs[20:24]
    ng = len(dils)
    qd, kd, vd = refs[24:24 + ng], refs[24 + ng:24 + 2 * ng], refs[24 + 2 * ng:24 + 3 * ng]
    ubuf, act_s, tail_s, hbuf, stage = refs[24 + 3 * ng:]
    step = pl.program_id(0)

    @pl.when(step == 0)
    def _():
        ubuf[...] = jnp.zeros(ubuf.shape, F32)
        hbuf[...] = jnp.zeros(hbuf.shape, F32)

    h_prev = hbuf[...]
    result = []
    _weave(_ffn_stages(_delta_out_mix(*mix_in), p_ref, None, *ffn_w, tail_s, ubuf, act_s, result, tm=tm),
           _kvq_stages(h_prev, tab_ref, *kvq_w, None, k_ref, v_ref, qd, kd, vd, stage, tm=tm, dils=dils))
    hbuf[...] = result[0]

    @pl.when(step < n_tiles)
    def _():
        out_ref[...] = result[0]
        tail_ref[...] = tail_s[...]


def _ffn0_kvq(o, z, x, p, table, w, tm, dils):
    m, d = x.shape
    dff = w["w_down"].shape[1]
    n_kv = w["w_kv"].shape[1] // 2
    qw, kw = Q_PER_GROUP * HEAD, KV_PER_GROUP * HEAD
    assert m % tm == 0
    n_tiles = m // tm
    cur = lambda s: (jnp.minimum(s, n_tiles - 1), 0)
    prev = lambda s: (jnp.maximum(s - 1, 0), 0)
    at = lambda a, imap: (a, pl.BlockSpec((tm, a.shape[1]), imap))
    ins = [at(o, cur), at(z, cur), at(x, cur), _whole(w["g_aout"]), _whole(w["w_aout"]),
           (p, pl.BlockSpec((None, tm, p.shape[2]), lambda s: (0, jnp.minimum(s, n_tiles - 1), 0)))]
    ins += [_layer(w[name], 0) for name in ("g_ffn", "w_up", "w_fconv", "w_down", "g_ple", "w_gate", "w_proj")]
    ins += [at(table, prev)] + [_whole(w[name]) for name in ("g_kv", "g_mix1", "w_kv", "w_q", "g_k", "g_q")]
    nat = lambda c, imap: (jax.ShapeDtypeStruct((m, c), F32), pl.BlockSpec((tm, c), imap))
    outs = [nat(d, cur), _out_whole((SUBLANES, 2 * dff)), nat(n_kv, prev), nat(n_kv, prev)]
    for width in (qw, kw, kw):
        for dil in dils:
            assert tm % (dil * 2 * SUBLANES) == 0
            outs.append((jax.ShapeDtypeStruct((dil, m // dil, width), BF16),
                         pl.BlockSpec((dil, tm // dil, width), lambda s: (0, jnp.maximum(s - 1, 0), 0))))
    scratch = [pltpu.VMEM((SUBLANES, 2 * dff), F32), pltpu.VMEM((tm, dff), BF16), pltpu.VMEM((SUBLANES, 2 * dff), F32),
               pltpu.VMEM((tm, d), F32), pltpu.VMEM((MXU_COLS // LANES, tm, LANES), F32)]
    return _call(functools.partial(_ffn0_kvq_body, tm=tm, dils=dils, n_tiles=n_tiles), "ffn0_kvq", n_tiles + 1,
                 ins, outs, scratch)


def _trunk(x, p, first_pos, state, w, tm, tm_wide):
    m = x.shape[0]
    per_row = state is not None
    n_heads = w["n_heads"]
    aw = w["w_z"].shape[1]

    if per_row:
        consts = [w["g_mix0"], w["w_qkv"], w["w_ab"], w["w_z"], w["w_aconv"], w["a_log"], w["dt_bias"],
                  jnp.swapaxes(state["qkv_conv"], 0, 1)]
        q, k, v, z, gb, qkv_tail = _call(
            functools.partial(_a_in_step_body, n_heads=n_heads), "a_in_step", m // tm,
            [_rows(x, tm)] + [_whole(c) for c in consts],
            [_out_rows(m, aw, tm)] * 4 + [_out_rows(m, LANES, tm), _out_whole((m, 3 * aw))])
        o, s1 = _delta_step(q, k, v, gb, state["delta"], n_heads)
    else:
        o, z, s1, qkv_tail = _mixer_a_prompt(x, w, n_heads)
    table = _rope_table(first_pos, m, consecutive=not per_row)
    if per_row:
        mix = [_rows(o, tm_wide), _rows(z, tm_wide), _rows(x, tm_wide), _whole(w["g_aout"]), _whole(w["w_aout"])]
        h, ffn_tail0 = _mix_ffn(mix, [], None, p, w, 0, m, tm_wide, state["ffn_conv"][0])
        qa, ka, va = _kvq(h, table, w, tm, None)
        outs, lses = _attn_sample(qa, ka, va, state["caches"])
        outs, lses = [a[None] for a in outs], [a[None] for a in lses]
        dils = (1,) * len(GROUPS)
    else:
        dils = tuple(dil for _, dil in GROUPS)
        ng = len(dils)
        res = _ffn0_kvq(o, z, x, p, table, w, tm, dils)
        h, ffn_tail0, ka, va = res[:4]
        qd, kd, vd = res[4:4 + ng], res[4 + ng:4 + 2 * ng], res[4 + 2 * ng:]
        att = [_attn_prompt(qd[gi], kd[gi], vd[gi], gi, window) for gi, (window, _) in enumerate(GROUPS)]
        outs, lses = [a[0] for a in att], [a[1] for a in att]
    mix = [_grouped(a, tm_wide) for a in (*outs, *lses)] + [_rows(h, tm_wide), _whole(w["w_o"])]
    mix_scratch = []
    for dil in dils:
        if dil > 1:
            assert tm_wide % (dil * SUBLANES) == 0
            mix_scratch += [pltpu.VMEM((Q_PER_GROUP, tm_wide, HEAD), F32), pltpu.VMEM((tm_wide, LANES), F32)]
    h, ffn_tail1 = _mix_ffn(mix, mix_scratch, dils, p, w, 1, m, tm_wide, state["ffn_conv"][1] if per_row else None)
    return h, s1, qkv_tail, (ffn_tail0, ffn_tail1), ka, va


def _kv_rows(k, v, gi, n_rows):
    kw = KV_PER_GROUP * HEAD
    sl = slice(kw * gi, kw * (gi + 1))
    shape = (n_rows, 1, KV_PER_GROUP, HEAD)
    return jnp.concatenate([k[k.shape[0] - n_rows:, sl].reshape(shape), v[v.shape[0] - n_rows:, sl].reshape(shape)],
                           axis=1)


def kernel(x_prompt, x_sample, p_prompt, p_sample, state_delta, state_qkv_conv, state_ffn_conv, cache_kv_w128, cache_kv_w512, cache_kv_w2048, g_mix_norm, g_ffn_norm, w_ffn_up, w_ffn_conv, w_ffn_down, g_ple_norm, w_ple_gate, w_ple_proj, w_a_in, w_a_conv, a_log, a_dt_bias, g_a_out_norm, w_a_out, g_kv_norm, w_kv, g_k_norm, w_q, g_q_norm, w_o):
    n_heads = a_log.shape[1]
    aw = n_heads * HEAD
    assert w_a_in.shape[0] == 1 and w_q.shape[0] == 1 and w_a_in.shape[2] == 4 * aw + 2 * n_heads
    row = lambda a: a.reshape(1, -1)
    lane_pad = lambda a: jnp.pad(a, ((0, 0), (0, LANES - a.shape[1])))
    w = {
        "n_heads": n_heads,
        "g_mix0": row(g_mix_norm[0]), "g_mix1": row(g_mix_norm[1]), "g_ffn": g_ffn_norm[:, None], "g_ple": g_ple_norm[:, None],
        "w_up": _bf(w_ffn_up), "w_fconv": w_ffn_conv, "w_down": _bf(w_ffn_down),
        "w_gate": _bf(w_ple_gate), "w_proj": _bf(w_ple_proj),
        "w_qkv": _bf(w_a_in[0][:, :3 * aw]), "w_ab": _bf(lane_pad(w_a_in[0][:, 3 * aw:3 * aw + 2 * n_heads])),
        "w_z": _bf(w_a_in[0][:, 3 * aw + 2 * n_heads:]), "w_aconv": w_a_conv[0],
        "a_log": lane_pad(a_log), "dt_bias": lane_pad(a_dt_bias),
        "g_aout": row(g_a_out_norm[0]), "w_aout": _bf(w_a_out[0]),
        "g_kv": row(g_kv_norm), "w_kv": _bf(w_kv), "g_k": row(g_k_norm),
        "w_q": _bf(w_q[0]), "g_q": row(g_q_norm[0]), "w_o": _bf(w_o[0]),
    }
    assert x_prompt.shape[0] == 1 and x_sample.shape[1] == 1
    seq, d = x_prompt.shape[1], x_prompt.shape[2]
    n_seq = x_sample.shape[0]

    yp, sd_p, qkv_tail_p, ffn_tails_p, kp, vp = _trunk(
        x_prompt.reshape(seq, d), p_prompt.reshape(p_prompt.shape[0], seq, -1), 0, None, w,
        tm=ROW_TILE, tm_wide=WIDE_ROW_TILE)
    a_keep, f_keep = state_qkv_conv.shape[2], state_ffn_conv.shape[2]
    sq_p = qkv_tail_p[SUBLANES - a_keep:][None, None]
    sf_p = jnp.stack([t[SUBLANES - f_keep:] for t in ffn_tails_p])[:, None]
    kv_p = [_kv_rows(kp, vp, gi, min(window, seq))[None] for gi, (window, _) in enumerate(GROUPS)]

    state = {"delta": state_delta[0], "qkv_conv": state_qkv_conv[0], "ffn_conv": state_ffn_conv,
             "caches": (cache_kv_w128, cache_kv_w512, cache_kv_w2048)}
    ys, sd_s, qkv_tail_s, ffn_tails_s, ks, vs = _trunk(
        x_sample.reshape(n_seq, d), p_sample.reshape(p_sample.shape[0], n_seq, -1), PAST_LEN,
        state, w, tm=n_seq, tm_wide=n_seq)
    sq_s = jnp.concatenate([state_qkv_conv[0][:, 1:], qkv_tail_s[:, None]], axis=1)[None]
    sf_s = jnp.stack([jnp.concatenate([state_ffn_conv[i][:, 1:], ffn_tails_s[i][:, None]], axis=1)
                      for i in range(2)])
    kv_s = [_kv_rows(ks, vs, gi, n_seq).reshape(n_seq, 1, 2, KV_PER_GROUP, HEAD) for gi in range(len(GROUPS))]

    return (yp[None], ys[:, None], sd_p[None, None], sq_p, sf_p, kv_p[0], kv_p[1], kv_p[2],
            sd_s[None], sq_s, sf_s, kv_s[0], kv_s[1], kv_s[2])
```

```python
import functools

import jax
import jax.numpy as jnp
from jax import lax
from jax.experimental import pallas as pl
from jax.experimental.pallas import tpu as pltpu

F32 = jnp.float32
BF16 = jnp.bfloat16

EPS = 1e-6
HEAD = 128
CHUNK = 64
GROUPS = ((128, 1), (512, 4), (2048, 16))
KV_PER_GROUP = 2
Q_REP = 2
Q_PER_GROUP = KV_PER_GROUP * Q_REP
ROPE_DIM = HEAD // 4
ROPE_THETA = 500000.0
PAST_LEN = 16384

LANES = 128
SUBLANES = 8
MXU_COLS = 256
VMEM_LIMIT_BYTES = 56 * 1024 * 1024

ROW_TILE = 256
WIDE_ROW_TILE = 512
PREP_CHUNKS = 4
ATTN_BLOCKS = 8
SAMPLE_SEQS = SUBLANES


def _dot(a, b):
    return jnp.dot(a, b, preferred_element_type=F32)


def _dot_nt(a, b):
    return lax.dot_general(a, b, (((1,), (1,)), ((), ())), preferred_element_type=F32)


def _bf(x):
    return x.astype(BF16)


def _rms(x):
    return x * lax.rsqrt(jnp.mean(x * x, axis=-1, keepdims=True) + EPS)


def _sigmoid(x):
    return 1.0 / (1.0 + jnp.exp(-x))


def _silu(x):
    return x * _sigmoid(x)


def _split3f(x):
    hi = _bf(x).astype(F32)
    r = x - hi
    mid = _bf(r).astype(F32)
    return hi, mid, _bf(r - mid).astype(F32)


def _causal_conv_tile(x, w, carry_ref, cols):
    rows = x.shape[0]
    first = lax.broadcasted_iota(jnp.int32, (SUBLANES, x.shape[1]), 0) == 0
    z = x * w[0:1]
    for j in range(1, w.shape[0]):
        rolled = pltpu.roll(z, 1, 0)
        head = jnp.where(first, carry_ref[j - 1:j, cols], rolled[0:SUBLANES])
        carry_ref[j - 1:j, cols] = z[rows - 1:, :]
        z = jnp.concatenate([head, rolled[SUBLANES:]], axis=0) + x * w[j:j + 1]
    return z


def _const_spec(a):
    nd = a.ndim
    return pl.BlockSpec(a.shape, lambda *_: (0,) * nd, pipeline_mode=pl.Buffered(1))


def _params(n_axes=1):
    return pltpu.CompilerParams(dimension_semantics=("arbitrary",) * n_axes,
                                vmem_limit_bytes=VMEM_LIMIT_BYTES)


def _call(body, name, steps, ins, outs, scratch=()):
    return pl.pallas_call(body, out_shape=[o for o, _ in outs], grid=(steps,), in_specs=[sp for _, sp in ins],
                          out_specs=[sp for _, sp in outs], scratch_shapes=list(scratch),
                          compiler_params=_params(), name=name)(*[a for a, _ in ins])


def _rows(a, tm):
    return a, pl.BlockSpec((tm, a.shape[1]), lambda i: (i, 0))


def _whole(a):
    return a, _const_spec(a)


def _layer(a, layer):
    nd = a.ndim
    return a, pl.BlockSpec((None,) + a.shape[1:], lambda i: (layer,) + (0,) * (nd - 1), pipeline_mode=pl.Buffered(1))


def _layer_rows(a, layer, tm):
    return a, pl.BlockSpec((None, tm, a.shape[2]), lambda i: (layer, i, 0))


def _grouped(a, tm):
    return a, pl.BlockSpec((a.shape[0], tm // a.shape[0], a.shape[2]), lambda i: (0, i, 0))


def _out_rows(m, c, tm):
    return jax.ShapeDtypeStruct((m, c), F32), pl.BlockSpec((tm, c), lambda i: (i, 0))


def _out_whole(shape):
    return jax.ShapeDtypeStruct(shape, F32), pl.BlockSpec(shape, lambda i: (0,) * len(shape))


def _qkv_activation(y, part):
    a = _silu(y)
    if part == 2:
        return a
    halves = []
    for hh in range(MXU_COLS // HEAD):
        ah = a[:, HEAD * hh:HEAD * (hh + 1)]
        nrm = ah * lax.rsqrt(jnp.sum(ah * ah, axis=-1, keepdims=True) + EPS)
        halves.append(nrm * (HEAD ** -0.5) if part == 0 else nrm)
    return jnp.concatenate(halves, axis=1)


def _decay_and_beta(pab, alog_ref, dtb_ref, n_heads):
    lane = lax.broadcasted_iota(jnp.int32, pab.shape, 1)
    ap = pab + dtb_ref[...]
    softplus = jnp.maximum(ap, 0.0) + jnp.log1p(jnp.exp(-jnp.abs(ap)))
    return jnp.where(lane < n_heads, -jnp.exp(alog_ref[...]) * softplus, _sigmoid(pab))


def _a_in_step_body(x_ref, g_ref, wqkv_ref, wab_ref, wz_ref, wconv_ref, alog_ref, dtb_ref, buf_ref,
                    q_ref, k_ref, v_ref, z_ref, gb_ref, tail_ref, *, n_heads):
    hn = _bf(_rms(x_ref[...]) * g_ref[...])
    outs = (q_ref, k_ref, v_ref)
    per_part = q_ref.shape[1] // MXU_COLS
    for c in range(3 * per_part):
        cols = slice(MXU_COLS * c, MXU_COLS * (c + 1))
        pq = _dot(hn, wqkv_ref[:, cols])
        w = wconv_ref[:, cols]
        y = buf_ref[0, :, cols] * w[0:1]
        y = y + buf_ref[1, :, cols] * w[1:2]
        y = y + buf_ref[2, :, cols] * w[2:3]
        y = y + pq * w[3:4]
        tail_ref[:, cols] = pq
        part, sub = divmod(c, per_part)
        outs[part][:, MXU_COLS * sub:MXU_COLS * (sub + 1)] = _qkv_activation(y, part)
    gb_ref[...] = _decay_and_beta(_dot(hn, wab_ref[...]), alog_ref, dtb_ref, n_heads)
    z_ref[...] = _dot(hn, wz_ref[...])


def _weave(*stages):
    live = list(stages)
    while live:
        for g in list(live):
            try:
                next(g)
            except StopIteration:
                live.remove(g)


def _mixer_a_prompt_body(x_ref, g_ref, wqkv_ref, wab_ref, wz_ref, wconv_ref, alog_ref, dtb_ref,
                         z_ref, tail_ref, o_ref, sout_ref, cbuf, s_scr, *bufs, n_heads, n_chunks):
    step = pl.program_id(0)
    n_proj = 4
    proj_sets = bufs[:n_proj], bufs[n_proj:2 * n_proj]
    rest = bufs[2 * n_proj:]
    prep_sets = rest[:len(rest) // 2], rest[len(rest) // 2:]

    @pl.when(step == 0)
    def _():
        cbuf[...] = jnp.zeros(cbuf.shape, F32)
        s_scr[...] = jnp.zeros(s_scr.shape, F32)
        for ref in (*proj_sets[1], *prep_sets[0]):
            ref[...] = jnp.zeros(ref.shape, ref.dtype)

    sub = n_chunks // 2 * CHUNK

    def project(i, q_s, k_s, v_s, gb_s):
        rows = pl.ds(pl.multiple_of(i * sub, sub), sub)
        hn = _bf(_rms(x_ref[rows, :]) * g_ref[...])
        outs = (q_s, k_s, v_s)
        per_part = q_s.shape[1] // MXU_COLS
        for c in range(3 * per_part):
            cols = slice(MXU_COLS * c, MXU_COLS * (c + 1))
            pq = _dot(hn, wqkv_ref[:, cols])
            y = _causal_conv_tile(pq, wconv_ref[:, cols], cbuf, cols)
            tail_ref[:, cols] = pq[sub - SUBLANES:, :]
            part, sc = divmod(c, per_part)
            outs[part][rows, MXU_COLS * sc:MXU_COLS * (sc + 1)] = _qkv_activation(y, part)
            yield
        gb_s[rows, :] = _decay_and_beta(_dot(hn, wab_ref[...]), alog_ref, dtb_ref, n_heads)
        z_ref[rows, :] = _dot(hn, wz_ref[...])
        yield

    group = MXU_COLS // CHUNK
    shift = CHUNK.bit_length() - 1
    lane_p = lax.broadcasted_iota(jnp.int32, (CHUNK, MXU_COLS), 1)
    row_p = lax.broadcasted_iota(jnp.int32, (CHUNK, MXU_COLS), 0)
    col_p = lane_p & (CHUNK - 1)
    blk = [(lane_p >> shift) == u for u in range(group)]
    incl_p = row_p >= col_p
    strict_p = row_p > col_p
    eye_p = jnp.where(row_p == col_p, 1.0, 0.0).astype(F32)
    bd_r = lax.broadcasted_iota(jnp.int32, (MXU_COLS, MXU_COLS), 0)
    bd_c = lax.broadcasted_iota(jnp.int32, (MXU_COLS, MXU_COLS), 1)
    bd_mask = (bd_r >> shift) == (bd_c >> shift)
    ri = lax.broadcasted_iota(jnp.int32, (CHUNK, CHUNK), 0)
    ci = lax.broadcasted_iota(jnp.int32, (CHUNK, CHUNK), 1)
    ltri3 = jnp.concatenate([_bf(jnp.where(ri >= ci, 1.0, 0.0))] * 3, axis=1)
    g_r = lax.broadcasted_iota(jnp.int32, (MXU_COLS, 3 * LANES), 0) >> shift
    g_c = lax.broadcasted_iota(jnp.int32, (MXU_COLS, 3 * LANES), 1) & (LANES - 1)
    ones3 = jnp.ones((CHUNK, 3 * LANES), BF16)
    er = lax.broadcasted_iota(jnp.int32, (HEAD, HEAD), 0)
    ec = lax.broadcasted_iota(jnp.int32, (HEAD, HEAD), 1)
    eye_h = _bf(jnp.where(er == ec, 1.0, 0.0))
    heads = range(n_heads)
    hs = [slice(HEAD * h, HEAD * (h + 1)) for h in heads]

    def block_diag(x):
        return _bf(jnp.where(bd_mask, jnp.concatenate([x] * group, axis=0), 0.0))

    def pack_cols(cols):
        out = cols[-1]
        for u in reversed(range(group - 1)):
            out = jnp.where(blk[u], cols[u], out)
        return out

    wide = group * HEAD
    kd_r = lax.broadcasted_iota(jnp.int32, (MXU_COLS, wide), 0) >> shift
    kd_c = lax.broadcasted_iota(jnp.int32, (MXU_COLS, wide), 1) >> (HEAD.bit_length() - 1)
    kd_mask = kd_r == kd_c

    def key_blocks(kc):
        return _bf(jnp.where(kd_mask, jnp.concatenate([kc] * group, axis=0), 0.0))

    def stack(ref, rows, h0):
        return jnp.concatenate([ref[rows, hs[h]] for h in range(h0, h0 + group)], axis=0)

    def stack_cols(x, lo):
        return jnp.concatenate([x[:, lo + u:lo + u + 1] for u in range(group)], axis=0)

    def prep(i, q_ref, k_ref, v_ref, gb_ref, u0_s, wkqg_s, qk_s, kdt_s, gl_s):
        chunks = [i * PREP_CHUNKS + j for j in range(PREP_CHUNKS)]
        rows = [pl.ds(pl.multiple_of(c * CHUNK, CHUNK), CHUNK) for c in chunks]
        gbc = [gb_ref[r, :] for r in rows]
        gc = [_dot(ltri3, _bf(jnp.concatenate(_split3f(x), axis=0))) for x in gbc]
        glast = [x[CHUNK - 1:CHUNK, :] for x in gc]
        for j, c in enumerate(chunks):
            gl_s[c] = jnp.broadcast_to(glast[j], (SUBLANES, LANES))
        groups = [(j, h0) for j in range(PREP_CHUNKS) for h0 in range(0, n_heads, group)]
        ng = range(len(groups))
        kc = [k_ref[rows[j], HEAD * h0:HEAD * (h0 + group)] for j, h0 in groups]
        qc = [q_ref[rows[j], HEAD * h0:HEAD * (h0 + group)] for j, h0 in groups]
        pairs = [_dot_nt(_bf(jnp.concatenate([qc[g], kc[g]], axis=0)), key_blocks(kc[g])) for g in ng]
        yield
        gcol_p = [pack_cols([gc[j][:, h:h + 1] for h in range(h0, h0 + group)]) for j, h0 in groups]
        bcol_p = [pack_cols([gbc[j][:, n_heads + h:n_heads + h + 1] for h in range(h0, h0 + group)])
                  for j, h0 in groups]
        grow = [_dot_nt(ones3, _bf(jnp.where(g_c == g_r + h0,
                                               jnp.concatenate([jnp.concatenate([x] * group, axis=0)
                                                                for x in _split3f(gc[j])], axis=1), 0.0)))
                for j, h0 in groups]
        yield
        dec = [jnp.exp(jnp.where(incl_p, gcol_p[g] - grow[g], -jnp.inf)) for g in ng]
        a = [jnp.where(strict_p, bcol_p[g] * pairs[g][CHUNK:] * dec[g], 0.0) for g in ng]
        for g, (j, h0) in enumerate(groups):
            qk = _bf(pairs[g][:CHUNK] * dec[g])
            for u in range(group):
                qk_s[chunks[j], h0 + u] = qk[:, CHUNK * u:CHUNK * (u + 1)]
        tinv = [eye_p - x for x in a]
        pw = [_dot(_bf(-x), block_diag(-x)) for x in a]
        yield
        for _ in range(4):
            res = [_dot(jnp.concatenate([_bf(tinv[g]), _bf(pw[g])], axis=0), block_diag(pw[g])) for g in ng]
            tinv = [tinv[g] + res[g][:CHUNK] for g in ng]
            pw = [r[CHUNK:] for r in res]
            yield
        tinv = [tinv[g] + _dot(_bf(tinv[g]), block_diag(pw[g])) for g in ng]
        yield
        tbd = [block_diag(x) for x in tinv]
        kst = [stack(k_ref, rows[j], h0) for j, h0 in groups]
        gcol_s = [stack_cols(gc[j], h0) for j, h0 in groups]
        bcol_s = [stack_cols(gbc[j], n_heads + h0) for j, h0 in groups]
        gam_s = [jnp.exp(x) for x in gcol_s]
        rhs = [jnp.concatenate([stack(v_ref, rows[j], h0) * bcol_s[g], kst[g] * (bcol_s[g] * gam_s[g])], axis=1)
               for g, (j, h0) in enumerate(groups)]
        sol = [_dot(tbd[g], _bf(rhs[g])) for g in ng]
        yield
        kdt = [_dot_nt(eye_h, _bf(kst[g] * jnp.exp(stack_cols(jnp.broadcast_to(glast[j], (CHUNK, LANES)), h0)
                                                   - gcol_s[g])))
               for g, (j, h0) in enumerate(groups)]
        for g, (j, h0) in enumerate(groups):
            c = chunks[j]
            qg = stack(q_ref, rows[j], h0) * gam_s[g]
            for u in range(group):
                r = slice(CHUNK * u, CHUNK * (u + 1))
                u0_s[c, h0 + u] = sol[g][r, :HEAD]
                wkqg_s[c, h0 + u] = _bf(jnp.concatenate([sol[g][r, HEAD:], qg[r]], axis=0))
                kdt_s[c, h0 + u] = _bf(kdt[g][:, r])

    def scan(i, u0_s, wkqg_s, qk_s, kdt_s, gl_s):
        for j in range(PREP_CHUNKS):
            c = i * PREP_CHUNKS + j
            rows = pl.ds(pl.multiple_of(c * CHUNK, CHUNK), CHUNK)
            glr = gl_s[c]
            ws = [_dot(wkqg_s[c, h], _bf(s_scr[h])) for h in heads]
            yield
            ub = [_bf(u0_s[c, h] - ws[h][:CHUNK]) for h in heads]
            for h in heads:
                o_ref[rows, HEAD * h:HEAD * (h + 1)] = ws[h][CHUNK:] + _dot(qk_s[c, h], ub[h])
            yield
            for h in heads:
                s_scr[h] = jnp.exp(glr[0:1, h:h + 1]) * s_scr[h] + _dot(kdt_s[c, h], ub[h])
            yield

    def run(mine, other):
        def body(i, carry):
            _weave(scan(i, *prep_sets[mine]), prep(i, *proj_sets[other], *prep_sets[other]),
                   project(i, *proj_sets[mine]))
            return carry
        lax.fori_loop(0, n_chunks // PREP_CHUNKS, body, 0)

    @pl.when(step % 2 == 0)
    def _():
        run(0, 1)

    @pl.when(step % 2 == 1)
    def _():
        run(1, 0)

    sout_ref[...] = s_scr[...]


def _mixer_a_prompt(x, w, n_heads):
    t, d = x.shape
    aw = w["w_z"].shape[1]
    n_chunks = 2 * PREP_CHUNKS
    tt = n_chunks * CHUNK
    assert t % tt == 0 and n_chunks % PREP_CHUNKS == 0
    n_tiles = t // tt
    projected = [pltpu.VMEM((tt, aw), F32)] * 3 + [pltpu.VMEM((tt, LANES), F32)]
    prepared = [pltpu.VMEM((n_chunks, n_heads, CHUNK, HEAD), F32),
                pltpu.VMEM((n_chunks, n_heads, 2 * CHUNK, HEAD), BF16),
                pltpu.VMEM((n_chunks, n_heads, CHUNK, CHUNK), BF16),
                pltpu.VMEM((n_chunks, n_heads, HEAD, CHUNK), BF16),
                pltpu.VMEM((n_chunks, SUBLANES, LANES), F32)]
    scratch = [pltpu.VMEM((SUBLANES, 3 * aw), F32), pltpu.VMEM((n_heads, HEAD, HEAD), F32)]
    scratch += projected + projected + prepared + prepared
    first = lambda s: (jnp.minimum(s, n_tiles - 1), 0)
    last = lambda s: (jnp.maximum(s - 2, 0), 0)
    ins = [(x, pl.BlockSpec((tt, d), first))] + [_whole(w[name]) for name in
                                                  ("g_mix0", "w_qkv", "w_ab", "w_z", "w_aconv", "a_log", "dt_bias")]
    outs = [(jax.ShapeDtypeStruct((t, aw), F32), pl.BlockSpec((tt, aw), first)), _out_whole((SUBLANES, 3 * aw)),
            (jax.ShapeDtypeStruct((t, aw), F32), pl.BlockSpec((tt, aw), last)), _out_whole((n_heads, HEAD, HEAD))]
    z, tail, o, s1 = _call(functools.partial(_mixer_a_prompt_body, n_heads=n_heads, n_chunks=n_chunks),
                           "mixer_a_prompt", n_tiles + 2, ins, outs, scratch)
    return o, z, s1, tail


def _delta_step_body(qt_ref, kt_ref, v_ref, gbt_ref, s_ref, o_ref, so_ref, *, n_seq, n_heads):
    h = pl.program_id(0)
    decay = jnp.exp(gbt_ref[pl.ds(h, 1), :])
    beta = gbt_ref[pl.ds(n_heads + h, 1), :]
    for b in range(n_seq):
        s = decay[:, b:b + 1] * s_ref[b]
        kc = kt_ref[:, b:b + 1]
        u = beta[:, b:b + 1] * (v_ref[b:b + 1, :] - jnp.sum(kc * s, axis=0, keepdims=True))
        s = s + kc * u
        so_ref[b] = s
        o_ref[b:b + 1, :] = jnp.sum(qt_ref[:, b:b + 1] * s, axis=0, keepdims=True)


def _delta_step(q, k, v, gb, state, n_heads):
    n_seq = q.shape[0]
    col = pl.BlockSpec((HEAD, n_seq), lambda h: (h, 0))
    row = pl.BlockSpec((n_seq, HEAD), lambda h: (0, h))
    st = pl.BlockSpec((n_seq, None, HEAD, HEAD), lambda h: (0, h, 0, 0))
    gbt = gb.T
    return pl.pallas_call(
        functools.partial(_delta_step_body, n_seq=n_seq, n_heads=n_heads),
        out_shape=[jax.ShapeDtypeStruct(q.shape, F32), jax.ShapeDtypeStruct(state.shape, F32)],
        grid=(n_heads,), in_specs=[col, col, row, _const_spec(gbt), st], out_specs=[row, st],
        compiler_params=_params(), name="delta_step")(q.T, k.T, v, gbt, state)


def _delta_out_mix(o_ref, z_ref, res_ref, gout_ref, w_ref):
    parts = []
    for h in range(o_ref.shape[1] // HEAD):
        hs = slice(HEAD * h, HEAD * (h + 1))
        parts.append(_bf(_rms(o_ref[:, hs]) * gout_ref[...] * _silu(z_ref[:, hs])))
    return res_ref[...] + _dot(jnp.concatenate(parts, axis=1), w_ref[...])


def _attn_out_mix(refs, scratch, tm, dils):
    ng = len(dils)
    o_refs, l_refs = refs[:ng], refs[ng:2 * ng]
    res_ref, w_ref = refs[2 * ng:]
    get_o, lses = [], []
    for gi, dil in enumerate(dils):
        if dil == 1:
            get_o.append(lambda hh, r=o_refs[gi]: r[0, :, HEAD * hh:HEAD * (hh + 1)])
            lses.append(l_refs[gi][0])
        else:
            o_s, l_s = scratch.pop(0), scratch.pop(0)
            for r in range(dil):
                dst = pl.ds(r, tm // dil, stride=dil)
                for hh in range(Q_PER_GROUP):
                    o_s.at[hh][dst, :] = o_refs[gi][r, :, HEAD * hh:HEAD * (hh + 1)]
                l_s[dst, :] = l_refs[gi][r]
            get_o.append(lambda hh, r=o_s: r[hh])
            lses.append(l_s[...])
    parts = []
    for hh in range(Q_PER_GROUP):
        lh = [l[:, hh:hh + 1] for l in lses]
        mx = functools.reduce(jnp.maximum, lh)
        ex = [jnp.exp(x - mx) for x in lh]
        den = functools.reduce(lambda a, b: a + b, ex)
        acc = (ex[0] / den) * get_o[0](hh)
        for gi in range(1, ng):
            acc = acc + (ex[gi] / den) * get_o[gi](hh)
        parts.append(_bf(acc))
    return res_ref[...] + _dot(jnp.concatenate(parts, axis=1), w_ref[...])


def _ffn_stages(h, p_ref, buf_ref, gffn_ref, wup_ref, wconv_ref, wdown_ref, gple_ref, wgate_ref, wproj_ref,
                tail_ref, ubuf, act_s, result, *, tm):
    hn = _bf(_rms(h) * gffn_ref[...])
    dff = wdown_ref.shape[0]
    for c in range(dff // MXU_COLS):
        ys = []
        for half in range(2):
            lo = half * dff + MXU_COLS * c
            cols = slice(lo, lo + MXU_COLS)
            u = _dot(hn, wup_ref[:, cols])
            w = wconv_ref[:, cols]
            if buf_ref is not None:
                y = buf_ref[:, cols] * w[0:1]
                y = y + buf_ref[:, 2 * dff + lo:2 * dff + lo + MXU_COLS] * w[1:2]
                y = y + u * w[2:3]
                tail_ref[:, cols] = u
            else:
                y = _causal_conv_tile(u, w, ubuf, cols)
                tail_ref[:, cols] = u[tm - SUBLANES:, :]
            ys.append(y)
        act_s[:, MXU_COLS * c:MXU_COLS * (c + 1)] = _bf(_silu(ys[0]) * ys[1])
        yield
    h2 = h + _dot(act_s[...], wdown_ref[...])
    yield
    gate = _sigmoid(_dot(_bf(_rms(h2) * gple_ref[...]), wgate_ref[...]))
    result.append(h2 + gate * _dot(_bf(p_ref[...]), wproj_ref[...]))
    yield


def _mix_ffn_body(*refs, tm, per_row_state, dils):
    n_mix = 5 if dils is None else 2 * len(dils) + 2
    mix_in, refs = refs[:n_mix], refs[n_mix:]
    p_ref, refs = refs[0], refs[1:]
    buf_ref = None
    if per_row_state:
        buf_ref, refs = refs[0], refs[1:]
    ffn_w, (out_ref, tail_ref), scratch = refs[:7], refs[7:9], list(refs[9:])
    act_s = scratch.pop()
    ubuf = None
    if not per_row_state:
        ubuf = scratch.pop()

        @pl.when(pl.program_id(0) == 0)
        def _():
            ubuf[...] = jnp.zeros(ubuf.shape, F32)

    h = _delta_out_mix(*mix_in) if dils is None else _attn_out_mix(mix_in, scratch, tm, dils)
    result = []
    for _ in _ffn_stages(h, p_ref, buf_ref, *ffn_w, tail_ref, ubuf, act_s, result, tm=tm):
        pass
    out_ref[...] = result[0]


def _mix_ffn(mix_ins, mix_scratch, dils, p, w, layer, m, tm, state):
    per_row = state is not None
    d, dff = w["w_down"].shape[2], w["w_down"].shape[1]
    ins = list(mix_ins) + [_layer_rows(p, layer, tm)] + ([_rows(state.reshape(m, -1), tm)] if per_row else [])
    ins += [_layer(w[name], layer) for name in ("g_ffn", "w_up", "w_fconv", "w_down", "g_ple", "w_gate", "w_proj")]
    outs = [_out_rows(m, d, tm), _out_whole((m if per_row else SUBLANES, 2 * dff))]
    scratch = list(mix_scratch) + ([] if per_row else [pltpu.VMEM((SUBLANES, 2 * dff), F32)])
    scratch.append(pltpu.VMEM((tm, dff), BF16))
    return _call(functools.partial(_mix_ffn_body, tm=tm, per_row_state=per_row, dils=dils), f"mix_ffn{layer}",
                 m // tm, ins, outs, scratch)


def _kvq_stages(h, tab_ref, gkv_ref, gmix_ref, wkv_ref, wq_ref, gk_ref, gq_ref, q_ref, k_ref, v_ref, qd, kd, vd,
                stage, *, tm, dils):
    base = _rms(h)
    hkv = _bf(base * gkv_ref[...])
    hq = _bf(base * gmix_ref[...])
    tab = tab_ref[...]
    lane = lax.broadcasted_iota(jnp.int32, tab.shape, 1)
    half = ROPE_DIM // 2
    cos = jnp.where(lane < half, tab, jnp.where(lane < ROPE_DIM, pltpu.roll(tab, half, 1), 1.0))
    sa = jnp.where((lane >= half) & (lane < ROPE_DIM), tab, 0.0)
    sb = jnp.where(lane < half, -pltpu.roll(tab, HEAD - half, 1), 0.0)
    kw = KV_PER_GROUP * HEAD

    def norm_rope(x, g):
        y = _rms(x) * g
        return y * cos + pltpu.roll(y, ROPE_DIM // 2, 1) * sa + pltpu.roll(y, HEAD - ROPE_DIM // 2, 1) * sb

    def heads(x, g):
        return jnp.concatenate([norm_rope(x[:, HEAD * j:HEAD * (j + 1)], g) for j in range(MXU_COLS // HEAD)], axis=1)

    def by_class(x, dst_ref, dil, cols):
        if dil == 1:
            dst_ref[0, :, cols] = _bf(x)
        else:
            for j in range(MXU_COLS // LANES):
                stage[j] = x[:, LANES * j:LANES * (j + 1)]
            for r in range(dil):
                for j in range(MXU_COLS // LANES):
                    lo = cols.start + LANES * j
                    dst_ref[r, :, lo:lo + LANES] = _bf(stage.at[j][pl.ds(r, tm // dil, stride=dil), :])

    assert kw == MXU_COLS and Q_PER_GROUP * HEAD == 2 * MXU_COLS
    n_kv = k_ref.shape[1]

    def project(kind, c):
        lo = MXU_COLS * c + (n_kv if kind == "v" else 0)
        return _dot(hq if kind == "q" else hkv, (wq_ref if kind == "q" else wkv_ref)[:, lo:lo + MXU_COLS])

    def finish(kind, c, x):
        cols = slice(MXU_COLS * c, MXU_COLS * (c + 1))
        if kind == "k":
            x = heads(x, gk_ref[...])
            k_ref[:, cols] = x
        elif kind == "v":
            v_ref[:, cols] = x
        else:
            x = heads(x, gq_ref[...])
            if dils is None:
                q_ref[:, cols] = x
        if dils is not None:
            if kind == "q":
                by_class(x, qd[c // 2], dils[c // 2], slice(MXU_COLS * (c % 2), MXU_COLS * (c % 2 + 1)))
            else:
                by_class(x, (kd if kind == "k" else vd)[c], dils[c], slice(0, kw))

    yield
    for kind, n in (("k", n_kv), ("v", n_kv), ("q", wq_ref.shape[1])):
        for c in range(n // MXU_COLS):
            finish(kind, c, project(kind, c))
            yield


def _kvq_body(*refs, tm, dils):
    h_ref, consts, outs = refs[0], refs[1:8], refs[8:]
    if dils is None:
        q_ref, k_ref, v_ref = outs
        qd = kd = vd = stage = None
    else:
        ng = len(dils)
        q_ref, (k_ref, v_ref) = None, outs[:2]
        qd, kd, vd, stage = outs[2:2 + ng], outs[2 + ng:2 + 2 * ng], outs[2 + 2 * ng:2 + 3 * ng], outs[2 + 3 * ng]
    for _ in _kvq_stages(h_ref[...], *consts, q_ref, k_ref, v_ref, qd, kd, vd, stage, tm=tm, dils=dils):
        pass


def _kvq(h, table, w, tm, dils):
    m = h.shape[0]
    n_q, n_kv = w["w_q"].shape[1], w["w_kv"].shape[1] // 2
    qw, kw = Q_PER_GROUP * HEAD, KV_PER_GROUP * HEAD
    consts = [w["g_kv"], w["g_mix1"], w["w_kv"], w["w_q"], w["g_k"], w["g_q"]]
    rows = [h, table]
    row_spec = lambda c: pl.BlockSpec((tm, c), lambda i: (i, 0))
    in_specs = [row_spec(a.shape[1]) for a in rows] + [_const_spec(a) for a in consts]
    nat = lambda c: jax.ShapeDtypeStruct((m, c), F32)
    if dils is None:
        out_shape = [nat(n_q), nat(n_kv), nat(n_kv)]
        out_specs, scratch = [row_spec(n_q), row_spec(n_kv), row_spec(n_kv)], []
    else:
        out_shape, out_specs = [nat(n_kv), nat(n_kv)], [row_spec(n_kv), row_spec(n_kv)]
        for width in (qw, kw, kw):
            for dil in dils:
                assert tm % (dil * 2 * SUBLANES) == 0
                out_shape.append(jax.ShapeDtypeStruct((dil, m // dil, width), BF16))
                out_specs.append(pl.BlockSpec((dil, tm // dil, width), lambda i: (0, i, 0)))
        scratch = [pltpu.VMEM((MXU_COLS // LANES, tm, LANES), F32)]
    return pl.pallas_call(functools.partial(_kvq_body, tm=tm, dils=dils), out_shape=out_shape, grid=(m // tm,),
                          in_specs=in_specs, out_specs=out_specs, scratch_shapes=scratch,
                          compiler_params=_params(), name="kvq")(*rows, *consts)


def _rope_table(first, count, consecutive):
    half = ROPE_DIM // 2
    inv_freq = ROPE_THETA ** (-jnp.arange(half, dtype=F32) * 2.0 / ROPE_DIM)
    pad = jnp.zeros((count, HEAD - ROPE_DIM), F32)
    if not consecutive:
        ang = jnp.full((count, 1), first, F32) * inv_freq[None, :]
        return jnp.concatenate([jnp.cos(ang), jnp.sin(ang), pad], axis=1)
    assert count % LANES == 0
    a = (first + LANES * jnp.arange(count // LANES, dtype=F32))[:, None] * inv_freq[None, :]
    b = jnp.arange(LANES, dtype=F32)[:, None] * inv_freq[None, :]
    ca, sa, cb, sb = jnp.cos(a)[:, None], jnp.sin(a)[:, None], jnp.cos(b)[None], jnp.sin(b)[None]
    cos = (ca * cb - sa * sb).reshape(count, half)
    sin = (sa * cb + ca * sb).reshape(count, half)
    return jnp.concatenate([cos, sin, pad], axis=1)


def _attn_prompt_body(q_ref, kc_ref, kp_ref, vc_ref, vp_ref, o_ref, lse_ref, *, n, blocks):
    first_lo = jnp.where(pl.program_id(1) > 0, 0, n)
    qi = lax.broadcasted_iota(jnp.int32, (Q_REP * n, 2 * n), 0) & (n - 1)
    kj = lax.broadcasted_iota(jnp.int32, (Q_REP * n, 2 * n), 1)
    band = (kj >= qi) & (kj <= qi + n)
    lane = lax.broadcasted_iota(jnp.int32, (n, LANES), 1)
    for j in range(blocks):
        rows = slice(n * j, n * (j + 1))
        lse_tile = jnp.zeros((n, LANES), F32)
        for g in range(KV_PER_GROUP):
            hs = slice(HEAD * g, HEAD * (g + 1))
            if j == 0:
                kprev, vprev = kp_ref[:, hs], vp_ref[:, hs]
                valid = band & (kj >= first_lo)
            else:
                prev = slice(n * (j - 1), n * j)
                kprev, vprev = kc_ref[prev, hs], vc_ref[prev, hs]
                valid = band
            kwin = jnp.concatenate([kprev, kc_ref[rows, hs]], axis=0)
            vwin = jnp.concatenate([vprev, vc_ref[rows, hs]], axis=0)
            q0 = Q_REP * HEAD * g
            qg = jnp.concatenate([q_ref[rows, q0 + HEAD * e:q0 + HEAD * (e + 1)] for e in range(Q_REP)], axis=0)
            s = jnp.where(valid, _dot_nt(qg, kwin) * (HEAD ** -0.5), -jnp.inf)
            mx = jnp.max(s, axis=-1, keepdims=True)
            p = jnp.exp(s - mx)
            den = jnp.sum(p, axis=-1, keepdims=True)
            o = _dot(_bf(p), vwin) / den
            lse = mx + jnp.log(den)
            for e in range(Q_REP):
                o_ref[rows, q0 + HEAD * e:q0 + HEAD * (e + 1)] = o[n * e:n * (e + 1)]
                lse_tile = jnp.where(lane == Q_REP * g + e, lse[n * e:n * (e + 1)], lse_tile)
        lse_ref[rows, :] = lse_tile


def _attn_prompt(qd, kd, vd, gi, window):
    dil, ln, qw = qd.shape
    kw = kd.shape[2]
    n = window // dil
    blocks = min(ATTN_BLOCKS, ln // n)
    rows = n * blocks
    assert n == HEAD and ln % rows == 0
    cur = lambda r, b: (r, b, 0)
    prev = lambda r, b: (r, jnp.maximum(b * blocks - 1, 0), 0)
    return pl.pallas_call(
        functools.partial(_attn_prompt_body, n=n, blocks=blocks),
        out_shape=[jax.ShapeDtypeStruct((dil, ln, qw), F32), jax.ShapeDtypeStruct((dil, ln, LANES), F32)],
        grid=(dil, ln // rows),
        in_specs=[pl.BlockSpec((None, rows, qw), cur), pl.BlockSpec((None, rows, kw), cur),
                  pl.BlockSpec((None, n, kw), prev), pl.BlockSpec((None, rows, kw), cur),
                  pl.BlockSpec((None, n, kw), prev)],
        out_specs=[pl.BlockSpec((None, rows, qw), cur), pl.BlockSpec((None, rows, LANES), cur)],
        compiler_params=_params(2), name=f"attn_prompt{gi}")(qd, kd, kd, vd, vd)


def _attn_sample_body(q_ref, kn_ref, vn_ref, *refs, bb):
    n_groups = len(GROUPS)
    caches, outs, lses = refs[:n_groups], refs[n_groups:2 * n_groups], refs[2 * n_groups:]
    lane = lax.broadcasted_iota(jnp.int32, (1, LANES), 1)
    kw = KV_PER_GROUP * HEAD
    units = [(gi, b, g) for gi in range(n_groups) for b in range(bb) for g in range(KV_PER_GROUP)]
    scale = HEAD ** -0.5
    qm, kn, vn = [], [], []
    for gi, b, g in units:
        q0 = Q_PER_GROUP * HEAD * gi + Q_REP * HEAD * g
        qm.append(jnp.concatenate([q_ref[b:b + 1, q0 + HEAD * e:q0 + HEAD * (e + 1)] for e in range(Q_REP)]
                                  + [jnp.zeros((SUBLANES - Q_REP, HEAD), F32)], axis=0))
        kn.append(kn_ref[b:b + 1, kw * gi + HEAD * g:kw * gi + HEAD * (g + 1)])
        vn.append(vn_ref[b:b + 1, kw * gi + HEAD * g:kw * gi + HEAD * (g + 1)])
    n = range(len(units))
    s = [_dot_nt(_bf(qm[u]), _bf(caches[gi][b, :, g, :])) * scale for u, (gi, b, g) in enumerate(units)]
    sn = [jnp.sum(qm[u] * kn[u], axis=1, keepdims=True) * scale for u in n]
    mx = [jnp.maximum(jnp.max(s[u], axis=1, keepdims=True), sn[u]) for u in n]
    p = [jnp.exp(s[u] - mx[u]) for u in n]
    pn = [jnp.exp(sn[u] - mx[u]) for u in n]
    den = [jnp.sum(p[u], axis=1, keepdims=True) + pn[u] for u in n]
    pv = [_dot(_bf(p[u]), _bf(caches[gi][b, :, KV_PER_GROUP + g, :])) for u, (gi, b, g) in enumerate(units)]
    lrows = {}
    for u, (gi, b, g) in enumerate(units):
        o = (pv[u] + pn[u] * vn[u]) / den[u]
        lse = mx[u] + jnp.log(den[u])
        for e in range(Q_REP):
            oc = HEAD * (Q_REP * g + e)
            outs[gi][b:b + 1, oc:oc + HEAD] = o[e:e + 1]
            lrows[gi, b] = jnp.where(lane == Q_REP * g + e, lse[e:e + 1], lrows.get((gi, b), jnp.zeros((1, LANES), F32)))
    for (gi, b), lrow in lrows.items():
        lses[gi][b:b + 1, :] = lrow


def _attn_sample(q, k, v, caches):
    n_seq = q.shape[0]
    bb = SAMPLE_SEQS
    assert n_seq % bb == 0
    qw = Q_PER_GROUP * HEAD
    views, cache_specs = [], []
    for cache, (window, dil) in zip(caches, GROUPS):
        n = window // dil
        assert cache.shape[1] == window and n == HEAD
        views.append(cache.reshape(n_seq, n, dil, 2 * KV_PER_GROUP, HEAD))
        cache_specs.append(pl.BlockSpec((bb, n, None, 2 * KV_PER_GROUP, HEAD), lambda i: (i, 0, 0, 0, 0)))
    row = lambda c: pl.BlockSpec((bb, c), lambda i: (i, 0))
    n_groups = len(GROUPS)
    res = pl.pallas_call(
        functools.partial(_attn_sample_body, bb=bb),
        out_shape=[jax.ShapeDtypeStruct((n_seq, qw), F32)] * n_groups
        + [jax.ShapeDtypeStruct((n_seq, LANES), F32)] * n_groups,
        grid=(n_seq // bb,),
        in_specs=[row(q.shape[1]), row(k.shape[1]), row(v.shape[1])] + cache_specs,
        out_specs=[row(qw)] * n_groups + [row(LANES)] * n_groups,
        compiler_params=_params(), name="attn_sample")(q, k, v, *views)
    return res[:n_groups], res[n_groups:]


def _ffn0_kvq_body(*refs, tm, dils, n_tiles):
    mix_in, p_ref, ffn_w, tab_ref, kvq_w = refs[:5], refs[5], refs[6:13], refs[13], refs[14:20]
    out_ref, tail_ref, k_ref, v_ref = refs[20:24]
    ng = len(dils)
    qd, kd, vd = refs[24:24 + ng], refs[24 + ng:24 + 2 * ng], refs[24 + 2 * ng:24 + 3 * ng]
    ubuf, act_s, tail_s, hbuf, stage = refs[24 + 3 * ng:]
    step = pl.program_id(0)

    @pl.when(step == 0)
    def _():
        ubuf[...] = jnp.zeros(ubuf.shape, F32)
        hbuf[...] = jnp.zeros(hbuf.shape, F32)

    h_prev = hbuf[...]
    result = []
    _weave(_ffn_stages(_delta_out_mix(*mix_in), p_ref, None, *ffn_w, tail_s, ubuf, act_s, result, tm=tm),
           _kvq_stages(h_prev, tab_ref, *kvq_w, None, k_ref, v_ref, qd, kd, vd, stage, tm=tm, dils=dils))
    hbuf[...] = result[0]

    @pl.when(step < n_tiles)
    def _():
        out_ref[...] = result[0]
        tail_ref[...] = tail_s[...]


def _ffn0_kvq(o, z, x, p, table, w, tm, dils):
    m, d = x.shape
    dff = w["w_down"].shape[1]
    n_kv = w["w_kv"].shape[1] // 2
    qw, kw = Q_PER_GROUP * HEAD, KV_PER_GROUP * HEAD
    assert m % tm == 0
    n_tiles = m // tm
    cur = lambda s: (jnp.minimum(s, n_tiles - 1), 0)
    prev = lambda s: (jnp.maximum(s - 1, 0), 0)
    at = lambda a, imap: (a, pl.BlockSpec((tm, a.shape[1]), imap))
    ins = [at(o, cur), at(z, cur), at(x, cur), _whole(w["g_aout"]), _whole(w["w_aout"]),
           (p, pl.BlockSpec((None, tm, p.shape[2]), lambda s: (0, jnp.minimum(s, n_tiles - 1), 0)))]
    ins += [_layer(w[name], 0) for name in ("g_ffn", "w_up", "w_fconv", "w_down", "g_ple", "w_gate", "w_proj")]
    ins += [at(table, prev)] + [_whole(w[name]) for name in ("g_kv", "g_mix1", "w_kv", "w_q", "g_k", "g_q")]
    nat = lambda c, imap: (jax.ShapeDtypeStruct((m, c), F32), pl.BlockSpec((tm, c), imap))
    outs = [nat(d, cur), _out_whole((SUBLANES, 2 * dff)), nat(n_kv, prev), nat(n_kv, prev)]
    for width in (qw, kw, kw):
        for dil in dils:
            assert tm % (dil * 2 * SUBLANES) == 0
            outs.append((jax.ShapeDtypeStruct((dil, m // dil, width), BF16),
                         pl.BlockSpec((dil, tm // dil, width), lambda s: (0, jnp.maximum(s - 1, 0), 0))))
    scratch = [pltpu.VMEM((SUBLANES, 2 * dff), F32), pltpu.VMEM((tm, dff), BF16), pltpu.VMEM((SUBLANES, 2 * dff), F32),
               pltpu.VMEM((tm, d), F32), pltpu.VMEM((MXU_COLS // LANES, tm, LANES), F32)]
    return _call(functools.partial(_ffn0_kvq_body, tm=tm, dils=dils, n_tiles=n_tiles), "ffn0_kvq", n_tiles + 1,
                 ins, outs, scratch)


def _trunk(x, p, first_pos, state, w, tm, tm_wide):
    m = x.shape[0]
    per_row = state is not None
    n_heads = w["n_heads"]
    aw = w["w_z"].shape[1]

    if per_row:
        consts = [w["g_mix0"], w["w_qkv"], w["w_ab"], w["w_z"], w["w_aconv"], w["a_log"], w["dt_bias"],
                  jnp.swapaxes(state["qkv_conv"], 0, 1)]
        q, k, v, z, gb, qkv_tail = _call(
            functools.partial(_a_in_step_body, n_heads=n_heads), "a_in_step", m // tm,
            [_rows(x, tm)] + [_whole(c) for c in consts],
            [_out_rows(m, aw, tm)] * 4 + [_out_rows(m, LANES, tm), _out_whole((m, 3 * aw))])
        o, s1 = _delta_step(q, k, v, gb, state["delta"], n_heads)
    else:
        o, z, s1, qkv_tail = _mixer_a_prompt(x, w, n_heads)
    table = _rope_table(first_pos, m, consecutive=not per_row)
    if per_row:
        mix = [_rows(o, tm_wide), _rows(z, tm_wide), _rows(x, tm_wide), _whole(w["g_aout"]), _whole(w["w_aout"])]
        h, ffn_tail0 = _mix_ffn(mix, [], None, p, w, 0, m, tm_wide, state["ffn_conv"][0])
        qa, ka, va = _kvq(h, table, w, tm, None)
        outs, lses = _attn_sample(qa, ka, va, state["caches"])
        outs, lses = [a[None] for a in outs], [a[None] for a in lses]
        dils = (1,) * len(GROUPS)
    else:
        dils = tuple(dil for _, dil in GROUPS)
        ng = len(dils)
        res = _ffn0_kvq(o, z, x, p, table, w, tm, dils)
        h, ffn_tail0, ka, va = res[:4]
        qd, kd, vd = res[4:4 + ng], res[4 + ng:4 + 2 * ng], res[4 + 2 * ng:]
        att = [_attn_prompt(qd[gi], kd[gi], vd[gi], gi, window) for gi, (window, _) in enumerate(GROUPS)]
        outs, lses = [a[0] for a in att], [a[1] for a in att]
    mix = [_grouped(a, tm_wide) for a in (*outs, *lses)] + [_rows(h, tm_wide), _whole(w["w_o"])]
    mix_scratch = []
    for dil in dils:
        if dil > 1:
            assert tm_wide % (dil * SUBLANES) == 0
            mix_scratch += [pltpu.VMEM((Q_PER_GROUP, tm_wide, HEAD), F32), pltpu.VMEM((tm_wide, LANES), F32)]
    h, ffn_tail1 = _mix_ffn(mix, mix_scratch, dils, p, w, 1, m, tm_wide, state["ffn_conv"][1] if per_row else None)
    return h, s1, qkv_tail, (ffn_tail0, ffn_tail1), ka, va


def _kv_rows(k, v, gi, n_rows):
    kw = KV_PER_GROUP * HEAD
    sl = slice(kw * gi, kw * (gi + 1))
    shape = (n_rows, 1, KV_PER_GROUP, HEAD)
    return jnp.concatenate([k[k.shape[0] - n_rows:, sl].reshape(shape), v[v.shape[0] - n_rows:, sl].reshape(shape)],
                           axis=1)


def kernel(x_prompt, x_sample, p_prompt, p_sample, state_delta, state_qkv_conv, state_ffn_conv, cache_kv_w128, cache_kv_w512, cache_kv_w2048, g_mix_norm, g_ffn_norm, w_ffn_up, w_ffn_conv, w_ffn_down, g_ple_norm, w_ple_gate, w_ple_proj, w_a_in, w_a_conv, a_log, a_dt_bias, g_a_out_norm, w_a_out, g_kv_norm, w_kv, g_k_norm, w_q, g_q_norm, w_o):
    n_heads = a_log.shape[1]
    aw = n_heads * HEAD
    assert w_a_in.shape[0] == 1 and w_q.shape[0] == 1 and w_a_in.shape[2] == 4 * aw + 2 * n_heads
    row = lambda a: a.reshape(1, -1)
    lane_pad = lambda a: jnp.pad(a, ((0, 0), (0, LANES - a.shape[1])))
    w = {
        "n_heads": n_heads,
        "g_mix0": row(g_mix_norm[0]), "g_mix1": row(g_mix_norm[1]), "g_ffn": g_ffn_norm[:, None], "g_ple": g_ple_norm[:, None],
        "w_up": _bf(w_ffn_up), "w_fconv": w_ffn_conv, "w_down": _bf(w_ffn_down),
        "w_gate": _bf(w_ple_gate), "w_proj": _bf(w_ple_proj),
        "w_qkv": _bf(w_a_in[0][:, :3 * aw]), "w_ab": _bf(lane_pad(w_a_in[0][:, 3 * aw:3 * aw + 2 * n_heads])),
        "w_z": _bf(w_a_in[0][:, 3 * aw + 2 * n_heads:]), "w_aconv": w_a_conv[0],
        "a_log": lane_pad(a_log), "dt_bias": lane_pad(a_dt_bias),
        "g_aout": row(g_a_out_norm[0]), "w_aout": _bf(w_a_out[0]),
        "g_kv": row(g_kv_norm), "w_kv": _bf(w_kv), "g_k": row(g_k_norm),
        "w_q": _bf(w_q[0]), "g_q": row(g_q_norm[0]), "w_o": _bf(w_o[0]),
    }
    assert x_prompt.shape[0] == 1 and x_sample.shape[1] == 1
    seq, d = x_prompt.shape[1], x_prompt.shape[2]
    n_seq = x_sample.shape[0]

    yp, sd_p, qkv_tail_p, ffn_tails_p, kp, vp = _trunk(
        x_prompt.reshape(seq, d), p_prompt.reshape(p_prompt.shape[0], seq, -1), 0, None, w,
        tm=ROW_TILE, tm_wide=WIDE_ROW_TILE)
    a_keep, f_keep = state_qkv_conv.shape[2], state_ffn_conv.shape[2]
    sq_p = qkv_tail_p[SUBLANES - a_keep:][None, None]
    sf_p = jnp.stack([t[SUBLANES - f_keep:] for t in ffn_tails_p])[:, None]
    kv_p = [_kv_rows(kp, vp, gi, min(window, seq))[None] for gi, (window, _) in enumerate(GROUPS)]

    state = {"delta": state_delta[0], "qkv_conv": state_qkv_conv[0], "ffn_conv": state_ffn_conv,
             "caches": (cache_kv_w128, cache_kv_w512, cache_kv_w2048)}
    ys, sd_s, qkv_tail_s, ffn_tails_s, ks, vs = _trunk(
        x_sample.reshape(n_seq, d), p_sample.reshape(p_sample.shape[0], n_seq, -1), PAST_LEN,
        state, w, tm=n_seq, tm_wide=n_seq)
    sq_s = jnp.concatenate([state_qkv_conv[0][:, 1:], qkv_tail_s[:, None]], axis=1)[None]
    sf_s = jnp.stack([jnp.concatenate([state_ffn_conv[i][:, 1:], ffn_tails_s[i][:, None]], axis=1)
                      for i in range(2)])
    kv_s = [_kv_rows(ks, vs, gi, n_seq).reshape(n_seq, 1, 2, KV_PER_GROUP, HEAD) for gi in range(len(GROUPS))]

    return (yp[None], ys[:, None], sd_p[None, None], sq_p, sf_p, kv_p[0], kv_p[1], kv_p[2],
            sd_s[None], sq_s, sf_s, kv_s[0], kv_s[1], kv_s[2])
```

```python
import functools

import jax
import jax.numpy as jnp
from jax import lax
from jax.experimental import pallas as pl
from jax.experimental.pallas import tpu as pltpu

F32 = jnp.float32
BF16 = jnp.bfloat16

EPS = 1e-6
HEAD = 128
CHUNK = 64
GROUPS = ((128, 1), (512, 4), (2048, 16))
KV_PER_GROUP = 2
Q_REP = 2
Q_PER_GROUP = KV_PER_GROUP * Q_REP
ROPE_DIM = HEAD // 4
ROPE_THETA = 500000.0
PAST_LEN = 16384

LANES = 128
SUBLANES = 8
MXU_COLS = 256
VMEM_LIMIT_BYTES = 56 * 1024 * 1024

ROW_TILE = 256
WIDE_ROW_TILE = 512
PREP_CHUNKS = 4
ATTN_BLOCKS = 8
SAMPLE_SEQS = SUBLANES


def _dot(a, b):
    return jnp.dot(a, b, preferred_element_type=F32)


def _dot_nt(a, b):
    return lax.dot_general(a, b, (((1,), (1,)), ((), ())), preferred_element_type=F32)


def _bf(x):
    return x.astype(BF16)


def _rms(x):
    return x * lax.rsqrt(jnp.mean(x * x, axis=-1, keepdims=True) + EPS)


def _sigmoid(x):
    return 1.0 / (1.0 + jnp.exp(-x))


def _silu(x):
    return x * _sigmoid(x)


def _split3f(x):
    hi = _bf(x).astype(F32)
    r = x - hi
    mid = _bf(r).astype(F32)
    return hi, mid, _bf(r - mid).astype(F32)


def _causal_conv_tile(x, w, carry_ref, cols):
    rows = x.shape[0]
    first = lax.broadcasted_iota(jnp.int32, (SUBLANES, x.shape[1]), 0) == 0
    z = x * w[0:1]
    for j in range(1, w.shape[0]):
        rolled = pltpu.roll(z, 1, 0)
        head = jnp.where(first, carry_ref[j - 1:j, cols], rolled[0:SUBLANES])
        carry_ref[j - 1:j, cols] = z[rows - 1:, :]
        z = jnp.concatenate([head, rolled[SUBLANES:]], axis=0) + x * w[j:j + 1]
    return z


def _const_spec(a):
    nd = a.ndim
    return pl.BlockSpec(a.shape, lambda *_: (0,) * nd, pipeline_mode=pl.Buffered(1))


def _params(n_axes=1):
    return pltpu.CompilerParams(dimension_semantics=("arbitrary",) * n_axes,
                                vmem_limit_bytes=VMEM_LIMIT_BYTES)


def _call(body, name, steps, ins, outs, scratch=()):
    return pl.pallas_call(body, out_shape=[o for o, _ in outs], grid=(steps,), in_specs=[sp for _, sp in ins],
                          out_specs=[sp for _, sp in outs], scratch_shapes=list(scratch),
                          compiler_params=_params(), name=name)(*[a for a, _ in ins])


def _rows(a, tm):
    return a, pl.BlockSpec((tm, a.shape[1]), lambda i: (i, 0))


def _whole(a):
    return a, _const_spec(a)


def _layer(a, layer):
    nd = a.ndim
    return a, pl.BlockSpec((None,) + a.shape[1:], lambda i: (layer,) + (0,) * (nd - 1), pipeline_mode=pl.Buffered(1))


def _layer_rows(a, layer, tm):
    return a, pl.BlockSpec((None, tm, a.shape[2]), lambda i: (layer, i, 0))


def _grouped(a, tm):
    return a, pl.BlockSpec((a.shape[0], tm // a.shape[0], a.shape[2]), lambda i: (0, i, 0))


def _out_rows(m, c, tm):
    return jax.ShapeDtypeStruct((m, c), F32), pl.BlockSpec((tm, c), lambda i: (i, 0))


def _out_whole(shape):
    return jax.ShapeDtypeStruct(shape, F32), pl.BlockSpec(shape, lambda i: (0,) * len(shape))


def _qkv_activation(y, part):
    a = _silu(y)
    if part == 2:
        return a
    halves = []
    for hh in range(MXU_COLS // HEAD):
        ah = a[:, HEAD * hh:HEAD * (hh + 1)]
        nrm = ah * lax.rsqrt(jnp.sum(ah * ah, axis=-1, keepdims=True) + EPS)
        halves.append(nrm * (HEAD ** -0.5) if part == 0 else nrm)
    return jnp.concatenate(halves, axis=1)


def _decay_and_beta(pab, alog_ref, dtb_ref, n_heads):
    lane = lax.broadcasted_iota(jnp.int32, pab.shape, 1)
    ap = pab + dtb_ref[...]
    softplus = jnp.maximum(ap, 0.0) + jnp.log1p(jnp.exp(-jnp.abs(ap)))
    return jnp.where(lane < n_heads, -jnp.exp(alog_ref[...]) * softplus, _sigmoid(pab))


def _a_in_step_body(x_ref, g_ref, wqkv_ref, wab_ref, wz_ref, wconv_ref, alog_ref, dtb_ref, buf_ref,
                    q_ref, k_ref, v_ref, z_ref, gb_ref, tail_ref, *, n_heads):
    hn = _bf(_rms(x_ref[...]) * g_ref[...])
    outs = (q_ref, k_ref, v_ref)
    per_part = q_ref.shape[1] // MXU_COLS
    for c in range(3 * per_part):
        cols = slice(MXU_COLS * c, MXU_COLS * (c + 1))
        pq = _dot(hn, wqkv_ref[:, cols])
        w = wconv_ref[:, cols]
        y = buf_ref[0, :, cols] * w[0:1]
        y = y + buf_ref[1, :, cols] * w[1:2]
        y = y + buf_ref[2, :, cols] * w[2:3]
        y = y + pq * w[3:4]
        tail_ref[:, cols] = pq
        part, sub = divmod(c, per_part)
        outs[part][:, MXU_COLS * sub:MXU_COLS * (sub + 1)] = _qkv_activation(y, part)
    gb_ref[...] = _decay_and_beta(_dot(hn, wab_ref[...]), alog_ref, dtb_ref, n_heads)
    z_ref[...] = _dot(hn, wz_ref[...])


def _weave(*stages):
    live = list(stages)
    while live:
        for g in list(live):
            try:
                next(g)
            except StopIteration:
                live.remove(g)


def _mixer_a_prompt_body(x_ref, g_ref, wqkv_ref, wab_ref, wz_ref, wconv_ref, alog_ref, dtb_ref,
                         z_ref, tail_ref, o_ref, sout_ref, cbuf, s_scr, *bufs, n_heads, n_chunks):
    step = pl.program_id(0)
    n_proj = 4
    proj_sets = bufs[:n_proj], bufs[n_proj:2 * n_proj]
    rest = bufs[2 * n_proj:]
    prep_sets = rest[:len(rest) // 2], rest[len(rest) // 2:]

    @pl.when(step == 0)
    def _():
        cbuf[...] = jnp.zeros(cbuf.shape, F32)
        s_scr[...] = jnp.zeros(s_scr.shape, F32)
        for ref in (*proj_sets[1], *prep_sets[0]):
            ref[...] = jnp.zeros(ref.shape, ref.dtype)

    sub = n_chunks // 2 * CHUNK

    def project(i, q_s, k_s, v_s, gb_s):
        rows = pl.ds(pl.multiple_of(i * sub, sub), sub)
        hn = _bf(_rms(x_ref[rows, :]) * g_ref[...])
        outs = (q_s, k_s, v_s)
        per_part = q_s.shape[1] // MXU_COLS
        for c in range(3 * per_part):
            cols = slice(MXU_COLS * c, MXU_COLS * (c + 1))
            pq = _dot(hn, wqkv_ref[:, cols])
            y = _causal_conv_tile(pq, wconv_ref[:, cols], cbuf, cols)
            tail_ref[:, cols] = pq[sub - SUBLANES:, :]
            part, sc = divmod(c, per_part)
            outs[part][rows, MXU_COLS * sc:MXU_COLS * (sc + 1)] = _qkv_activation(y, part)
            yield
        gb_s[rows, :] = _decay_and_beta(_dot(hn, wab_ref[...]), alog_ref, dtb_ref, n_heads)
        z_ref[rows, :] = _dot(hn, wz_ref[...])
        yield

    group = MXU_COLS // CHUNK
    shift = CHUNK.bit_length() - 1
    lane_p = lax.broadcasted_iota(jnp.int32, (CHUNK, MXU_COLS), 1)
    row_p = lax.broadcasted_iota(jnp.int32, (CHUNK, MXU_COLS), 0)
    col_p = lane_p & (CHUNK - 1)
    blk = [(lane_p >> shift) == u for u in range(group)]
    incl_p = row_p >= col_p
    strict_p = row_p > col_p
    eye_p = jnp.where(row_p == col_p, 1.0, 0.0).astype(F32)
    bd_r = lax.broadcasted_iota(jnp.int32, (MXU_COLS, MXU_COLS), 0)
    bd_c = lax.broadcasted_iota(jnp.int32, (MXU_COLS, MXU_COLS), 1)
    bd_mask = (bd_r >> shift) == (bd_c >> shift)
    ri = lax.broadcasted_iota(jnp.int32, (CHUNK, CHUNK), 0)
    ci = lax.broadcasted_iota(jnp.int32, (CHUNK, CHUNK), 1)
    ltri3 = jnp.concatenate([_bf(jnp.where(ri >= ci, 1.0, 0.0))] * 3, axis=1)
    g_r = lax.broadcasted_iota(jnp.int32, (MXU_COLS, 3 * LANES), 0) >> shift
    g_c = lax.broadcasted_iota(jnp.int32, (MXU_COLS, 3 * LANES), 1) & (LANES - 1)
    ones3 = jnp.ones((CHUNK, 3 * LANES), BF16)
    er = lax.broadcasted_iota(jnp.int32, (HEAD, HEAD), 0)
    ec = lax.broadcasted_iota(jnp.int32, (HEAD, HEAD), 1)
    eye_h = _bf(jnp.where(er == ec, 1.0, 0.0))
    heads = range(n_heads)
    hs = [slice(HEAD * h, HEAD * (h + 1)) for h in heads]

    def block_diag(x):
        return _bf(jnp.where(bd_mask, jnp.concatenate([x] * group, axis=0), 0.0))

    def pack_cols(cols):
        out = cols[-1]
        for u in reversed(range(group - 1)):
            out = jnp.where(blk[u], cols[u], out)
        return out

    wide = group * HEAD
    kd_r = lax.broadcasted_iota(jnp.int32, (MXU_COLS, wide), 0) >> shift
    kd_c = lax.broadcasted_iota(jnp.int32, (MXU_COLS, wide), 1) >> (HEAD.bit_length() - 1)
    kd_mask = kd_r == kd_c

    def key_blocks(kc):
        return _bf(jnp.where(kd_mask, jnp.concatenate([kc] * group, axis=0), 0.0))

    def stack(ref, rows, h0):
        return jnp.concatenate([ref[rows, hs[h]] for h in range(h0, h0 + group)], axis=0)

    def stack_cols(x, lo):
        return jnp.concatenate([x[:, lo + u:lo + u + 1] for u in range(group)], axis=0)

    def prep(i, q_ref, k_ref, v_ref, gb_ref, u0_s, wkqg_s, qk_s, kdt_s, gl_s):
        chunks = [i * PREP_CHUNKS + j for j in range(PREP_CHUNKS)]
        rows = [pl.ds(pl.multiple_of(c * CHUNK, CHUNK), CHUNK) for c in chunks]
        gbc = [gb_ref[r, :] for r in rows]
        gc = [_dot(ltri3, _bf(jnp.concatenate(_split3f(x), axis=0))) for x in gbc]
        glast = [x[CHUNK - 1:CHUNK, :] for x in gc]
        for j, c in enumerate(chunks):
            gl_s[c] = jnp.broadcast_to(glast[j], (SUBLANES, LANES))
        groups = [(j, h0) for j in range(PREP_CHUNKS) for h0 in range(0, n_heads, group)]
        ng = range(len(groups))
        kc = [k_ref[rows[j], HEAD * h0:HEAD * (h0 + group)] for j, h0 in groups]
        qc = [q_ref[rows[j], HEAD * h0:HEAD * (h0 + group)] for j, h0 in groups]
        pairs = [_dot_nt(_bf(jnp.concatenate([qc[g], kc[g]], axis=0)), key_blocks(kc[g])) for g in ng]
        yield
        gcol_p = [pack_cols([gc[j][:, h:h + 1] for h in range(h0, h0 + group)]) for j, h0 in groups]
        bcol_p = [pack_cols([gbc[j][:, n_heads + h:n_heads + h + 1] for h in range(h0, h0 + group)])
                  for j, h0 in groups]
        grow = [_dot_nt(ones3, _bf(jnp.where(g_c == g_r + h0,
                                               jnp.concatenate([jnp.concatenate([x] * group, axis=0)
                                                                for x in _split3f(gc[j])], axis=1), 0.0)))
                for j, h0 in groups]
        yield
        dec = [jnp.exp(jnp.where(incl_p, gcol_p[g] - grow[g], -jnp.inf)) for g in ng]
        a = [jnp.where(strict_p, bcol_p[g] * pairs[g][CHUNK:] * dec[g], 0.0) for g in ng]
        for g, (j, h0) in enumerate(groups):
            qk = _bf(pairs[g][:CHUNK] * dec[g])
            for u in range(group):
                qk_s[chunks[j], h0 + u] = qk[:, CHUNK * u:CHUNK * (u + 1)]
        tinv = [eye_p - x for x in a]
        pw = [_dot(_bf(-x), block_diag(-x)) for x in a]
        yield
        for _ in range(4):
            res = [_dot(jnp.concatenate([_bf(tinv[g]), _bf(pw[g])], axis=0), block_diag(pw[g])) for g in ng]
            tinv = [tinv[g] + res[g][:CHUNK] for g in ng]
            pw = [r[CHUNK:] for r in res]
            yield
        tinv = [tinv[g] + _dot(_bf(tinv[g]), block_diag(pw[g])) for g in ng]
        yield
        tbd = [block_diag(x) for x in tinv]
        kst = [stack(k_ref, rows[j], h0) for j, h0 in groups]
        gcol_s = [stack_cols(gc[j], h0) for j, h0 in groups]
        bcol_s = [stack_cols(gbc[j], n_heads + h0) for j, h0 in groups]
        gam_s = [jnp.exp(x) for x in gcol_s]
        rhs = [jnp.concatenate([stack(v_ref, rows[j], h0) * bcol_s[g], kst[g] * (bcol_s[g] * gam_s[g])], axis=1)
               for g, (j, h0) in enumerate(groups)]
        sol = [_dot(tbd[g], _bf(rhs[g])) for g in ng]
        yield
        kdt = [_dot_nt(eye_h, _bf(kst[g] * jnp.exp(stack_cols(jnp.broadcast_to(glast[j], (CHUNK, LANES)), h0)
                                                   - gcol_s[g])))
               for g, (j, h0) in enumerate(groups)]
        for g, (j, h0) in enumerate(groups):
            c = chunks[j]
            qg = stack(q_ref, rows[j], h0) * gam_s[g]
            for u in range(group):
                r = slice(CHUNK * u, CHUNK * (u + 1))
                u0_s[c, h0 + u] = sol[g][r, :HEAD]
                wkqg_s[c, h0 + u] = _bf(jnp.concatenate([sol[g][r, HEAD:], qg[r]], axis=0))
                kdt_s[c, h0 + u] = _bf(kdt[g][:, r])

    def scan(i, u0_s, wkqg_s, qk_s, kdt_s, gl_s):
        for j in range(PREP_CHUNKS):
            c = i * PREP_CHUNKS + j
            rows = pl.ds(pl.multiple_of(c * CHUNK, CHUNK), CHUNK)
            glr = gl_s[c]
            ws = [_dot(wkqg_s[c, h], _bf(s_scr[h])) for h in heads]
            yield
            ub = [_bf(u0_s[c, h] - ws[h][:CHUNK]) for h in heads]
            for h in heads:
                o_ref[rows, HEAD * h:HEAD * (h + 1)] = ws[h][CHUNK:] + _dot(qk_s[c, h], ub[h])
            yield
            for h in heads:
                s_scr[h] = jnp.exp(glr[0:1, h:h + 1]) * s_scr[h] + _dot(kdt_s[c, h], ub[h])
            yield

    def run(mine, other):
        def body(i, carry):
            _weave(project(i, *proj_sets[mine]), prep(i, *proj_sets[other], *prep_sets[other]),
                   scan(i, *prep_sets[mine]))
            return carry
        lax.fori_loop(0, n_chunks // PREP_CHUNKS, body, 0)

    @pl.when(step % 2 == 0)
    def _():
        run(0, 1)

    @pl.when(step % 2 == 1)
    def _():
        run(1, 0)

    sout_ref[...] = s_scr[...]


def _mixer_a_prompt(x, w, n_heads):
    t, d = x.shape
    aw = w["w_z"].shape[1]
    n_chunks = 2 * PREP_CHUNKS
    tt = n_chunks * CHUNK
    assert t % tt == 0 and n_chunks % PREP_CHUNKS == 0
    n_tiles = t // tt
    projected = [pltpu.VMEM((tt, aw), F32)] * 3 + [pltpu.VMEM((tt, LANES), F32)]
    prepared = [pltpu.VMEM((n_chunks, n_heads, CHUNK, HEAD), F32),
                pltpu.VMEM((n_chunks, n_heads, 2 * CHUNK, HEAD), BF16),
                pltpu.VMEM((n_chunks, n_heads, CHUNK, CHUNK), BF16),
                pltpu.VMEM((n_chunks, n_heads, HEAD, CHUNK), BF16),
                pltpu.VMEM((n_chunks, SUBLANES, LANES), F32)]
    scratch = [pltpu.VMEM((SUBLANES, 3 * aw), F32), pltpu.VMEM((n_heads, HEAD, HEAD), F32)]
    scratch += projected + projected + prepared + prepared
    first = lambda s: (jnp.minimum(s, n_tiles - 1), 0)
    last = lambda s: (jnp.maximum(s - 2, 0), 0)
    ins = [(x, pl.BlockSpec((tt, d), first))] + [_whole(w[name]) for name in
                                                  ("g_mix0", "w_qkv", "w_ab", "w_z", "w_aconv", "a_log", "dt_bias")]
    outs = [(jax.ShapeDtypeStruct((t, aw), F32), pl.BlockSpec((tt, aw), first)), _out_whole((SUBLANES, 3 * aw)),
            (jax.ShapeDtypeStruct((t, aw), F32), pl.BlockSpec((tt, aw), last)), _out_whole((n_heads, HEAD, HEAD))]
    z, tail, o, s1 = _call(functools.partial(_mixer_a_prompt_body, n_heads=n_heads, n_chunks=n_chunks),
                           "mixer_a_prompt", n_tiles + 2, ins, outs, scratch)
    return o, z, s1, tail


def _delta_step_body(qt_ref, kt_ref, v_ref, gbt_ref, s_ref, o_ref, so_ref, *, n_seq, n_heads):
    h = pl.program_id(0)
    decay = jnp.exp(gbt_ref[pl.ds(h, 1), :])
    beta = gbt_ref[pl.ds(n_heads + h, 1), :]
    for b in range(n_seq):
        s = decay[:, b:b + 1] * s_ref[b]
        kc = kt_ref[:, b:b + 1]
        u = beta[:, b:b + 1] * (v_ref[b:b + 1, :] - jnp.sum(kc * s, axis=0, keepdims=True))
        s = s + kc * u
        so_ref[b] = s
        o_ref[b:b + 1, :] = jnp.sum(qt_ref[:, b:b + 1] * s, axis=0, keepdims=True)


def _delta_step(q, k, v, gb, state, n_heads):
    n_seq = q.shape[0]
    col = pl.BlockSpec((HEAD, n_seq), lambda h: (h, 0))
    row = pl.BlockSpec((n_seq, HEAD), lambda h: (0, h))
    st = pl.BlockSpec((n_seq, None, HEAD, HEAD), lambda h: (0, h, 0, 0))
    gbt = gb.T
    return pl.pallas_call(
        functools.partial(_delta_step_body, n_seq=n_seq, n_heads=n_heads),
        out_shape=[jax.ShapeDtypeStruct(q.shape, F32), jax.ShapeDtypeStruct(state.shape, F32)],
        grid=(n_heads,), in_specs=[col, col, row, _const_spec(gbt), st], out_specs=[row, st],
        compiler_params=_params(), name="delta_step")(q.T, k.T, v, gbt, state)


def _delta_out_mix(o_ref, z_ref, res_ref, gout_ref, w_ref):
    parts = []
    for h in range(o_ref.shape[1] // HEAD):
        hs = slice(HEAD * h, HEAD * (h + 1))
        parts.append(_bf(_rms(o_ref[:, hs]) * gout_ref[...] * _silu(z_ref[:, hs])))
    return res_ref[...] + _dot(jnp.concatenate(parts, axis=1), w_ref[...])


def _attn_out_mix(refs, scratch, tm, dils):
    ng = len(dils)
    o_refs, l_refs = refs[:ng], refs[ng:2 * ng]
    res_ref, w_ref = refs[2 * ng:]
    get_o, lses = [], []
    for gi, dil in enumerate(dils):
        if dil == 1:
            get_o.append(lambda hh, r=o_refs[gi]: r[0, :, HEAD * hh:HEAD * (hh + 1)])
            lses.append(l_refs[gi][0])
        else:
            o_s, l_s = scratch.pop(0), scratch.pop(0)
            for r in range(dil):
                dst = pl.ds(r, tm // dil, stride=dil)
                for hh in range(Q_PER_GROUP):
                    o_s.at[hh][dst, :] = o_refs[gi][r, :, HEAD * hh:HEAD * (hh + 1)]
                l_s[dst, :] = l_refs[gi][r]
            get_o.append(lambda hh, r=o_s: r[hh])
            lses.append(l_s[...])
    parts = []
    for hh in range(Q_PER_GROUP):
        lh = [l[:, hh:hh + 1] for l in lses]
        mx = functools.reduce(jnp.maximum, lh)
        ex = [jnp.exp(x - mx) for x in lh]
        den = functools.reduce(lambda a, b: a + b, ex)
        acc = (ex[0] / den) * get_o[0](hh)
        for gi in range(1, ng):
            acc = acc + (ex[gi] / den) * get_o[gi](hh)
        parts.append(_bf(acc))
    return res_ref[...] + _dot(jnp.concatenate(parts, axis=1), w_ref[...])


def _ffn_stages(h, p_ref, buf_ref, gffn_ref, wup_ref, wconv_ref, wdown_ref, gple_ref, wgate_ref, wproj_ref,
                tail_ref, ubuf, act_s, result, *, tm):
    hn = _bf(_rms(h) * gffn_ref[...])
    dff = wdown_ref.shape[0]
    for c in range(dff // MXU_COLS):
        ys = []
        for half in range(2):
            lo = half * dff + MXU_COLS * c
            cols = slice(lo, lo + MXU_COLS)
            u = _dot(hn, wup_ref[:, cols])
            w = wconv_ref[:, cols]
            if buf_ref is not None:
                y = buf_ref[:, cols] * w[0:1]
                y = y + buf_ref[:, 2 * dff + lo:2 * dff + lo + MXU_COLS] * w[1:2]
                y = y + u * w[2:3]
                tail_ref[:, cols] = u
            else:
                y = _causal_conv_tile(u, w, ubuf, cols)
                tail_ref[:, cols] = u[tm - SUBLANES:, :]
            ys.append(y)
        act_s[:, MXU_COLS * c:MXU_COLS * (c + 1)] = _bf(_silu(ys[0]) * ys[1])
        yield
    h2 = h + _dot(act_s[...], wdown_ref[...])
    yield
    gate = _sigmoid(_dot(_bf(_rms(h2) * gple_ref[...]), wgate_ref[...]))
    result.append(h2 + gate * _dot(_bf(p_ref[...]), wproj_ref[...]))
    yield


def _mix_ffn_body(*refs, tm, per_row_state, dils):
    n_mix = 5 if dils is None else 2 * len(dils) + 2
    mix_in, refs = refs[:n_mix], refs[n_mix:]
    p_ref, refs = refs[0], refs[1:]
    buf_ref = None
    if per_row_state:
        buf_ref, refs = refs[0], refs[1:]
    ffn_w, (out_ref, tail_ref), scratch = refs[:7], refs[7:9], list(refs[9:])
    act_s = scratch.pop()
    ubuf = None
    if not per_row_state:
        ubuf = scratch.pop()

        @pl.when(pl.program_id(0) == 0)
        def _():
            ubuf[...] = jnp.zeros(ubuf.shape, F32)

    h = _delta_out_mix(*mix_in) if dils is None else _attn_out_mix(mix_in, scratch, tm, dils)
    result = []
    for _ in _ffn_stages(h, p_ref, buf_ref, *ffn_w, tail_ref, ubuf, act_s, result, tm=tm):
        pass
    out_ref[...] = result[0]


def _mix_ffn(mix_ins, mix_scratch, dils, p, w, layer, m, tm, state):
    per_row = state is not None
    d, dff = w["w_down"].shape[2], w["w_down"].shape[1]
    ins = list(mix_ins) + [_layer_rows(p, layer, tm)] + ([_rows(state.reshape(m, -1), tm)] if per_row else [])
    ins += [_layer(w[name], layer) for name in ("g_ffn", "w_up", "w_fconv", "w_down", "g_ple", "w_gate", "w_proj")]
    outs = [_out_rows(m, d, tm), _out_whole((m if per_row else SUBLANES, 2 * dff))]
    scratch = list(mix_scratch) + ([] if per_row else [pltpu.VMEM((SUBLANES, 2 * dff), F32)])
    scratch.append(pltpu.VMEM((tm, dff), BF16))
    return _call(functools.partial(_mix_ffn_body, tm=tm, per_row_state=per_row, dils=dils), f"mix_ffn{layer}",
                 m // tm, ins, outs, scratch)


def _kvq_stages(h, tab_ref, gkv_ref, gmix_ref, wkv_ref, wq_ref, gk_ref, gq_ref, q_ref, k_ref, v_ref, qd, kd, vd,
                stage, *, tm, dils):
    base = _rms(h)
    hkv = _bf(base * gkv_ref[...])
    hq = _bf(base * gmix_ref[...])
    tab = tab_ref[...]
    lane = lax.broadcasted_iota(jnp.int32, tab.shape, 1)
    half = ROPE_DIM // 2
    cos = jnp.where(lane < half, tab, jnp.where(lane < ROPE_DIM, pltpu.roll(tab, half, 1), 1.0))
    sa = jnp.where((lane >= half) & (lane < ROPE_DIM), tab, 0.0)
    sb = jnp.where(lane < half, -pltpu.roll(tab, HEAD - half, 1), 0.0)
    kw = KV_PER_GROUP * HEAD

    def norm_rope(x, g):
        y = _rms(x) * g
        return y * cos + pltpu.roll(y, ROPE_DIM // 2, 1) * sa + pltpu.roll(y, HEAD - ROPE_DIM // 2, 1) * sb

    def heads(x, g):
        return jnp.concatenate([norm_rope(x[:, HEAD * j:HEAD * (j + 1)], g) for j in range(MXU_COLS // HEAD)], axis=1)

    def by_class(x, dst_ref, dil, cols):
        if dil == 1:
            dst_ref[0, :, cols] = _bf(x)
        else:
            for j in range(MXU_COLS // LANES):
                stage[j] = x[:, LANES * j:LANES * (j + 1)]
            for r in range(dil):
                for j in range(MXU_COLS // LANES):
                    lo = cols.start + LANES * j
                    dst_ref[r, :, lo:lo + LANES] = _bf(stage.at[j][pl.ds(r, tm // dil, stride=dil), :])

    assert kw == MXU_COLS and Q_PER_GROUP * HEAD == 2 * MXU_COLS
    n_kv = k_ref.shape[1]

    def project(kind, c):
        lo = MXU_COLS * c + (n_kv if kind == "v" else 0)
        return _dot(hq if kind == "q" else hkv, (wq_ref if kind == "q" else wkv_ref)[:, lo:lo + MXU_COLS])

    def finish(kind, c, x):
        cols = slice(MXU_COLS * c, MXU_COLS * (c + 1))
        if kind == "k":
            x = heads(x, gk_ref[...])
            k_ref[:, cols] = x
        elif kind == "v":
            v_ref[:, cols] = x
        else:
            x = heads(x, gq_ref[...])
            if dils is None:
                q_ref[:, cols] = x
        if dils is not None:
            if kind == "q":
                by_class(x, qd[c // 2], dils[c // 2], slice(MXU_COLS * (c % 2), MXU_COLS * (c % 2 + 1)))
            else:
                by_class(x, (kd if kind == "k" else vd)[c], dils[c], slice(0, kw))

    yield
    for kind, n in (("k", n_kv), ("v", n_kv), ("q", wq_ref.shape[1])):
        for c in range(n // MXU_COLS):
            finish(kind, c, project(kind, c))
            yield


def _kvq_body(*refs, tm, dils):
    h_ref, consts, outs = refs[0], refs[1:8], refs[8:]
    if dils is None:
        q_ref, k_ref, v_ref = outs
        qd = kd = vd = stage = None
    else:
        ng = len(dils)
        q_ref, (k_ref, v_ref) = None, outs[:2]
        qd, kd, vd, stage = outs[2:2 + ng], outs[2 + ng:2 + 2 * ng], outs[2 + 2 * ng:2 + 3 * ng], outs[2 + 3 * ng]
    for _ in _kvq_stages(h_ref[...], *consts, q_ref, k_ref, v_ref, qd, kd, vd, stage, tm=tm, dils=dils):
        pass


def _kvq(h, table, w, tm, dils):
    m = h.shape[0]
    n_q, n_kv = w["w_q"].shape[1], w["w_kv"].shape[1] // 2
    qw, kw = Q_PER_GROUP * HEAD, KV_PER_GROUP * HEAD
    consts = [w["g_kv"], w["g_mix1"], w["w_kv"], w["w_q"], w["g_k"], w["g_q"]]
    rows = [h, table]
    row_spec = lambda c: pl.BlockSpec((tm, c), lambda i: (i, 0))
    in_specs = [row_spec(a.shape[1]) for a in rows] + [_const_spec(a) for a in consts]
    nat = lambda c: jax.ShapeDtypeStruct((m, c), F32)
    if dils is None:
        out_shape = [nat(n_q), nat(n_kv), nat(n_kv)]
        out_specs, scratch = [row_spec(n_q), row_spec(n_kv), row_spec(n_kv)], []
    else:
        out_shape, out_specs = [nat(n_kv), nat(n_kv)], [row_spec(n_kv), row_spec(n_kv)]
        for width in (qw, kw, kw):
            for dil in dils:
                assert tm % (dil * 2 * SUBLANES) == 0
                out_shape.append(jax.ShapeDtypeStruct((dil, m // dil, width), BF16))
                out_specs.append(pl.BlockSpec((dil, tm // dil, width), lambda i: (0, i, 0)))
        scratch = [pltpu.VMEM((MXU_COLS // LANES, tm, LANES), F32)]
    return pl.pallas_call(functools.partial(_kvq_body, tm=tm, dils=dils), out_shape=out_shape, grid=(m // tm,),
                          in_specs=in_specs, out_specs=out_specs, scratch_shapes=scratch,
                          compiler_params=_params(), name="kvq")(*rows, *consts)


def _rope_table(first, count, consecutive):
    half = ROPE_DIM // 2
    inv_freq = ROPE_THETA ** (-jnp.arange(half, dtype=F32) * 2.0 / ROPE_DIM)
    pad = jnp.zeros((count, HEAD - ROPE_DIM), F32)
    if not consecutive:
        ang = jnp.full((count, 1), first, F32) * inv_freq[None, :]
        return jnp.concatenate([jnp.cos(ang), jnp.sin(ang), pad], axis=1)
    assert count % LANES == 0
    a = (first + LANES * jnp.arange(count // LANES, dtype=F32))[:, None] * inv_freq[None, :]
    b = jnp.arange(LANES, dtype=F32)[:, None] * inv_freq[None, :]
    ca, sa, cb, sb = jnp.cos(a)[:, None], jnp.sin(a)[:, None], jnp.cos(b)[None], jnp.sin(b)[None]
    cos = (ca * cb - sa * sb).reshape(count, half)
    sin = (sa * cb + ca * sb).reshape(count, half)
    return jnp.concatenate([cos, sin, pad], axis=1)


def _attn_prompt_body(q_ref, kc_ref, kp_ref, vc_ref, vp_ref, o_ref, lse_ref, *, n, blocks):
    first_lo = jnp.where(pl.program_id(1) > 0, 0, n)
    qi = lax.broadcasted_iota(jnp.int32, (Q_REP * n, 2 * n), 0) & (n - 1)
    kj = lax.broadcasted_iota(jnp.int32, (Q_REP * n, 2 * n), 1)
    band = (kj >= qi) & (kj <= qi + n)
    lane = lax.broadcasted_iota(jnp.int32, (n, LANES), 1)
    for j in range(blocks):
        rows = slice(n * j, n * (j + 1))
        lse_tile = jnp.zeros((n, LANES), F32)
        for g in range(KV_PER_GROUP):
            hs = slice(HEAD * g, HEAD * (g + 1))
            if j == 0:
                kprev, vprev = kp_ref[:, hs], vp_ref[:, hs]
                valid = band & (kj >= first_lo)
            else:
                prev = slice(n * (j - 1), n * j)
                kprev, vprev = kc_ref[prev, hs], vc_ref[prev, hs]
                valid = band
            kwin = jnp.concatenate([kprev, kc_ref[rows, hs]], axis=0)
            vwin = jnp.concatenate([vprev, vc_ref[rows, hs]], axis=0)
            q0 = Q_REP * HEAD * g
            qg = jnp.concatenate([q_ref[rows, q0 + HEAD * e:q0 + HEAD * (e + 1)] for e in range(Q_REP)], axis=0)
            s = jnp.where(valid, _dot_nt(qg, kwin) * (HEAD ** -0.5), -jnp.inf)
            mx = jnp.max(s, axis=-1, keepdims=True)
            p = jnp.exp(s - mx)
            den = jnp.sum(p, axis=-1, keepdims=True)
            o = _dot(_bf(p), vwin) / den
            lse = mx + jnp.log(den)
            for e in range(Q_REP):
                o_ref[rows, q0 + HEAD * e:q0 + HEAD * (e + 1)] = o[n * e:n * (e + 1)]
                lse_tile = jnp.where(lane == Q_REP * g + e, lse[n * e:n * (e + 1)], lse_tile)
        lse_ref[rows, :] = lse_tile


def _attn_prompt(qd, kd, vd, gi, window):
    dil, ln, qw = qd.shape
    kw = kd.shape[2]
    n = window // dil
    blocks = min(ATTN_BLOCKS, ln // n)
    rows = n * blocks
    assert n == HEAD and ln % rows == 0
    cur = lambda r, b: (r, b, 0)
    prev = lambda r, b: (r, jnp.maximum(b * blocks - 1, 0), 0)
    return pl.pallas_call(
        functools.partial(_attn_prompt_body, n=n, blocks=blocks),
        out_shape=[jax.ShapeDtypeStruct((dil, ln, qw), F32), jax.ShapeDtypeStruct((dil, ln, LANES), F32)],
        grid=(dil, ln // rows),
        in_specs=[pl.BlockSpec((None, rows, qw), cur), pl.BlockSpec((None, rows, kw), cur),
                  pl.BlockSpec((None, n, kw), prev), pl.BlockSpec((None, rows, kw), cur),
                  pl.BlockSpec((None, n, kw), prev)],
        out_specs=[pl.BlockSpec((None, rows, qw), cur), pl.BlockSpec((None, rows, LANES), cur)],
        compiler_params=_params(2), name=f"attn_prompt{gi}")(qd, kd, kd, vd, vd)


def _attn_sample_body(q_ref, kn_ref, vn_ref, *refs, bb):
    n_groups = len(GROUPS)
    caches, outs, lses = refs[:n_groups], refs[n_groups:2 * n_groups], refs[2 * n_groups:]
    lane = lax.broadcasted_iota(jnp.int32, (1, LANES), 1)
    kw = KV_PER_GROUP * HEAD
    units = [(gi, b, g) for gi in range(n_groups) for b in range(bb) for g in range(KV_PER_GROUP)]
    scale = HEAD ** -0.5
    qm, kn, vn = [], [], []
    for gi, b, g in units:
        q0 = Q_PER_GROUP * HEAD * gi + Q_REP * HEAD * g
        qm.append(jnp.concatenate([q_ref[b:b + 1, q0 + HEAD * e:q0 + HEAD * (e + 1)] for e in range(Q_REP)]
                                  + [jnp.zeros((SUBLANES - Q_REP, HEAD), F32)], axis=0))
        kn.append(kn_ref[b:b + 1, kw * gi + HEAD * g:kw * gi + HEAD * (g + 1)])
        vn.append(vn_ref[b:b + 1, kw * gi + HEAD * g:kw * gi + HEAD * (g + 1)])
    n = range(len(units))
    s = [_dot_nt(_bf(qm[u]), _bf(caches[gi][b, :, g, :])) * scale for u, (gi, b, g) in enumerate(units)]
    sn = [jnp.sum(qm[u] * kn[u], axis=1, keepdims=True) * scale for u in n]
    mx = [jnp.maximum(jnp.max(s[u], axis=1, keepdims=True), sn[u]) for u in n]
    p = [jnp.exp(s[u] - mx[u]) for u in n]
    pn = [jnp.exp(sn[u] - mx[u]) for u in n]
    den = [jnp.sum(p[u], axis=1, keepdims=True) + pn[u] for u in n]
    pv = [_dot(_bf(p[u]), _bf(caches[gi][b, :, KV_PER_GROUP + g, :])) for u, (gi, b, g) in enumerate(units)]
    lrows = {}
    for u, (gi, b, g) in enumerate(units):
        o = (pv[u] + pn[u] * vn[u]) / den[u]
        lse = mx[u] + jnp.log(den[u])
        for e in range(Q_REP):
            oc = HEAD * (Q_REP * g + e)
            outs[gi][b:b + 1, oc:oc + HEAD] = o[e:e + 1]
            lrows[gi, b] = jnp.where(lane == Q_REP * g + e, lse[e:e + 1], lrows.get((gi, b), jnp.zeros((1, LANES), F32)))
    for (gi, b), lrow in lrows.items():
        lses[gi][b:b + 1, :] = lrow


def _attn_sample(q, k, v, caches):
    n_seq = q.shape[0]
    bb = SAMPLE_SEQS
    assert n_seq % bb == 0
    qw = Q_PER_GROUP * HEAD
    views, cache_specs = [], []
    for cache, (window, dil) in zip(caches, GROUPS):
        n = window // dil
        assert cache.shape[1] == window and n == HEAD
        views.append(cache.reshape(n_seq, n, dil, 2 * KV_PER_GROUP, HEAD))
        cache_specs.append(pl.BlockSpec((bb, n, None, 2 * KV_PER_GROUP, HEAD), lambda i: (i, 0, 0, 0, 0)))
    row = lambda c: pl.BlockSpec((bb, c), lambda i: (i, 0))
    n_groups = len(GROUPS)
    res = pl.pallas_call(
        functools.partial(_attn_sample_body, bb=bb),
        out_shape=[jax.ShapeDtypeStruct((n_seq, qw), F32)] * n_groups
        + [jax.ShapeDtypeStruct((n_seq, LANES), F32)] * n_groups,
        grid=(n_seq // bb,),
        in_specs=[row(q.shape[1]), row(k.shape[1]), row(v.shape[1])] + cache_specs,
        out_specs=[row(qw)] * n_groups + [row(LANES)] * n_groups,
        compiler_params=_params(), name="attn_sample")(q, k, v, *views)
    return res[:n_groups], res[n_groups:]


def _ffn0_kvq_body(*refs, tm, dils, n_tiles):
    mix_in, p_ref, ffn_w, tab_ref, kvq_w = refs[:5], refs[5], refs[6:13], refs[13], refs[14:20]
    out_ref, tail_ref, k_ref, v_ref = refs[20:24]
    ng = len(dils)
    qd, kd, vd = refs[24:24 + ng], refs[24 + ng:24 + 2 * ng], refs[24 + 2 * ng:24 + 3 * ng]
    ubuf, act_s, tail_s, hbuf, stage = refs[24 + 3 * ng:]
    step = pl.program_id(0)

    @pl.when(step == 0)
    def _():
        ubuf[...] = jnp.zeros(ubuf.shape, F32)
        hbuf[...] = jnp.zeros(hbuf.shape, F32)

    h_prev = hbuf[...]
    result = []
    _weave(_kvq_stages(h_prev, tab_ref, *kvq_w, None, k_ref, v_ref, qd, kd, vd, stage, tm=tm, dils=dils),
           _ffn_stages(_delta_out_mix(*mix_in), p_ref, None, *ffn_w, tail_s, ubuf, act_s, result, tm=tm))
    hbuf[...] = result[0]

    @pl.when(step < n_tiles)
    def _():
        out_ref[...] = result[0]
        tail_ref[...] = tail_s[...]


def _ffn0_kvq(o, z, x, p, table, w, tm, dils):
    m, d = x.shape
    dff = w["w_down"].shape[1]
    n_kv = w["w_kv"].shape[1] // 2
    qw, kw = Q_PER_GROUP * HEAD, KV_PER_GROUP * HEAD
    assert m % tm == 0
    n_tiles = m // tm
    cur = lambda s: (jnp.minimum(s, n_tiles - 1), 0)
    prev = lambda s: (jnp.maximum(s - 1, 0), 0)
    at = lambda a, imap: (a, pl.BlockSpec((tm, a.shape[1]), imap))
    ins = [at(o, cur), at(z, cur), at(x, cur), _whole(w["g_aout"]), _whole(w["w_aout"]),
           (p, pl.BlockSpec((None, tm, p.shape[2]), lambda s: (0, jnp.minimum(s, n_tiles - 1), 0)))]
    ins += [_layer(w[name], 0) for name in ("g_ffn", "w_up", "w_fconv", "w_down", "g_ple", "w_gate", "w_proj")]
    ins += [at(table, prev)] + [_whole(w[name]) for name in ("g_kv", "g_mix1", "w_kv", "w_q", "g_k", "g_q")]
    nat = lambda c, imap: (jax.ShapeDtypeStruct((m, c), F32), pl.BlockSpec((tm, c), imap))
    outs = [nat(d, cur), _out_whole((SUBLANES, 2 * dff)), nat(n_kv, prev), nat(n_kv, prev)]
    for width in (qw, kw, kw):
        for dil in dils:
            assert tm % (dil * 2 * SUBLANES) == 0
            outs.append((jax.ShapeDtypeStruct((dil, m // dil, width), BF16),
                         pl.BlockSpec((dil, tm // dil, width), lambda s: (0, jnp.maximum(s - 1, 0), 0))))
    scratch = [pltpu.VMEM((SUBLANES, 2 * dff), F32), pltpu.VMEM((tm, dff), BF16), pltpu.VMEM((SUBLANES, 2 * dff), F32),
               pltpu.VMEM((tm, d), F32), pltpu.VMEM((MXU_COLS // LANES, tm, LANES), F32)]
    return _call(functools.partial(_ffn0_kvq_body, tm=tm, dils=dils, n_tiles=n_tiles), "ffn0_kvq", n_tiles + 1,
                 ins, outs, scratch)


def _trunk(x, p, first_pos, state, w, tm, tm_wide):
    m = x.shape[0]
    per_row = state is not None
    n_heads = w["n_heads"]
    aw = w["w_z"].shape[1]

    if per_row:
        consts = [w["g_mix0"], w["w_qkv"], w["w_ab"], w["w_z"], w["w_aconv"], w["a_log"], w["dt_bias"],
                  jnp.swapaxes(state["qkv_conv"], 0, 1)]
        q, k, v, z, gb, qkv_tail = _call(
            functools.partial(_a_in_step_body, n_heads=n_heads), "a_in_step", m // tm,
            [_rows(x, tm)] + [_whole(c) for c in consts],
            [_out_rows(m, aw, tm)] * 4 + [_out_rows(m, LANES, tm), _out_whole((m, 3 * aw))])
        o, s1 = _delta_step(q, k, v, gb, state["delta"], n_heads)
    else:
        o, z, s1, qkv_tail = _mixer_a_prompt(x, w, n_heads)
    table = _rope_table(first_pos, m, consecutive=not per_row)
    if per_row:
        mix = [_rows(o, tm_wide), _rows(z, tm_wide), _rows(x, tm_wide), _whole(w["g_aout"]), _whole(w["w_aout"])]
        h, ffn_tail0 = _mix_ffn(mix, [], None, p, w, 0, m, tm_wide, state["ffn_conv"][0])
        qa, ka, va = _kvq(h, table, w, tm, None)
        outs, lses = _attn_sample(qa, ka, va, state["caches"])
        outs, lses = [a[None] for a in outs], [a[None] for a in lses]
        dils = (1,) * len(GROUPS)
    else:
        dils = tuple(dil for _, dil in GROUPS)
        ng = len(dils)
        res = _ffn0_kvq(o, z, x, p, table, w, tm, dils)
        h, ffn_tail0, ka, va = res[:4]
        qd, kd, vd = res[4:4 + ng], res[4 + ng:4 + 2 * ng], res[4 + 2 * ng:]
        att = [_attn_prompt(qd[gi], kd[gi], vd[gi], gi, window) for gi, (window, _) in enumerate(GROUPS)]
        outs, lses = [a[0] for a in att], [a[1] for a in att]
    mix = [_grouped(a, tm_wide) for a in (*outs, *lses)] + [_rows(h, tm_wide), _whole(w["w_o"])]
    mix_scratch = []
    for dil in dils:
        if dil > 1:
            assert tm_wide % (dil * SUBLANES) == 0
            mix_scratch += [pltpu.VMEM((Q_PER_GROUP, tm_wide, HEAD), F32), pltpu.VMEM((tm_wide, LANES), F32)]
    h, ffn_tail1 = _mix_ffn(mix, mix_scratch, dils, p, w, 1, m, tm_wide, state["ffn_conv"][1] if per_row else None)
    return h, s1, qkv_tail, (ffn_tail0, ffn_tail1), ka, va


def _kv_rows(k, v, gi, n_rows):
    kw = KV_PER_GROUP * HEAD
    sl = slice(kw * gi, kw * (gi + 1))
    shape = (n_rows, 1, KV_PER_GROUP, HEAD)
    return jnp.concatenate([k[k.shape[0] - n_rows:, sl].reshape(shape), v[v.shape[0] - n_rows:, sl].reshape(shape)],
                           axis=1)


def kernel(x_prompt, x_sample, p_prompt, p_sample, state_delta, state_qkv_conv, state_ffn_conv, cache_kv_w128, cache_kv_w512, cache_kv_w2048, g_mix_norm, g_ffn_norm, w_ffn_up, w_ffn_conv, w_ffn_down, g_ple_norm, w_ple_gate, w_ple_proj, w_a_in, w_a_conv, a_log, a_dt_bias, g_a_out_norm, w_a_out, g_kv_norm, w_kv, g_k_norm, w_q, g_q_norm, w_o):
    n_heads = a_log.shape[1]
    aw = n_heads * HEAD
    assert w_a_in.shape[0] == 1 and w_q.shape[0] == 1 and w_a_in.shape[2] == 4 * aw + 2 * n_heads
    row = lambda a: a.reshape(1, -1)
    lane_pad = lambda a: jnp.pad(a, ((0, 0), (0, LANES - a.shape[1])))
    w = {
        "n_heads": n_heads,
        "g_mix0": row(g_mix_norm[0]), "g_mix1": row(g_mix_norm[1]), "g_ffn": g_ffn_norm[:, None], "g_ple": g_ple_norm[:, None],
        "w_up": _bf(w_ffn_up), "w_fconv": w_ffn_conv, "w_down": _bf(w_ffn_down),
        "w_gate": _bf(w_ple_gate), "w_proj": _bf(w_ple_proj),
        "w_qkv": _bf(w_a_in[0][:, :3 * aw]), "w_ab": _bf(lane_pad(w_a_in[0][:, 3 * aw:3 * aw + 2 * n_heads])),
        "w_z": _bf(w_a_in[0][:, 3 * aw + 2 * n_heads:]), "w_aconv": w_a_conv[0],
        "a_log": lane_pad(a_log), "dt_bias": lane_pad(a_dt_bias),
        "g_aout": row(g_a_out_norm[0]), "w_aout": _bf(w_a_out[0]),
        "g_kv": row(g_kv_norm), "w_kv": _bf(w_kv), "g_k": row(g_k_norm),
        "w_q": _bf(w_q[0]), "g_q": row(g_q_norm[0]), "w_o": _bf(w_o[0]),
    }
    assert x_prompt.shape[0] == 1 and x_sample.shape[1] == 1
    seq, d = x_prompt.shape[1], x_prompt.shape[2]
    n_seq = x_sample.shape[0]

    yp, sd_p, qkv_tail_p, ffn_tails_p, kp, vp = _trunk(
        x_prompt.reshape(seq, d), p_prompt.reshape(p_prompt.shape[0], seq, -1), 0, None, w,
        tm=ROW_TILE, tm_wide=WIDE_ROW_TILE)
    a_keep, f_keep = state_qkv_conv.shape[2], state_ffn_conv.shape[2]
    sq_p = qkv_tail_p[SUBLANES - a_keep:][None, None]
    sf_p = jnp.stack([t[SUBLANES - f_keep:] for t in ffn_tails_p])[:, None]
    kv_p = [_kv_rows(kp, vp, gi, min(window, seq))[None] for gi, (window, _) in enumerate(GROUPS)]

    state = {"delta": state_delta[0], "qkv_conv": state_qkv_conv[0], "ffn_conv": state_ffn_conv,
             "caches": (cache_kv_w128, cache_kv_w512, cache_kv_w2048)}
    ys, sd_s, qkv_tail_s, ffn_tails_s, ks, vs = _trunk(
        x_sample.reshape(n_seq, d), p_sample.reshape(p_sample.shape[0], n_seq, -1), PAST_LEN,
        state, w, tm=n_seq, tm_wide=n_seq)
    sq_s = jnp.concatenate([state_qkv_conv[0][:, 1:], qkv_tail_s[:, None]], axis=1)[None]
    sf_s = jnp.stack([jnp.concatenate([state_ffn_conv[i][:, 1:], ffn_tails_s[i][:, None]], axis=1)
                      for i in range(2)])
    kv_s = [_kv_rows(ks, vs, gi, n_seq).reshape(n_seq, 1, 2, KV_PER_GROUP, HEAD) for gi in range(len(GROUPS))]

    return (yp[None], ys[:, None], sd_p[None, None], sq_p, sf_p, kv_p[0], kv_p[1], kv_p[2],
            sd_s[None], sq_s, sf_s, kv_s[0], kv_s[1], kv_s[2])
```

```python
import functools

import jax
import jax.numpy as jnp
from jax import lax
from jax.experimental import pallas as pl
from jax.experimental.pallas import tpu as pltpu

F32 = jnp.float32
BF16 = jnp.bfloat16

EPS = 1e-6
HEAD = 128
CHUNK = 64
GROUPS = ((128, 1), (512, 4), (2048, 16))
KV_PER_GROUP = 2
Q_REP = 2
Q_PER_GROUP = KV_PER_GROUP * Q_REP
ROPE_DIM = HEAD // 4
ROPE_THETA = 500000.0
PAST_LEN = 16384

LANES = 128
SUBLANES = 8
MXU_COLS = 256
VMEM_LIMIT_BYTES = 56 * 1024 * 1024

ROW_TILE = 256
WIDE_ROW_TILE = 512
PREP_CHUNKS = 4
ATTN_BLOCKS = 16
SAMPLE_SEQS = SUBLANES


def _dot(a, b):
    return jnp.dot(a, b, preferred_element_type=F32)


def _dot_nt(a, b):
    return lax.dot_general(a, b, (((1,), (1,)), ((), ())), preferred_element_type=F32)


def _bf(x):
    return x.astype(BF16)


def _rms(x):
    return x * lax.rsqrt(jnp.mean(x * x, axis=-1, keepdims=True) + EPS)


def _sigmoid(x):
    return 1.0 / (1.0 + jnp.exp(-x))


def _silu(x):
    return x * _sigmoid(x)


def _split3f(x):
    hi = _bf(x).astype(F32)
    r = x - hi
    mid = _bf(r).astype(F32)
    return hi, mid, _bf(r - mid).astype(F32)


def _causal_conv_tile(x, w, carry_ref, cols):
    rows = x.shape[0]
    first = lax.broadcasted_iota(jnp.int32, (SUBLANES, x.shape[1]), 0) == 0
    z = x * w[0:1]
    for j in range(1, w.shape[0]):
        rolled = pltpu.roll(z, 1, 0)
        head = jnp.where(first, carry_ref[j - 1:j, cols], rolled[0:SUBLANES])
        carry_ref[j - 1:j, cols] = z[rows - 1:, :]
        z = jnp.concatenate([head, rolled[SUBLANES:]], axis=0) + x * w[j:j + 1]
    return z


def _const_spec(a):
    nd = a.ndim
    return pl.BlockSpec(a.shape, lambda *_: (0,) * nd, pipeline_mode=pl.Buffered(1))


def _params(n_axes=1):
    return pltpu.CompilerParams(dimension_semantics=("arbitrary",) * n_axes,
                                vmem_limit_bytes=VMEM_LIMIT_BYTES)


def _call(body, name, steps, ins, outs, scratch=()):
    return pl.pallas_call(body, out_shape=[o for o, _ in outs], grid=(steps,), in_specs=[sp for _, sp in ins],
                          out_specs=[sp for _, sp in outs], scratch_shapes=list(scratch),
                          compiler_params=_params(), name=name)(*[a for a, _ in ins])


def _rows(a, tm):
    return a, pl.BlockSpec((tm, a.shape[1]), lambda i: (i, 0))


def _whole(a):
    return a, _const_spec(a)


def _layer(a, layer):
    nd = a.ndim
    return a, pl.BlockSpec((None,) + a.shape[1:], lambda i: (layer,) + (0,) * (nd - 1), pipeline_mode=pl.Buffered(1))


def _layer_rows(a, layer, tm):
    return a, pl.BlockSpec((None, tm, a.shape[2]), lambda i: (layer, i, 0))


def _grouped(a, tm):
    return a, pl.BlockSpec((a.shape[0], tm // a.shape[0], a.shape[2]), lambda i: (0, i, 0))


def _out_rows(m, c, tm):
    return jax.ShapeDtypeStruct((m, c), F32), pl.BlockSpec((tm, c), lambda i: (i, 0))


def _out_whole(shape):
    return jax.ShapeDtypeStruct(shape, F32), pl.BlockSpec(shape, lambda i: (0,) * len(shape))


def _qkv_activation(y, part):
    a = _silu(y)
    if part == 2:
        return a
    halves = []
    for hh in range(MXU_COLS // HEAD):
        ah = a[:, HEAD * hh:HEAD * (hh + 1)]
        nrm = ah * lax.rsqrt(jnp.sum(ah * ah, axis=-1, keepdims=True) + EPS)
        halves.append(nrm * (HEAD ** -0.5) if part == 0 else nrm)
    return jnp.concatenate(halves, axis=1)


def _decay_and_beta(pab, alog_ref, dtb_ref, n_heads):
    lane = lax.broadcasted_iota(jnp.int32, pab.shape, 1)
    ap = pab + dtb_ref[...]
    softplus = jnp.maximum(ap, 0.0) + jnp.log1p(jnp.exp(-jnp.abs(ap)))
    return jnp.where(lane < n_heads, -jnp.exp(alog_ref[...]) * softplus, _sigmoid(pab))


def _a_in_step_body(x_ref, g_ref, wqkv_ref, wab_ref, wz_ref, wconv_ref, alog_ref, dtb_ref, buf_ref,
                    q_ref, k_ref, v_ref, z_ref, gb_ref, tail_ref, *, n_heads):
    hn = _bf(_rms(x_ref[...]) * g_ref[...])
    outs = (q_ref, k_ref, v_ref)
    per_part = q_ref.shape[1] // MXU_COLS
    for c in range(3 * per_part):
        cols = slice(MXU_COLS * c, MXU_COLS * (c + 1))
        pq = _dot(hn, wqkv_ref[:, cols])
        w = wconv_ref[:, cols]
        y = buf_ref[0, :, cols] * w[0:1]
        y = y + buf_ref[1, :, cols] * w[1:2]
        y = y + buf_ref[2, :, cols] * w[2:3]
        y = y + pq * w[3:4]
        tail_ref[:, cols] = pq
        part, sub = divmod(c, per_part)
        outs[part][:, MXU_COLS * sub:MXU_COLS * (sub + 1)] = _qkv_activation(y, part)
    gb_ref[...] = _decay_and_beta(_dot(hn, wab_ref[...]), alog_ref, dtb_ref, n_heads)
    z_ref[...] = _dot(hn, wz_ref[...])


def _weave(*stages):
    live = list(stages)
    while live:
        for g in list(live):
            try:
                next(g)
            except StopIteration:
                live.remove(g)


def _mixer_a_prompt_body(x_ref, g_ref, wqkv_ref, wab_ref, wz_ref, wconv_ref, alog_ref, dtb_ref,
                         z_ref, tail_ref, o_ref, sout_ref, cbuf, s_scr, *bufs, n_heads, n_chunks):
    step = pl.program_id(0)
    n_proj = 4
    proj_sets = bufs[:n_proj], bufs[n_proj:2 * n_proj]
    rest = bufs[2 * n_proj:]
    prep_sets = rest[:len(rest) // 2], rest[len(rest) // 2:]

    @pl.when(step == 0)
    def _():
        cbuf[...] = jnp.zeros(cbuf.shape, F32)
        s_scr[...] = jnp.zeros(s_scr.shape, F32)
        for ref in (*proj_sets[1], *prep_sets[0]):
            ref[...] = jnp.zeros(ref.shape, ref.dtype)

    sub = n_chunks // 2 * CHUNK

    def project(i, q_s, k_s, v_s, gb_s):
        rows = pl.ds(pl.multiple_of(i * sub, sub), sub)
        hn = _bf(_rms(x_ref[rows, :]) * g_ref[...])
        outs = (q_s, k_s, v_s)
        per_part = q_s.shape[1] // MXU_COLS
        for c in range(3 * per_part):
            cols = slice(MXU_COLS * c, MXU_COLS * (c + 1))
            pq = _dot(hn, wqkv_ref[:, cols])
            y = _causal_conv_tile(pq, wconv_ref[:, cols], cbuf, cols)
            tail_ref[:, cols] = pq[sub - SUBLANES:, :]
            part, sc = divmod(c, per_part)
            outs[part][rows, MXU_COLS * sc:MXU_COLS * (sc + 1)] = _qkv_activation(y, part)
            yield
        gb_s[rows, :] = _decay_and_beta(_dot(hn, wab_ref[...]), alog_ref, dtb_ref, n_heads)
        z_ref[rows, :] = _dot(hn, wz_ref[...])
        yield

    group = MXU_COLS // CHUNK
    shift = CHUNK.bit_length() - 1
    lane_p = lax.broadcasted_iota(jnp.int32, (CHUNK, MXU_COLS), 1)
    row_p = lax.broadcasted_iota(jnp.int32, (CHUNK, MXU_COLS), 0)
    col_p = lane_p & (CHUNK - 1)
    blk = [(lane_p >> shift) == u for u in range(group)]
    incl_p = row_p >= col_p
    strict_p = row_p > col_p
    eye_p = jnp.where(row_p == col_p, 1.0, 0.0).astype(F32)
    bd_r = lax.broadcasted_iota(jnp.int32, (MXU_COLS, MXU_COLS), 0)
    bd_c = lax.broadcasted_iota(jnp.int32, (MXU_COLS, MXU_COLS), 1)
    bd_mask = (bd_r >> shift) == (bd_c >> shift)
    ri = lax.broadcasted_iota(jnp.int32, (CHUNK, CHUNK), 0)
    ci = lax.broadcasted_iota(jnp.int32, (CHUNK, CHUNK), 1)
    ltri3 = jnp.concatenate([_bf(jnp.where(ri >= ci, 1.0, 0.0))] * 3, axis=1)
    g_r = lax.broadcasted_iota(jnp.int32, (MXU_COLS, 3 * LANES), 0) >> shift
    g_c = lax.broadcasted_iota(jnp.int32, (MXU_COLS, 3 * LANES), 1) & (LANES - 1)
    ones3 = jnp.ones((CHUNK, 3 * LANES), BF16)
    er = lax.broadcasted_iota(jnp.int32, (HEAD, HEAD), 0)
    ec = lax.broadcasted_iota(jnp.int32, (HEAD, HEAD), 1)
    eye_h = _bf(jnp.where(er == ec, 1.0, 0.0))
    heads = range(n_heads)
    hs = [slice(HEAD * h, HEAD * (h + 1)) for h in heads]

    def block_diag(x):
        return _bf(jnp.where(bd_mask, jnp.concatenate([x] * group, axis=0), 0.0))

    def pack_cols(cols):
        out = cols[-1]
        for u in reversed(range(group - 1)):
            out = jnp.where(blk[u], cols[u], out)
        return out

    wide = group * HEAD
    kd_r = lax.broadcasted_iota(jnp.int32, (MXU_COLS, wide), 0) >> shift
    kd_c = lax.broadcasted_iota(jnp.int32, (MXU_COLS, wide), 1) >> (HEAD.bit_length() - 1)
    kd_mask = kd_r == kd_c

    def key_blocks(kc):
        return _bf(jnp.where(kd_mask, jnp.concatenate([kc] * group, axis=0), 0.0))

    def stack(ref, rows, h0):
        return jnp.concatenate([ref[rows, hs[h]] for h in range(h0, h0 + group)], axis=0)

    def stack_cols(x, lo):
        return jnp.concatenate([x[:, lo + u:lo + u + 1] for u in range(group)], axis=0)

    def prep(i, q_ref, k_ref, v_ref, gb_ref, u0_s, wkqg_s, qk_s, kdt_s, gl_s):
        chunks = [i * PREP_CHUNKS + j for j in range(PREP_CHUNKS)]
        rows = [pl.ds(pl.multiple_of(c * CHUNK, CHUNK), CHUNK) for c in chunks]
        gbc = [gb_ref[r, :] for r in rows]
        gc = [_dot(ltri3, _bf(jnp.concatenate(_split3f(x), axis=0))) for x in gbc]
        glast = [x[CHUNK - 1:CHUNK, :] for x in gc]
        for j, c in enumerate(chunks):
            gl_s[c] = jnp.broadcast_to(glast[j], (SUBLANES, LANES))
        groups = [(j, h0) for j in range(PREP_CHUNKS) for h0 in range(0, n_heads, group)]
        ng = range(len(groups))
        kc = [k_ref[rows[j], HEAD * h0:HEAD * (h0 + group)] for j, h0 in groups]
        qc = [q_ref[rows[j], HEAD * h0:HEAD * (h0 + group)] for j, h0 in groups]
        pairs = [_dot_nt(_bf(jnp.concatenate([qc[g], kc[g]], axis=0)), key_blocks(kc[g])) for g in ng]
        yield
        gcol_p = [pack_cols([gc[j][:, h:h + 1] for h in range(h0, h0 + group)]) for j, h0 in groups]
        bcol_p = [pack_cols([gbc[j][:, n_heads + h:n_heads + h + 1] for h in range(h0, h0 + group)])
                  for j, h0 in groups]
        grow = [_dot_nt(ones3, _bf(jnp.where(g_c == g_r + h0,
                                               jnp.concatenate([jnp.concatenate([x] * group, axis=0)
                                                                for x in _split3f(gc[j])], axis=1), 0.0)))
                for j, h0 in groups]
        yield
        dec = [jnp.exp(jnp.where(incl_p, gcol_p[g] - grow[g], -jnp.inf)) for g in ng]
        a = [jnp.where(strict_p, bcol_p[g] * pairs[g][CHUNK:] * dec[g], 0.0) for g in ng]
        for g, (j, h0) in enumerate(groups):
            qk = _bf(pairs[g][:CHUNK] * dec[g])
            for u in range(group):
                qk_s[chunks[j], h0 + u] = qk[:, CHUNK * u:CHUNK * (u + 1)]
        tinv = [eye_p - x for x in a]
        pw = [_dot(_bf(-x), block_diag(-x)) for x in a]
        yield
        for _ in range(4):
            res = [_dot(jnp.concatenate([_bf(tinv[g]), _bf(pw[g])], axis=0), block_diag(pw[g])) for g in ng]
            tinv = [tinv[g] + res[g][:CHUNK] for g in ng]
            pw = [r[CHUNK:] for r in res]
            yield
        tinv = [tinv[g] + _dot(_bf(tinv[g]), block_diag(pw[g])) for g in ng]
        yield
        tbd = [block_diag(x) for x in tinv]
        kst = [stack(k_ref, rows[j], h0) for j, h0 in groups]
        gcol_s = [stack_cols(gc[j], h0) for j, h0 in groups]
        bcol_s = [stack_cols(gbc[j], n_heads + h0) for j, h0 in groups]
        gam_s = [jnp.exp(x) for x in gcol_s]
        rhs = [jnp.concatenate([stack(v_ref, rows[j], h0) * bcol_s[g], kst[g] * (bcol_s[g] * gam_s[g])], axis=1)
               for g, (j, h0) in enumerate(groups)]
        sol = [_dot(tbd[g], _bf(rhs[g])) for g in ng]
        yield
        kdt = [_dot_nt(eye_h, _bf(kst[g] * jnp.exp(stack_cols(jnp.broadcast_to(glast[j], (CHUNK, LANES)), h0)
                                                   - gcol_s[g])))
               for g, (j, h0) in enumerate(groups)]
        for g, (j, h0) in enumerate(groups):
            c = chunks[j]
            qg = stack(q_ref, rows[j], h0) * gam_s[g]
            for u in range(group):
                r = slice(CHUNK * u, CHUNK * (u + 1))
                u0_s[c, h0 + u] = sol[g][r, :HEAD]
                wkqg_s[c, h0 + u] = _bf(jnp.concatenate([sol[g][r, HEAD:], qg[r]], axis=0))
                kdt_s[c, h0 + u] = _bf(kdt[g][:, r])

    def scan(i, u0_s, wkqg_s, qk_s, kdt_s, gl_s):
        for j in range(PREP_CHUNKS):
            c = i * PREP_CHUNKS + j
            rows = pl.ds(pl.multiple_of(c * CHUNK, CHUNK), CHUNK)
            glr = gl_s[c]
            ws = [_dot(wkqg_s[c, h], _bf(s_scr[h])) for h in heads]
            yield
            ub = [_bf(u0_s[c, h] - ws[h][:CHUNK]) for h in heads]
            for h in heads:
                o_ref[rows, HEAD * h:HEAD * (h + 1)] = ws[h][CHUNK:] + _dot(qk_s[c, h], ub[h])
            yield
            for h in heads:
                s_scr[h] = jnp.exp(glr[0:1, h:h + 1]) * s_scr[h] + _dot(kdt_s[c, h], ub[h])
            yield

    def run(mine, other):
        def body(i, carry):
            _weave(project(i, *proj_sets[mine]), prep(i, *proj_sets[other], *prep_sets[other]),
                   scan(i, *prep_sets[mine]))
            return carry
        lax.fori_loop(0, n_chunks // PREP_CHUNKS, body, 0)

    @pl.when(step % 2 == 0)
    def _():
        run(0, 1)

    @pl.when(step % 2 == 1)
    def _():
        run(1, 0)

    sout_ref[...] = s_scr[...]


def _mixer_a_prompt(x, w, n_heads):
    t, d = x.shape
    aw = w["w_z"].shape[1]
    n_chunks = 2 * PREP_CHUNKS
    tt = n_chunks * CHUNK
    assert t % tt == 0 and n_chunks % PREP_CHUNKS == 0
    n_tiles = t // tt
    projected = [pltpu.VMEM((tt, aw), F32)] * 3 + [pltpu.VMEM((tt, LANES), F32)]
    prepared = [pltpu.VMEM((n_chunks, n_heads, CHUNK, HEAD), F32),
                pltpu.VMEM((n_chunks, n_heads, 2 * CHUNK, HEAD), BF16),
                pltpu.VMEM((n_chunks, n_heads, CHUNK, CHUNK), BF16),
                pltpu.VMEM((n_chunks, n_heads, HEAD, CHUNK), BF16),
                pltpu.VMEM((n_chunks, SUBLANES, LANES), F32)]
    scratch = [pltpu.VMEM((SUBLANES, 3 * aw), F32), pltpu.VMEM((n_heads, HEAD, HEAD), F32)]
    scratch += projected + projected + prepared + prepared
    first = lambda s: (jnp.minimum(s, n_tiles - 1), 0)
    last = lambda s: (jnp.maximum(s - 2, 0), 0)
    ins = [(x, pl.BlockSpec((tt, d), first))] + [_whole(w[name]) for name in
                                                  ("g_mix0", "w_qkv", "w_ab", "w_z", "w_aconv", "a_log", "dt_bias")]
    outs = [(jax.ShapeDtypeStruct((t, aw), F32), pl.BlockSpec((tt, aw), first)), _out_whole((SUBLANES, 3 * aw)),
            (jax.ShapeDtypeStruct((t, aw), F32), pl.BlockSpec((tt, aw), last)), _out_whole((n_heads, HEAD, HEAD))]
    z, tail, o, s1 = _call(functools.partial(_mixer_a_prompt_body, n_heads=n_heads, n_chunks=n_chunks),
                           "mixer_a_prompt", n_tiles + 2, ins, outs, scratch)
    return o, z, s1, tail


def _delta_step_body(qt_ref, kt_ref, v_ref, gbt_ref, s_ref, o_ref, so_ref, *, n_seq, n_heads):
    h = pl.program_id(0)
    decay = jnp.exp(gbt_ref[pl.ds(h, 1), :])
    beta = gbt_ref[pl.ds(n_heads + h, 1), :]
    for b in range(n_seq):
        s = decay[:, b:b + 1] * s_ref[b]
        kc = kt_ref[:, b:b + 1]
        u = beta[:, b:b + 1] * (v_ref[b:b + 1, :] - jnp.sum(kc * s, axis=0, keepdims=True))
        s = s + kc * u
        so_ref[b] = s
        o_ref[b:b + 1, :] = jnp.sum(qt_ref[:, b:b + 1] * s, axis=0, keepdims=True)


def _delta_step(q, k, v, gb, state, n_heads):
    n_seq = q.shape[0]
    col = pl.BlockSpec((HEAD, n_seq), lambda h: (h, 0))
    row = pl.BlockSpec((n_seq, HEAD), lambda h: (0, h))
    st = pl.BlockSpec((n_seq, None, HEAD, HEAD), lambda h: (0, h, 0, 0))
    gbt = gb.T
    return pl.pallas_call(
        functools.partial(_delta_step_body, n_seq=n_seq, n_heads=n_heads),
        out_shape=[jax.ShapeDtypeStruct(q.shape, F32), jax.ShapeDtypeStruct(state.shape, F32)],
        grid=(n_heads,), in_specs=[col, col, row, _const_spec(gbt), st], out_specs=[row, st],
        compiler_params=_params(), name="delta_step")(q.T, k.T, v, gbt, state)


def _delta_out_mix(o_ref, z_ref, res_ref, gout_ref, w_ref):
    parts = []
    for h in range(o_ref.shape[1] // HEAD):
        hs = slice(HEAD * h, HEAD * (h + 1))
        parts.append(_bf(_rms(o_ref[:, hs]) * gout_ref[...] * _silu(z_ref[:, hs])))
    return res_ref[...] + _dot(jnp.concatenate(parts, axis=1), w_ref[...])


def _attn_out_mix(refs, scratch, tm, dils):
    ng = len(dils)
    o_refs, l_refs = refs[:ng], refs[ng:2 * ng]
    res_ref, w_ref = refs[2 * ng:]
    get_o, lses = [], []
    for gi, dil in enumerate(dils):
        if dil == 1:
            get_o.append(lambda hh, r=o_refs[gi]: r[0, :, HEAD * hh:HEAD * (hh + 1)])
            lses.append(l_refs[gi][0])
        else:
            o_s, l_s = scratch.pop(0), scratch.pop(0)
            for r in range(dil):
                dst = pl.ds(r, tm // dil, stride=dil)
                for hh in range(Q_PER_GROUP):
                    o_s.at[hh][dst, :] = o_refs[gi][r, :, HEAD * hh:HEAD * (hh + 1)]
                l_s[dst, :] = l_refs[gi][r]
            get_o.append(lambda hh, r=o_s: r[hh])
            lses.append(l_s[...])
    parts = []
    for hh in range(Q_PER_GROUP):
        lh = [l[:, hh:hh + 1] for l in lses]
        mx = functools.reduce(jnp.maximum, lh)
        ex = [jnp.exp(x - mx) for x in lh]
        den = functools.reduce(lambda a, b: a + b, ex)
        acc = (ex[0] / den) * get_o[0](hh)
        for gi in range(1, ng):
            acc = acc + (ex[gi] / den) * get_o[gi](hh)
        parts.append(_bf(acc))
    return res_ref[...] + _dot(jnp.concatenate(parts, axis=1), w_ref[...])


def _ffn_stages(h, p_ref, buf_ref, gffn_ref, wup_ref, wconv_ref, wdown_ref, gple_ref, wgate_ref, wproj_ref,
                tail_ref, ubuf, act_s, result, *, tm):
    hn = _bf(_rms(h) * gffn_ref[...])
    dff = wdown_ref.shape[0]
    for c in range(dff // MXU_COLS):
        ys = []
        for half in range(2):
            lo = half * dff + MXU_COLS * c
            cols = slice(lo, lo + MXU_COLS)
            u = _dot(hn, wup_ref[:, cols])
            w = wconv_ref[:, cols]
            if buf_ref is not None:
                y = buf_ref[:, cols] * w[0:1]
                y = y + buf_ref[:, 2 * dff + lo:2 * dff + lo + MXU_COLS] * w[1:2]
                y = y + u * w[2:3]
                tail_ref[:, cols] = u
            else:
                y = _causal_conv_tile(u, w, ubuf, cols)
                tail_ref[:, cols] = u[tm - SUBLANES:, :]
            ys.append(y)
        act_s[:, MXU_COLS * c:MXU_COLS * (c + 1)] = _bf(_silu(ys[0]) * ys[1])
        yield
    h2 = h + _dot(act_s[...], wdown_ref[...])
    yield
    gate = _sigmoid(_dot(_bf(_rms(h2) * gple_ref[...]), wgate_ref[...]))
    result.append(h2 + gate * _dot(_bf(p_ref[...]), wproj_ref[...]))
    yield


def _mix_ffn_body(*refs, tm, per_row_state, dils):
    n_mix = 5 if dils is None else 2 * len(dils) + 2
    mix_in, refs = refs[:n_mix], refs[n_mix:]
    p_ref, refs = refs[0], refs[1:]
    buf_ref = None
    if per_row_state:
        buf_ref, refs = refs[0], refs[1:]
    ffn_w, (out_ref, tail_ref), scratch = refs[:7], refs[7:9], list(refs[9:])
    act_s = scratch.pop()
    ubuf = None
    if not per_row_state:
        ubuf = scratch.pop()

        @pl.when(pl.program_id(0) == 0)
        def _():
            ubuf[...] = jnp.zeros(ubuf.shape, F32)

    h = _delta_out_mix(*mix_in) if dils is None else _attn_out_mix(mix_in, scratch, tm, dils)
    result = []
    for _ in _ffn_stages(h, p_ref, buf_ref, *ffn_w, tail_ref, ubuf, act_s, result, tm=tm):
        pass
    out_ref[...] = result[0]


def _mix_ffn(mix_ins, mix_scratch, dils, p, w, layer, m, tm, state):
    per_row = state is not None
    d, dff = w["w_down"].shape[2], w["w_down"].shape[1]
    ins = list(mix_ins) + [_layer_rows(p, layer, tm)] + ([_rows(state.reshape(m, -1), tm)] if per_row else [])
    ins += [_layer(w[name], layer) for name in ("g_ffn", "w_up", "w_fconv", "w_down", "g_ple", "w_gate", "w_proj")]
    outs = [_out_rows(m, d, tm), _out_whole((m if per_row else SUBLANES, 2 * dff))]
    scratch = list(mix_scratch) + ([] if per_row else [pltpu.VMEM((SUBLANES, 2 * dff), F32)])
    scratch.append(pltpu.VMEM((tm, dff), BF16))
    return _call(functools.partial(_mix_ffn_body, tm=tm, per_row_state=per_row, dils=dils), f"mix_ffn{layer}",
                 m // tm, ins, outs, scratch)


def _kvq_stages(h, tab_ref, gkv_ref, gmix_ref, wkv_ref, wq_ref, gk_ref, gq_ref, q_ref, k_ref, v_ref, qd, kd, vd,
                stage, *, tm, dils):
    base = _rms(h)
    hkv = _bf(base * gkv_ref[...])
    hq = _bf(base * gmix_ref[...])
    tab = tab_ref[...]
    lane = lax.broadcasted_iota(jnp.int32, tab.shape, 1)
    half = ROPE_DIM // 2
    cos = jnp.where(lane < half, tab, jnp.where(lane < ROPE_DIM, pltpu.roll(tab, half, 1), 1.0))
    sa = jnp.where((lane >= half) & (lane < ROPE_DIM), tab, 0.0)
    sb = jnp.where(lane < half, -pltpu.roll(tab, HEAD - half, 1), 0.0)
    kw = KV_PER_GROUP * HEAD

    def norm_rope(x, g):
        y = _rms(x) * g
        return y * cos + pltpu.roll(y, ROPE_DIM // 2, 1) * sa + pltpu.roll(y, HEAD - ROPE_DIM // 2, 1) * sb

    def heads(x, g):
        return jnp.concatenate([norm_rope(x[:, HEAD * j:HEAD * (j + 1)], g) for j in range(MXU_COLS // HEAD)], axis=1)

    def by_class(x, dst_ref, dil, cols):
        if dil == 1:
            dst_ref[0, :, cols] = _bf(x)
        else:
            for j in range(MXU_COLS // LANES):
                stage[j] = x[:, LANES * j:LANES * (j + 1)]
            for r in range(dil):
                for j in range(MXU_COLS // LANES):
                    lo = cols.start + LANES * j
                    dst_ref[r, :, lo:lo + LANES] = _bf(stage.at[j][pl.ds(r, tm // dil, stride=dil), :])

    assert kw == MXU_COLS and Q_PER_GROUP * HEAD == 2 * MXU_COLS
    n_kv = k_ref.shape[1]

    def project(kind, c):
        lo = MXU_COLS * c + (n_kv if kind == "v" else 0)
        return _dot(hq if kind == "q" else hkv, (wq_ref if kind == "q" else wkv_ref)[:, lo:lo + MXU_COLS])

    def finish(kind, c, x):
        cols = slice(MXU_COLS * c, MXU_COLS * (c + 1))
        if kind == "k":
            x = heads(x, gk_ref[...])
            k_ref[:, cols] = x
        elif kind == "v":
            v_ref[:, cols] = x
        else:
            x = heads(x, gq_ref[...])
            if dils is None:
                q_ref[:, cols] = x
        if dils is not None:
            if kind == "q":
                by_class(x, qd[c // 2], dils[c // 2], slice(MXU_COLS * (c % 2), MXU_COLS * (c % 2 + 1)))
            else:
                by_class(x, (kd if kind == "k" else vd)[c], dils[c], slice(0, kw))

    yield
    for kind, n in (("k", n_kv), ("v", n_kv), ("q", wq_ref.shape[1])):
        for c in range(n // MXU_COLS):
            finish(kind, c, project(kind, c))
            yield


def _kvq_body(*refs, tm, dils):
    h_ref, consts, outs = refs[0], refs[1:8], refs[8:]
    if dils is None:
        q_ref, k_ref, v_ref = outs
        qd = kd = vd = stage = None
    else:
        ng = len(dils)
        q_ref, (k_ref, v_ref) = None, outs[:2]
        qd, kd, vd, stage = outs[2:2 + ng], outs[2 + ng:2 + 2 * ng], outs[2 + 2 * ng:2 + 3 * ng], outs[2 + 3 * ng]
    for _ in _kvq_stages(h_ref[...], *consts, q_ref, k_ref, v_ref, qd, kd, vd, stage, tm=tm, dils=dils):
        pass


def _kvq(h, table, w, tm, dils):
    m = h.shape[0]
    n_q, n_kv = w["w_q"].shape[1], w["w_kv"].shape[1] // 2
    qw, kw = Q_PER_GROUP * HEAD, KV_PER_GROUP * HEAD
    consts = [w["g_kv"], w["g_mix1"], w["w_kv"], w["w_q"], w["g_k"], w["g_q"]]
    rows = [h, table]
    row_spec = lambda c: pl.BlockSpec((tm, c), lambda i: (i, 0))
    in_specs = [row_spec(a.shape[1]) for a in rows] + [_const_spec(a) for a in consts]
    nat = lambda c: jax.ShapeDtypeStruct((m, c), F32)
    if dils is None:
        out_shape = [nat(n_q), nat(n_kv), nat(n_kv)]
        out_specs, scratch = [row_spec(n_q), row_spec(n_kv), row_spec(n_kv)], []
    else:
        out_shape, out_specs = [nat(n_kv), nat(n_kv)], [row_spec(n_kv), row_spec(n_kv)]
        for width in (qw, kw, kw):
            for dil in dils:
                assert tm % (dil * 2 * SUBLANES) == 0
                out_shape.append(jax.ShapeDtypeStruct((dil, m // dil, width), BF16))
                out_specs.append(pl.BlockSpec((dil, tm // dil, width), lambda i: (0, i, 0)))
        scratch = [pltpu.VMEM((MXU_COLS // LANES, tm, LANES), F32)]
    return pl.pallas_call(functools.partial(_kvq_body, tm=tm, dils=dils), out_shape=out_shape, grid=(m // tm,),
                          in_specs=in_specs, out_specs=out_specs, scratch_shapes=scratch,
                          compiler_params=_params(), name="kvq")(*rows, *consts)


def _rope_table(first, count, consecutive):
    half = ROPE_DIM // 2
    inv_freq = ROPE_THETA ** (-jnp.arange(half, dtype=F32) * 2.0 / ROPE_DIM)
    pad = jnp.zeros((count, HEAD - ROPE_DIM), F32)
    if not consecutive:
        ang = jnp.full((count, 1), first, F32) * inv_freq[None, :]
        return jnp.concatenate([jnp.cos(ang), jnp.sin(ang), pad], axis=1)
    assert count % LANES == 0
    a = (first + LANES * jnp.arange(count // LANES, dtype=F32))[:, None] * inv_freq[None, :]
    b = jnp.arange(LANES, dtype=F32)[:, None] * inv_freq[None, :]
    ca, sa, cb, sb = jnp.cos(a)[:, None], jnp.sin(a)[:, None], jnp.cos(b)[None], jnp.sin(b)[None]
    cos = (ca * cb - sa * sb).reshape(count, half)
    sin = (sa * cb + ca * sb).reshape(count, half)
    return jnp.concatenate([cos, sin, pad], axis=1)


def _attn_prompt_body(q_ref, kc_ref, kp_ref, vc_ref, vp_ref, o_ref, lse_ref, *, n, blocks):
    first_lo = jnp.where(pl.program_id(1) > 0, 0, n)
    qi = lax.broadcasted_iota(jnp.int32, (Q_REP * n, 2 * n), 0) & (n - 1)
    kj = lax.broadcasted_iota(jnp.int32, (Q_REP * n, 2 * n), 1)
    band = (kj >= qi) & (kj <= qi + n)
    lane = lax.broadcasted_iota(jnp.int32, (n, LANES), 1)
    for j in range(blocks):
        rows = slice(n * j, n * (j + 1))
        lse_tile = jnp.zeros((n, LANES), F32)
        for g in range(KV_PER_GROUP):
            hs = slice(HEAD * g, HEAD * (g + 1))
            if j == 0:
                kprev, vprev = kp_ref[:, hs], vp_ref[:, hs]
                valid = band & (kj >= first_lo)
            else:
                prev = slice(n * (j - 1), n * j)
                kprev, vprev = kc_ref[prev, hs], vc_ref[prev, hs]
                valid = band
            kwin = jnp.concatenate([kprev, kc_ref[rows, hs]], axis=0)
            vwin = jnp.concatenate([vprev, vc_ref[rows, hs]], axis=0)
            q0 = Q_REP * HEAD * g
            qg = jnp.concatenate([q_ref[rows, q0 + HEAD * e:q0 + HEAD * (e + 1)] for e in range(Q_REP)], axis=0)
            s = jnp.where(valid, _dot_nt(qg, kwin) * (HEAD ** -0.5), -jnp.inf)
            mx = jnp.max(s, axis=-1, keepdims=True)
            p = jnp.exp(s - mx)
            den = jnp.sum(p, axis=-1, keepdims=True)
            o = _dot(_bf(p), vwin) / den
            lse = mx + jnp.log(den)
            for e in range(Q_REP):
                o_ref[rows, q0 + HEAD * e:q0 + HEAD * (e + 1)] = o[n * e:n * (e + 1)]
                lse_tile = jnp.where(lane == Q_REP * g + e, lse[n * e:n * (e + 1)], lse_tile)
        lse_ref[rows, :] = lse_tile


def _attn_prompt(qd, kd, vd, gi, window):
    dil, ln, qw = qd.shape
    kw = kd.shape[2]
    n = window // dil
    blocks = min(ATTN_BLOCKS, ln // n)
    rows = n * blocks
    assert n == HEAD and ln % rows == 0
    cur = lambda r, b: (r, b, 0)
    prev = lambda r, b: (r, jnp.maximum(b * blocks - 1, 0), 0)
    return pl.pallas_call(
        functools.partial(_attn_prompt_body, n=n, blocks=blocks),
        out_shape=[jax.ShapeDtypeStruct((dil, ln, qw), F32), jax.ShapeDtypeStruct((dil, ln, LANES), F32)],
        grid=(dil, ln // rows),
        in_specs=[pl.BlockSpec((None, rows, qw), cur), pl.BlockSpec((None, rows, kw), cur),
                  pl.BlockSpec((None, n, kw), prev), pl.BlockSpec((None, rows, kw), cur),
                  pl.BlockSpec((None, n, kw), prev)],
        out_specs=[pl.BlockSpec((None, rows, qw), cur), pl.BlockSpec((None, rows, LANES), cur)],
        compiler_params=_params(2), name=f"attn_prompt{gi}")(qd, kd, kd, vd, vd)


def _attn_sample_body(q_ref, kn_ref, vn_ref, *refs, bb):
    n_groups = len(GROUPS)
    caches, outs, lses = refs[:n_groups], refs[n_groups:2 * n_groups], refs[2 * n_groups:]
    lane = lax.broadcasted_iota(jnp.int32, (1, LANES), 1)
    kw = KV_PER_GROUP * HEAD
    units = [(gi, b, g) for gi in range(n_groups) for b in range(bb) for g in range(KV_PER_GROUP)]
    scale = HEAD ** -0.5
    qm, kn, vn = [], [], []
    for gi, b, g in units:
        q0 = Q_PER_GROUP * HEAD * gi + Q_REP * HEAD * g
        qm.append(jnp.concatenate([q_ref[b:b + 1, q0 + HEAD * e:q0 + HEAD * (e + 1)] for e in range(Q_REP)]
                                  + [jnp.zeros((SUBLANES - Q_REP, HEAD), F32)], axis=0))
        kn.append(kn_ref[b:b + 1, kw * gi + HEAD * g:kw * gi + HEAD * (g + 1)])
        vn.append(vn_ref[b:b + 1, kw * gi + HEAD * g:kw * gi + HEAD * (g + 1)])
    n = range(len(units))
    s = [_dot_nt(_bf(qm[u]), _bf(caches[gi][b, :, g, :])) * scale for u, (gi, b, g) in enumerate(units)]
    sn = [jnp.sum(qm[u] * kn[u], axis=1, keepdims=True) * scale for u in n]
    mx = [jnp.maximum(jnp.max(s[u], axis=1, keepdims=True), sn[u]) for u in n]
    p = [jnp.exp(s[u] - mx[u]) for u in n]
    pn = [jnp.exp(sn[u] - mx[u]) for u in n]
    den = [jnp.sum(p[u], axis=1, keepdims=True) + pn[u] for u in n]
    pv = [_dot(_bf(p[u]), _bf(caches[gi][b, :, KV_PER_GROUP + g, :])) for u, (gi, b, g) in enumerate(units)]
    lrows = {}
    for u, (gi, b, g) in enumerate(units):
        o = (pv[u] + pn[u] * vn[u]) / den[u]
        lse = mx[u] + jnp.log(den[u])
        for e in range(Q_REP):
            oc = HEAD * (Q_REP * g + e)
            outs[gi][b:b + 1, oc:oc + HEAD] = o[e:e + 1]
            lrows[gi, b] = jnp.where(lane == Q_REP * g + e, lse[e:e + 1], lrows.get((gi, b), jnp.zeros((1, LANES), F32)))
    for (gi, b), lrow in lrows.items():
        lses[gi][b:b + 1, :] = lrow


def _attn_sample(q, k, v, caches):
    n_seq = q.shape[0]
    bb = SAMPLE_SEQS
    assert n_seq % bb == 0
    qw = Q_PER_GROUP * HEAD
    views, cache_specs = [], []
    for cache, (window, dil) in zip(caches, GROUPS):
        n = window // dil
        assert cache.shape[1] == window and n == HEAD
        views.append(cache.reshape(n_seq, n, dil, 2 * KV_PER_GROUP, HEAD))
        cache_specs.append(pl.BlockSpec((bb, n, None, 2 * KV_PER_GROUP, HEAD), lambda i: (i, 0, 0, 0, 0)))
    row = lambda c: pl.BlockSpec((bb, c), lambda i: (i, 0))
    n_groups = len(GROUPS)
    res = pl.pallas_call(
        functools.partial(_attn_sample_body, bb=bb),
        out_shape=[jax.ShapeDtypeStruct((n_seq, qw), F32)] * n_groups
        + [jax.ShapeDtypeStruct((n_seq, LANES), F32)] * n_groups,
        grid=(n_seq // bb,),
        in_specs=[row(q.shape[1]), row(k.shape[1]), row(v.shape[1])] + cache_specs,
        out_specs=[row(qw)] * n_groups + [row(LANES)] * n_groups,
        compiler_params=_params(), name="attn_sample")(q, k, v, *views)
    return res[:n_groups], res[n_groups:]


def _ffn0_kvq_body(*refs, tm, dils, n_tiles):
    mix_in, p_ref, ffn_w, tab_ref, kvq_w = refs[:5], refs[5], refs[6:13], refs[13], refs[14:20]
    out_ref, tail_ref, k_ref, v_ref = refs[20:24]
    ng = len(dils)
    qd, kd, vd = refs[24:24 + ng], refs[24 + ng:24 + 2 * ng], refs[24 + 2 * ng:24 + 3 * ng]
    ubuf, act_s, tail_s, hbuf, stage = refs[24 + 3 * ng:]
    step = pl.program_id(0)

    @pl.when(step == 0)
    def _():
        ubuf[...] = jnp.zeros(ubuf.shape, F32)
        hbuf[...] = jnp.zeros(hbuf.shape, F32)

    h_prev = hbuf[...]
    result = []
    _weave(_ffn_stages(_delta_out_mix(*mix_in), p_ref, None, *ffn_w, tail_s, ubuf, act_s, result, tm=tm),
           _kvq_stages(h_prev, tab_ref, *kvq_w, None, k_ref, v_ref, qd, kd, vd, stage, tm=tm, dils=dils))
    hbuf[...] = result[0]

    @pl.when(step < n_tiles)
    def _():
        out_ref[...] = result[0]
        tail_ref[...] = tail_s[...]


def _ffn0_kvq(o, z, x, p, table, w, tm, dils):
    m, d = x.shape
    dff = w["w_down"].shape[1]
    n_kv = w["w_kv"].shape[1] // 2
    qw, kw = Q_PER_GROUP * HEAD, KV_PER_GROUP * HEAD
    assert m % tm == 0
    n_tiles = m // tm
    cur = lambda s: (jnp.minimum(s, n_tiles - 1), 0)
    prev = lambda s: (jnp.maximum(s - 1, 0), 0)
    at = lambda a, imap: (a, pl.BlockSpec((tm, a.shape[1]), imap))
    ins = [at(o, cur), at(z, cur), at(x, cur), _whole(w["g_aout"]), _whole(w["w_aout"]),
           (p, pl.BlockSpec((None, tm, p.shape[2]), lambda s: (0, jnp.minimum(s, n_tiles - 1), 0)))]
    ins += [_layer(w[name], 0) for name in ("g_ffn", "w_up", "w_fconv", "w_down", "g_ple", "w_gate", "w_proj")]
    ins += [at(table, prev)] + [_whole(w[name]) for name in ("g_kv", "g_mix1", "w_kv", "w_q", "g_k", "g_q")]
    nat = lambda c, imap: (jax.ShapeDtypeStruct((m, c), F32), pl.BlockSpec((tm, c), imap))
    outs = [nat(d, cur), _out_whole((SUBLANES, 2 * dff)), nat(n_kv, prev), nat(n_kv, prev)]
    for width in (qw, kw, kw):
        for dil in dils:
            assert tm % (dil * 2 * SUBLANES) == 0
            outs.append((jax.ShapeDtypeStruct((dil, m // dil, width), BF16),
                         pl.BlockSpec((dil, tm // dil, width), lambda s: (0, jnp.maximum(s - 1, 0), 0))))
    scratch = [pltpu.VMEM((SUBLANES, 2 * dff), F32), pltpu.VMEM((tm, dff), BF16), pltpu.VMEM((SUBLANES, 2 * dff), F32),
               pltpu.VMEM((tm, d), F32), pltpu.VMEM((MXU_COLS // LANES, tm, LANES), F32)]
    return _call(functools.partial(_ffn0_kvq_body, tm=tm, dils=dils, n_tiles=n_tiles), "ffn0_kvq", n_tiles + 1,
                 ins, outs, scratch)


def _trunk(x, p, first_pos, state, w, tm, tm_wide):
    m = x.shape[0]
    per_row = state is not None
    n_heads = w["n_heads"]
    aw = w["w_z"].shape[1]

    if per_row:
        consts = [w["g_mix0"], w["w_qkv"], w["w_ab"], w["w_z"], w["w_aconv"], w["a_log"], w["dt_bias"],
                  jnp.swapaxes(state["qkv_conv"], 0, 1)]
        q, k, v, z, gb, qkv_tail = _call(
            functools.partial(_a_in_step_body, n_heads=n_heads), "a_in_step", m // tm,
            [_rows(x, tm)] + [_whole(c) for c in consts],
            [_out_rows(m, aw, tm)] * 4 + [_out_rows(m, LANES, tm), _out_whole((m, 3 * aw))])
        o, s1 = _delta_step(q, k, v, gb, state["delta"], n_heads)
    else:
        o, z, s1, qkv_tail = _mixer_a_prompt(x, w, n_heads)
    table = _rope_table(first_pos, m, consecutive=not per_row)
    if per_row:
        mix = [_rows(o, tm_wide), _rows(z, tm_wide), _rows(x, tm_wide), _whole(w["g_aout"]), _whole(w["w_aout"])]
        h, ffn_tail0 = _mix_ffn(mix, [], None, p, w, 0, m, tm_wide, state["ffn_conv"][0])
        qa, ka, va = _kvq(h, table, w, tm, None)
        outs, lses = _attn_sample(qa, ka, va, state["caches"])
        outs, lses = [a[None] for a in outs], [a[None] for a in lses]
        dils = (1,) * len(GROUPS)
    else:
        dils = tuple(dil for _, dil in GROUPS)
        ng = len(dils)
        res = _ffn0_kvq(o, z, x, p, table, w, tm, dils)
        h, ffn_tail0, ka, va = res[:4]
        qd, kd, vd = res[4:4 + ng], res[4 + ng:4 + 2 * ng], res[4 + 2 * ng:]
        att = [_attn_prompt(qd[gi], kd[gi], vd[gi], gi, window) for gi, (window, _) in enumerate(GROUPS)]
        outs, lses = [a[0] for a in att], [a[1] for a in att]
    mix = [_grouped(a, tm_wide) for a in (*outs, *lses)] + [_rows(h, tm_wide), _whole(w["w_o"])]
    mix_scratch = []
    for dil in dils:
        if dil > 1:
            assert tm_wide % (dil * SUBLANES) == 0
            mix_scratch += [pltpu.VMEM((Q_PER_GROUP, tm_wide, HEAD), F32), pltpu.VMEM((tm_wide, LANES), F32)]
    h, ffn_tail1 = _mix_ffn(mix, mix_scratch, dils, p, w, 1, m, tm_wide, state["ffn_conv"][1] if per_row else None)
    return h, s1, qkv_tail, (ffn_tail0, ffn_tail1), ka, va


def _kv_rows(k, v, gi, n_rows):
    kw = KV_PER_GROUP * HEAD
    sl = slice(kw * gi, kw * (gi + 1))
    shape = (n_rows, 1, KV_PER_GROUP, HEAD)
    return jnp.concatenate([k[k.shape[0] - n_rows:, sl].reshape(shape), v[v.shape[0] - n_rows:, sl].reshape(shape)],
                           axis=1)


def kernel(x_prompt, x_sample, p_prompt, p_sample, state_delta, state_qkv_conv, state_ffn_conv, cache_kv_w128, cache_kv_w512, cache_kv_w2048, g_mix_norm, g_ffn_norm, w_ffn_up, w_ffn_conv, w_ffn_down, g_ple_norm, w_ple_gate, w_ple_proj, w_a_in, w_a_conv, a_log, a_dt_bias, g_a_out_norm, w_a_out, g_kv_norm, w_kv, g_k_norm, w_q, g_q_norm, w_o):
    n_heads = a_log.shape[1]
    aw = n_heads * HEAD
    assert w_a_in.shape[0] == 1 and w_q.shape[0] == 1 and w_a_in.shape[2] == 4 * aw + 2 * n_heads
    row = lambda a: a.reshape(1, -1)
    lane_pad = lambda a: jnp.pad(a, ((0, 0), (0, LANES - a.shape[1])))
    w = {
        "n_heads": n_heads,
        "g_mix0": row(g_mix_norm[0]), "g_mix1": row(g_mix_norm[1]), "g_ffn": g_ffn_norm[:, None], "g_ple": g_ple_norm[:, None],
        "w_up": _bf(w_ffn_up), "w_fconv": w_ffn_conv, "w_down": _bf(w_ffn_down),
        "w_gate": _bf(w_ple_gate), "w_proj": _bf(w_ple_proj),
        "w_qkv": _bf(w_a_in[0][:, :3 * aw]), "w_ab": _bf(lane_pad(w_a_in[0][:, 3 * aw:3 * aw + 2 * n_heads])),
        "w_z": _bf(w_a_in[0][:, 3 * aw + 2 * n_heads:]), "w_aconv": w_a_conv[0],
        "a_log": lane_pad(a_log), "dt_bias": lane_pad(a_dt_bias),
        "g_aout": row(g_a_out_norm[0]), "w_aout": _bf(w_a_out[0]),
        "g_kv": row(g_kv_norm), "w_kv": _bf(w_kv), "g_k": row(g_k_norm),
        "w_q": _bf(w_q[0]), "g_q": row(g_q_norm[0]), "w_o": _bf(w_o[0]),
    }
    assert x_prompt.shape[0] == 1 and x_sample.shape[1] == 1
    seq, d = x_prompt.shape[1], x_prompt.shape[2]
    n_seq = x_sample.shape[0]

    yp, sd_p, qkv_tail_p, ffn_tails_p, kp, vp = _trunk(
        x_prompt.reshape(seq, d), p_prompt.reshape(p_prompt.shape[0], seq, -1), 0, None, w,
        tm=ROW_TILE, tm_wide=WIDE_ROW_TILE)
    a_keep, f_keep = state_qkv_conv.shape[2], state_ffn_conv.shape[2]
    sq_p = qkv_tail_p[SUBLANES - a_keep:][None, None]
    sf_p = jnp.stack([t[SUBLANES - f_keep:] for t in ffn_tails_p])[:, None]
    kv_p = [_kv_rows(kp, vp, gi, min(window, seq))[None] for gi, (window, _) in enumerate(GROUPS)]

    state = {"delta": state_delta[0], "qkv_conv": state_qkv_conv[0], "ffn_conv": state_ffn_conv,
             "caches": (cache_kv_w128, cache_kv_w512, cache_kv_w2048)}
    ys, sd_s, qkv_tail_s, ffn_tails_s, ks, vs = _trunk(
        x_sample.reshape(n_seq, d), p_sample.reshape(p_sample.shape[0], n_seq, -1), PAST_LEN,
        state, w, tm=n_seq, tm_wide=n_seq)
    sq_s = jnp.concatenate([state_qkv_conv[0][:, 1:], qkv_tail_s[:, None]], axis=1)[None]
    sf_s = jnp.stack([jnp.concatenate([state_ffn_conv[i][:, 1:], ffn_tails_s[i][:, None]], axis=1)
                      for i in range(2)])
    kv_s = [_kv_rows(ks, vs, gi, n_seq).reshape(n_seq, 1, 2, KV_PER_GROUP, HEAD) for gi in range(len(GROUPS))]

    return (yp[None], ys[:, None], sd_p[None, None], sq_p, sf_p, kv_p[0], kv_p[1], kv_p[2],
            sd_s[None], sq_s, sf_s, kv_s[0], kv_s[1], kv_s[2])
```
